```python
import jax
import jax.numpy as jnp
from jax import lax
import numpy as np

D_MODEL = 2048
BATCH = 4
SEQ = 2048
DEPTH = 4

N_MIXERS = 4
HEAD_DIM = 128
N_HEADS = 16
MIX_WIDTH = N_HEADS * HEAD_DIM
N_MEM = 256
MEM_HEADS = 4
MEM_WIDTH = MEM_HEADS * HEAD_DIM
BLOCK = 128
ROPE_THETA = 10000.0
EPS = 1e-6
IDX_HEADS = 16
IDX_DIM = 64
TOPK_MAX = 256
DILATED_PAIRS = ((128, 1), (512, 4), (2048, 16))
N_DIL_GROUPS = len(DILATED_PAIRS)
DIL_HEADS = 6
DIL_WIDTH = DIL_HEADS * HEAD_DIM
Q_LORA = 512
KV_LORA = 512
QK_NOPE = 128
QK_ROPE = 64
V_HEAD = 128

A_SIZES = (MIX_WIDTH, MIX_WIDTH, MIX_WIDTH, N_HEADS, MEM_WIDTH, MIX_WIDTH + MEM_WIDTH)
B_SIZES = (MIX_WIDTH, HEAD_DIM, HEAD_DIM, IDX_HEADS * IDX_DIM, IDX_DIM, IDX_HEADS, MEM_WIDTH, MIX_WIDTH + MEM_WIDTH)
C_SIZES = (N_DIL_GROUPS * DIL_WIDTH, N_DIL_GROUPS * DIL_WIDTH, N_DIL_GROUPS * DIL_WIDTH, MEM_WIDTH, DIL_WIDTH + MEM_WIDTH)
D_SIZES = (Q_LORA, KV_LORA, QK_ROPE, MEM_WIDTH, N_HEADS * V_HEAD + MEM_WIDTH)

F32 = jnp.float32

kernel_name = 'hybrid_fox_dsa_dilated_mla_block'


def rms_norm(x, g):
    x32 = x.astype(F32)
    y = x32 * lax.rsqrt(jnp.mean(x32 * x32, axis=-1, keepdims=True) + EPS)
    return y.astype(x.dtype) * g


def rope(x, pos):
    dh = x.shape[-1]
    half = dh // 2
    inv_freq = jnp.power(ROPE_THETA, -jnp.arange(half, dtype=F32) * 2.0 / dh)
    ang = pos.astype(F32)[:, :, None, None] * inv_freq
    cos, sin = jnp.cos(ang), jnp.sin(ang)
    x32 = x.astype(F32)
    x1, x2 = x32[..., :half], x32[..., half:]
    return jnp.concatenate([x1 * cos - x2 * sin, x2 * cos + x1 * sin], axis=-1).astype(x.dtype)


def split_cols(u, sizes):
    bounds = []
    acc = 0
    for s in sizes[:-1]:
        acc += s
        bounds.append(acc)
    return jnp.split(u, bounds, axis=-1)


def to_blocks(a):
    b, s = a.shape[:2]
    return jnp.moveaxis(a.reshape(b, s // BLOCK, BLOCK, *a.shape[2:]), 1, 0)


def from_blocks(a):
    nb, b, blk = a.shape[:3]
    return jnp.moveaxis(a, 0, 1).reshape(b, nb * blk, *a.shape[3:])


def masked_softmax(logits, mask):
    return jax.nn.softmax(jnp.where(mask, logits, -jnp.inf), axis=-1)


def forgetting_attention(q, k, v, log_f):
    s_len, dh = q.shape[1], q.shape[-1]
    c = jnp.cumsum(log_f, axis=1)
    c_keys = jnp.moveaxis(c, 1, 2)[:, :, None, :]
    kpos = jnp.arange(s_len)
    scale = dh ** -0.5

    def block(args):
        qb, cb, i = args
        qpos = i * BLOCK + jnp.arange(BLOCK)
        s = jnp.einsum('bqhd,bkhd->bhqk', qb, k).astype(F32) * scale
        s = s + jnp.moveaxis(cb, 1, 2)[..., None] - c_keys
        p = masked_softmax(s, kpos[None, :] <= qpos[:, None])
        return jnp.einsum('bhqk,bkhd->bqhd', p.astype(v.dtype), v)

    out = lax.map(block, (to_blocks(q), to_blocks(c), jnp.arange(s_len // BLOCK)))
    return from_blocks(out)


def dsa_attention(q, k, v, q_idx, k_idx, w_idx):
    s_len, dh = q.shape[1], q.shape[-1]
    n_sel = min(TOPK_MAX, s_len // 4)
    scale = dh ** -0.5
    idx_scale = (IDX_DIM ** -0.5) * (IDX_HEADS ** -0.5)
    kpos = jnp.arange(s_len)
    gather = jax.vmap(lambda a, i: a[i])

    def block(args):
        qb, qib, wb, i = args
        qpos = i * BLOCK + jnp.arange(BLOCK)
        rel = jax.nn.relu(jnp.einsum('bqhd,bsd->bqhs', qib, k_idx).astype(F32))
        score = jnp.einsum('bqh,bqhs->bqs', wb.astype(F32), rel) * idx_scale
        score = jnp.where(kpos[None, None, :] <= qpos[None, :, None], score, -jnp.inf)
        _, sel = lax.top_k(score, n_sel)
        valid = sel <= qpos[None, :, None]
        kg = gather(k, sel)
        vg = gather(v, sel)
        s = jnp.einsum('bqhd,bqjd->bhqj', qb, kg).astype(F32) * scale
        p = masked_softmax(s, valid[:, None])
        return jnp.einsum('bhqj,bqjd->bqhd', p.astype(v.dtype), vg)

    out = lax.map(block, (to_blocks(q), to_blocks(q_idx), to_blocks(w_idx), jnp.arange(s_len // BLOCK)))
    return from_blocks(out)


def dilated_attention(q, k, v):
    s_len, dh = q.shape[1], q.shape[-1]
    scale = dh ** -0.5
    k_groups = [k[:, :, g] for g in range(N_DIL_GROUPS)]
    v_groups = [v[:, :, g] for g in range(N_DIL_GROUPS)]

    def block(args):
        qb, i = args
        qpos = i * BLOCK + jnp.arange(BLOCK)
        outs, lses = [], []
        for g, (window, dil) in enumerate(DILATED_PAIRS):
            offs = jnp.arange(window // dil + 1) * dil
            kpos = qpos[:, None] - offs[None, :]
            valid = kpos >= 0
            kidx = jnp.maximum(kpos, 0)
            kg = jnp.take(k_groups[g], kidx, axis=1)
            vg = jnp.take(v_groups[g], kidx, axis=1)
            s = jnp.einsum('bqhd,bqjhd->bhqj', qb[:, :, g], kg).astype(F32) * scale
            s = jnp.where(valid, s, -jnp.inf)
            lse = jax.nn.logsumexp(s, axis=-1, keepdims=True)
            p = jnp.exp(s - lse)
            outs.append(jnp.einsum('bhqj,bqjhd->bqhd', p.astype(v.dtype), vg))
            lses.append(jnp.moveaxis(lse[..., 0], 1, 2))
        alpha = jax.nn.softmax(jnp.stack(lses, axis=-1), axis=-1)
        o = jnp.stack(outs, axis=-1).astype(F32)
        return jnp.einsum('bqhdg,bqhg->bqhd', o, alpha).astype(v.dtype)

    out = lax.map(block, (to_blocks(q), jnp.arange(s_len // BLOCK)))
    return from_blocks(out)


def mla_attention(q_nope, q_rope, k_nope, k_rope, v):
    s_len = q_nope.shape[1]
    scale = (QK_NOPE + QK_ROPE) ** -0.5
    kpos = jnp.arange(s_len)

    def block(args):
        qn, qr, i = args
        qpos = i * BLOCK + jnp.arange(BLOCK)
        s = (jnp.einsum('bqhd,bkhd->bhqk', qn, k_nope)
             + jnp.einsum('bqhr,bkr->bhqk', qr, k_rope)).astype(F32) * scale
        p = masked_softmax(s, kpos[None, :] <= qpos[:, None])
        return jnp.einsum('bhqk,bkhd->bqhd', p.astype(v.dtype), v)

    out = lax.map(block, (to_blocks(q_nope), to_blocks(q_rope), jnp.arange(s_len // BLOCK)))
    return from_blocks(out)


def memory_attention(q_mem, mem_k, mem_v):
    s = jnp.einsum('bqhd,bnhd->bhqn', q_mem, mem_k).astype(F32) * HEAD_DIM ** -0.5
    p = jax.nn.softmax(s, axis=-1).astype(mem_v.dtype)
    return jnp.einsum('bhqn,bnhd->bqhd', p, mem_v)


def forgetting_mixer(h, pos, w_in, forget_bias):
    b, s, _ = h.shape
    q, k, v, f, q_mem, z = split_cols(h @ w_in, A_SIZES)
    log_f = jax.nn.log_sigmoid(f.astype(F32) + forget_bias.astype(F32))
    hs = (b, s, N_HEADS, HEAD_DIM)
    y = forgetting_attention(q.reshape(hs), k.reshape(hs), v.reshape(hs), log_f)
    return y.reshape(b, s, MIX_WIDTH), q_mem, z


def dsa_mixer(h, pos, w_in):
    b, s, _ = h.shape
    q, k, v, q_idx, k_idx, w_idx, q_mem, z = split_cols(h @ w_in, B_SIZES)
    q = rope(q.reshape(b, s, N_HEADS, HEAD_DIM), pos)
    k = rope(k[:, :, None, :], pos)[:, :, 0]
    q_idx = rope(q_idx.reshape(b, s, IDX_HEADS, IDX_DIM), pos)
    k_idx = rope(k_idx[:, :, None, :], pos)[:, :, 0]
    y = dsa_attention(q, k, v, q_idx, k_idx, w_idx)
    return y.reshape(b, s, MIX_WIDTH), q_mem, z


def dilated_mixer(h, pos, w_in):
    b, s, _ = h.shape
    q, k, v, q_mem, z = split_cols(h @ w_in, C_SIZES)
    flat = (b, s, N_DIL_GROUPS * DIL_HEADS, HEAD_DIM)
    grp = (b, s, N_DIL_GROUPS, DIL_HEADS, HEAD_DIM)
    q = rope(q.reshape(flat), pos).reshape(grp)
    k = rope(k.reshape(flat), pos).reshape(grp)
    y = dilated_attention(q, k, v.reshape(grp))
    return y.reshape(b, s, DIL_WIDTH), q_mem, z


def mla_mixer(h, pos, w_in, q_norm, w_uq, kv_norm, w_ukv):
    b, s, _ = h.shape
    c_q, c_kv, k_rope, q_mem, z = split_cols(h @ w_in, D_SIZES)
    qf = (rms_norm(c_q, q_norm) @ w_uq).reshape(b, s, N_HEADS, QK_NOPE + QK_ROPE)
    q_nope, q_rope = qf[..., :QK_NOPE], rope(qf[..., QK_NOPE:], pos)
    kvf = (rms_norm(c_kv, kv_norm) @ w_ukv).reshape(b, s, N_HEADS, QK_NOPE + V_HEAD)
    k_nope, v = kvf[..., :QK_NOPE], kvf[..., QK_NOPE:]
    k_rope = rope(k_rope[:, :, None, :], pos)[:, :, 0]
    y = mla_attention(q_nope, q_rope, k_nope, k_rope, v)
    return y.reshape(b, s, N_HEADS * V_HEAD), q_mem, z


def hybrid_layer(x, mem, norm_g, mem_norm_g, w_mem_kv, w_out, mixer):
    b, s, _ = x.shape
    y, q_mem, z = mixer(rms_norm(x, norm_g))
    kv = (rms_norm(mem, mem_norm_g) @ w_mem_kv).reshape(b, mem.shape[1], 2, MEM_HEADS, HEAD_DIM)
    y_mem = memory_attention(q_mem.reshape(b, s, MEM_HEADS, HEAD_DIM), kv[:, :, 0], kv[:, :, 1])
    gated = jnp.concatenate([y, y_mem.reshape(b, s, MEM_WIDTH)], axis=-1) * jax.nn.silu(z)
    return x + gated @ w_out


def setup_inputs(seed: int = 0) -> dict:
    key = jax.random.key(seed)
    keys = iter(jax.random.split(key, 64))

    def normal(shape, scale):
        return jax.random.normal(next(keys), shape, jnp.float32) * scale

    def gain(n):
        return 1.0 + 0.02 * jax.random.normal(next(keys), (n,), jnp.float32)

    def w_out(width):
        return normal((width, D_MODEL), 0.5 * width ** -0.5)

    s_in = D_MODEL ** -0.5
    inputs = {}
    inputs['x'] = normal((BATCH, SEQ, D_MODEL), 1.0)
    inputs['mem'] = normal((BATCH, N_MEM, D_MODEL), 1.0)
    inputs['positions'] = (jax.random.randint(next(keys), (BATCH, 1), 0, 1024, dtype=jnp.int32)
                           + jnp.arange(SEQ, dtype=jnp.int32)[None, :])
    inputs['l0_norm'] = gain(D_MODEL)
    inputs['l0_w_in'] = jnp.concatenate([
        normal((D_MODEL, 3 * MIX_WIDTH), s_in),
        normal((D_MODEL, N_HEADS), 0.1 * s_in),
        normal((D_MODEL, MEM_WIDTH + MIX_WIDTH + MEM_WIDTH), s_in)], axis=1)
    inputs['l0_forget_bias'] = 3.0 + 0.5 * jax.random.normal(next(keys), (N_HEADS,), jnp.float32)
    inputs['l0_mem_norm'] = gain(D_MODEL)
    inputs['l0_w_mem_kv'] = normal((D_MODEL, 2 * MEM_WIDTH), s_in)
    inputs['l0_w_out'] = w_out(MIX_WIDTH + MEM_WIDTH)
    inputs['l1_norm'] = gain(D_MODEL)
    inputs['l1_w_in'] = normal((D_MODEL, sum(B_SIZES)), s_in)
    inputs['l1_mem_norm'] = gain(D_MODEL)
    inputs['l1_w_mem_kv'] = normal((D_MODEL, 2 * MEM_WIDTH), s_in)
    inputs['l1_w_out'] = w_out(MIX_WIDTH + MEM_WIDTH)
    inputs['l2_norm'] = gain(D_MODEL)
    inputs['l2_w_in'] = normal((D_MODEL, sum(C_SIZES)), s_in)
    inputs['l2_mem_norm'] = gain(D_MODEL)
    inputs['l2_w_mem_kv'] = normal((D_MODEL, 2 * MEM_WIDTH), s_in)
    inputs['l2_w_out'] = w_out(DIL_WIDTH + MEM_WIDTH)
    inputs['l3_norm'] = gain(D_MODEL)
    inputs['l3_w_in'] = normal((D_MODEL, sum(D_SIZES)), s_in)
    inputs['l3_q_norm'] = gain(Q_LORA)
    inputs['l3_w_uq'] = normal((Q_LORA, N_HEADS * (QK_NOPE + QK_ROPE)), Q_LORA ** -0.5)
    inputs['l3_kv_norm'] = gain(KV_LORA)
    inputs['l3_w_ukv'] = normal((KV_LORA, N_HEADS * (QK_NOPE + V_HEAD)), KV_LORA ** -0.5)
    inputs['l3_mem_norm'] = gain(D_MODEL)
    inputs['l3_w_mem_kv'] = normal((D_MODEL, 2 * MEM_WIDTH), s_in)
    inputs['l3_w_out'] = w_out(N_HEADS * V_HEAD + MEM_WIDTH)
    inputs['final_norm'] = gain(D_MODEL)
    return inputs


def reference(x, mem, positions,
              l0_norm, l0_w_in, l0_forget_bias, l0_mem_norm, l0_w_mem_kv, l0_w_out,
              l1_norm, l1_w_in, l1_mem_norm, l1_w_mem_kv, l1_w_out,
              l2_norm, l2_w_in, l2_mem_norm, l2_w_mem_kv, l2_w_out,
              l3_norm, l3_w_in, l3_q_norm, l3_w_uq, l3_kv_norm, l3_w_ukv, l3_mem_norm, l3_w_mem_kv, l3_w_out,
              final_norm):
    mixers = (
        lambda h: forgetting_mixer(h, positions, l0_w_in, l0_forget_bias),
        lambda h: dsa_mixer(h, positions, l1_w_in),
        lambda h: dilated_mixer(h, positions, l2_w_in),
        lambda h: mla_mixer(h, positions, l3_w_in, l3_q_norm, l3_w_uq, l3_kv_norm, l3_w_ukv),
    )
    layer_params = (
        (l0_norm, l0_mem_norm, l0_w_mem_kv, l0_w_out),
        (l1_norm, l1_mem_norm, l1_w_mem_kv, l1_w_out),
        (l2_norm, l2_mem_norm, l2_w_mem_kv, l2_w_out),
        (l3_norm, l3_mem_norm, l3_w_mem_kv, l3_w_out),
    )
    for i in range(DEPTH):
        norm_g, mem_g, w_mem_kv, w_out = layer_params[i]
        x = hybrid_layer(x, mem, norm_g, mem_g, w_mem_kv, w_out, mixers[i % N_MIXERS])
    return rms_norm(x, final_norm)
```

```python
import functools

import jax
import jax.numpy as jnp
from jax import lax
from jax.experimental import pallas as pl
from jax.experimental.pallas import tpu as pltpu

F32 = jnp.float32
BF16 = jnp.bfloat16
I32 = jnp.int32

EPS = 1e-6
ROPE_THETA = 10000.0
HEAD_DIM = 128
N_HEADS = 16
MEM_HEADS = 4
MEM_WIDTH = MEM_HEADS * HEAD_DIM
IDX_HEADS = 16
IDX_DIM = 64
TOPK_MAX = 256
DILATED_PAIRS = ((128, 1), (512, 4), (2048, 16))
DIL_HEADS = 6
DIL_WIDTH = DIL_HEADS * HEAD_DIM
QK_NOPE = 128
QK_ROPE = 64
LANES = 128
NEG = -1e30
INT_MIN = -(2 ** 31)
VMEM_LIMIT = 56 * 1024 * 1024

NT_DIMS = (((1,), (1,)), ((), ()))


def _cparams(n_axes):
    return pltpu.CompilerParams(
        dimension_semantics=("arbitrary",) * n_axes, vmem_limit_bytes=VMEM_LIMIT)


def _rope128(x, cos_f, sin_s):
    return x * cos_f + pltpu.roll(x, 64, 1) * sin_s


def _rope64(x, cos_f, sin_s):
    lane = lax.broadcasted_iota(I32, x.shape, 1)
    partner = jnp.where((lane & 32) == 0, pltpu.roll(x, 96, 1), pltpu.roll(x, 32, 1))
    return x * cos_f + partner * sin_s


def _rope_table_kernel(pos_ref, inv_ref, sgn_ref, cos_ref, sin_ref):
    ang = pos_ref[...] * inv_ref[...]
    cos_ref[...] = jnp.cos(ang)
    sin_ref[...] = jnp.sin(ang) * sgn_ref[...]


def _rope_tables(pos_b, dh):
    m = pos_b.shape[0]
    half = dh // 2
    inv = jnp.power(ROPE_THETA, -jnp.arange(half, dtype=F32) * 2.0 / dh)
    reps = LANES // half
    inv_l = jnp.tile(inv, reps).reshape(1, LANES)
    sgn = jnp.tile(jnp.concatenate([-jnp.ones((half,), F32), jnp.ones((half,), F32)]),
                   reps // 2).reshape(1, LANES)
    ts = min(512, m)
    return pl.pallas_call(
        _rope_table_kernel,
        grid=(m // ts,),
        in_specs=[pl.BlockSpec((ts, LANES), lambda i: (i, 0)),
                  pl.BlockSpec((1, LANES), lambda i: (0, 0)),
                  pl.BlockSpec((1, LANES), lambda i: (0, 0))],
        out_specs=[pl.BlockSpec((ts, LANES), lambda i: (i, 0))] * 2,
        out_shape=[jax.ShapeDtypeStruct((m, LANES), F32)] * 2,
        compiler_params=_cparams(1),
        name="rope_tables",
    )(pos_b, inv_l, sgn)


def _norm_proj_kernel(x_ref, g_ref, w_ref, o_ref, h_ref):
    @pl.when(pl.program_id(1) == 0)
    def _():
        x = x_ref[...].astype(F32)
        ms = jnp.mean(x * x, axis=-1, keepdims=True)
        h_ref[...] = (x * lax.rsqrt(ms + EPS) * g_ref[...]).astype(BF16)

    o_ref[...] = jnp.dot(h_ref[...], w_ref[...], preferred_element_type=F32).astype(o_ref.dtype)


def _norm_proj(x, g, w, *, x_col_block=0, tn=512):
    m = x.shape[0]
    k, n = w.shape
    tm = min(1024, m)
    assert m % tm == 0 and n % tn == 0
    return pl.pallas_call(
        _norm_proj_kernel,
        grid=(m // tm, n // tn),
        in_specs=[pl.BlockSpec((tm, k), lambda i, j: (i, x_col_block)),
                  pl.BlockSpec((1, k), lambda i, j: (0, 0)),
                  pl.BlockSpec((k, tn), lambda i, j: (0, j))],
        out_specs=pl.BlockSpec((tm, tn), lambda i, j: (i, j)),
        out_shape=jax.ShapeDtypeStruct((m, n), BF16),
        scratch_shapes=[pltpu.VMEM((tm, k), BF16)],
        compiler_params=_cparams(2),
        name="norm_proj",
    )(x, g.reshape(1, k).astype(F32), w)


def _out_proj_kernel(y_ref, zy_ref, zm_ref, qm_ref, mkv_ref, x_ref, w_ref, *rest, wy, final):
    if final:
        gf_ref, o_ref, gated_ref = rest
    else:
        o_ref, gated_ref = rest
    zy = zy_ref[...].astype(F32)
    gated_ref[:, :wy] = (y_ref[...].astype(F32) * (zy * jax.nn.sigmoid(zy))).astype(BF16)
    scale = HEAD_DIM ** -0.5
    for h in range(MEM_HEADS):
        lo, hi = h * HEAD_DIM, (h + 1) * HEAD_DIM
        s = lax.dot_general(qm_ref[:, lo:hi], mkv_ref[:, lo:hi], NT_DIMS,
                            preferred_element_type=F32) * scale
        m = jnp.max(s, axis=-1, keepdims=True)
        p = jnp.exp(s - m)
        l = jnp.sum(p, axis=-1, keepdims=True)
        o = jnp.dot(p.astype(BF16), mkv_ref[:, MEM_WIDTH + lo:MEM_WIDTH + hi],
                    preferred_element_type=F32) / l
        zm = zm_ref[:, lo:hi].astype(F32)
        gated_ref[:, wy + lo:wy + hi] = (o * (zm * jax.nn.sigmoid(zm))).astype(BF16)
    out = x_ref[...] + jnp.dot(gated_ref[...], w_ref[...], preferred_element_type=F32)
    if final:
        ms = jnp.mean(out * out, axis=-1, keepdims=True)
        out = out * lax.rsqrt(ms + EPS) * gf_ref[...]
    o_ref[...] = out


def _out_proj(y, u, mkv, x, w_out, *, wy, zy_blk, zm_blk, qm_blk, seq, n_mem, final_g=None):
    m, d = x.shape
    tm = min(256, seq)
    final = final_g is not None
    in_specs = [
        pl.BlockSpec((tm, wy), lambda i: (i, 0)),
        pl.BlockSpec((tm, wy), lambda i: (i, zy_blk)),
        pl.BlockSpec((tm, MEM_WIDTH), lambda i: (i, zm_blk)),
        pl.BlockSpec((tm, MEM_WIDTH), lambda i: (i, qm_blk)),
        pl.BlockSpec((n_mem, 2 * MEM_WIDTH), lambda i: ((i * tm) // seq, 0)),
        pl.BlockSpec((tm, d), lambda i: (i, 0)),
        pl.BlockSpec((wy + MEM_WIDTH, d), lambda i: (0, 0)),
    ]
    args = [y, u, u, u, mkv, x, w_out]
    if final:
        in_specs.append(pl.BlockSpec((1, d), lambda i: (0, 0)))
        args.append(final_g.reshape(1, d).astype(F32))
    return pl.pallas_call(
        functools.partial(_out_proj_kernel, wy=wy, final=final),
        grid=(m // tm,),
        in_specs=in_specs,
        out_specs=pl.BlockSpec((tm, d), lambda i: (i, 0)),
        out_shape=jax.ShapeDtypeStruct((m, d), F32),
        scratch_shapes=[pltpu.VMEM((tm, wy + MEM_WIDTH), BF16)],
        compiler_params=_cparams(1),
        name="out_proj",
    )(*args)


def _fox_gate_kernel(f_ref, b_ref, c_ref, carry_ref):
    @pl.when(pl.program_id(1) == 0)
    def _():
        carry_ref[...] = jnp.zeros_like(carry_ref)

    f_t = f_ref[...].astype(F32).T[:N_HEADS, :] + b_ref[...]
    log_f = jnp.minimum(f_t, 0.0) - jnp.log(1.0 + jnp.exp(-jnp.abs(f_t)))
    ts = log_f.shape[1]
    r = lax.broadcasted_iota(I32, (ts, ts), 0)
    c = lax.broadcasted_iota(I32, (ts, ts), 1)
    upper = jnp.where(r <= c, 1.0, 0.0).astype(F32)
    cs = jnp.dot(log_f, upper, precision=lax.Precision.HIGHEST,
                 preferred_element_type=F32) + carry_ref[...]
    c_ref[...] = cs
    carry_ref[...] = cs[:, ts - 1:ts]


def _fox_gate(u3, bias, f_blk):
    b, s, _ = u3.shape
    ts = min(256, s)
    return pl.pallas_call(
        _fox_gate_kernel,
        grid=(b, s // ts),
        in_specs=[pl.BlockSpec((None, ts, LANES), lambda bi, j: (bi, j, f_blk)),
                  pl.BlockSpec((N_HEADS, 1), lambda bi, j: (0, 0))],
        out_specs=pl.BlockSpec((None, N_HEADS, ts), lambda bi, j: (bi, 0, j)),
        out_shape=jax.ShapeDtypeStruct((b, N_HEADS, s), F32),
        scratch_shapes=[pltpu.VMEM((N_HEADS, 1), F32)],
        compiler_params=_cparams(2),
        name="fox_gate",
    )(u3, bias.reshape(N_HEADS, 1).astype(F32))


def _online_softmax_step(s, v, carry):
    m, l, acc = carry
    m_new = jnp.maximum(m, jnp.max(s, axis=-1, keepdims=True))
    alpha = jnp.exp(m - m_new)
    p = jnp.exp(s - m_new)
    l = alpha * l + jnp.sum(p, axis=-1, keepdims=True)
    acc = alpha * acc + jnp.dot(p.astype(BF16), v, preferred_element_type=F32)
    return m_new, l, acc


def _causal_mask(s, i, j, tq, tk):
    row = lax.broadcasted_iota(I32, s.shape, 0) + i * tq
    col = lax.broadcasted_iota(I32, s.shape, 1) + j * tk
    return jnp.where(col <= row, s, NEG)


def _fox_attn_kernel(q_ref, k_ref, v_ref, c_ref, o_ref, *, tq):
    i = pl.program_id(2)
    scale = HEAD_DIM ** -0.5
    q = q_ref[...]

    def scores(j):
        off = pl.multiple_of(j * tq, tq)
        k = k_ref[pl.ds(off, tq), :]
        s = lax.dot_general(q, k, NT_DIMS, preferred_element_type=F32) * scale - c_ref[j]
        return s, v_ref[pl.ds(off, tq), :]

    def body(j, carry):
        s, v = scores(j)
        return _online_softmax_step(s, v, carry)

    init = (jnp.full((tq, 1), NEG, F32), jnp.zeros((tq, 1), F32), jnp.zeros((tq, HEAD_DIM), F32))
    carry = lax.fori_loop(0, i, body, init)
    s, v = scores(i)
    _, l, acc = _online_softmax_step(_causal_mask(s, i, i, tq, tq), v, carry)
    o_ref[...] = (acc / l).astype(o_ref.dtype)


def _fox_attn(u3, c_t, *, q_blk, k_blk, v_blk):
    b, s, _ = u3.shape
    tq = min(512, s)
    nq = s // tq
    c5 = c_t.reshape(b, N_HEADS, nq, 1, tq)
    return pl.pallas_call(
        functools.partial(_fox_attn_kernel, tq=tq),
        grid=(b, N_HEADS, nq),
        in_specs=[pl.BlockSpec((None, tq, HEAD_DIM), lambda bi, h, i: (bi, i, q_blk + h)),
                  pl.BlockSpec((None, s, HEAD_DIM), lambda bi, h, i: (bi, 0, k_blk + h)),
                  pl.BlockSpec((None, s, HEAD_DIM), lambda bi, h, i: (bi, 0, v_blk + h)),
                  pl.BlockSpec((None, None, nq, 1, tq), lambda bi, h, i: (bi, h, 0, 0, 0))],
        out_specs=pl.BlockSpec((None, tq, HEAD_DIM), lambda bi, h, i: (bi, i, h)),
        out_shape=jax.ShapeDtypeStruct((b, s, N_HEADS * HEAD_DIM), BF16),
        compiler_params=_cparams(3),
        name="fox_attn",
    )(u3, u3, u3, c5)


def _dsa_kernel(q_ref, qi_ref, mq_ref, k_ref, v_ref, mk_ref,
                cq128_ref, sq128_ref, cq64_ref, sq64_ref,
                ck128_ref, sk128_ref, ck64_ref, sk64_ref,
                o_ref,
                qr_ref, qir_ref, kr_ref, kidx_ref, sc_ref, key_ref, eq_ref, yo_ref,
                *, tq, kl, q_off, n_sel):
    t0 = q_off + pl.program_id(1) * tq
    scale = HEAD_DIM ** -0.5
    idx_scale = (IDX_DIM ** -0.5) * (IDX_HEADS ** -0.5)

    kr_ref[...] = _rope128(k_ref[...].astype(F32), ck128_ref[...], sk128_ref[...]).astype(BF16)
    kidx_ref[...] = _rope64(mk_ref[...].astype(F32), ck64_ref[...],
                            sk64_ref[...])[:, :IDX_DIM].astype(BF16)
    cq128, sq128 = cq128_ref[...], sq128_ref[...]
    for h in range(N_HEADS):
        qh = q_ref[:, h * HEAD_DIM:(h + 1) * HEAD_DIM].astype(F32)
        qr_ref[h] = _rope128(qh, cq128, sq128).astype(BF16)
    cq64, sq64 = cq64_ref[...], sq64_ref[...]
    for a in range(IDX_HEADS // 2):
        pair = _rope64(qi_ref[:, a * LANES:(a + 1) * LANES].astype(F32), cq64, sq64).astype(BF16)
        qir_ref[2 * a] = pair[:, :IDX_DIM]
        qir_ref[2 * a + 1] = pair[:, IDX_DIM:]

    w = mq_ref[...].astype(F32) * idx_scale
    lane = lax.broadcasted_iota(I32, (tq, LANES), 1)
    sc_ref[...] = jnp.zeros_like(sc_ref)

    def idx_body(h, _):
        rel = jnp.maximum(lax.dot_general(qir_ref[h], kidx_ref[...], NT_DIMS,
                                          preferred_element_type=F32), 0.0)
        w_col = jnp.sum(jnp.where(lane == IDX_DIM + h, w, 0.0), axis=-1, keepdims=True)
        sc_ref[...] += w_col * rel
        return 0

    lax.fori_loop(0, IDX_HEADS, idx_body, 0)

    row = lax.broadcasted_iota(I32, (tq, kl), 0) + t0
    col = lax.broadcasted_iota(I32, (tq, kl), 1)
    causal = col <= row
    score = jnp.where(causal, sc_ref[...] + 0.0, -jnp.inf)
    bits = pltpu.bitcast(score, I32)
    key_ref[...] = jnp.where(bits < 0, bits ^ jnp.int32(0x7FFFFFFF), bits)

    def count(x):
        return jnp.sum(x, axis=-1, keepdims=True)

    def thr_body(it, thr):
        cand = thr + (jnp.int32(1) << (31 - it))
        c = count(jnp.where(key_ref[...] >= cand, 1.0, 0.0))
        return jnp.where(c >= n_sel, cand, thr)

    thr = lax.fori_loop(0, 32, thr_body, jnp.full((tq, 1), INT_MIN, I32))

    key = key_ref[...]
    gt = key > thr
    eq_ref[...] = jnp.where(key == thr, 1.0, 0.0)
    need = n_sel - count(jnp.where(gt, 1.0, 0.0))

    n_bits = kl.bit_length()

    def tie_body(it, jm):
        cand = jm + (jnp.int32(1) << (n_bits - 1 - it))
        f = count(jnp.where(col < cand, eq_ref[...], 0.0))
        return jnp.where(jnp.logical_and(f < need, cand <= kl), cand, jm)

    jm = lax.fori_loop(0, n_bits, tie_body, jnp.zeros((tq, 1), I32))
    sel = jnp.where(gt, 1.0, jnp.where(col <= jm, eq_ref[...], 0.0))
    sc_ref[...] = jnp.where(causal, jnp.where(sel > 0.0, 0.0, NEG), NEG)

    def attn_body(h, _):
        s = lax.dot_general(qr_ref[h], kr_ref[...], NT_DIMS,
                            preferred_element_type=F32) * scale + sc_ref[...]
        m = jnp.max(s, axis=-1, keepdims=True)
        p = jnp.exp(s - m)
        l = count(p)
        yo_ref[h] = jnp.dot(p.astype(BF16), v_ref[...], preferred_element_type=F32) / l
        return 0

    lax.fori_loop(0, N_HEADS, attn_body, 0)
    for h in range(N_HEADS):
        o_ref[:, h * HEAD_DIM:(h + 1) * HEAD_DIM] = yo_ref[h].astype(o_ref.dtype)


def _dsa_group(u3, tabs, *, q_off, rows, kl, n_sel, q_blk, k_blk, v_blk, qi_blk, misc_blk):
    b, s, _ = u3.shape
    tq = min(128, rows)
    qb0 = q_off // tq
    c128, s128, c64, s64 = tabs
    qw = N_HEADS * HEAD_DIM
    iw = IDX_HEADS * IDX_DIM

    def qspec(width, blk):
        return pl.BlockSpec((None, tq, width), lambda bi, i: (bi, qb0 + i, blk))

    def kspec(blk):
        return pl.BlockSpec((None, kl, LANES), lambda bi, i: (bi, 0, blk))

    return pl.pallas_call(
        functools.partial(_dsa_kernel, tq=tq, kl=kl, q_off=q_off, n_sel=n_sel),
        grid=(b, rows // tq),
        in_specs=[qspec(qw, q_blk), qspec(iw, qi_blk),
                  qspec(LANES, misc_blk),
                  kspec(k_blk), kspec(v_blk), kspec(misc_blk),
                  qspec(LANES, 0), qspec(LANES, 0), qspec(LANES, 0), qspec(LANES, 0),
                  kspec(0), kspec(0), kspec(0), kspec(0)],
        out_specs=pl.BlockSpec((None, tq, qw), lambda bi, i: (bi, i, 0)),
        out_shape=jax.ShapeDtypeStruct((b, rows, qw), BF16),
        scratch_shapes=[pltpu.VMEM((N_HEADS, tq, HEAD_DIM), BF16),
                        pltpu.VMEM((IDX_HEADS, tq, IDX_DIM), BF16),
                        pltpu.VMEM((kl, HEAD_DIM), BF16),
                        pltpu.VMEM((kl, IDX_DIM), BF16),
                        pltpu.VMEM((tq, kl), F32),
                        pltpu.VMEM((tq, kl), I32),
                        pltpu.VMEM((tq, kl), F32),
                        pltpu.VMEM((N_HEADS, tq, HEAD_DIM), F32)],
        compiler_params=_cparams(2),
        name="dsa_attn",
    )(u3, u3, u3, u3, u3, u3, c128, s128, c64, s64, c128, s128, c64, s64)


def _dsa_attn(u3, tabs, **blks):
    b, s, _ = u3.shape
    n_sel = min(TOPK_MAX, s // 4)
    rows = min(512, s)
    outs = [_dsa_group(u3, tabs, q_off=q_off, rows=rows, kl=q_off + rows, n_sel=n_sel, **blks)
            for q_off in range(0, s, rows)]
    return outs[0] if len(outs) == 1 else jnp.concatenate(outs, axis=1)


def _dil_kernel(*refs, qb, has_prev):
    if has_prev:
        (q_ref, kc_ref, kp_ref, vc_ref, vp_ref, cq_ref, sq_ref, cp_ref, sp_ref,
         o_ref, lse_ref) = refs
    else:
        q_ref, kc_ref, vc_ref, cq_ref, sq_ref, o_ref, lse_ref = refs
    i = pl.program_id(2)
    scale = HEAD_DIM ** -0.5
    cq, sq = cq_ref[...], sq_ref[...]
    row = lax.broadcasted_iota(I32, (qb, qb), 0)
    col = lax.broadcasted_iota(I32, (qb, qb), 1)
    mask_c = col <= row
    if has_prev:
        cp, sp = cp_ref[...], sp_ref[...]
        mask_p = jnp.logical_and(col >= row, i > 0)
    for a in range(DIL_HEADS):
        lo, hi = a * HEAD_DIM, (a + 1) * HEAD_DIM
        q = _rope128(q_ref[:, lo:hi].astype(F32), cq, sq).astype(BF16)
        kc = _rope128(kc_ref[:, lo:hi].astype(F32), cq, sq).astype(BF16)
        s_c = lax.dot_general(q, kc, NT_DIMS, preferred_element_type=F32) * scale
        s_c = jnp.where(mask_c, s_c, NEG)
        m = jnp.max(s_c, axis=-1, keepdims=True)
        if has_prev:
            kp = _rope128(kp_ref[:, lo:hi].astype(F32), cp, sp).astype(BF16)
            s_p = lax.dot_general(q, kp, NT_DIMS, preferred_element_type=F32) * scale
            s_p = jnp.where(mask_p, s_p, NEG)
            m = jnp.maximum(m, jnp.max(s_p, axis=-1, keepdims=True))
        p_c = jnp.exp(s_c - m)
        l = jnp.sum(p_c, axis=-1, keepdims=True)
        acc = jnp.dot(p_c.astype(BF16), vc_ref[:, lo:hi], preferred_element_type=F32)
        if has_prev:
            p_p = jnp.exp(s_p - m)
            l = l + jnp.sum(p_p, axis=-1, keepdims=True)
            acc = acc + jnp.dot(p_p.astype(BF16), vp_ref[:, lo:hi], preferred_element_type=F32)
        o_ref[:, lo:hi] = acc / l
        lse_ref[:, lo:hi] = jnp.broadcast_to(m + jnp.log(l), (qb, HEAD_DIM))


def _dil_group(u3, tabs128, *, g, dil, q_blk, k_blk, v_blk):
    b, s, n = u3.shape
    sub = s // dil
    qb = min(128, sub)
    nb = sub // qb
    has_prev = nb > 1
    assert has_prev or sub <= 128
    uv = u3.reshape(b, sub, dil * n)
    cos_t, sin_t = (t.reshape(b, sub, dil * LANES) for t in tabs128)
    per_res = n // DIL_WIDTH
    assert n % DIL_WIDTH == 0

    def cur(blk0):
        return pl.BlockSpec((None, qb, DIL_WIDTH), lambda bi, r, i: (bi, i, r * per_res + blk0 + g))

    def prev(blk0):
        return pl.BlockSpec((None, qb, DIL_WIDTH),
                            lambda bi, r, i: (bi, jnp.maximum(i - 1, 0), r * per_res + blk0 + g))

    tab_cur = pl.BlockSpec((None, qb, LANES), lambda bi, r, i: (bi, i, r))
    tab_prev = pl.BlockSpec((None, qb, LANES), lambda bi, r, i: (bi, jnp.maximum(i - 1, 0), r))
    if has_prev:
        in_specs = [cur(q_blk), cur(k_blk), prev(k_blk), cur(v_blk), prev(v_blk),
                    tab_cur, tab_cur, tab_prev, tab_prev]
        args = [uv, uv, uv, uv, uv, cos_t, sin_t, cos_t, sin_t]
    else:
        in_specs = [cur(q_blk), cur(k_blk), cur(v_blk), tab_cur, tab_cur]
        args = [uv, uv, uv, cos_t, sin_t]
    out_spec = pl.BlockSpec((None, qb, DIL_WIDTH), lambda bi, r, i: (bi, i, r))
    o, lse = pl.pallas_call(
        functools.partial(_dil_kernel, qb=qb, has_prev=has_prev),
        grid=(b, dil, nb),
        in_specs=in_specs,
        out_specs=[out_spec, out_spec],
        out_shape=[jax.ShapeDtypeStruct((b, sub, dil * DIL_WIDTH), F32)] * 2,
        compiler_params=_cparams(3),
        name=f"dilated_g{g}",
    )(*args)
    return o.reshape(b * s, DIL_WIDTH), lse.reshape(b * s, DIL_WIDTH)


def _dil_combine_kernel(o0, o1, o2, l0, l1, l2, y_ref):
    a0, a1, a2 = l0[...], l1[...], l2[...]
    m = jnp.maximum(jnp.maximum(a0, a1), a2)
    e0, e1, e2 = jnp.exp(a0 - m), jnp.exp(a1 - m), jnp.exp(a2 - m)
    y = (e0 * o0[...] + e1 * o1[...] + e2 * o2[...]) / (e0 + e1 + e2)
    y_ref[...] = y.astype(y_ref.dtype)


def _dil_combine(outs, lses):
    m = outs[0].shape[0]
    tm = min(512, m)
    spec = pl.BlockSpec((tm, DIL_WIDTH), lambda i: (i, 0))
    return pl.pallas_call(
        _dil_combine_kernel,
        grid=(m // tm,),
        in_specs=[spec] * 6,
        out_specs=spec,
        out_shape=jax.ShapeDtypeStruct((m, DIL_WIDTH), BF16),
        compiler_params=_cparams(1),
        name="dilated_combine",
    )(*outs, *lses)


def _mla_kernel(qn_ref, qr_ref, kn_ref, kr_ref, v_ref, cq_ref, sq_ref, ck_ref, sk_ref,
                o_ref, krr_ref, *, tq):
    i = pl.program_id(2)
    scale = (QK_NOPE + QK_ROPE) ** -0.5
    krr_ref[...] = _rope64(kr_ref[...].astype(F32), ck_ref[...],
                           sk_ref[...])[:, :QK_ROPE].astype(BF16)
    q_rope = _rope64(qr_ref[...].astype(F32), cq_ref[...], sq_ref[...]).astype(BF16)
    for a in range(2):
        lo, hi = a * HEAD_DIM, (a + 1) * HEAD_DIM
        qn = qn_ref[:, lo:hi]
        qr = q_rope[:, a * QK_ROPE:(a + 1) * QK_ROPE]

        def scores(j, qn=qn, qr=qr, lo=lo, hi=hi):
            off = pl.multiple_of(j * tq, tq)
            s = lax.dot_general(qn, kn_ref[pl.ds(off, tq), lo:hi], NT_DIMS,
                                preferred_element_type=F32)
            s = s + lax.dot_general(qr, krr_ref[pl.ds(off, tq), :], NT_DIMS,
                                    preferred_element_type=F32)
            return s * scale, v_ref[pl.ds(off, tq), lo:hi]

        def body(j, carry, scores=scores):
            s, v = scores(j)
            return _online_softmax_step(s, v, carry)

        init = (jnp.full((tq, 1), NEG, F32), jnp.zeros((tq, 1), F32),
                jnp.zeros((tq, HEAD_DIM), F32))
        carry = lax.fori_loop(0, i, body, init)
        s, v = scores(i)
        _, l, acc = _online_softmax_step(_causal_mask(s, i, i, tq, tq), v, carry)
        o_ref[:, lo:hi] = (acc / l).astype(o_ref.dtype)


def _mla_attn(qf3, kvf3, u3, tabs64, *, kr_blk):
    b, s, _ = qf3.shape
    tq = min(512, s)
    pw = 2 * HEAD_DIM
    n_pairs = N_HEADS // 2
    cos_t, sin_t = tabs64
    return pl.pallas_call(
        functools.partial(_mla_kernel, tq=tq),
        grid=(b, n_pairs, s // tq),
        in_specs=[pl.BlockSpec((None, tq, pw), lambda bi, p, i: (bi, i, p)),
                  pl.BlockSpec((None, tq, LANES), lambda bi, p, i: (bi, i, N_HEADS + p)),
                  pl.BlockSpec((None, s, pw), lambda bi, p, i: (bi, 0, p)),
                  pl.BlockSpec((None, s, LANES), lambda bi, p, i: (bi, 0, kr_blk)),
                  pl.BlockSpec((None, s, pw), lambda bi, p, i: (bi, 0, n_pairs + p)),
                  pl.BlockSpec((None, tq, LANES), lambda bi, p, i: (bi, i, 0)),
                  pl.BlockSpec((None, tq, LANES), lambda bi, p, i: (bi, i, 0)),
                  pl.BlockSpec((None, s, LANES), lambda bi, p, i: (bi, 0, 0)),
                  pl.BlockSpec((None, s, LANES), lambda bi, p, i: (bi, 0, 0))],
        out_specs=pl.BlockSpec((None, tq, pw), lambda bi, p, i: (bi, i, p)),
        out_shape=jax.ShapeDtypeStruct((b, s, N_HEADS * HEAD_DIM), BF16),
        scratch_shapes=[pltpu.VMEM((s, QK_ROPE), BF16)],
        compiler_params=_cparams(3),
        name="mla_attn",
    )(qf3, qf3, kvf3, u3, kvf3, cos_t, sin_t, cos_t, sin_t)


def _pack(parts, multiple=512):
    k = next(p.shape[0] for p, _ in parts if p is not None)
    cols = [jnp.zeros((k, w), BF16) if p is None else p.astype(BF16) for p, w in parts]
    n = sum(w for _, w in parts)
    pad = (-n) % multiple
    if pad:
        cols.append(jnp.zeros((k, pad), BF16))
    return jnp.concatenate(cols, axis=1)


def _cols(w, sizes):
    out, acc = [], 0
    for sz in sizes:
        out.append(w[:, acc:acc + sz])
        acc += sz
    return out


def kernel(x, mem, positions, l0_norm, l0_w_in, l0_forget_bias, l0_mem_norm, l0_w_mem_kv, l0_w_out, l1_norm, l1_w_in, l1_mem_norm, l1_w_mem_kv, l1_w_out, l2_norm, l2_w_in, l2_mem_norm, l2_w_mem_kv, l2_w_out, l3_norm, l3_w_in, l3_q_norm, l3_w_uq, l3_kv_norm, l3_w_ukv, l3_mem_norm, l3_w_mem_kv, l3_w_out, final_norm):
    b, s, d = x.shape
    n_mem = mem.shape[1]
    m = b * s
    mw = N_HEADS * HEAD_DIM
    x2 = x.reshape(m, d)
    mem2 = mem.reshape(b * n_mem, d)

    pos_b = jnp.broadcast_to(positions.astype(F32).reshape(m, 1), (m, LANES))
    tabs128 = tuple(t.reshape(b, s, LANES) for t in _rope_tables(pos_b, HEAD_DIM))
    tabs64 = tuple(t.reshape(b, s, LANES) for t in _rope_tables(pos_b, IDX_DIM))

    def mem_kv(g, w):
        return _norm_proj(mem2, g, w.astype(BF16))

    def finish(y, u, x_in, g_mem, w_mem_kv, w_out, *, wy, zy_blk, zm_blk, qm_blk, final_g=None):
        return _out_proj(y, u, mem_kv(g_mem, w_mem_kv), x_in, w_out.astype(BF16), wy=wy,
                         zy_blk=zy_blk, zm_blk=zm_blk, qm_blk=qm_blk, seq=s, n_mem=n_mem,
                         final_g=final_g)

    q_w, k_w, v_w, f_w, qm_w, z_w = _cols(l0_w_in, (mw, mw, mw, N_HEADS, MEM_WIDTH, mw + MEM_WIDTH))
    w0 = _pack([(z_w[:, :mw], mw), (z_w[:, mw:], MEM_WIDTH), (qm_w, MEM_WIDTH), (q_w, mw),
                (k_w, mw), (v_w, mw), (f_w, N_HEADS), (None, LANES - N_HEADS)])
    u = _norm_proj(x2, l0_norm, w0)
    u3 = u.reshape(b, s, -1)
    base = (mw + 2 * MEM_WIDTH) // LANES
    c_t = _fox_gate(u3, l0_forget_bias, base + 3 * N_HEADS)
    y = _fox_attn(u3, c_t, q_blk=base, k_blk=base + N_HEADS, v_blk=base + 2 * N_HEADS)
    x2 = finish(y.reshape(m, mw), u, x2, l0_mem_norm, l0_w_mem_kv, l0_w_out, wy=mw, zy_blk=0,
                zm_blk=mw // MEM_WIDTH, qm_blk=mw // MEM_WIDTH + 1)

    iw = IDX_HEADS * IDX_DIM
    q_w, k_w, v_w, qi_w, ki_w, wi_w, qm_w, z_w = _cols(
        l1_w_in, (mw, HEAD_DIM, HEAD_DIM, iw, IDX_DIM, IDX_HEADS, MEM_WIDTH, mw + MEM_WIDTH))
    w1 = _pack([(z_w[:, :mw], mw), (q_w, mw), (qi_w, iw), (z_w[:, mw:], MEM_WIDTH),
                (qm_w, MEM_WIDTH), (k_w, HEAD_DIM), (v_w, HEAD_DIM), (ki_w, IDX_DIM),
                (wi_w, IDX_HEADS), (None, LANES - IDX_DIM - IDX_HEADS)])
    u = _norm_proj(x2, l1_norm, w1)
    u3 = u.reshape(b, s, -1)
    kv0 = (2 * mw + iw + 2 * MEM_WIDTH) // LANES
    y = _dsa_attn(u3, tabs128 + tabs64, q_blk=1, qi_blk=2 * mw // iw, k_blk=kv0, v_blk=kv0 + 1,
                  misc_blk=kv0 + 2)
    zoff = (2 * mw + iw) // MEM_WIDTH
    x2 = finish(y.reshape(m, mw), u, x2, l1_mem_norm, l1_w_mem_kv, l1_w_out, wy=mw, zy_blk=0,
                zm_blk=zoff, qm_blk=zoff + 1)

    gw = len(DILATED_PAIRS) * DIL_WIDTH
    q_w, k_w, v_w, qm_w, z_w = _cols(l2_w_in, (gw, gw, gw, MEM_WIDTH, DIL_WIDTH + MEM_WIDTH))
    w2 = _pack([(q_w, gw), (z_w[:, :DIL_WIDTH], DIL_WIDTH), (k_w, gw), (v_w, gw),
                (z_w[:, DIL_WIDTH:], MEM_WIDTH), (qm_w, MEM_WIDTH)], multiple=2 * DIL_WIDTH)
    u = _norm_proj(x2, l2_norm, w2)
    u3 = u.reshape(b, s, -1)
    ng = len(DILATED_PAIRS)
    outs, lses = [], []
    for g, (window, dil) in enumerate(DILATED_PAIRS):
        assert window // dil == 128
        o, lse = _dil_group(u3, tabs128, g=g, dil=dil, q_blk=0, k_blk=ng + 1, v_blk=2 * ng + 1)
        outs.append(o)
        lses.append(lse)
    y = _dil_combine(outs, lses)
    zoff = (3 * gw + DIL_WIDTH) // MEM_WIDTH
    x2 = finish(y, u, x2, l2_mem_norm, l2_w_mem_kv, l2_w_out, wy=DIL_WIDTH, zy_blk=ng,
                zm_blk=zoff, qm_blk=zoff + 1)

    q_lora, kv_lora = l3_w_uq.shape[0], l3_w_ukv.shape[0]
    cq_w, ckv_w, kr_w, qm_w, z_w = _cols(l3_w_in, (q_lora, kv_lora, QK_ROPE, MEM_WIDTH,
                                                  mw + MEM_WIDTH))
    assert q_lora == MEM_WIDTH and kv_lora == MEM_WIDTH
    w3 = _pack([(z_w[:, :mw], mw), (z_w[:, mw:], MEM_WIDTH), (qm_w, MEM_WIDTH), (cq_w, q_lora),
                (ckv_w, kv_lora), (kr_w, QK_ROPE), (None, LANES - QK_ROPE)])
    u = _norm_proj(x2, l3_norm, w3)
    u3 = u.reshape(b, s, -1)
    cq_blk = (mw + 2 * MEM_WIDTH) // MEM_WIDTH
    uq = l3_w_uq.reshape(q_lora, N_HEADS, QK_NOPE + QK_ROPE)
    w_uq = _pack([(uq[:, :, :QK_NOPE].reshape(q_lora, -1), N_HEADS * QK_NOPE),
                  (uq[:, :, QK_NOPE:].reshape(q_lora, -1), N_HEADS * QK_ROPE)])
    ukv = l3_w_ukv.reshape(kv_lora, N_HEADS, QK_NOPE + HEAD_DIM)
    w_ukv = _pack([(ukv[:, :, :QK_NOPE].reshape(kv_lora, -1), N_HEADS * QK_NOPE),
                   (ukv[:, :, QK_NOPE:].reshape(kv_lora, -1), N_HEADS * HEAD_DIM)])
    qf = _norm_proj(u, l3_q_norm, w_uq, x_col_block=cq_blk)
    kvf = _norm_proj(u, l3_kv_norm, w_ukv, x_col_block=cq_blk + 1)
    y = _mla_attn(qf.reshape(b, s, -1), kvf.reshape(b, s, -1), u3, tabs64,
                  kr_blk=(mw + 2 * MEM_WIDTH + q_lora + kv_lora) // LANES)
    x2 = finish(y.reshape(m, mw), u, x2, l3_mem_norm, l3_w_mem_kv, l3_w_out, wy=mw, zy_blk=0,
                zm_blk=mw // MEM_WIDTH, qm_blk=mw // MEM_WIDTH + 1, final_g=final_norm)
    return x2.reshape(b, s, d)
```

```python
import functools

import jax
import jax.numpy as jnp
from jax import lax
from jax.experimental import pallas as pl
from jax.experimental.pallas import tpu as pltpu

F32 = jnp.float32
BF16 = jnp.bfloat16
I32 = jnp.int32

EPS = 1e-6
ROPE_THETA = 10000.0
HEAD_DIM = 128
N_HEADS = 16
MEM_HEADS = 4
MEM_WIDTH = MEM_HEADS * HEAD_DIM
IDX_HEADS = 16
IDX_DIM = 64
TOPK_MAX = 256
DILATED_PAIRS = ((128, 1), (512, 4), (2048, 16))
DIL_HEADS = 6
DIL_WIDTH = DIL_HEADS * HEAD_DIM
QK_NOPE = 128
QK_ROPE = 64
LANES = 128
NEG = -1e30
INT_MIN = -(2 ** 31)
VMEM_LIMIT = 56 * 1024 * 1024

NT_DIMS = (((1,), (1,)), ((), ()))
LOG2E = 1.4426950408889634


def _cparams(n_axes):
    return pltpu.CompilerParams(
        dimension_semantics=("arbitrary",) * n_axes, vmem_limit_bytes=VMEM_LIMIT)


def _rope128(x, cos_f, sin_s):
    return x * cos_f + pltpu.roll(x, 64, 1) * sin_s


def _rope64(x, cos_f, sin_s):
    lane = lax.broadcasted_iota(I32, x.shape, 1)
    partner = jnp.where((lane & 32) == 0, pltpu.roll(x, 96, 1), pltpu.roll(x, 32, 1))
    return x * cos_f + partner * sin_s


def _rope_table_kernel(pos_ref, inv_ref, sgn_ref, cos_ref, sin_ref):
    ang = pos_ref[...] * inv_ref[...]
    cos_ref[...] = jnp.cos(ang)
    sin_ref[...] = jnp.sin(ang) * sgn_ref[...]


def _rope_tables(pos_b, dh):
    m = pos_b.shape[0]
    half = dh // 2
    inv = jnp.power(ROPE_THETA, -jnp.arange(half, dtype=F32) * 2.0 / dh)
    reps = LANES // half
    inv_l = jnp.tile(inv, reps).reshape(1, LANES)
    sgn = jnp.tile(jnp.concatenate([-jnp.ones((half,), F32), jnp.ones((half,), F32)]),
                   reps // 2).reshape(1, LANES)
    ts = min(512, m)
    return pl.pallas_call(
        _rope_table_kernel,
        grid=(m // ts,),
        in_specs=[pl.BlockSpec((ts, LANES), lambda i: (i, 0)),
                  pl.BlockSpec((1, LANES), lambda i: (0, 0)),
                  pl.BlockSpec((1, LANES), lambda i: (0, 0))],
        out_specs=[pl.BlockSpec((ts, LANES), lambda i: (i, 0))] * 2,
        out_shape=[jax.ShapeDtypeStruct((m, LANES), F32)] * 2,
        compiler_params=_cparams(1),
        name="rope_tables",
    )(pos_b, inv_l, sgn)


def _norm_proj_kernel(x_ref, g_ref, w_ref, o_ref, h_ref):
    @pl.when(pl.program_id(1) == 0)
    def _():
        x = x_ref[...].astype(F32)
        ms = jnp.mean(x * x, axis=-1, keepdims=True)
        h_ref[...] = (x * lax.rsqrt(ms + EPS) * g_ref[...]).astype(BF16)

    o_ref[...] = jnp.dot(h_ref[...], w_ref[...], preferred_element_type=F32).astype(o_ref.dtype)


def _norm_proj(x, g, w, *, x_col_block=0, tn=512, out_dtype=BF16):
    m = x.shape[0]
    k, n = w.shape
    tm = min(1024, m)
    assert m % tm == 0 and n % tn == 0
    return pl.pallas_call(
        _norm_proj_kernel,
        grid=(m // tm, n // tn),
        in_specs=[pl.BlockSpec((tm, k), lambda i, j: (i, x_col_block)),
                  pl.BlockSpec((1, k), lambda i, j: (0, 0)),
                  pl.BlockSpec((k, tn), lambda i, j: (0, j))],
        out_specs=pl.BlockSpec((tm, tn), lambda i, j: (i, j)),
        out_shape=jax.ShapeDtypeStruct((m, n), out_dtype),
        scratch_shapes=[pltpu.VMEM((tm, k), BF16)],
        compiler_params=_cparams(2),
        name="norm_proj",
    )(x, g.reshape(1, k).astype(F32), w)


def _out_proj_kernel(y_ref, zy_ref, zm_ref, qm_ref, mkv_ref, x_ref, w_ref, *rest, wy, final):
    if final:
        gf_ref, o_ref, gated_ref = rest
    else:
        o_ref, gated_ref = rest
    zy = zy_ref[...].astype(F32)
    gated_ref[:, :wy] = (y_ref[...].astype(F32) * (zy * jax.nn.sigmoid(zy))).astype(BF16)
    scale = HEAD_DIM ** -0.5
    for h in range(MEM_HEADS):
        lo, hi = h * HEAD_DIM, (h + 1) * HEAD_DIM
        s = lax.dot_general(qm_ref[:, lo:hi], mkv_ref[:, lo:hi], NT_DIMS,
                            preferred_element_type=F32) * scale
        m = jnp.max(s, axis=-1, keepdims=True)
        p = jnp.exp(s - m)
        l = jnp.sum(p, axis=-1, keepdims=True)
        o = jnp.dot(p.astype(BF16), mkv_ref[:, MEM_WIDTH + lo:MEM_WIDTH + hi],
                    preferred_element_type=F32) / l
        zm = zm_ref[:, lo:hi].astype(F32)
        gated_ref[:, wy + lo:wy + hi] = (o * (zm * jax.nn.sigmoid(zm))).astype(BF16)
    out = x_ref[...] + jnp.dot(gated_ref[...], w_ref[...], preferred_element_type=F32)
    if final:
        ms = jnp.mean(out * out, axis=-1, keepdims=True)
        out = out * lax.rsqrt(ms + EPS) * gf_ref[...]
    o_ref[...] = out


def _out_proj(y, u, mkv, x, w_out, *, wy, zy_blk, zm_blk, qm_blk, seq, n_mem, final_g=None):
    m, d = x.shape
    tm = min(256, seq)
    final = final_g is not None
    in_specs = [
        pl.BlockSpec((tm, wy), lambda i: (i, 0)),
        pl.BlockSpec((tm, wy), lambda i: (i, zy_blk)),
        pl.BlockSpec((tm, MEM_WIDTH), lambda i: (i, zm_blk)),
        pl.BlockSpec((tm, MEM_WIDTH), lambda i: (i, qm_blk)),
        pl.BlockSpec((n_mem, 2 * MEM_WIDTH), lambda i: ((i * tm) // seq, 0)),
        pl.BlockSpec((tm, d), lambda i: (i, 0)),
        pl.BlockSpec((wy + MEM_WIDTH, d), lambda i: (0, 0)),
    ]
    args = [y, u, u, u, mkv, x, w_out]
    if final:
        in_specs.append(pl.BlockSpec((1, d), lambda i: (0, 0)))
        args.append(final_g.reshape(1, d).astype(F32))
    return pl.pallas_call(
        functools.partial(_out_proj_kernel, wy=wy, final=final),
        grid=(m // tm,),
        in_specs=in_specs,
        out_specs=pl.BlockSpec((tm, d), lambda i: (i, 0)),
        out_shape=jax.ShapeDtypeStruct((m, d), F32),
        scratch_shapes=[pltpu.VMEM((tm, wy + MEM_WIDTH), BF16)],
        compiler_params=_cparams(1),
        name="out_proj",
    )(*args)


def _fox_gate_kernel(f_ref, b_ref, c_ref, carry_ref):
    @pl.when(pl.program_id(1) == 0)
    def _():
        carry_ref[...] = jnp.zeros_like(carry_ref)

    f_t = f_ref[...].astype(F32).T[:N_HEADS, :] + b_ref[...]
    log_f = jnp.minimum(f_t, 0.0) - jnp.log(1.0 + jnp.exp(-jnp.abs(f_t)))
    ts = log_f.shape[1]
    r = lax.broadcasted_iota(I32, (ts, ts), 0)
    c = lax.broadcasted_iota(I32, (ts, ts), 1)
    upper = jnp.where(r <= c, 1.0, 0.0).astype(F32)
    cs = jnp.dot(log_f, upper, precision=lax.Precision.HIGHEST,
                 preferred_element_type=F32) + carry_ref[...]
    c_ref[...] = cs
    carry_ref[...] = cs[:, ts - 1:ts]


def _fox_gate(u3, bias, f_blk):
    b, s, _ = u3.shape
    ts = min(256, s)
    return pl.pallas_call(
        _fox_gate_kernel,
        grid=(b, s // ts),
        in_specs=[pl.BlockSpec((None, ts, LANES), lambda bi, j: (bi, j, f_blk)),
                  pl.BlockSpec((N_HEADS, 1), lambda bi, j: (0, 0))],
        out_specs=pl.BlockSpec((None, N_HEADS, ts), lambda bi, j: (bi, 0, j)),
        out_shape=jax.ShapeDtypeStruct((b, N_HEADS, s), F32),
        scratch_shapes=[pltpu.VMEM((N_HEADS, 1), F32)],
        compiler_params=_cparams(2),
        name="fox_gate",
    )(u3, bias.reshape(N_HEADS, 1).astype(F32))


def _online_softmax_step(t, v, carry):
    m, l, acc = carry
    m_new = jnp.maximum(m, jnp.max(t, axis=-1, keepdims=True))
    alpha = jnp.exp2(m - m_new)
    p = jnp.exp2(t - m_new)
    l = alpha * l + jnp.sum(p, axis=-1, keepdims=True)
    acc = alpha * acc + jnp.dot(p.astype(BF16), v, preferred_element_type=F32)
    return m_new, l, acc


def _causal_mask(s, i, j, tq, tk):
    row = lax.broadcasted_iota(I32, s.shape, 0) + i * tq
    col = lax.broadcasted_iota(I32, s.shape, 1) + j * tk
    return jnp.where(col <= row, s, NEG)


def _causal_flash(i, tq, n_streams, logits, values):
    init = tuple((jnp.full((tq, 1), NEG, F32), jnp.zeros((tq, 1), F32),
                  jnp.zeros((tq, HEAD_DIM), F32)) for _ in range(n_streams))

    def body(j, carry):
        return tuple(_online_softmax_step(logits(a, j), values(a, j), carry[a])
                     for a in range(n_streams))

    carry = lax.fori_loop(0, i, body, init)
    outs = []
    for a in range(n_streams):
        t = _causal_mask(logits(a, i), i, i, tq, tq)
        _, l, acc = _online_softmax_step(t, values(a, i), carry[a])
        outs.append(acc / l)
    return outs


def _fox_attn_kernel(q_ref, k_ref, v_ref, c_ref, o_ref, *, tq):
    i = pl.program_id(2)
    qk_scale = HEAD_DIM ** -0.5 * LOG2E

    def logits(a, j):
        off = pl.multiple_of(j * tq, tq)
        lo, hi = a * HEAD_DIM, (a + 1) * HEAD_DIM
        s = lax.dot_general(q_ref[:, lo:hi], k_ref[pl.ds(off, tq), lo:hi], NT_DIMS,
                            preferred_element_type=F32)
        return s * qk_scale - c_ref[a, j] * LOG2E

    def values(a, j):
        off = pl.multiple_of(j * tq, tq)
        return v_ref[pl.ds(off, tq), a * HEAD_DIM:(a + 1) * HEAD_DIM]

    outs = _causal_flash(i, tq, 2, logits, values)
    for a in range(2):
        o_ref[:, a * HEAD_DIM:(a + 1) * HEAD_DIM] = outs[a].astype(o_ref.dtype)


def _fox_attn(u3, c_t, *, q_blk, k_blk, v_blk):
    b, s, _ = u3.shape
    tq = min(512, s)
    nq = s // tq
    pw = 2 * HEAD_DIM
    c5 = c_t.reshape(b, N_HEADS, nq, 1, tq)
    return pl.pallas_call(
        functools.partial(_fox_attn_kernel, tq=tq),
        grid=(b, N_HEADS // 2, nq),
        in_specs=[pl.BlockSpec((None, tq, pw), lambda bi, p, i: (bi, i, q_blk + p)),
                  pl.BlockSpec((None, s, pw), lambda bi, p, i: (bi, 0, k_blk + p)),
                  pl.BlockSpec((None, s, pw), lambda bi, p, i: (bi, 0, v_blk + p)),
                  pl.BlockSpec((None, 2, nq, 1, tq), lambda bi, p, i: (bi, p, 0, 0, 0))],
        out_specs=pl.BlockSpec((None, tq, pw), lambda bi, p, i: (bi, i, p)),
        out_shape=jax.ShapeDtypeStruct((b, s, N_HEADS * HEAD_DIM), BF16),
        compiler_params=_cparams(3),
        name="fox_attn",
    )(u3, u3, u3, c5)


def _dsa_kernel(q_ref, qi_ref, mq_ref, k_ref, v_ref, mk_ref,
                cq128_ref, sq128_ref, cq64_ref, sq64_ref,
                ck128_ref, sk128_ref, ck64_ref, sk64_ref, y_in_ref,
                o_ref,
                qr_ref, qir_ref, kr_ref, kidx_ref, sc_ref, eq_ref, yo_ref,
                *, tq, kl, q_off, n_sel):
    del y_in_ref
    t0 = q_off + pl.program_id(1) * tq
    qk_scale = HEAD_DIM ** -0.5 * LOG2E
    idx_scale = (IDX_DIM ** -0.5) * (IDX_HEADS ** -0.5)

    kr_ref[...] = _rope128(k_ref[...].astype(F32), ck128_ref[...], sk128_ref[...]).astype(BF16)
    kidx_ref[...] = _rope64(mk_ref[...].astype(F32), ck64_ref[...],
                            sk64_ref[...])[:, :IDX_DIM].astype(BF16)
    cq128, sq128 = cq128_ref[...], sq128_ref[...]
    for h in range(N_HEADS):
        qh = q_ref[:, h * HEAD_DIM:(h + 1) * HEAD_DIM].astype(F32)
        qr_ref[h] = _rope128(qh, cq128, sq128).astype(BF16)
    cq64, sq64 = cq64_ref[...], sq64_ref[...]
    for a in range(IDX_HEADS // 2):
        pair = _rope64(qi_ref[:, a * LANES:(a + 1) * LANES].astype(F32), cq64, sq64).astype(BF16)
        qir_ref[2 * a] = pair[:, :IDX_DIM]
        qir_ref[2 * a + 1] = pair[:, IDX_DIM:]

    w = mq_ref[...].astype(F32) * idx_scale
    lane = lax.broadcasted_iota(I32, (tq, LANES), 1)
    sc_ref[...] = jnp.zeros_like(sc_ref)

    def idx_body(h, _):
        rel = jnp.maximum(lax.dot_general(qir_ref[h], kidx_ref[...], NT_DIMS,
                                          preferred_element_type=F32), 0.0)
        w_col = jnp.sum(jnp.where(lane == IDX_DIM + h, w, 0.0), axis=-1, keepdims=True)
        sc_ref[...] += w_col * rel
        return 0

    lax.fori_loop(0, IDX_HEADS, idx_body, 0)

    row = lax.broadcasted_iota(I32, (tq, kl), 0) + t0
    col = lax.broadcasted_iota(I32, (tq, kl), 1)
    causal = col <= row
    sc_ref[...] = jnp.where(causal, sc_ref[...], -jnp.inf)

    def count(x):
        return jnp.sum(x, axis=-1, keepdims=True)

    def key_to_f32(key):
        return pltpu.bitcast(jnp.where(key < 0, key ^ jnp.int32(0x7FFFFFFF), key), F32)

    def thr_body(it, key):
        cand = key + (jnp.int32(1) << (31 - it))
        c = count(jnp.where(sc_ref[...] >= key_to_f32(cand), 1.0, 0.0))
        return jnp.where(c >= n_sel, cand, key)

    thr_key = lax.fori_loop(0, 32, thr_body, jnp.full((tq, 1), INT_MIN, I32))
    take_all = thr_key == INT_MIN
    thr = key_to_f32(thr_key)
    score = sc_ref[...]
    gt = score > thr
    eq_ref[...] = jnp.where(score == thr, 1.0, 0.0)
    need = n_sel - count(jnp.where(gt, 1.0, 0.0))
    surplus = jnp.where(take_all, 0.0, count(eq_ref[...]) - need)
    has_surplus = jnp.max(surplus) > 0.0

    n_bits = kl.bit_length()

    def tie_body(it, jm):
        cand = jm + (jnp.int32(1) << (n_bits - 1 - it))
        f = count(jnp.where(col < cand, eq_ref[...], 0.0))
        return jnp.where(jnp.logical_and(f < need, cand <= kl), cand, jm)

    jm = lax.fori_loop(0, jnp.where(has_surplus, n_bits, 0), tie_body,
                       jnp.broadcast_to(jnp.where(has_surplus, 0, kl), (tq, 1)).astype(I32))
    sel = jnp.where(jnp.logical_or(gt, take_all), 1.0, jnp.where(col <= jm, eq_ref[...], 0.0))
    sc_ref[...] = jnp.where(causal, jnp.where(sel > 0.0, 0.0, NEG), NEG)

    def attn_body(h, _):
        t = lax.dot_general(qr_ref[h], kr_ref[...], NT_DIMS,
                            preferred_element_type=F32) * qk_scale + sc_ref[...]
        m = jnp.max(t, axis=-1, keepdims=True)
        p = jnp.exp2(t - m)
        l = count(p)
        yo_ref[h] = jnp.dot(p.astype(BF16), v_ref[...], preferred_element_type=F32) / l
        return 0

    lax.fori_loop(0, N_HEADS, attn_body, 0)
    for h in range(N_HEADS):
        o_ref[:, h * HEAD_DIM:(h + 1) * HEAD_DIM] = yo_ref[h].astype(o_ref.dtype)


def _dsa_group(u3, tabs, y, *, q_off, rows, kl, n_sel, q_blk, k_blk, v_blk, qi_blk, misc_blk):
    b, s, _ = u3.shape
    tq = min(256, rows)
    qb0 = q_off // tq
    c128, s128, c64, s64 = tabs
    qw = N_HEADS * HEAD_DIM
    iw = IDX_HEADS * IDX_DIM

    def qspec(width, blk):
        return pl.BlockSpec((None, tq, width), lambda bi, i: (bi, qb0 + i, blk))

    def kspec(blk):
        return pl.BlockSpec((None, kl, LANES), lambda bi, i: (bi, 0, blk))

    return pl.pallas_call(
        functools.partial(_dsa_kernel, tq=tq, kl=kl, q_off=q_off, n_sel=n_sel),
        grid=(b, rows // tq),
        in_specs=[qspec(qw, q_blk), qspec(iw, qi_blk),
                  qspec(LANES, misc_blk),
                  kspec(k_blk), kspec(v_blk), kspec(misc_blk),
                  qspec(LANES, 0), qspec(LANES, 0), qspec(LANES, 0), qspec(LANES, 0),
                  kspec(0), kspec(0), kspec(0), kspec(0),
                  pl.BlockSpec(memory_space=pl.ANY)],
        out_specs=pl.BlockSpec((None, tq, qw), lambda bi, i: (bi, qb0 + i, 0)),
        out_shape=jax.ShapeDtypeStruct((b, s, qw), BF16),
        input_output_aliases={14: 0},
        scratch_shapes=[pltpu.VMEM((N_HEADS, tq, HEAD_DIM), BF16),
                        pltpu.VMEM((IDX_HEADS, tq, IDX_DIM), BF16),
                        pltpu.VMEM((kl, HEAD_DIM), BF16),
                        pltpu.VMEM((kl, IDX_DIM), BF16),
                        pltpu.VMEM((tq, kl), F32),
                        pltpu.VMEM((tq, kl), F32),
                        pltpu.VMEM((N_HEADS, tq, HEAD_DIM), F32)],
        compiler_params=_cparams(2),
        name="dsa_attn",
    )(u3, u3, u3, u3, u3, u3, c128, s128, c64, s64, c128, s128, c64, s64, y)


def _dsa_attn(u3, tabs, **blks):
    b, s, _ = u3.shape
    n_sel = min(TOPK_MAX, s // 4)
    rows = min(512, s)
    y = jnp.zeros((b, s, N_HEADS * HEAD_DIM), BF16)
    for q_off in range(0, s, rows):
        y = _dsa_group(u3, tabs, y, q_off=q_off, rows=rows, kl=q_off + rows, n_sel=n_sel, **blks)
    return y


def _dil_kernel(q0_ref, k0_ref, v0_ref, q1_ref, k1_ref, v1_ref, q2_ref, k2_ref, v2_ref,
                cos_ref, sin_ref, y_ref, qr_ref, kr_ref, o_ref, lse_ref, *, seq):
    qk_scale = HEAD_DIM ** -0.5 * LOG2E
    cos_f, sin_s = cos_ref[...], sin_ref[...]
    groups = ((q0_ref, k0_ref, v0_ref), (q1_ref, k1_ref, v1_ref), (q2_ref, k2_ref, v2_ref))
    for g, (q_ref, k_ref, _) in enumerate(groups):
        qr_ref[g] = _rope128(q_ref[...].astype(F32), cos_f, sin_s)
        kr_ref[g] = _rope128(k_ref[...].astype(F32), cos_f, sin_s)

    for g, (window, dil) in enumerate(DILATED_PAIRS):
        v_ref = groups[g][2]
        sub = seq // dil
        qb = min(window // dil, sub)
        row = lax.broadcasted_iota(I32, (qb, qb), 0)
        col = lax.broadcasted_iota(I32, (qb, qb), 1)
        mask_cur = col <= row
        mask_prev = col >= row
        for r in range(dil):
            for i in range(sub // qb):
                def rows(blk, r=r, dil=dil, qb=qb):
                    start = r + dil * qb * blk
                    return pl.ds(start, qb) if dil == 1 else pl.ds(start, qb, stride=dil)

                q = qr_ref[g, rows(i), :].astype(BF16)
                k_cur = kr_ref[g, rows(i), :].astype(BF16)
                t_cur = lax.dot_general(q, k_cur, NT_DIMS, preferred_element_type=F32) * qk_scale
                t_cur = jnp.where(mask_cur, t_cur, NEG)
                m = jnp.max(t_cur, axis=-1, keepdims=True)
                if i > 0:
                    k_prev = kr_ref[g, rows(i - 1), :].astype(BF16)
                    t_prev = lax.dot_general(q, k_prev, NT_DIMS,
                                             preferred_element_type=F32) * qk_scale
                    t_prev = jnp.where(mask_prev, t_prev, NEG)
                    m = jnp.maximum(m, jnp.max(t_prev, axis=-1, keepdims=True))
                p_cur = jnp.exp2(t_cur - m)
                l = jnp.sum(p_cur, axis=-1, keepdims=True)
                acc = jnp.dot(p_cur.astype(BF16), v_ref[rows(i), :].astype(BF16),
                              preferred_element_type=F32)
                if i > 0:
                    p_prev = jnp.exp2(t_prev - m)
                    l = l + jnp.sum(p_prev, axis=-1, keepdims=True)
                    acc = acc + jnp.dot(p_prev.astype(BF16), v_ref[rows(i - 1), :].astype(BF16),
                                        preferred_element_type=F32)
                o_ref[g, rows(i), :] = acc / l
                lse_ref[g, rows(i), :] = jnp.broadcast_to(m + jnp.log2(l), (qb, HEAD_DIM))

    l0, l1, l2 = lse_ref[0], lse_ref[1], lse_ref[2]
    m = jnp.maximum(jnp.maximum(l0, l1), l2)
    e0, e1, e2 = jnp.exp2(l0 - m), jnp.exp2(l1 - m), jnp.exp2(l2 - m)
    y = (e0 * o_ref[0] + e1 * o_ref[1] + e2 * o_ref[2]) / (e0 + e1 + e2)
    y_ref[...] = y.astype(y_ref.dtype)


def _dil_attn(ua3, ub3, tabs128, *, q0_blk, k0_blk, v0_blk):
    b, s, _ = ua3.shape
    for window, dil in DILATED_PAIRS:
        assert s % dil == 0 and (s // dil) % min(window // dil, s // dil) == 0
    cos_t, sin_t = tabs128

    def spec(blk0):
        return pl.BlockSpec((None, s, HEAD_DIM), lambda bi, a: (bi, 0, blk0 + a))

    tab = pl.BlockSpec((None, s, LANES), lambda bi, a: (bi, 0, 0))
    return pl.pallas_call(
        functools.partial(_dil_kernel, seq=s),
        grid=(b, DIL_HEADS),
        in_specs=[spec(q0_blk), spec(k0_blk), spec(v0_blk)]
        + [spec(j * DIL_HEADS) for j in range(6)] + [tab, tab],
        out_specs=pl.BlockSpec((None, s, HEAD_DIM), lambda bi, a: (bi, 0, a)),
        out_shape=jax.ShapeDtypeStruct((b, s, DIL_WIDTH), BF16),
        scratch_shapes=[pltpu.VMEM((3, s, HEAD_DIM), F32)] * 4,
        compiler_params=_cparams(2),
        name="dilated_attn",
    )(ua3, ua3, ua3, ub3, ub3, ub3, ub3, ub3, ub3, cos_t, sin_t)


def _mla_kernel(qn_ref, qr_ref, kn_ref, kr_ref, v_ref, cq_ref, sq_ref, ck_ref, sk_ref,
                o_ref, qrr_ref, krr_ref, *, tq):
    i = pl.program_id(2)
    qk_scale = (QK_NOPE + QK_ROPE) ** -0.5 * LOG2E
    krr_ref[...] = _rope64(kr_ref[...].astype(F32), ck_ref[...],
                           sk_ref[...])[:, :QK_ROPE].astype(BF16)
    q_rope = _rope64(qr_ref[...].astype(F32), cq_ref[...], sq_ref[...]).astype(BF16)
    for a in range(2):
        qrr_ref[a] = q_rope[:, a * QK_ROPE:(a + 1) * QK_ROPE]

    def logits(a, j):
        off = pl.multiple_of(j * tq, tq)
        lo, hi = a * HEAD_DIM, (a + 1) * HEAD_DIM
        s = lax.dot_general(qn_ref[:, lo:hi], kn_ref[pl.ds(off, tq), lo:hi], NT_DIMS,
                            preferred_element_type=F32)
        s = s + lax.dot_general(qrr_ref[a], krr_ref[pl.ds(off, tq), :], NT_DIMS,
                                preferred_element_type=F32)
        return s * qk_scale

    def values(a, j):
        off = pl.multiple_of(j * tq, tq)
        return v_ref[pl.ds(off, tq), a * HEAD_DIM:(a + 1) * HEAD_DIM]

    outs = _causal_flash(i, tq, 2, logits, values)
    for a in range(2):
        o_ref[:, a * HEAD_DIM:(a + 1) * HEAD_DIM] = outs[a].astype(o_ref.dtype)


def _mla_attn(qf3, kvf3, u3, tabs64, *, kr_blk):
    b, s, _ = qf3.shape
    tq = min(512, s)
    pw = 2 * HEAD_DIM
    n_pairs = N_HEADS // 2
    cos_t, sin_t = tabs64
    return pl.pallas_call(
        functools.partial(_mla_kernel, tq=tq),
        grid=(b, n_pairs, s // tq),
        in_specs=[pl.BlockSpec((None, tq, pw), lambda bi, p, i: (bi, i, p)),
                  pl.BlockSpec((None, tq, LANES), lambda bi, p, i: (bi, i, N_HEADS + p)),
                  pl.BlockSpec((None, s, pw), lambda bi, p, i: (bi, 0, p)),
                  pl.BlockSpec((None, s, LANES), lambda bi, p, i: (bi, 0, kr_blk)),
                  pl.BlockSpec((None, s, pw), lambda bi, p, i: (bi, 0, n_pairs + p)),
                  pl.BlockSpec((None, tq, LANES), lambda bi, p, i: (bi, i, 0)),
                  pl.BlockSpec((None, tq, LANES), lambda bi, p, i: (bi, i, 0)),
                  pl.BlockSpec((None, s, LANES), lambda bi, p, i: (bi, 0, 0)),
                  pl.BlockSpec((None, s, LANES), lambda bi, p, i: (bi, 0, 0))],
        out_specs=pl.BlockSpec((None, tq, pw), lambda bi, p, i: (bi, i, p)),
        out_shape=jax.ShapeDtypeStruct((b, s, N_HEADS * HEAD_DIM), BF16),
        scratch_shapes=[pltpu.VMEM((2, tq, QK_ROPE), BF16), pltpu.VMEM((s, QK_ROPE), BF16)],
        compiler_params=_cparams(3),
        name="mla_attn",
    )(qf3, qf3, kvf3, u3, kvf3, cos_t, sin_t, cos_t, sin_t)


def _pack(parts, multiple=512):
    k = next(p.shape[0] for p, _ in parts if p is not None)
    cols = [jnp.zeros((k, w), BF16) if p is None else p.astype(BF16) for p, w in parts]
    n = sum(w for _, w in parts)
    pad = (-n) % multiple
    if pad:
        cols.append(jnp.zeros((k, pad), BF16))
    return jnp.concatenate(cols, axis=1)


def _cols(w, sizes):
    out, acc = [], 0
    for sz in sizes:
        out.append(w[:, acc:acc + sz])
        acc += sz
    return out


def kernel(x, mem, positions, l0_norm, l0_w_in, l0_forget_bias, l0_mem_norm, l0_w_mem_kv, l0_w_out, l1_norm, l1_w_in, l1_mem_norm, l1_w_mem_kv, l1_w_out, l2_norm, l2_w_in, l2_mem_norm, l2_w_mem_kv, l2_w_out, l3_norm, l3_w_in, l3_q_norm, l3_w_uq, l3_kv_norm, l3_w_ukv, l3_mem_norm, l3_w_mem_kv, l3_w_out, final_norm):
    b, s, d = x.shape
    n_mem = mem.shape[1]
    m = b * s
    mw = N_HEADS * HEAD_DIM
    x2 = x.reshape(m, d)
    mem2 = mem.reshape(b * n_mem, d)

    pos_b = jnp.broadcast_to(positions.astype(F32).reshape(m, 1), (m, LANES))
    tabs128 = tuple(t.reshape(b, s, LANES) for t in _rope_tables(pos_b, HEAD_DIM))
    tabs64 = tuple(t.reshape(b, s, LANES) for t in _rope_tables(pos_b, IDX_DIM))

    def mem_kv(g, w):
        return _norm_proj(mem2, g, w.astype(BF16))

    def finish(y, u, x_in, g_mem, w_mem_kv, w_out, *, wy, zy_blk, zm_blk, qm_blk, final_g=None):
        return _out_proj(y, u, mem_kv(g_mem, w_mem_kv), x_in, w_out.astype(BF16), wy=wy,
                         zy_blk=zy_blk, zm_blk=zm_blk, qm_blk=qm_blk, seq=s, n_mem=n_mem,
                         final_g=final_g)

    q_w, k_w, v_w, f_w, qm_w, z_w = _cols(l0_w_in, (mw, mw, mw, N_HEADS, MEM_WIDTH, mw + MEM_WIDTH))
    w0 = _pack([(z_w[:, :mw], mw), (z_w[:, mw:], MEM_WIDTH), (qm_w, MEM_WIDTH), (q_w, mw),
                (k_w, mw), (v_w, mw), (f_w, N_HEADS), (None, LANES - N_HEADS)])
    u = _norm_proj(x2, l0_norm, w0)
    u3 = u.reshape(b, s, -1)
    base = (mw + 2 * MEM_WIDTH) // LANES
    c_t = _fox_gate(u3, l0_forget_bias, base + 3 * N_HEADS)
    y = _fox_attn(u3, c_t, q_blk=base // 2, k_blk=(base + N_HEADS) // 2,
                  v_blk=(base + 2 * N_HEADS) // 2)
    x2 = finish(y.reshape(m, mw), u, x2, l0_mem_norm, l0_w_mem_kv, l0_w_out, wy=mw, zy_blk=0,
                zm_blk=mw // MEM_WIDTH, qm_blk=mw // MEM_WIDTH + 1)

    iw = IDX_HEADS * IDX_DIM
    q_w, k_w, v_w, qi_w, ki_w, wi_w, qm_w, z_w = _cols(
        l1_w_in, (mw, HEAD_DIM, HEAD_DIM, iw, IDX_DIM, IDX_HEADS, MEM_WIDTH, mw + MEM_WIDTH))
    w1 = _pack([(z_w[:, :mw], mw), (q_w, mw), (qi_w, iw), (z_w[:, mw:], MEM_WIDTH),
                (qm_w, MEM_WIDTH), (k_w, HEAD_DIM), (v_w, HEAD_DIM), (ki_w, IDX_DIM),
                (wi_w, IDX_HEADS), (None, LANES - IDX_DIM - IDX_HEADS)])
    u = _norm_proj(x2, l1_norm, w1)
    u3 = u.reshape(b, s, -1)
    kv0 = (2 * mw + iw + 2 * MEM_WIDTH) // LANES
    y = _dsa_attn(u3, tabs128 + tabs64, q_blk=1, qi_blk=2 * mw // iw, k_blk=kv0, v_blk=kv0 + 1,
                  misc_blk=kv0 + 2)
    zoff = (2 * mw + iw) // MEM_WIDTH
    x2 = finish(y.reshape(m, mw), u, x2, l1_mem_norm, l1_w_mem_kv, l1_w_out, wy=mw, zy_blk=0,
                zm_blk=zoff, qm_blk=zoff + 1)

    gw = len(DILATED_PAIRS) * DIL_WIDTH
    q_w, k_w, v_w, qm_w, z_w = _cols(l2_w_in, (gw, gw, gw, MEM_WIDTH, DIL_WIDTH + MEM_WIDTH))

    def grp(w, g):
        return (w[:, g * DIL_WIDTH:(g + 1) * DIL_WIDTH], DIL_WIDTH)

    wa = _pack([grp(q_w, 0), (z_w[:, :DIL_WIDTH], DIL_WIDTH), grp(k_w, 0), grp(v_w, 0),
                (z_w[:, DIL_WIDTH:], MEM_WIDTH), (qm_w, MEM_WIDTH)])
    wb = _pack([grp(q_w, 1), grp(k_w, 1), grp(v_w, 1), grp(q_w, 2), grp(k_w, 2), grp(v_w, 2)])
    u = _norm_proj(x2, l2_norm, wa)
    ub = _norm_proj(x2, l2_norm, wb, out_dtype=F32)
    y = _dil_attn(u.reshape(b, s, -1), ub.reshape(b, s, -1), tabs128, q0_blk=0,
                  k0_blk=2 * DIL_HEADS, v0_blk=3 * DIL_HEADS)
    zoff = 4 * DIL_WIDTH // MEM_WIDTH
    x2 = finish(y.reshape(m, DIL_WIDTH), u, x2, l2_mem_norm, l2_w_mem_kv, l2_w_out, wy=DIL_WIDTH,
                zy_blk=1, zm_blk=zoff, qm_blk=zoff + 1)

    q_lora, kv_lora = l3_w_uq.shape[0], l3_w_ukv.shape[0]
    cq_w, ckv_w, kr_w, qm_w, z_w = _cols(l3_w_in, (q_lora, kv_lora, QK_ROPE, MEM_WIDTH,
                                                  mw + MEM_WIDTH))
    assert q_lora == MEM_WIDTH and kv_lora == MEM_WIDTH
    w3 = _pack([(z_w[:, :mw], mw), (z_w[:, mw:], MEM_WIDTH), (qm_w, MEM_WIDTH), (cq_w, q_lora),
                (ckv_w, kv_lora), (kr_w, QK_ROPE), (None, LANES - QK_ROPE)])
    u = _norm_proj(x2, l3_norm, w3)
    u3 = u.reshape(b, s, -1)
    cq_blk = (mw + 2 * MEM_WIDTH) // MEM_WIDTH
    uq = l3_w_uq.reshape(q_lora, N_HEADS, QK_NOPE + QK_ROPE)
    w_uq = _pack([(uq[:, :, :QK_NOPE].reshape(q_lora, -1), N_HEADS * QK_NOPE),
                  (uq[:, :, QK_NOPE:].reshape(q_lora, -1), N_HEADS * QK_ROPE)])
    ukv = l3_w_ukv.reshape(kv_lora, N_HEADS, QK_NOPE + HEAD_DIM)
    w_ukv = _pack([(ukv[:, :, :QK_NOPE].reshape(kv_lora, -1), N_HEADS * QK_NOPE),
                   (ukv[:, :, QK_NOPE:].reshape(kv_lora, -1), N_HEADS * HEAD_DIM)])
    qf = _norm_proj(u, l3_q_norm, w_uq, x_col_block=cq_blk)
    kvf = _norm_proj(u, l3_kv_norm, w_ukv, x_col_block=cq_blk + 1)
    y = _mla_attn(qf.reshape(b, s, -1), kvf.reshape(b, s, -1), u3, tabs64,
                  kr_blk=(mw + 2 * MEM_WIDTH + q_lora + kv_lora) // LANES)
    x2 = finish(y.reshape(m, mw), u, x2, l3_mem_norm, l3_w_mem_kv, l3_w_out, wy=mw, zy_blk=0,
                zm_blk=mw // MEM_WIDTH, qm_blk=mw // MEM_WIDTH + 1, final_g=final_norm)
    return x2.reshape(b, s, d)
```

```python
import functools

import jax
import jax.numpy as jnp
from jax import lax
from jax.experimental import pallas as pl
from jax.experimental.pallas import tpu as pltpu

F32 = jnp.float32
BF16 = jnp.bfloat16
I32 = jnp.int32

EPS = 1e-6
ROPE_THETA = 10000.0
HEAD_DIM = 128
N_HEADS = 16
MEM_HEADS = 4
MEM_WIDTH = MEM_HEADS * HEAD_DIM
IDX_HEADS = 16
IDX_DIM = 64
TOPK_MAX = 256
DILATED_PAIRS = ((128, 1), (512, 4), (2048, 16))
DIL_HEADS = 6
DIL_WIDTH = DIL_HEADS * HEAD_DIM
QK_NOPE = 128
QK_ROPE = 64
LANES = 128
NEG = -1e30
INT_MIN = -(2 ** 31)
VMEM_LIMIT = 56 * 1024 * 1024

NT_DIMS = (((1,), (1,)), ((), ()))
LOG2E = 1.4426950408889634


def _cparams(n_axes):
    return pltpu.CompilerParams(
        dimension_semantics=("arbitrary",) * n_axes, vmem_limit_bytes=VMEM_LIMIT)


def _rope128(x, cos_f, sin_s):
    return x * cos_f + pltpu.roll(x, 64, 1) * sin_s


def _rope64(x, cos_f, sin_s):
    lane = lax.broadcasted_iota(I32, x.shape, 1)
    partner = jnp.where((lane & 32) == 0, pltpu.roll(x, 96, 1), pltpu.roll(x, 32, 1))
    return x * cos_f + partner * sin_s


def _rope_table_kernel(pos_ref, inv_ref, sgn_ref, cos_ref, sin_ref):
    ang = pos_ref[...] * inv_ref[...]
    cos_ref[...] = jnp.cos(ang)
    sin_ref[...] = jnp.sin(ang) * sgn_ref[...]


def _rope_tables(pos_b, dh):
    m = pos_b.shape[0]
    half = dh // 2
    inv = jnp.power(ROPE_THETA, -jnp.arange(half, dtype=F32) * 2.0 / dh)
    reps = LANES // half
    inv_l = jnp.tile(inv, reps).reshape(1, LANES)
    sgn = jnp.tile(jnp.concatenate([-jnp.ones((half,), F32), jnp.ones((half,), F32)]),
                   reps // 2).reshape(1, LANES)
    ts = min(512, m)
    return pl.pallas_call(
        _rope_table_kernel,
        grid=(m // ts,),
        in_specs=[pl.BlockSpec((ts, LANES), lambda i: (i, 0)),
                  pl.BlockSpec((1, LANES), lambda i: (0, 0)),
                  pl.BlockSpec((1, LANES), lambda i: (0, 0))],
        out_specs=[pl.BlockSpec((ts, LANES), lambda i: (i, 0))] * 2,
        out_shape=[jax.ShapeDtypeStruct((m, LANES), F32)] * 2,
        compiler_params=_cparams(1),
        name="rope_tables",
    )(pos_b, inv_l, sgn)


def _norm_proj_kernel(x_ref, g_ref, w_ref, o_ref, h_ref):
    @pl.when(pl.program_id(1) == 0)
    def _():
        x = x_ref[...].astype(F32)
        ms = jnp.mean(x * x, axis=-1, keepdims=True)
        h_ref[...] = (x * lax.rsqrt(ms + EPS) * g_ref[...]).astype(BF16)

    o_ref[...] = jnp.dot(h_ref[...], w_ref[...], preferred_element_type=F32).astype(o_ref.dtype)


def _norm_proj(x, g, w, *, x_col_block=0, tn=512, out_dtype=BF16):
    m = x.shape[0]
    k, n = w.shape
    tm = min(1024, m)
    assert m % tm == 0 and n % tn == 0
    return pl.pallas_call(
        _norm_proj_kernel,
        grid=(m // tm, n // tn),
        in_specs=[pl.BlockSpec((tm, k), lambda i, j: (i, x_col_block)),
                  pl.BlockSpec((1, k), lambda i, j: (0, 0)),
                  pl.BlockSpec((k, tn), lambda i, j: (0, j))],
        out_specs=pl.BlockSpec((tm, tn), lambda i, j: (i, j)),
        out_shape=jax.ShapeDtypeStruct((m, n), out_dtype),
        scratch_shapes=[pltpu.VMEM((tm, k), BF16)],
        compiler_params=_cparams(2),
        name="norm_proj",
    )(x, g.reshape(1, k).astype(F32), w)


def _out_proj_kernel(y_ref, zy_ref, zm_ref, qm_ref, mkv_ref, x_ref, w_ref, *rest, wy, final):
    if final:
        gf_ref, o_ref, gated_ref = rest
    else:
        o_ref, gated_ref = rest
    zy = zy_ref[...].astype(F32)
    gated_ref[:, :wy] = (y_ref[...].astype(F32) * (zy * jax.nn.sigmoid(zy))).astype(BF16)
    scale = HEAD_DIM ** -0.5
    for h in range(MEM_HEADS):
        lo, hi = h * HEAD_DIM, (h + 1) * HEAD_DIM
        s = lax.dot_general(qm_ref[:, lo:hi], mkv_ref[:, lo:hi], NT_DIMS,
                            preferred_element_type=F32) * scale
        m = jnp.max(s, axis=-1, keepdims=True)
        p = jnp.exp(s - m)
        l = jnp.sum(p, axis=-1, keepdims=True)
        o = jnp.dot(p.astype(BF16), mkv_ref[:, MEM_WIDTH + lo:MEM_WIDTH + hi],
                    preferred_element_type=F32) / l
        zm = zm_ref[:, lo:hi].astype(F32)
        gated_ref[:, wy + lo:wy + hi] = (o * (zm * jax.nn.sigmoid(zm))).astype(BF16)
    out = x_ref[...] + jnp.dot(gated_ref[...], w_ref[...], preferred_element_type=F32)
    if final:
        ms = jnp.mean(out * out, axis=-1, keepdims=True)
        out = out * lax.rsqrt(ms + EPS) * gf_ref[...]
    o_ref[...] = out


def _out_proj(y, u, mkv, x, w_out, *, wy, zy_blk, zm_blk, qm_blk, seq, n_mem, final_g=None):
    m, d = x.shape
    tm = min(256, seq)
    final = final_g is not None
    in_specs = [
        pl.BlockSpec((tm, wy), lambda i: (i, 0)),
        pl.BlockSpec((tm, wy), lambda i: (i, zy_blk)),
        pl.BlockSpec((tm, MEM_WIDTH), lambda i: (i, zm_blk)),
        pl.BlockSpec((tm, MEM_WIDTH), lambda i: (i, qm_blk)),
        pl.BlockSpec((n_mem, 2 * MEM_WIDTH), lambda i: ((i * tm) // seq, 0)),
        pl.BlockSpec((tm, d), lambda i: (i, 0)),
        pl.BlockSpec((wy + MEM_WIDTH, d), lambda i: (0, 0)),
    ]
    args = [y, u, u, u, mkv, x, w_out]
    if final:
        in_specs.append(pl.BlockSpec((1, d), lambda i: (0, 0)))
        args.append(final_g.reshape(1, d).astype(F32))
    return pl.pallas_call(
        functools.partial(_out_proj_kernel, wy=wy, final=final),
        grid=(m // tm,),
        in_specs=in_specs,
        out_specs=pl.BlockSpec((tm, d), lambda i: (i, 0)),
        out_shape=jax.ShapeDtypeStruct((m, d), F32),
        scratch_shapes=[pltpu.VMEM((tm, wy + MEM_WIDTH), BF16)],
        compiler_params=_cparams(1),
        name="out_proj",
    )(*args)


def _fox_gate_kernel(f_ref, b_ref, c_ref, carry_ref):
    @pl.when(pl.program_id(1) == 0)
    def _():
        carry_ref[...] = jnp.zeros_like(carry_ref)

    f_t = f_ref[...].astype(F32).T[:N_HEADS, :] + b_ref[...]
    log_f = jnp.minimum(f_t, 0.0) - jnp.log(1.0 + jnp.exp(-jnp.abs(f_t)))
    ts = log_f.shape[1]
    r = lax.broadcasted_iota(I32, (ts, ts), 0)
    c = lax.broadcasted_iota(I32, (ts, ts), 1)
    upper = jnp.where(r <= c, 1.0, 0.0).astype(F32)
    cs = jnp.dot(log_f, upper, precision=lax.Precision.HIGHEST,
                 preferred_element_type=F32) + carry_ref[...]
    c_ref[...] = cs * LOG2E
    carry_ref[...] = cs[:, ts - 1:ts]


def _fox_gate(u3, bias, f_blk):
    b, s, _ = u3.shape
    ts = min(256, s)
    return pl.pallas_call(
        _fox_gate_kernel,
        grid=(b, s // ts),
        in_specs=[pl.BlockSpec((None, ts, LANES), lambda bi, j: (bi, j, f_blk)),
                  pl.BlockSpec((N_HEADS, 1), lambda bi, j: (0, 0))],
        out_specs=pl.BlockSpec((None, N_HEADS, ts), lambda bi, j: (bi, 0, j)),
        out_shape=jax.ShapeDtypeStruct((b, N_HEADS, s), F32),
        scratch_shapes=[pltpu.VMEM((N_HEADS, 1), F32)],
        compiler_params=_cparams(2),
        name="fox_gate",
    )(u3, bias.reshape(N_HEADS, 1).astype(F32))


STRIP = 32


def _softmax_strips(t_ref, p_ref, m_ref, l_ref, alpha_ref, adjust):
    tq, tk = t_ref.shape
    for r0 in range(0, tq, min(STRIP, tq)):
        rs = pl.ds(r0, min(STRIP, tq))
        blocks = [adjust(r0, c, t_ref[rs, c * LANES:(c + 1) * LANES]) for c in range(tk // LANES)]
        live = [x for x in blocks if x is not None]
        mx = live[0]
        for x in live[1:]:
            mx = jnp.maximum(mx, x)
        m_old = m_ref[rs, :]
        m_new = jnp.maximum(m_old, jnp.max(mx, axis=-1, keepdims=True))
        alpha = jnp.exp2(m_old - m_new)
        ps = [None if x is None else jnp.exp2(x - m_new) for x in blocks]
        live = [x for x in ps if x is not None]
        sm = live[0]
        for x in live[1:]:
            sm = sm + x
        l_ref[rs, :] = alpha * l_ref[rs, :] + jnp.sum(sm, axis=-1, keepdims=True)
        m_ref[rs, :] = m_new
        alpha_ref[rs, :] = alpha
        for c, x in enumerate(ps):
            p_ref[rs, c * LANES:(c + 1) * LANES] = (
                jnp.zeros((min(STRIP, tq), LANES), BF16) if x is None else x.astype(BF16))


def _flash_scratch(n_streams, tq):
    return [pltpu.VMEM((n_streams, tq, tq), F32), pltpu.VMEM((n_streams, tq, tq), BF16),
            pltpu.VMEM((n_streams, tq, LANES), F32), pltpu.VMEM((n_streams, tq, LANES), F32),
            pltpu.VMEM((n_streams, tq, LANES), F32), pltpu.VMEM((n_streams, tq, HEAD_DIM), F32)]


def _causal_flash(i, n_streams, raw_scores, values, to_logits, scratch):
    t_ref, p_ref, m_ref, l_ref, alpha_ref, acc_ref = scratch
    m_ref[...] = jnp.full(m_ref.shape, NEG, F32)
    l_ref[...] = jnp.zeros(l_ref.shape, F32)
    acc_ref[...] = jnp.zeros(acc_ref.shape, F32)
    strip = min(STRIP, t_ref.shape[1])
    row = lax.broadcasted_iota(I32, (strip, LANES), 0)
    col = lax.broadcasted_iota(I32, (strip, LANES), 1)

    def chunk(j, diagonal):
        for a in range(n_streams):
            t_ref[a] = raw_scores(a, j)
        for a in range(n_streams):
            def adjust(r0, c, x, a=a):
                t = to_logits(a, j, c, x)
                if not diagonal or c * LANES + LANES - 1 <= r0:
                    return t
                if c * LANES > r0 + strip - 1:
                    return None
                return jnp.where(col + c * LANES <= row + r0, t, NEG)

            _softmax_strips(t_ref.at[a], p_ref.at[a], m_ref.at[a], l_ref.at[a], alpha_ref.at[a],
                            adjust)
        for a in range(n_streams):
            acc_ref[a] = alpha_ref[a] * acc_ref[a] + jnp.dot(p_ref[a], values(a, j),
                                                             preferred_element_type=F32)

    def body(j, _):
        chunk(j, False)
        return 0

    lax.fori_loop(0, i, body, 0)
    chunk(i, True)
    return [acc_ref[a] / l_ref[a] for a in range(n_streams)]


def _fox_attn_kernel(q_ref, k_ref, v_ref, c_ref, o_ref, *scratch, tq):
    i = pl.program_id(2)
    qk_scale = HEAD_DIM ** -0.5 * LOG2E

    def raw_scores(a, j):
        off = pl.multiple_of(j * tq, tq)
        lo, hi = a * HEAD_DIM, (a + 1) * HEAD_DIM
        return lax.dot_general(q_ref[:, lo:hi], k_ref[pl.ds(off, tq), lo:hi], NT_DIMS,
                               preferred_element_type=F32)

    def to_logits(a, j, c, x):
        return x * qk_scale - c_ref[a, j, :, c * LANES:(c + 1) * LANES]

    def values(a, j):
        off = pl.multiple_of(j * tq, tq)
        return v_ref[pl.ds(off, tq), a * HEAD_DIM:(a + 1) * HEAD_DIM]

    outs = _causal_flash(i, 2, raw_scores, values, to_logits, scratch)
    for a in range(2):
        o_ref[:, a * HEAD_DIM:(a + 1) * HEAD_DIM] = outs[a].astype(o_ref.dtype)


def _fox_attn(u3, c_t, *, q_blk, k_blk, v_blk):
    b, s, _ = u3.shape
    tq = min(512, s)
    nq = s // tq
    pw = 2 * HEAD_DIM
    c5 = c_t.reshape(b, N_HEADS, nq, 1, tq)
    return pl.pallas_call(
        functools.partial(_fox_attn_kernel, tq=tq),
        grid=(b, N_HEADS // 2, nq),
        in_specs=[pl.BlockSpec((None, tq, pw), lambda bi, p, i: (bi, i, q_blk + p)),
                  pl.BlockSpec((None, s, pw), lambda bi, p, i: (bi, 0, k_blk + p)),
                  pl.BlockSpec((None, s, pw), lambda bi, p, i: (bi, 0, v_blk + p)),
                  pl.BlockSpec((None, 2, nq, 1, tq), lambda bi, p, i: (bi, p, 0, 0, 0))],
        out_specs=pl.BlockSpec((None, tq, pw), lambda bi, p, i: (bi, i, p)),
        out_shape=jax.ShapeDtypeStruct((b, s, N_HEADS * HEAD_DIM), BF16),
        scratch_shapes=_flash_scratch(2, tq),
        compiler_params=_cparams(3),
        name="fox_attn",
    )(u3, u3, u3, c5)


def _dsa_kernel(q_ref, qi_ref, mq_ref, k_ref, v_ref, mk_ref,
                cq128_ref, sq128_ref, cq64_ref, sq64_ref,
                ck128_ref, sk128_ref, ck64_ref, sk64_ref, y_in_ref,
                o_ref,
                qr_ref, qir_ref, kr_ref, kidx_ref, sc_ref, eq_ref, yo_ref,
                *, tq, kl, q_off, n_sel):
    del y_in_ref
    t0 = q_off + pl.program_id(1) * tq
    qk_scale = HEAD_DIM ** -0.5 * LOG2E
    idx_scale = (IDX_DIM ** -0.5) * (IDX_HEADS ** -0.5)

    kr_ref[...] = _rope128(k_ref[...].astype(F32), ck128_ref[...], sk128_ref[...]).astype(BF16)
    kidx_ref[...] = _rope64(mk_ref[...].astype(F32), ck64_ref[...],
                            sk64_ref[...])[:, :IDX_DIM].astype(BF16)
    cq128, sq128 = cq128_ref[...], sq128_ref[...]
    for h in range(N_HEADS):
        qh = q_ref[:, h * HEAD_DIM:(h + 1) * HEAD_DIM].astype(F32)
        qr_ref[h] = _rope128(qh, cq128, sq128).astype(BF16)
    cq64, sq64 = cq64_ref[...], sq64_ref[...]
    for a in range(IDX_HEADS // 2):
        pair = _rope64(qi_ref[:, a * LANES:(a + 1) * LANES].astype(F32), cq64, sq64).astype(BF16)
        qir_ref[2 * a] = pair[:, :IDX_DIM]
        qir_ref[2 * a + 1] = pair[:, IDX_DIM:]

    w = mq_ref[...].astype(F32) * idx_scale
    lane = lax.broadcasted_iota(I32, (tq, LANES), 1)
    sc_ref[...] = jnp.zeros_like(sc_ref)

    def idx_body(h, _):
        rel = jnp.maximum(lax.dot_general(qir_ref[h], kidx_ref[...], NT_DIMS,
                                          preferred_element_type=F32), 0.0)
        w_col = jnp.sum(jnp.where(lane == IDX_DIM + h, w, 0.0), axis=-1, keepdims=True)
        sc_ref[...] += w_col * rel
        return 0

    lax.fori_loop(0, IDX_HEADS, idx_body, 0)

    row = lax.broadcasted_iota(I32, (tq, kl), 0) + t0
    col = lax.broadcasted_iota(I32, (tq, kl), 1)
    causal = col <= row
    sc_ref[...] = jnp.where(causal, sc_ref[...], -jnp.inf)

    def count(x):
        return jnp.sum(x, axis=-1, keepdims=True)

    def key_to_f32(key):
        return pltpu.bitcast(jnp.where(key < 0, key ^ jnp.int32(0x7FFFFFFF), key), F32)

    def thr_body(it, key):
        cand = key + (jnp.int32(1) << (31 - it))
        c = count(jnp.where(sc_ref[...] >= key_to_f32(cand), 1.0, 0.0))
        return jnp.where(c >= n_sel, cand, key)

    thr_key = lax.fori_loop(0, 32, thr_body, jnp.full((tq, 1), INT_MIN, I32))
    take_all = thr_key == INT_MIN
    thr = key_to_f32(thr_key)
    score = sc_ref[...]
    gt = score > thr
    eq_ref[...] = jnp.where(score == thr, 1.0, 0.0)
    need = n_sel - count(jnp.where(gt, 1.0, 0.0))
    surplus = jnp.where(take_all, 0.0, count(eq_ref[...]) - need)
    has_surplus = jnp.max(surplus) > 0.0

    n_bits = kl.bit_length()

    def tie_body(it, jm):
        cand = jm + (jnp.int32(1) << (n_bits - 1 - it))
        f = count(jnp.where(col < cand, eq_ref[...], 0.0))
        return jnp.where(jnp.logical_and(f < need, cand <= kl), cand, jm)

    jm = lax.fori_loop(0, jnp.where(has_surplus, n_bits, 0), tie_body,
                       jnp.broadcast_to(jnp.where(has_surplus, 0, kl), (tq, 1)).astype(I32))
    sel = jnp.where(jnp.logical_or(gt, take_all), 1.0, jnp.where(col <= jm, eq_ref[...], 0.0))
    sc_ref[...] = jnp.where(causal, jnp.where(sel > 0.0, 0.0, NEG), NEG)

    def attn_body(h, _):
        t = lax.dot_general(qr_ref[h], kr_ref[...], NT_DIMS,
                            preferred_element_type=F32) * qk_scale + sc_ref[...]
        m = jnp.max(t, axis=-1, keepdims=True)
        p = jnp.exp2(t - m)
        l = count(p)
        yo_ref[h] = jnp.dot(p.astype(BF16), v_ref[...], preferred_element_type=F32) / l
        return 0

    lax.fori_loop(0, N_HEADS, attn_body, 0)
    for h in range(N_HEADS):
        o_ref[:, h * HEAD_DIM:(h + 1) * HEAD_DIM] = yo_ref[h].astype(o_ref.dtype)


def _dsa_group(u3, tabs, y, *, q_off, rows, kl, n_sel, q_blk, k_blk, v_blk, qi_blk, misc_blk):
    b, s, _ = u3.shape
    tq = min(256, rows)
    qb0 = q_off // tq
    c128, s128, c64, s64 = tabs
    qw = N_HEADS * HEAD_DIM
    iw = IDX_HEADS * IDX_DIM

    def qspec(width, blk):
        return pl.BlockSpec((None, tq, width), lambda bi, i: (bi, qb0 + i, blk))

    def kspec(blk):
        return pl.BlockSpec((None, kl, LANES), lambda bi, i: (bi, 0, blk))

    return pl.pallas_call(
        functools.partial(_dsa_kernel, tq=tq, kl=kl, q_off=q_off, n_sel=n_sel),
        grid=(b, rows // tq),
        in_specs=[qspec(qw, q_blk), qspec(iw, qi_blk),
                  qspec(LANES, misc_blk),
                  kspec(k_blk), kspec(v_blk), kspec(misc_blk),
                  qspec(LANES, 0), qspec(LANES, 0), qspec(LANES, 0), qspec(LANES, 0),
                  kspec(0), kspec(0), kspec(0), kspec(0),
                  pl.BlockSpec(memory_space=pl.ANY)],
        out_specs=pl.BlockSpec((None, tq, qw), lambda bi, i: (bi, qb0 + i, 0)),
        out_shape=jax.ShapeDtypeStruct((b, s, qw), BF16),
        input_output_aliases={14: 0},
        scratch_shapes=[pltpu.VMEM((N_HEADS, tq, HEAD_DIM), BF16),
                        pltpu.VMEM((IDX_HEADS, tq, IDX_DIM), BF16),
                        pltpu.VMEM((kl, HEAD_DIM), BF16),
                        pltpu.VMEM((kl, IDX_DIM), BF16),
                        pltpu.VMEM((tq, kl), F32),
                        pltpu.VMEM((tq, kl), F32),
                        pltpu.VMEM((N_HEADS, tq, HEAD_DIM), F32)],
        compiler_params=_cparams(2),
        name="dsa_attn",
    )(u3, u3, u3, u3, u3, u3, c128, s128, c64, s64, c128, s128, c64, s64, y)


def _dsa_attn(u3, tabs, **blks):
    b, s, _ = u3.shape
    n_sel = min(TOPK_MAX, s // 4)
    rows = min(512, s)
    y = jnp.zeros((b, s, N_HEADS * HEAD_DIM), BF16)
    for q_off in range(0, s, rows):
        y = _dsa_group(u3, tabs, y, q_off=q_off, rows=rows, kl=q_off + rows, n_sel=n_sel, **blks)
    return y


def _dil_kernel(q0_ref, k0_ref, v0_ref, q1_ref, k1_ref, v1_ref, q2_ref, k2_ref, v2_ref,
                cos_ref, sin_ref, y_ref, qr_ref, kr_ref, o_ref, lse_ref, *, seq):
    qk_scale = HEAD_DIM ** -0.5 * LOG2E
    cos_f, sin_s = cos_ref[...], sin_ref[...]
    groups = ((q0_ref, k0_ref, v0_ref), (q1_ref, k1_ref, v1_ref), (q2_ref, k2_ref, v2_ref))
    for g, (q_ref, k_ref, _) in enumerate(groups):
        qr_ref[g] = _rope128(q_ref[...].astype(F32), cos_f, sin_s)
        kr_ref[g] = _rope128(k_ref[...].astype(F32), cos_f, sin_s)

    for g, (window, dil) in enumerate(DILATED_PAIRS):
        v_ref = groups[g][2]
        sub = seq // dil
        qb = min(window // dil, sub)
        row = lax.broadcasted_iota(I32, (qb, qb), 0)
        col = lax.broadcasted_iota(I32, (qb, qb), 1)
        mask_cur = col <= row
        mask_prev = col >= row
        for r in range(dil):
            for i in range(sub // qb):
                def rows(blk, r=r, dil=dil, qb=qb):
                    start = r + dil * qb * blk
                    return pl.ds(start, qb) if dil == 1 else pl.ds(start, qb, stride=dil)

                q = qr_ref[g, rows(i), :].astype(BF16)
                k_cur = kr_ref[g, rows(i), :].astype(BF16)
                t_cur = lax.dot_general(q, k_cur, NT_DIMS, preferred_element_type=F32) * qk_scale
                t_cur = jnp.where(mask_cur, t_cur, NEG)
                m = jnp.max(t_cur, axis=-1, keepdims=True)
                if i > 0:
                    k_prev = kr_ref[g, rows(i - 1), :].astype(BF16)
                    t_prev = lax.dot_general(q, k_prev, NT_DIMS,
                                             preferred_element_type=F32) * qk_scale
                    t_prev = jnp.where(mask_prev, t_prev, NEG)
                    m = jnp.maximum(m, jnp.max(t_prev, axis=-1, keepdims=True))
                p_cur = jnp.exp2(t_cur - m)
                l = jnp.sum(p_cur, axis=-1, keepdims=True)
                acc = jnp.dot(p_cur.astype(BF16), v_ref[rows(i), :].astype(BF16),
                              preferred_element_type=F32)
                if i > 0:
                    p_prev = jnp.exp2(t_prev - m)
                    l = l + jnp.sum(p_prev, axis=-1, keepdims=True)
                    acc = acc + jnp.dot(p_prev.astype(BF16), v_ref[rows(i - 1), :].astype(BF16),
                                        preferred_element_type=F32)
                o_ref[g, rows(i), :] = acc / l
                lse_ref[g, rows(i), :] = jnp.broadcast_to(m + jnp.log2(l), (qb, HEAD_DIM))

    l0, l1, l2 = lse_ref[0], lse_ref[1], lse_ref[2]
    m = jnp.maximum(jnp.maximum(l0, l1), l2)
    e0, e1, e2 = jnp.exp2(l0 - m), jnp.exp2(l1 - m), jnp.exp2(l2 - m)
    y = (e0 * o_ref[0] + e1 * o_ref[1] + e2 * o_ref[2]) / (e0 + e1 + e2)
    y_ref[...] = y.astype(y_ref.dtype)


def _dil_attn(ua3, ub3, tabs128, *, q0_blk, k0_blk, v0_blk):
    b, s, _ = ua3.shape
    for window, dil in DILATED_PAIRS:
        assert s % dil == 0 and (s // dil) % min(window // dil, s // dil) == 0
    cos_t, sin_t = tabs128

    def spec(blk0):
        return pl.BlockSpec((None, s, HEAD_DIM), lambda bi, a: (bi, 0, blk0 + a))

    tab = pl.BlockSpec((None, s, LANES), lambda bi, a: (bi, 0, 0))
    return pl.pallas_call(
        functools.partial(_dil_kernel, seq=s),
        grid=(b, DIL_HEADS),
        in_specs=[spec(q0_blk), spec(k0_blk), spec(v0_blk)]
        + [spec(j * DIL_HEADS) for j in range(6)] + [tab, tab],
        out_specs=pl.BlockSpec((None, s, HEAD_DIM), lambda bi, a: (bi, 0, a)),
        out_shape=jax.ShapeDtypeStruct((b, s, DIL_WIDTH), BF16),
        scratch_shapes=[pltpu.VMEM((3, s, HEAD_DIM), F32)] * 4,
        compiler_params=_cparams(2),
        name="dilated_attn",
    )(ua3, ua3, ua3, ub3, ub3, ub3, ub3, ub3, ub3, cos_t, sin_t)


def _mla_kernel(qn_ref, qr_ref, kn_ref, kr_ref, v_ref, cq_ref, sq_ref, ck_ref, sk_ref,
                o_ref, qcat_ref, kcat_ref, *scratch, tq):
    i = pl.program_id(2)
    qk_scale = (QK_NOPE + QK_ROPE) ** -0.5 * LOG2E
    cat = QK_NOPE + QK_ROPE

    @pl.when(i == 0)
    def _():
        k_rope = _rope64(kr_ref[...].astype(F32), ck_ref[...],
                         sk_ref[...])[:, :QK_ROPE].astype(BF16)
        for a in range(2):
            kcat_ref[a, :, :QK_NOPE] = kn_ref[:, a * HEAD_DIM:(a + 1) * HEAD_DIM]
            kcat_ref[a, :, QK_NOPE:cat] = k_rope
            kcat_ref[a, :, cat:] = jnp.zeros((kcat_ref.shape[1], 2 * LANES - cat), BF16)

    q_rope = _rope64(qr_ref[...].astype(F32), cq_ref[...], sq_ref[...]).astype(BF16)
    for a in range(2):
        qcat_ref[a, :, :QK_NOPE] = qn_ref[:, a * HEAD_DIM:(a + 1) * HEAD_DIM]
        qcat_ref[a, :, QK_NOPE:cat] = q_rope[:, a * QK_ROPE:(a + 1) * QK_ROPE]
        qcat_ref[a, :, cat:] = jnp.zeros((tq, 2 * LANES - cat), BF16)

    def raw_scores(a, j):
        off = pl.multiple_of(j * tq, tq)
        return lax.dot_general(qcat_ref[a], kcat_ref[a, pl.ds(off, tq), :], NT_DIMS,
                               preferred_element_type=F32)

    def to_logits(a, j, c, x):
        return x * qk_scale

    def values(a, j):
        off = pl.multiple_of(j * tq, tq)
        return v_ref[pl.ds(off, tq), a * HEAD_DIM:(a + 1) * HEAD_DIM]

    outs = _causal_flash(i, 2, raw_scores, values, to_logits, scratch)
    for a in range(2):
        o_ref[:, a * HEAD_DIM:(a + 1) * HEAD_DIM] = outs[a].astype(o_ref.dtype)


def _mla_attn(qf3, kvf3, u3, tabs64, *, kr_blk):
    b, s, _ = qf3.shape
    tq = min(512, s)
    pw = 2 * HEAD_DIM
    n_pairs = N_HEADS // 2
    cos_t, sin_t = tabs64
    return pl.pallas_call(
        functools.partial(_mla_kernel, tq=tq),
        grid=(b, n_pairs, s // tq),
        in_specs=[pl.BlockSpec((None, tq, pw), lambda bi, p, i: (bi, i, p)),
                  pl.BlockSpec((None, tq, LANES), lambda bi, p, i: (bi, i, N_HEADS + p)),
                  pl.BlockSpec((None, s, pw), lambda bi, p, i: (bi, 0, p)),
                  pl.BlockSpec((None, s, LANES), lambda bi, p, i: (bi, 0, kr_blk)),
                  pl.BlockSpec((None, s, pw), lambda bi, p, i: (bi, 0, n_pairs + p)),
                  pl.BlockSpec((None, tq, LANES), lambda bi, p, i: (bi, i, 0)),
                  pl.BlockSpec((None, tq, LANES), lambda bi, p, i: (bi, i, 0)),
                  pl.BlockSpec((None, s, LANES), lambda bi, p, i: (bi, 0, 0)),
                  pl.BlockSpec((None, s, LANES), lambda bi, p, i: (bi, 0, 0))],
        out_specs=pl.BlockSpec((None, tq, pw), lambda bi, p, i: (bi, i, p)),
        out_shape=jax.ShapeDtypeStruct((b, s, N_HEADS * HEAD_DIM), BF16),
        scratch_shapes=[pltpu.VMEM((2, tq, 2 * LANES), BF16),
                        pltpu.VMEM((2, s, 2 * LANES), BF16)] + _flash_scratch(2, tq),
        compiler_params=_cparams(3),
        name="mla_attn",
    )(qf3, qf3, kvf3, u3, kvf3, cos_t, sin_t, cos_t, sin_t)


def _pack(parts, multiple=512):
    k = next(p.shape[0] for p, _ in parts if p is not None)
    cols = [jnp.zeros((k, w), BF16) if p is None else p.astype(BF16) for p, w in parts]
    n = sum(w for _, w in parts)
    pad = (-n) % multiple
    if pad:
        cols.append(jnp.zeros((k, pad), BF16))
    return jnp.concatenate(cols, axis=1)


def _cols(w, sizes):
    out, acc = [], 0
    for sz in sizes:
        out.append(w[:, acc:acc + sz])
        acc += sz
    return out


def kernel(x, mem, positions, l0_norm, l0_w_in, l0_forget_bias, l0_mem_norm, l0_w_mem_kv, l0_w_out, l1_norm, l1_w_in, l1_mem_norm, l1_w_mem_kv, l1_w_out, l2_norm, l2_w_in, l2_mem_norm, l2_w_mem_kv, l2_w_out, l3_norm, l3_w_in, l3_q_norm, l3_w_uq, l3_kv_norm, l3_w_ukv, l3_mem_norm, l3_w_mem_kv, l3_w_out, final_norm):
    b, s, d = x.shape
    n_mem = mem.shape[1]
    m = b * s
    mw = N_HEADS * HEAD_DIM
    x2 = x.reshape(m, d)
    mem2 = mem.reshape(b * n_mem, d)

    pos_b = jnp.broadcast_to(positions.astype(F32).reshape(m, 1), (m, LANES))
    tabs128 = tuple(t.reshape(b, s, LANES) for t in _rope_tables(pos_b, HEAD_DIM))
    tabs64 = tuple(t.reshape(b, s, LANES) for t in _rope_tables(pos_b, IDX_DIM))

    def mem_kv(g, w):
        return _norm_proj(mem2, g, w.astype(BF16))

    def finish(y, u, x_in, g_mem, w_mem_kv, w_out, *, wy, zy_blk, zm_blk, qm_blk, final_g=None):
        return _out_proj(y, u, mem_kv(g_mem, w_mem_kv), x_in, w_out.astype(BF16), wy=wy,
                         zy_blk=zy_blk, zm_blk=zm_blk, qm_blk=qm_blk, seq=s, n_mem=n_mem,
                         final_g=final_g)

    q_w, k_w, v_w, f_w, qm_w, z_w = _cols(l0_w_in, (mw, mw, mw, N_HEADS, MEM_WIDTH, mw + MEM_WIDTH))
    w0 = _pack([(z_w[:, :mw], mw), (z_w[:, mw:], MEM_WIDTH), (qm_w, MEM_WIDTH), (q_w, mw),
                (k_w, mw), (v_w, mw), (f_w, N_HEADS), (None, LANES - N_HEADS)])
    u = _norm_proj(x2, l0_norm, w0)
    u3 = u.reshape(b, s, -1)
    base = (mw + 2 * MEM_WIDTH) // LANES
    c_t = _fox_gate(u3, l0_forget_bias, base + 3 * N_HEADS)
    y = _fox_attn(u3, c_t, q_blk=base // 2, k_blk=(base + N_HEADS) // 2,
                  v_blk=(base + 2 * N_HEADS) // 2)
    x2 = finish(y.reshape(m, mw), u, x2, l0_mem_norm, l0_w_mem_kv, l0_w_out, wy=mw, zy_blk=0,
                zm_blk=mw // MEM_WIDTH, qm_blk=mw // MEM_WIDTH + 1)

    iw = IDX_HEADS * IDX_DIM
    q_w, k_w, v_w, qi_w, ki_w, wi_w, qm_w, z_w = _cols(
        l1_w_in, (mw, HEAD_DIM, HEAD_DIM, iw, IDX_DIM, IDX_HEADS, MEM_WIDTH, mw + MEM_WIDTH))
    w1 = _pack([(z_w[:, :mw], mw), (q_w, mw), (qi_w, iw), (z_w[:, mw:], MEM_WIDTH),
                (qm_w, MEM_WIDTH), (k_w, HEAD_DIM), (v_w, HEAD_DIM), (ki_w, IDX_DIM),
                (wi_w, IDX_HEADS), (None, LANES - IDX_DIM - IDX_HEADS)])
    u = _norm_proj(x2, l1_norm, w1)
    u3 = u.reshape(b, s, -1)
    kv0 = (2 * mw + iw + 2 * MEM_WIDTH) // LANES
    y = _dsa_attn(u3, tabs128 + tabs64, q_blk=1, qi_blk=2 * mw // iw, k_blk=kv0, v_blk=kv0 + 1,
                  misc_blk=kv0 + 2)
    zoff = (2 * mw + iw) // MEM_WIDTH
    x2 = finish(y.reshape(m, mw), u, x2, l1_mem_norm, l1_w_mem_kv, l1_w_out, wy=mw, zy_blk=0,
                zm_blk=zoff, qm_blk=zoff + 1)

    gw = len(DILATED_PAIRS) * DIL_WIDTH
    q_w, k_w, v_w, qm_w, z_w = _cols(l2_w_in, (gw, gw, gw, MEM_WIDTH, DIL_WIDTH + MEM_WIDTH))

    def grp(w, g):
        return (w[:, g * DIL_WIDTH:(g + 1) * DIL_WIDTH], DIL_WIDTH)

    wa = _pack([grp(q_w, 0), (z_w[:, :DIL_WIDTH], DIL_WIDTH), grp(k_w, 0), grp(v_w, 0),
                (z_w[:, DIL_WIDTH:], MEM_WIDTH), (qm_w, MEM_WIDTH)])
    wb = _pack([grp(q_w, 1), grp(k_w, 1), grp(v_w, 1), grp(q_w, 2), grp(k_w, 2), grp(v_w, 2)])
    u = _norm_proj(x2, l2_norm, wa)
    ub = _norm_proj(x2, l2_norm, wb, out_dtype=F32)
    y = _dil_attn(u.reshape(b, s, -1), ub.reshape(b, s, -1), tabs128, q0_blk=0,
                  k0_blk=2 * DIL_HEADS, v0_blk=3 * DIL_HEADS)
    zoff = 4 * DIL_WIDTH // MEM_WIDTH
    x2 = finish(y.reshape(m, DIL_WIDTH), u, x2, l2_mem_norm, l2_w_mem_kv, l2_w_out, wy=DIL_WIDTH,
                zy_blk=1, zm_blk=zoff, qm_blk=zoff + 1)

    q_lora, kv_lora = l3_w_uq.shape[0], l3_w_ukv.shape[0]
    cq_w, ckv_w, kr_w, qm_w, z_w = _cols(l3_w_in, (q_lora, kv_lora, QK_ROPE, MEM_WIDTH,
                                                  mw + MEM_WIDTH))
    assert q_lora == MEM_WIDTH and kv_lora == MEM_WIDTH
    w3 = _pack([(z_w[:, :mw], mw), (z_w[:, mw:], MEM_WIDTH), (qm_w, MEM_WIDTH), (cq_w, q_lora),
                (ckv_w, kv_lora), (kr_w, QK_ROPE), (None, LANES - QK_ROPE)])
    u = _norm_proj(x2, l3_norm, w3)
    u3 = u.reshape(b, s, -1)
    cq_blk = (mw + 2 * MEM_WIDTH) // MEM_WIDTH
    uq = l3_w_uq.reshape(q_lora, N_HEADS, QK_NOPE + QK_ROPE)
    w_uq = _pack([(uq[:, :, :QK_NOPE].reshape(q_lora, -1), N_HEADS * QK_NOPE),
                  (uq[:, :, QK_NOPE:].reshape(q_lora, -1), N_HEADS * QK_ROPE)])
    ukv = l3_w_ukv.reshape(kv_lora, N_HEADS, QK_NOPE + HEAD_DIM)
    w_ukv = _pack([(ukv[:, :, :QK_NOPE].reshape(kv_lora, -1), N_HEADS * QK_NOPE),
                   (ukv[:, :, QK_NOPE:].reshape(kv_lora, -1), N_HEADS * HEAD_DIM)])
    qf = _norm_proj(u, l3_q_norm, w_uq, x_col_block=cq_blk)
    kvf = _norm_proj(u, l3_kv_norm, w_ukv, x_col_block=cq_blk + 1)
    y = _mla_attn(qf.reshape(b, s, -1), kvf.reshape(b, s, -1), u3, tabs64,
                  kr_blk=(mw + 2 * MEM_WIDTH + q_lora + kv_lora) // LANES)
    x2 = finish(y.reshape(m, mw), u, x2, l3_mem_norm, l3_w_mem_kv, l3_w_out, wy=mw, zy_blk=0,
                zm_blk=mw // MEM_WIDTH, qm_blk=mw // MEM_WIDTH + 1, final_g=final_norm)
    return x2.reshape(b, s, d)
```

```python
import functools

import jax
import jax.numpy as jnp
from jax import lax
from jax.experimental import pallas as pl
from jax.experimental.pallas import tpu as pltpu

F32 = jnp.float32
BF16 = jnp.bfloat16
I32 = jnp.int32

EPS = 1e-6
ROPE_THETA = 10000.0
HEAD_DIM = 128
N_HEADS = 16
MEM_HEADS = 4
MEM_WIDTH = MEM_HEADS * HEAD_DIM
IDX_HEADS = 16
IDX_DIM = 64
TOPK_MAX = 256
DILATED_PAIRS = ((128, 1), (512, 4), (2048, 16))
DIL_HEADS = 6
DIL_WIDTH = DIL_HEADS * HEAD_DIM
QK_NOPE = 128
QK_ROPE = 64
LANES = 128
NEG = -1e30
INT_MIN = -(2 ** 31)
VMEM_LIMIT = 56 * 1024 * 1024

NT_DIMS = (((1,), (1,)), ((), ()))
LOG2E = 1.4426950408889634


def _cparams(n_axes):
    return pltpu.CompilerParams(
        dimension_semantics=("arbitrary",) * n_axes, vmem_limit_bytes=VMEM_LIMIT)


def _rope128(x, cos_f, sin_s):
    return x * cos_f + pltpu.roll(x, 64, 1) * sin_s


def _rope64(x, cos_f, sin_s):
    lane = lax.broadcasted_iota(I32, x.shape, 1)
    partner = jnp.where((lane & 32) == 0, pltpu.roll(x, 96, 1), pltpu.roll(x, 32, 1))
    return x * cos_f + partner * sin_s


def _rope_table_kernel(pos_ref, inv_ref, sgn_ref, cos_ref, sin_ref):
    ang = pos_ref[...] * inv_ref[...]
    cos_ref[...] = jnp.cos(ang)
    sin_ref[...] = jnp.sin(ang) * sgn_ref[...]


def _rope_tables(pos_b, dh):
    m = pos_b.shape[0]
    half = dh // 2
    inv = jnp.power(ROPE_THETA, -jnp.arange(half, dtype=F32) * 2.0 / dh)
    reps = LANES // half
    inv_l = jnp.tile(inv, reps).reshape(1, LANES)
    sgn = jnp.tile(jnp.concatenate([-jnp.ones((half,), F32), jnp.ones((half,), F32)]),
                   reps // 2).reshape(1, LANES)
    ts = min(512, m)
    return pl.pallas_call(
        _rope_table_kernel,
        grid=(m // ts,),
        in_specs=[pl.BlockSpec((ts, LANES), lambda i: (i, 0)),
                  pl.BlockSpec((1, LANES), lambda i: (0, 0)),
                  pl.BlockSpec((1, LANES), lambda i: (0, 0))],
        out_specs=[pl.BlockSpec((ts, LANES), lambda i: (i, 0))] * 2,
        out_shape=[jax.ShapeDtypeStruct((m, LANES), F32)] * 2,
        compiler_params=_cparams(1),
        name="rope_tables",
    )(pos_b, inv_l, sgn)


def _norm_proj_kernel(x_ref, g_ref, w_ref, *rest, has_extra):
    if has_extra:
        wx_ref, o_ref, ox_ref, h_ref = rest
    else:
        o_ref, h_ref = rest

    @pl.when(pl.program_id(1) == 0)
    def _():
        x = x_ref[...].astype(F32)
        ms = jnp.mean(x * x, axis=-1, keepdims=True)
        h_ref[...] = (x * lax.rsqrt(ms + EPS) * g_ref[...]).astype(BF16)
        if has_extra:
            ox_ref[...] = jnp.dot(h_ref[...], wx_ref[...],
                                  preferred_element_type=F32).astype(ox_ref.dtype)

    o_ref[...] = jnp.dot(h_ref[...], w_ref[...], preferred_element_type=F32).astype(o_ref.dtype)


def _norm_proj(x, g, w, *, x_col_block=0, out_dtype=BF16, w_extra=None, extra_dtype=BF16):
    m = x.shape[0]
    k, n = w.shape
    tm = min(1024, m)
    tn = 1024 if n % 1024 == 0 else 512
    assert m % tm == 0 and n % tn == 0
    has_extra = w_extra is not None
    in_specs = [pl.BlockSpec((tm, k), lambda i, j: (i, x_col_block)),
                pl.BlockSpec((1, k), lambda i, j: (0, 0)),
                pl.BlockSpec((k, tn), lambda i, j: (0, j))]
    out_specs = [pl.BlockSpec((tm, tn), lambda i, j: (i, j))]
    out_shape = [jax.ShapeDtypeStruct((m, n), out_dtype)]
    args = [x, g.reshape(1, k).astype(F32), w]
    if has_extra:
        in_specs.append(pl.BlockSpec((k, LANES), lambda i, j: (0, 0)))
        out_specs.append(pl.BlockSpec((tm, LANES), lambda i, j: (i, 0)))
        out_shape.append(jax.ShapeDtypeStruct((m, LANES), extra_dtype))
        args.append(w_extra)
    outs = pl.pallas_call(
        functools.partial(_norm_proj_kernel, has_extra=has_extra),
        grid=(m // tm, n // tn),
        in_specs=in_specs,
        out_specs=out_specs,
        out_shape=out_shape,
        scratch_shapes=[pltpu.VMEM((tm, k), BF16)],
        compiler_params=_cparams(2),
        name="norm_proj",
    )(*args)
    return outs if has_extra else outs[0]


def _out_proj_kernel(y_ref, zy_ref, zm_ref, qm_ref, mkv_ref, x_ref, w_ref, *rest, wy, final):
    if final:
        gf_ref, o_ref, gated_ref = rest
    else:
        o_ref, gated_ref = rest
    zy = zy_ref[...].astype(F32)
    gated_ref[:, :wy] = (y_ref[...].astype(F32) * (zy * jax.nn.sigmoid(zy))).astype(BF16)
    scale = HEAD_DIM ** -0.5
    for h in range(MEM_HEADS):
        lo, hi = h * HEAD_DIM, (h + 1) * HEAD_DIM
        s = lax.dot_general(qm_ref[:, lo:hi], mkv_ref[:, lo:hi], NT_DIMS,
                            preferred_element_type=F32) * scale
        m = jnp.max(s, axis=-1, keepdims=True)
        p = jnp.exp(s - m)
        l = jnp.sum(p, axis=-1, keepdims=True)
        o = jnp.dot(p.astype(BF16), mkv_ref[:, MEM_WIDTH + lo:MEM_WIDTH + hi],
                    preferred_element_type=F32) / l
        zm = zm_ref[:, lo:hi].astype(F32)
        gated_ref[:, wy + lo:wy + hi] = (o * (zm * jax.nn.sigmoid(zm))).astype(BF16)
    out = x_ref[...] + jnp.dot(gated_ref[...], w_ref[...], preferred_element_type=F32)
    if final:
        ms = jnp.mean(out * out, axis=-1, keepdims=True)
        out = out * lax.rsqrt(ms + EPS) * gf_ref[...]
    o_ref[...] = out


def _out_proj(y, u, mkv, x, w_out, *, wy, zy_blk, zm_blk, qm_blk, seq, n_mem, final_g=None):
    m, d = x.shape
    tm = min(512, seq)
    final = final_g is not None
    in_specs = [
        pl.BlockSpec((tm, wy), lambda i: (i, 0)),
        pl.BlockSpec((tm, wy), lambda i: (i, zy_blk)),
        pl.BlockSpec((tm, MEM_WIDTH), lambda i: (i, zm_blk)),
        pl.BlockSpec((tm, MEM_WIDTH), lambda i: (i, qm_blk)),
        pl.BlockSpec((n_mem, 2 * MEM_WIDTH), lambda i: ((i * tm) // seq, 0)),
        pl.BlockSpec((tm, d), lambda i: (i, 0)),
        pl.BlockSpec((wy + MEM_WIDTH, d), lambda i: (0, 0), pipeline_mode=pl.Buffered(1)),
    ]
    args = [y, u, u, u, mkv, x, w_out]
    if final:
        in_specs.append(pl.BlockSpec((1, d), lambda i: (0, 0)))
        args.append(final_g.reshape(1, d).astype(F32))
    return pl.pallas_call(
        functools.partial(_out_proj_kernel, wy=wy, final=final),
        grid=(m // tm,),
        in_specs=in_specs,
        out_specs=pl.BlockSpec((tm, d), lambda i: (i, 0)),
        out_shape=jax.ShapeDtypeStruct((m, d), F32),
        scratch_shapes=[pltpu.VMEM((tm, wy + MEM_WIDTH), BF16)],
        compiler_params=_cparams(1),
        name="out_proj",
    )(*args)


def _fox_gate_kernel(f_ref, b_ref, c_ref, carry_ref):
    @pl.when(pl.program_id(1) == 0)
    def _():
        carry_ref[...] = jnp.zeros_like(carry_ref)

    f_t = f_ref[...].astype(F32).T[:N_HEADS, :] + b_ref[...]
    log_f = jnp.minimum(f_t, 0.0) - jnp.log(1.0 + jnp.exp(-jnp.abs(f_t)))
    ts = log_f.shape[1]
    r = lax.broadcasted_iota(I32, (ts, ts), 0)
    c = lax.broadcasted_iota(I32, (ts, ts), 1)
    upper = jnp.where(r <= c, 1.0, 0.0).astype(F32)
    cs = jnp.dot(log_f, upper, precision=lax.Precision.HIGHEST,
                 preferred_element_type=F32) + carry_ref[...]
    c_ref[...] = cs * LOG2E
    carry_ref[...] = cs[:, ts - 1:ts]


def _fox_gate(u3, bias, f_blk):
    b, s, _ = u3.shape
    ts = min(256, s)
    return pl.pallas_call(
        _fox_gate_kernel,
        grid=(b, s // ts),
        in_specs=[pl.BlockSpec((None, ts, LANES), lambda bi, j: (bi, j, f_blk)),
                  pl.BlockSpec((N_HEADS, 1), lambda bi, j: (0, 0))],
        out_specs=pl.BlockSpec((None, N_HEADS, ts), lambda bi, j: (bi, 0, j)),
        out_shape=jax.ShapeDtypeStruct((b, N_HEADS, s), F32),
        scratch_shapes=[pltpu.VMEM((N_HEADS, 1), F32)],
        compiler_params=_cparams(2),
        name="fox_gate",
    )(u3, bias.reshape(N_HEADS, 1).astype(F32))


STRIP = 32


def _softmax_strips(t_ref, p_ref, m_ref, l_ref, alpha_ref, adjust):
    tq, tk = t_ref.shape
    for r0 in range(0, tq, min(STRIP, tq)):
        rs = pl.ds(r0, min(STRIP, tq))
        blocks = [adjust(r0, c, t_ref[rs, c * LANES:(c + 1) * LANES]) for c in range(tk // LANES)]
        live = [x for x in blocks if x is not None]
        mx = live[0]
        for x in live[1:]:
            mx = jnp.maximum(mx, x)
        m_old = m_ref[rs, :]
        m_new = jnp.maximum(m_old, jnp.max(mx, axis=-1, keepdims=True))
        alpha = jnp.exp2(m_old - m_new)
        ps = [None if x is None else jnp.exp2(x - m_new) for x in blocks]
        live = [x for x in ps if x is not None]
        sm = live[0]
        for x in live[1:]:
            sm = sm + x
        l_ref[rs, :] = alpha * l_ref[rs, :] + jnp.sum(sm, axis=-1, keepdims=True)
        m_ref[rs, :] = m_new
        alpha_ref[rs, :] = alpha
        for c, x in enumerate(ps):
            p_ref[rs, c * LANES:(c + 1) * LANES] = (
                jnp.zeros((min(STRIP, tq), LANES), BF16) if x is None else x.astype(BF16))


def _flash_scratch(n_streams, tq):
    return [pltpu.VMEM((n_streams, tq, tq), F32), pltpu.VMEM((n_streams, tq, tq), BF16),
            pltpu.VMEM((n_streams, tq, LANES), F32), pltpu.VMEM((n_streams, tq, LANES), F32),
            pltpu.VMEM((n_streams, tq, LANES), F32), pltpu.VMEM((n_streams, tq, HEAD_DIM), F32)]


def _flash_init(scratch):
    _, _, m_ref, l_ref, _, acc_ref = scratch
    m_ref[...] = jnp.full(m_ref.shape, NEG, F32)
    l_ref[...] = jnp.zeros(l_ref.shape, F32)
    acc_ref[...] = jnp.zeros(acc_ref.shape, F32)


def _flash_chunk(n_streams, j, raw_scores, values, to_logits, scratch, *, diagonal):
    t_ref, p_ref, m_ref, l_ref, alpha_ref, acc_ref = scratch
    strip = min(STRIP, t_ref.shape[1])
    row = lax.broadcasted_iota(I32, (strip, LANES), 0)
    col = lax.broadcasted_iota(I32, (strip, LANES), 1)
    for a in range(n_streams):
        t_ref[a] = raw_scores(a, j)
    for a in range(n_streams):
        def adjust(r0, c, x, a=a):
            t = to_logits(a, j, r0, c, x)
            if not diagonal or c * LANES + LANES - 1 <= r0:
                return t
            if c * LANES > r0 + strip - 1:
                return None
            return jnp.where(col + c * LANES <= row + r0, t, NEG)

        _softmax_strips(t_ref.at[a], p_ref.at[a], m_ref.at[a], l_ref.at[a], alpha_ref.at[a],
                        adjust)
    for a in range(n_streams):
        acc_ref[a] = alpha_ref[a] * acc_ref[a] + jnp.dot(p_ref[a], values(a, j),
                                                         preferred_element_type=F32)


def _flash_finish(n_streams, scratch):
    _, _, _, l_ref, _, acc_ref = scratch
    return [acc_ref[a] / l_ref[a] for a in range(n_streams)]


def _causal_flash(i, n_streams, raw_scores, values, to_logits, scratch):
    _flash_init(scratch)

    def body(j, _):
        _flash_chunk(n_streams, j, raw_scores, values, to_logits, scratch, diagonal=False)
        return 0

    lax.fori_loop(0, i, body, 0)
    _flash_chunk(n_streams, i, raw_scores, values, to_logits, scratch, diagonal=True)
    return _flash_finish(n_streams, scratch)


def _fox_attn_kernel(q_ref, k_ref, v_ref, c_ref, o_ref, *scratch, tq):
    i = pl.program_id(2)
    qk_scale = HEAD_DIM ** -0.5 * LOG2E

    def raw_scores(a, j):
        off = pl.multiple_of(j * tq, tq)
        lo, hi = a * HEAD_DIM, (a + 1) * HEAD_DIM
        return lax.dot_general(q_ref[:, lo:hi], k_ref[pl.ds(off, tq), lo:hi], NT_DIMS,
                               preferred_element_type=F32)

    def to_logits(a, j, r0, c, x):
        return x * qk_scale - c_ref[a, j, :, c * LANES:(c + 1) * LANES]

    def values(a, j):
        off = pl.multiple_of(j * tq, tq)
        return v_ref[pl.ds(off, tq), a * HEAD_DIM:(a + 1) * HEAD_DIM]

    outs = _causal_flash(i, 2, raw_scores, values, to_logits, scratch)
    for a in range(2):
        o_ref[:, a * HEAD_DIM:(a + 1) * HEAD_DIM] = outs[a].astype(o_ref.dtype)


def _fox_attn(u3, c_t, *, q_blk, k_blk, v_blk):
    b, s, _ = u3.shape
    tq = min(512, s)
    nq = s // tq
    pw = 2 * HEAD_DIM
    c5 = c_t.reshape(b, N_HEADS, nq, 1, tq)
    return pl.pallas_call(
        functools.partial(_fox_attn_kernel, tq=tq),
        grid=(b, N_HEADS // 2, nq),
        in_specs=[pl.BlockSpec((None, tq, pw), lambda bi, p, i: (bi, i, q_blk + p)),
                  pl.BlockSpec((None, s, pw), lambda bi, p, i: (bi, 0, k_blk + p)),
                  pl.BlockSpec((None, s, pw), lambda bi, p, i: (bi, 0, v_blk + p)),
                  pl.BlockSpec((None, 2, nq, 1, tq), lambda bi, p, i: (bi, p, 0, 0, 0))],
        out_specs=pl.BlockSpec((None, tq, pw), lambda bi, p, i: (bi, i, p)),
        out_shape=jax.ShapeDtypeStruct((b, s, N_HEADS * HEAD_DIM), BF16),
        scratch_shapes=_flash_scratch(2, tq),
        compiler_params=_cparams(3),
        name="fox_attn",
    )(u3, u3, u3, c5)


def _dsa_kprep_kernel(k_ref, mk_ref, c128_ref, s128_ref, c64_ref, s64_ref,
                      kr_ref, klo_ref, khi_ref):
    kr_ref[...] = _rope128(k_ref[...].astype(F32), c128_ref[...], s128_ref[...]).astype(BF16)
    ki = _rope64(mk_ref[...].astype(F32), c64_ref[...], s64_ref[...])
    lane = lax.broadcasted_iota(I32, ki.shape, 1)
    lo = jnp.where(lane < IDX_DIM, ki, 0.0)
    klo_ref[...] = lo.astype(BF16)
    khi_ref[...] = pltpu.roll(lo, IDX_DIM, 1).astype(BF16)


def _dsa_kprep(u3, tabs, *, k_blk, misc_blk):
    b, s, _ = u3.shape
    ts = min(512, s)
    c128, s128, c64, s64 = tabs

    def spec(blk):
        return pl.BlockSpec((None, ts, LANES), lambda bi, i: (bi, i, blk))

    return pl.pallas_call(
        _dsa_kprep_kernel,
        grid=(b, s // ts),
        in_specs=[spec(k_blk), spec(misc_blk), spec(0), spec(0), spec(0), spec(0)],
        out_specs=[spec(0)] * 3,
        out_shape=[jax.ShapeDtypeStruct((b, s, LANES), BF16)] * 3,
        compiler_params=_cparams(2),
        name="dsa_kprep",
    )(u3, u3, c128, s128, c64, s64)


def _dsa_kernel(q_ref, qi_ref, mq_ref, v_ref, kr_ref, klo_ref, khi_ref,
                cq128_ref, sq128_ref, cq64_ref, sq64_ref, y_in_ref,
                o_ref,
                qr_ref, qir_ref, w_ref, sc_ref, t4_ref, yo_ref, *flash,
                tq, kl, q_off, n_sel):
    del y_in_ref
    t0 = q_off + pl.program_id(1) * tq
    qk_scale = HEAD_DIM ** -0.5 * LOG2E
    idx_scale = (IDX_DIM ** -0.5) * (IDX_HEADS ** -0.5)
    ck = min(512, kl)
    n_chunks = kl // ck
    strip = min(STRIP, tq)

    cq128, sq128 = cq128_ref[...], sq128_ref[...]
    for h in range(N_HEADS):
        qh = q_ref[:, h * HEAD_DIM:(h + 1) * HEAD_DIM].astype(F32)
        qr_ref[h] = (_rope128(qh, cq128, sq128) * qk_scale).astype(BF16)
    cq64, sq64 = cq64_ref[...], sq64_ref[...]
    for a in range(IDX_HEADS // 2):
        qir_ref[a] = _rope64(qi_ref[:, a * LANES:(a + 1) * LANES].astype(F32),
                             cq64, sq64).astype(BF16)
    w_ref[...] = mq_ref[...].astype(F32) * idx_scale

    row = lax.broadcasted_iota(I32, (strip, LANES), 0)
    col = lax.broadcasted_iota(I32, (strip, LANES), 1)
    group = 4

    def idx_chunk(j, _):
        off = pl.multiple_of(j * ck, ck)
        k_lo, k_hi = klo_ref[pl.ds(off, ck), :], khi_ref[pl.ds(off, ck), :]
        for g in range(IDX_HEADS // group):
            for a2 in range(group // 2):
                x = qir_ref[g * (group // 2) + a2]
                t4_ref[2 * a2] = lax.dot_general(x, k_lo, NT_DIMS, preferred_element_type=F32)
                t4_ref[2 * a2 + 1] = lax.dot_general(x, k_hi, NT_DIMS,
                                                     preferred_element_type=F32)
            last = g == IDX_HEADS // group - 1
            for r0 in range(0, tq, strip):
                rs = pl.ds(r0, strip)
                w_rows = w_ref[rs, :]
                ws = [jnp.broadcast_to(w_rows[:, IDX_DIM + g * group + u:IDX_DIM + g * group + u + 1],
                                       (strip, LANES)) for u in range(group)]
                for c in range(ck // LANES):
                    cs = pl.ds(c * LANES, LANES)
                    acc = ws[0] * jnp.maximum(t4_ref[0, rs, cs], 0.0)
                    for u in range(1, group):
                        acc = acc + ws[u] * jnp.maximum(t4_ref[u, rs, cs], 0.0)
                    if g > 0:
                        acc = acc + sc_ref[j, rs, cs]
                    if last:
                        causal = col + (off + c * LANES) <= row + (t0 + r0)
                        acc = jnp.where(causal, acc, -jnp.inf)
                    sc_ref[j, rs, cs] = acc
        return 0

    lax.fori_loop(0, n_chunks, idx_chunk, 0)

    shape3 = (n_chunks, tq, ck)
    col3 = lax.broadcasted_iota(I32, shape3, 0) * ck + lax.broadcasted_iota(I32, shape3, 2)
    causal3 = col3 <= lax.broadcasted_iota(I32, shape3, 1) + t0

    def count(x):
        return jnp.sum(jnp.sum(x, axis=0), axis=-1, keepdims=True)

    def key_to_f32(key):
        return pltpu.bitcast(jnp.where(key < 0, key ^ jnp.int32(0x7FFFFFFF), key), F32)

    def thr_body(it, key):
        cand = key + (jnp.int32(1) << (31 - it))
        c = count(jnp.where(sc_ref[...] >= key_to_f32(cand), 1.0, 0.0))
        return jnp.where(c >= n_sel, cand, key)

    thr_key = lax.fori_loop(0, 32, thr_body, jnp.full((tq, 1), INT_MIN, I32))
    take_all = thr_key == INT_MIN
    thr = key_to_f32(thr_key)
    score = sc_ref[...]
    need = n_sel - count(jnp.where(score > thr, 1.0, 0.0))
    surplus = jnp.where(take_all, 0.0, count(jnp.where(score == thr, 1.0, 0.0)) - need)
    has_surplus = jnp.max(surplus) > 0.0

    n_bits = kl.bit_length()

    def tie_body(it, jm):
        cand = jm + (jnp.int32(1) << (n_bits - 1 - it))
        f = count(jnp.where(sc_ref[...] == thr, jnp.where(col3 < cand, 1.0, 0.0), 0.0))
        return jnp.where(jnp.logical_and(f < need, cand <= kl), cand, jm)

    jm = lax.fori_loop(0, jnp.where(has_surplus, n_bits, 0), tie_body,
                       jnp.broadcast_to(jnp.where(has_surplus, 0, kl), (tq, 1)).astype(I32))
    keep = jnp.where(jnp.logical_or(score > thr, take_all), 1.0,
                     jnp.where(score == thr, jnp.where(col3 <= jm, 1.0, 0.0), 0.0))
    sc_ref[...] = jnp.where(causal3, jnp.where(keep > 0.0, 0.0, NEG), NEG)

    def raw_scores(h0):
        def f(a, j):
            return lax.dot_general(qr_ref[h0 + a], kr_ref[j * ck:(j + 1) * ck, :], NT_DIMS,
                                   preferred_element_type=F32)
        return f

    def values(a, j):
        return v_ref[j * ck:(j + 1) * ck, :]

    def to_logits(a, j, r0, c, x):
        return x + sc_ref[j, r0:r0 + strip, c * LANES:(c + 1) * LANES]

    def pair_body(pp, _):
        h0 = 2 * pp
        _flash_init(flash)
        for j in range(n_chunks):
            _flash_chunk(2, j, raw_scores(h0), values, to_logits, flash, diagonal=False)
        outs = _flash_finish(2, flash)
        for a in range(2):
            yo_ref[h0 + a] = outs[a].astype(yo_ref.dtype)
        return 0

    lax.fori_loop(0, N_HEADS // 2, pair_body, 0)
    for h in range(N_HEADS):
        o_ref[:, h * HEAD_DIM:(h + 1) * HEAD_DIM] = yo_ref[h]


def _dsa_group(u3, kprep, tabs, y, *, q_off, rows, kl, n_sel, q_blk, v_blk, qi_blk, misc_blk):
    b, s, _ = u3.shape
    tq = rows
    assert kl % min(512, kl) == 0 and tq == min(512, kl)
    qb0 = q_off // tq
    c128, s128, c64, s64 = tabs
    kr, klo, khi = kprep
    qw = N_HEADS * HEAD_DIM
    iw = IDX_HEADS * IDX_DIM
    ck = min(512, kl)

    def qspec(width, blk):
        return pl.BlockSpec((None, tq, width), lambda bi, i: (bi, qb0 + i, blk))

    def kspec(blk):
        return pl.BlockSpec((None, kl, LANES), lambda bi, i: (bi, 0, blk))

    return pl.pallas_call(
        functools.partial(_dsa_kernel, tq=tq, kl=kl, q_off=q_off, n_sel=n_sel),
        grid=(b, rows // tq),
        in_specs=[qspec(qw, q_blk), qspec(iw, qi_blk), qspec(LANES, misc_blk),
                  kspec(v_blk), kspec(0), kspec(0), kspec(0),
                  qspec(LANES, 0), qspec(LANES, 0), qspec(LANES, 0), qspec(LANES, 0),
                  pl.BlockSpec(memory_space=pl.ANY)],
        out_specs=pl.BlockSpec((None, tq, qw), lambda bi, i: (bi, qb0 + i, 0)),
        out_shape=jax.ShapeDtypeStruct((b, s, qw), BF16),
        input_output_aliases={11: 0},
        scratch_shapes=[pltpu.VMEM((N_HEADS, tq, HEAD_DIM), BF16),
                        pltpu.VMEM((IDX_HEADS // 2, tq, LANES), BF16),
                        pltpu.VMEM((tq, LANES), F32),
                        pltpu.VMEM((kl // ck, tq, ck), F32),
                        pltpu.VMEM((4, tq, ck), F32),
                        pltpu.VMEM((N_HEADS, tq, HEAD_DIM), BF16)] + _flash_scratch(2, tq),
        compiler_params=_cparams(2),
        name="dsa_attn",
    )(u3, u3, u3, u3, kr, klo, khi, c128, s128, c64, s64, y)


def _dsa_attn(u3, tabs, *, k_blk, misc_blk, **blks):
    b, s, _ = u3.shape
    n_sel = min(TOPK_MAX, s // 4)
    rows = min(512, s)
    kprep = _dsa_kprep(u3, tabs, k_blk=k_blk, misc_blk=misc_blk)
    y = jnp.zeros((b, s, N_HEADS * HEAD_DIM), BF16)
    for q_off in range(0, s, rows):
        y = _dsa_group(u3, kprep, tabs, y, q_off=q_off, rows=rows, kl=q_off + rows, n_sel=n_sel,
                       misc_blk=misc_blk, **blks)
    return y


def _dil_kernel(q0_ref, k0_ref, v0_ref, q1_ref, k1_ref, v1_ref, q2_ref, k2_ref, v2_ref,
                cos_ref, sin_ref, y_ref, qr_ref, kr_ref, o_ref, lse_ref, *, seq):
    qk_scale = HEAD_DIM ** -0.5 * LOG2E
    cos_f, sin_s = cos_ref[...], sin_ref[...]
    groups = ((q0_ref, k0_ref, v0_ref), (q1_ref, k1_ref, v1_ref), (q2_ref, k2_ref, v2_ref))
    for g, (q_ref, k_ref, _) in enumerate(groups):
        qr_ref[g] = _rope128(q_ref[...].astype(F32), cos_f, sin_s)
        kr_ref[g] = _rope128(k_ref[...].astype(F32), cos_f, sin_s)

    for g, (window, dil) in enumerate(DILATED_PAIRS):
        v_ref = groups[g][2]
        sub = seq // dil
        qb = min(window // dil, sub)
        row = lax.broadcasted_iota(I32, (qb, qb), 0)
        col = lax.broadcasted_iota(I32, (qb, qb), 1)
        mask_cur = col <= row
        mask_prev = col >= row
        for r in range(dil):
            for i in range(sub // qb):
                def rows(blk, r=r, dil=dil, qb=qb):
                    start = r + dil * qb * blk
                    return pl.ds(start, qb) if dil == 1 else pl.ds(start, qb, stride=dil)

                q = qr_ref[g, rows(i), :].astype(BF16)
                k_cur = kr_ref[g, rows(i), :].astype(BF16)
                t_cur = lax.dot_general(q, k_cur, NT_DIMS, preferred_element_type=F32) * qk_scale
                t_cur = jnp.where(mask_cur, t_cur, NEG)
                m = jnp.max(t_cur, axis=-1, keepdims=True)
                if i > 0:
                    k_prev = kr_ref[g, rows(i - 1), :].astype(BF16)
                    t_prev = lax.dot_general(q, k_prev, NT_DIMS,
                                             preferred_element_type=F32) * qk_scale
                    t_prev = jnp.where(mask_prev, t_prev, NEG)
                    m = jnp.maximum(m, jnp.max(t_prev, axis=-1, keepdims=True))
                p_cur = jnp.exp2(t_cur - m)
                l = jnp.sum(p_cur, axis=-1, keepdims=True)
                acc = jnp.dot(p_cur.astype(BF16), v_ref[rows(i), :].astype(BF16),
                              preferred_element_type=F32)
                if i > 0:
                    p_prev = jnp.exp2(t_prev - m)
                    l = l + jnp.sum(p_prev, axis=-1, keepdims=True)
                    acc = acc + jnp.dot(p_prev.astype(BF16), v_ref[rows(i - 1), :].astype(BF16),
                                        preferred_element_type=F32)
                o_ref[g, rows(i), :] = acc / l
                lse_ref[g, rows(i), :] = jnp.broadcast_to(m + jnp.log2(l), (qb, HEAD_DIM))

    l0, l1, l2 = lse_ref[0], lse_ref[1], lse_ref[2]
    m = jnp.maximum(jnp.maximum(l0, l1), l2)
    e0, e1, e2 = jnp.exp2(l0 - m), jnp.exp2(l1 - m), jnp.exp2(l2 - m)
    y = (e0 * o_ref[0] + e1 * o_ref[1] + e2 * o_ref[2]) / (e0 + e1 + e2)
    y_ref[...] = y.astype(y_ref.dtype)


def _dil_attn(ua3, ub3, tabs128, *, q0_blk, k0_blk, v0_blk):
    b, s, _ = ua3.shape
    for window, dil in DILATED_PAIRS:
        assert s % dil == 0 and (s // dil) % min(window // dil, s // dil) == 0
    cos_t, sin_t = tabs128

    def spec(blk0):
        return pl.BlockSpec((None, s, HEAD_DIM), lambda bi, a: (bi, 0, blk0 + a))

    tab = pl.BlockSpec((None, s, LANES), lambda bi, a: (bi, 0, 0))
    return pl.pallas_call(
        functools.partial(_dil_kernel, seq=s),
        grid=(b, DIL_HEADS),
        in_specs=[spec(q0_blk), spec(k0_blk), spec(v0_blk)]
        + [spec(j * DIL_HEADS) for j in range(6)] + [tab, tab],
        out_specs=pl.BlockSpec((None, s, HEAD_DIM), lambda bi, a: (bi, 0, a)),
        out_shape=jax.ShapeDtypeStruct((b, s, DIL_WIDTH), BF16),
        scratch_shapes=[pltpu.VMEM((3, s, HEAD_DIM), F32)] * 4,
        compiler_params=_cparams(2),
        name="dilated_attn",
    )(ua3, ua3, ua3, ub3, ub3, ub3, ub3, ub3, ub3, cos_t, sin_t)


def _mla_kernel(qn_ref, qr_ref, kn_ref, kr_ref, v_ref, cq_ref, sq_ref, ck_ref, sk_ref,
                o_ref, qcat_ref, kcat_ref, *scratch, tq):
    i = pl.program_id(2)
    qk_scale = (QK_NOPE + QK_ROPE) ** -0.5 * LOG2E
    cat = QK_NOPE + QK_ROPE

    @pl.when(i == 0)
    def _():
        k_rope = _rope64(kr_ref[...].astype(F32), ck_ref[...],
                         sk_ref[...])[:, :QK_ROPE].astype(BF16)
        for a in range(2):
            kcat_ref[a, :, :QK_NOPE] = kn_ref[:, a * HEAD_DIM:(a + 1) * HEAD_DIM]
            kcat_ref[a, :, QK_NOPE:cat] = k_rope
            kcat_ref[a, :, cat:] = jnp.zeros((kcat_ref.shape[1], 2 * LANES - cat), BF16)

    q_rope = _rope64(qr_ref[...].astype(F32), cq_ref[...], sq_ref[...]).astype(BF16)
    for a in range(2):
        qcat_ref[a, :, :QK_NOPE] = qn_ref[:, a * HEAD_DIM:(a + 1) * HEAD_DIM]
        qcat_ref[a, :, QK_NOPE:cat] = q_rope[:, a * QK_ROPE:(a + 1) * QK_ROPE]
        qcat_ref[a, :, cat:] = jnp.zeros((tq, 2 * LANES - cat), BF16)

    def raw_scores(a, j):
        off = pl.multiple_of(j * tq, tq)
        return lax.dot_general(qcat_ref[a], kcat_ref[a, pl.ds(off, tq), :], NT_DIMS,
                               preferred_element_type=F32)

    def to_logits(a, j, r0, c, x):
        return x * qk_scale

    def values(a, j):
        off = pl.multiple_of(j * tq, tq)
        return v_ref[pl.ds(off, tq), a * HEAD_DIM:(a + 1) * HEAD_DIM]

    outs = _causal_flash(i, 2, raw_scores, values, to_logits, scratch)
    for a in range(2):
        o_ref[:, a * HEAD_DIM:(a + 1) * HEAD_DIM] = outs[a].astype(o_ref.dtype)


def _mla_attn(qf3, kvf3, u3, tabs64, *, kr_blk):
    b, s, _ = qf3.shape
    tq = min(512, s)
    pw = 2 * HEAD_DIM
    n_pairs = N_HEADS // 2
    cos_t, sin_t = tabs64
    return pl.pallas_call(
        functools.partial(_mla_kernel, tq=tq),
        grid=(b, n_pairs, s // tq),
        in_specs=[pl.BlockSpec((None, tq, pw), lambda bi, p, i: (bi, i, p)),
                  pl.BlockSpec((None, tq, LANES), lambda bi, p, i: (bi, i, N_HEADS + p)),
                  pl.BlockSpec((None, s, pw), lambda bi, p, i: (bi, 0, p)),
                  pl.BlockSpec((None, s, LANES), lambda bi, p, i: (bi, 0, kr_blk)),
                  pl.BlockSpec((None, s, pw), lambda bi, p, i: (bi, 0, n_pairs + p)),
                  pl.BlockSpec((None, tq, LANES), lambda bi, p, i: (bi, i, 0)),
                  pl.BlockSpec((None, tq, LANES), lambda bi, p, i: (bi, i, 0)),
                  pl.BlockSpec((None, s, LANES), lambda bi, p, i: (bi, 0, 0)),
                  pl.BlockSpec((None, s, LANES), lambda bi, p, i: (bi, 0, 0))],
        out_specs=pl.BlockSpec((None, tq, pw), lambda bi, p, i: (bi, i, p)),
        out_shape=jax.ShapeDtypeStruct((b, s, N_HEADS * HEAD_DIM), BF16),
        scratch_shapes=[pltpu.VMEM((2, tq, 2 * LANES), BF16),
                        pltpu.VMEM((2, s, 2 * LANES), BF16)] + _flash_scratch(2, tq),
        compiler_params=_cparams(3),
        name="mla_attn",
    )(qf3, qf3, kvf3, u3, kvf3, cos_t, sin_t, cos_t, sin_t)


def _pack(parts, multiple=512):
    k = next(p.shape[0] for p, _ in parts if p is not None)
    cols = [jnp.zeros((k, w), BF16) if p is None else p.astype(BF16) for p, w in parts]
    n = sum(w for _, w in parts)
    pad = (-n) % multiple
    if pad:
        cols.append(jnp.zeros((k, pad), BF16))
    return jnp.concatenate(cols, axis=1)


def _cols(w, sizes):
    out, acc = [], 0
    for sz in sizes:
        out.append(w[:, acc:acc + sz])
        acc += sz
    return out


def kernel(x, mem, positions, l0_norm, l0_w_in, l0_forget_bias, l0_mem_norm, l0_w_mem_kv, l0_w_out, l1_norm, l1_w_in, l1_mem_norm, l1_w_mem_kv, l1_w_out, l2_norm, l2_w_in, l2_mem_norm, l2_w_mem_kv, l2_w_out, l3_norm, l3_w_in, l3_q_norm, l3_w_uq, l3_kv_norm, l3_w_ukv, l3_mem_norm, l3_w_mem_kv, l3_w_out, final_norm):
    b, s, d = x.shape
    n_mem = mem.shape[1]
    m = b * s
    mw = N_HEADS * HEAD_DIM
    x2 = x.reshape(m, d)
    mem2 = mem.reshape(b * n_mem, d)

    pos_b = jnp.broadcast_to(positions.astype(F32).reshape(m, 1), (m, LANES))
    tabs128 = tuple(t.reshape(b, s, LANES) for t in _rope_tables(pos_b, HEAD_DIM))
    tabs64 = tuple(t.reshape(b, s, LANES) for t in _rope_tables(pos_b, IDX_DIM))

    def mem_kv(g, w):
        return _norm_proj(mem2, g, w.astype(BF16))

    def finish(y, u, x_in, g_mem, w_mem_kv, w_out, *, wy, zy_blk, zm_blk, qm_blk, final_g=None):
        return _out_proj(y, u, mem_kv(g_mem, w_mem_kv), x_in, w_out.astype(BF16), wy=wy,
                         zy_blk=zy_blk, zm_blk=zm_blk, qm_blk=qm_blk, seq=s, n_mem=n_mem,
                         final_g=final_g)

    q_w, k_w, v_w, f_w, qm_w, z_w = _cols(l0_w_in, (mw, mw, mw, N_HEADS, MEM_WIDTH, mw + MEM_WIDTH))
    w0 = _pack([(z_w[:, :mw], mw), (z_w[:, mw:], MEM_WIDTH), (qm_w, MEM_WIDTH), (q_w, mw),
                (k_w, mw), (v_w, mw)])
    u, f_arr = _norm_proj(x2, l0_norm, w0, w_extra=_pack([(f_w, N_HEADS)], multiple=LANES),
                          extra_dtype=F32)
    u3 = u.reshape(b, s, -1)
    base = (mw + 2 * MEM_WIDTH) // LANES
    c_t = _fox_gate(f_arr.reshape(b, s, LANES), l0_forget_bias, 0)
    y = _fox_attn(u3, c_t, q_blk=base // 2, k_blk=(base + N_HEADS) // 2,
                  v_blk=(base + 2 * N_HEADS) // 2)
    x2 = finish(y.reshape(m, mw), u, x2, l0_mem_norm, l0_w_mem_kv, l0_w_out, wy=mw, zy_blk=0,
                zm_blk=mw // MEM_WIDTH, qm_blk=mw // MEM_WIDTH + 1)

    iw = IDX_HEADS * IDX_DIM
    q_w, k_w, v_w, qi_w, ki_w, wi_w, qm_w, z_w = _cols(
        l1_w_in, (mw, HEAD_DIM, HEAD_DIM, iw, IDX_DIM, IDX_HEADS, MEM_WIDTH, mw + MEM_WIDTH))
    w1 = _pack([(z_w[:, :mw], mw), (q_w, mw), (qi_w, iw), (z_w[:, mw:], MEM_WIDTH),
                (qm_w, MEM_WIDTH), (k_w, HEAD_DIM), (v_w, HEAD_DIM), (ki_w, IDX_DIM),
                (wi_w, IDX_HEADS), (None, LANES - IDX_DIM - IDX_HEADS)])
    u = _norm_proj(x2, l1_norm, w1)
    u3 = u.reshape(b, s, -1)
    kv0 = (2 * mw + iw + 2 * MEM_WIDTH) // LANES
    y = _dsa_attn(u3, tabs128 + tabs64, q_blk=1, qi_blk=2 * mw // iw, k_blk=kv0, v_blk=kv0 + 1,
                  misc_blk=kv0 + 2)
    zoff = (2 * mw + iw) // MEM_WIDTH
    x2 = finish(y.reshape(m, mw), u, x2, l1_mem_norm, l1_w_mem_kv, l1_w_out, wy=mw, zy_blk=0,
                zm_blk=zoff, qm_blk=zoff + 1)

    gw = len(DILATED_PAIRS) * DIL_WIDTH
    q_w, k_w, v_w, qm_w, z_w = _cols(l2_w_in, (gw, gw, gw, MEM_WIDTH, DIL_WIDTH + MEM_WIDTH))

    def grp(w, g):
        return (w[:, g * DIL_WIDTH:(g + 1) * DIL_WIDTH], DIL_WIDTH)

    wa = _pack([grp(q_w, 0), (z_w[:, :DIL_WIDTH], DIL_WIDTH), grp(k_w, 0), grp(v_w, 0),
                (z_w[:, DIL_WIDTH:], MEM_WIDTH), (qm_w, MEM_WIDTH)])
    wb = _pack([grp(q_w, 1), grp(k_w, 1), grp(v_w, 1), grp(q_w, 2), grp(k_w, 2), grp(v_w, 2)])
    u = _norm_proj(x2, l2_norm, wa)
    ub = _norm_proj(x2, l2_norm, wb, out_dtype=F32)
    y = _dil_attn(u.reshape(b, s, -1), ub.reshape(b, s, -1), tabs128, q0_blk=0,
                  k0_blk=2 * DIL_HEADS, v0_blk=3 * DIL_HEADS)
    zoff = 4 * DIL_WIDTH // MEM_WIDTH
    x2 = finish(y.reshape(m, DIL_WIDTH), u, x2, l2_mem_norm, l2_w_mem_kv, l2_w_out, wy=DIL_WIDTH,
                zy_blk=1, zm_blk=zoff, qm_blk=zoff + 1)

    q_lora, kv_lora = l3_w_uq.shape[0], l3_w_ukv.shape[0]
    cq_w, ckv_w, kr_w, qm_w, z_w = _cols(l3_w_in, (q_lora, kv_lora, QK_ROPE, MEM_WIDTH,
                                                  mw + MEM_WIDTH))
    assert q_lora == MEM_WIDTH and kv_lora == MEM_WIDTH
    w3 = _pack([(z_w[:, :mw], mw), (z_w[:, mw:], MEM_WIDTH), (qm_w, MEM_WIDTH), (cq_w, q_lora),
                (ckv_w, kv_lora)])
    u, kr_arr = _norm_proj(x2, l3_norm, w3, w_extra=_pack([(kr_w, QK_ROPE)], multiple=LANES))
    cq_blk = (mw + 2 * MEM_WIDTH) // MEM_WIDTH
    uq = l3_w_uq.reshape(q_lora, N_HEADS, QK_NOPE + QK_ROPE)
    w_uq = _pack([(uq[:, :, :QK_NOPE].reshape(q_lora, -1), N_HEADS * QK_NOPE),
                  (uq[:, :, QK_NOPE:].reshape(q_lora, -1), N_HEADS * QK_ROPE)])
    ukv = l3_w_ukv.reshape(kv_lora, N_HEADS, QK_NOPE + HEAD_DIM)
    w_ukv = _pack([(ukv[:, :, :QK_NOPE].reshape(kv_lora, -1), N_HEADS * QK_NOPE),
                   (ukv[:, :, QK_NOPE:].reshape(kv_lora, -1), N_HEADS * HEAD_DIM)])
    qf = _norm_proj(u, l3_q_norm, w_uq, x_col_block=cq_blk)
    kvf = _norm_proj(u, l3_kv_norm, w_ukv, x_col_block=cq_blk + 1)
    y = _mla_attn(qf.reshape(b, s, -1), kvf.reshape(b, s, -1), kr_arr.reshape(b, s, LANES), tabs64,
                  kr_blk=0)
    x2 = finish(y.reshape(m, mw), u, x2, l3_mem_norm, l3_w_mem_kv, l3_w_out, wy=mw, zy_blk=0,
                zm_blk=mw // MEM_WIDTH, qm_blk=mw // MEM_WIDTH + 1, final_g=final_norm)
    return x2.reshape(b, s, d)
```

```python
import functools

import jax
import jax.numpy as jnp
from jax import lax
from jax.experimental import pallas as pl
from jax.experimental.pallas import tpu as pltpu

F32 = jnp.float32
BF16 = jnp.bfloat16
I32 = jnp.int32

EPS = 1e-6
ROPE_THETA = 10000.0
HEAD_DIM = 128
N_HEADS = 16
MEM_HEADS = 4
MEM_WIDTH = MEM_HEADS * HEAD_DIM
IDX_HEADS = 16
IDX_DIM = 64
TOPK_MAX = 256
DILATED_PAIRS = ((128, 1), (512, 4), (2048, 16))
DIL_HEADS = 6
DIL_WIDTH = DIL_HEADS * HEAD_DIM
QK_NOPE = 128
QK_ROPE = 64
LANES = 128
NEG = -1e30
INT_MIN = -(2 ** 31)
VMEM_LIMIT = 56 * 1024 * 1024

NT_DIMS = (((1,), (1,)), ((), ()))
LOG2E = 1.4426950408889634


def _cparams(n_axes):
    return pltpu.CompilerParams(
        dimension_semantics=("arbitrary",) * n_axes, vmem_limit_bytes=VMEM_LIMIT)


def _rope128(x, cos_f, sin_s):
    return x * cos_f + pltpu.roll(x, 64, 1) * sin_s


def _rope64(x, cos_f, sin_s):
    lane = lax.broadcasted_iota(I32, x.shape, 1)
    partner = jnp.where((lane & 32) == 0, pltpu.roll(x, 96, 1), pltpu.roll(x, 32, 1))
    return x * cos_f + partner * sin_s


def _rope_table_kernel(pos_ref, inv_ref, sgn_ref, cos_ref, sin_ref):
    ang = pos_ref[...] * inv_ref[...]
    cos_ref[...] = jnp.cos(ang)
    sin_ref[...] = jnp.sin(ang) * sgn_ref[...]


def _rope_tables(pos_b, dh):
    m = pos_b.shape[0]
    half = dh // 2
    inv = jnp.power(ROPE_THETA, -jnp.arange(half, dtype=F32) * 2.0 / dh)
    reps = LANES // half
    inv_l = jnp.tile(inv, reps).reshape(1, LANES)
    sgn = jnp.tile(jnp.concatenate([-jnp.ones((half,), F32), jnp.ones((half,), F32)]),
                   reps // 2).reshape(1, LANES)
    ts = min(512, m)
    return pl.pallas_call(
        _rope_table_kernel,
        grid=(m // ts,),
        in_specs=[pl.BlockSpec((ts, LANES), lambda i: (i, 0)),
                  pl.BlockSpec((1, LANES), lambda i: (0, 0)),
                  pl.BlockSpec((1, LANES), lambda i: (0, 0))],
        out_specs=[pl.BlockSpec((ts, LANES), lambda i: (i, 0))] * 2,
        out_shape=[jax.ShapeDtypeStruct((m, LANES), F32)] * 2,
        compiler_params=_cparams(1),
        name="rope_tables",
    )(pos_b, inv_l, sgn)


def _norm_proj_kernel(x_ref, g_ref, w_ref, *rest, has_extra):
    if has_extra:
        wx_ref, o_ref, ox_ref, h_ref = rest
    else:
        o_ref, h_ref = rest

    @pl.when(pl.program_id(1) == 0)
    def _():
        x = x_ref[...].astype(F32)
        ms = jnp.mean(x * x, axis=-1, keepdims=True)
        h_ref[...] = (x * lax.rsqrt(ms + EPS) * g_ref[...]).astype(BF16)
        if has_extra:
            ox_ref[...] = jnp.dot(h_ref[...], wx_ref[...],
                                  preferred_element_type=F32).astype(ox_ref.dtype)

    o_ref[...] = jnp.dot(h_ref[...], w_ref[...], preferred_element_type=F32).astype(o_ref.dtype)


def _norm_proj(x, g, w, *, x_col_block=0, out_dtype=BF16, w_extra=None, extra_dtype=BF16):
    m = x.shape[0]
    k, n = w.shape
    tm = min(1024, m)
    tn = 1024 if n % 1024 == 0 else 512
    assert m % tm == 0 and n % tn == 0
    has_extra = w_extra is not None
    in_specs = [pl.BlockSpec((tm, k), lambda i, j: (i, x_col_block)),
                pl.BlockSpec((1, k), lambda i, j: (0, 0)),
                pl.BlockSpec((k, tn), lambda i, j: (0, j))]
    out_specs = [pl.BlockSpec((tm, tn), lambda i, j: (i, j))]
    out_shape = [jax.ShapeDtypeStruct((m, n), out_dtype)]
    args = [x, g.reshape(1, k).astype(F32), w]
    if has_extra:
        in_specs.append(pl.BlockSpec((k, LANES), lambda i, j: (0, 0)))
        out_specs.append(pl.BlockSpec((tm, LANES), lambda i, j: (i, 0)))
        out_shape.append(jax.ShapeDtypeStruct((m, LANES), extra_dtype))
        args.append(w_extra)
    outs = pl.pallas_call(
        functools.partial(_norm_proj_kernel, has_extra=has_extra),
        grid=(m // tm, n // tn),
        in_specs=in_specs,
        out_specs=out_specs,
        out_shape=out_shape,
        scratch_shapes=[pltpu.VMEM((tm, k), BF16)],
        compiler_params=_cparams(2),
        name="norm_proj",
    )(*args)
    return outs if has_extra else outs[0]


def _out_proj_kernel(y_ref, zy_ref, zm_ref, qm_ref, mkv_ref, x_ref, w_ref, *rest, wy, final):
    if final:
        gf_ref, o_ref, gated_ref = rest
    else:
        o_ref, gated_ref = rest
    zy = zy_ref[...].astype(F32)
    gated_ref[:, :wy] = (y_ref[...].astype(F32) * (zy * jax.nn.sigmoid(zy))).astype(BF16)
    scale = HEAD_DIM ** -0.5
    for h in range(MEM_HEADS):
        lo, hi = h * HEAD_DIM, (h + 1) * HEAD_DIM
        s = lax.dot_general(qm_ref[:, lo:hi], mkv_ref[:, lo:hi], NT_DIMS,
                            preferred_element_type=F32) * scale
        m = jnp.max(s, axis=-1, keepdims=True)
        p = jnp.exp(s - m)
        l = jnp.sum(p, axis=-1, keepdims=True)
        o = jnp.dot(p.astype(BF16), mkv_ref[:, MEM_WIDTH + lo:MEM_WIDTH + hi],
                    preferred_element_type=F32) / l
        zm = zm_ref[:, lo:hi].astype(F32)
        gated_ref[:, wy + lo:wy + hi] = (o * (zm * jax.nn.sigmoid(zm))).astype(BF16)
    out = x_ref[...] + jnp.dot(gated_ref[...], w_ref[...], preferred_element_type=F32)
    if final:
        ms = jnp.mean(out * out, axis=-1, keepdims=True)
        out = out * lax.rsqrt(ms + EPS) * gf_ref[...]
    o_ref[...] = out


def _out_proj(y, u, mkv, x, w_out, *, wy, zy_blk, zm_blk, qm_blk, seq, n_mem, final_g=None):
    m, d = x.shape
    tm = min(512, seq)
    final = final_g is not None
    in_specs = [
        pl.BlockSpec((tm, wy), lambda i: (i, 0)),
        pl.BlockSpec((tm, wy), lambda i: (i, zy_blk)),
        pl.BlockSpec((tm, MEM_WIDTH), lambda i: (i, zm_blk)),
        pl.BlockSpec((tm, MEM_WIDTH), lambda i: (i, qm_blk)),
        pl.BlockSpec((n_mem, 2 * MEM_WIDTH), lambda i: ((i * tm) // seq, 0)),
        pl.BlockSpec((tm, d), lambda i: (i, 0)),
        pl.BlockSpec((wy + MEM_WIDTH, d), lambda i: (0, 0), pipeline_mode=pl.Buffered(1)),
    ]
    args = [y, u, u, u, mkv, x, w_out]
    if final:
        in_specs.append(pl.BlockSpec((1, d), lambda i: (0, 0)))
        args.append(final_g.reshape(1, d).astype(F32))
    return pl.pallas_call(
        functools.partial(_out_proj_kernel, wy=wy, final=final),
        grid=(m // tm,),
        in_specs=in_specs,
        out_specs=pl.BlockSpec((tm, d), lambda i: (i, 0)),
        out_shape=jax.ShapeDtypeStruct((m, d), F32),
        scratch_shapes=[pltpu.VMEM((tm, wy + MEM_WIDTH), BF16)],
        compiler_params=_cparams(1),
        name="out_proj",
    )(*args)


def _fox_gate_kernel(f_ref, b_ref, c_ref, carry_ref):
    @pl.when(pl.program_id(1) == 0)
    def _():
        carry_ref[...] = jnp.zeros_like(carry_ref)

    f = f_ref[...] + b_ref[...]
    log_f = jnp.minimum(f, 0.0) - jnp.log(1.0 + jnp.exp(-jnp.abs(f)))
    ts = log_f.shape[0]
    r = lax.broadcasted_iota(I32, (ts, ts), 0)
    c = lax.broadcasted_iota(I32, (ts, ts), 1)
    lower = jnp.where(c <= r, 1.0, 0.0).astype(F32)
    cs = jnp.dot(lower, log_f, precision=lax.Precision.HIGHEST,
                 preferred_element_type=F32) + carry_ref[...]
    c_ref[...] = cs * LOG2E
    carry_ref[...] = cs[ts - 1:ts, :]


def _fox_gate(f3, bias):
    b, s, _ = f3.shape
    ts = min(256, s)
    bias_l = jnp.zeros((1, LANES), F32).at[0, :N_HEADS].set(bias.astype(F32))
    return pl.pallas_call(
        _fox_gate_kernel,
        grid=(b, s // ts),
        in_specs=[pl.BlockSpec((None, ts, LANES), lambda bi, j: (bi, j, 0)),
                  pl.BlockSpec((1, LANES), lambda bi, j: (0, 0))],
        out_specs=pl.BlockSpec((None, ts, LANES), lambda bi, j: (bi, j, 0)),
        out_shape=jax.ShapeDtypeStruct((b, s, LANES), F32),
        scratch_shapes=[pltpu.VMEM((1, LANES), F32)],
        compiler_params=_cparams(2),
        name="fox_gate",
    )(f3, bias_l)


STRIP = 32
RG = 64


def _softmax_strips(t_ref, p_ref, m_ref, l_ref, alpha_ref, adjust):
    tq, tk = t_ref.shape
    for r0 in range(0, tq, min(STRIP, tq)):
        rs = pl.ds(r0, min(STRIP, tq))
        blocks = [adjust(r0, c, t_ref[rs, c * LANES:(c + 1) * LANES]) for c in range(tk // LANES)]
        live = [x for x in blocks if x is not None]
        mx = live[0]
        for x in live[1:]:
            mx = jnp.maximum(mx, x)
        m_old = m_ref[rs, :]
        m_new = jnp.maximum(m_old, jnp.max(mx, axis=-1, keepdims=True))
        alpha = jnp.exp2(m_old - m_new)
        ps = [None if x is None else jnp.exp2(x - m_new) for x in blocks]
        live = [x for x in ps if x is not None]
        sm = live[0]
        for x in live[1:]:
            sm = sm + x
        l_ref[rs, :] = alpha * l_ref[rs, :] + jnp.sum(sm, axis=-1, keepdims=True)
        m_ref[rs, :] = m_new
        alpha_ref[rs, :] = alpha
        for c, x in enumerate(ps):
            p_ref[rs, c * LANES:(c + 1) * LANES] = (
                jnp.zeros((min(STRIP, tq), LANES), BF16) if x is None else x.astype(BF16))


def _flash_scratch(n_streams, tq):
    return [pltpu.VMEM((n_streams, tq, tq), F32), pltpu.VMEM((n_streams, tq, tq), BF16),
            pltpu.VMEM((n_streams, tq, LANES), F32), pltpu.VMEM((n_streams, tq, LANES), F32),
            pltpu.VMEM((n_streams, tq, LANES), F32), pltpu.VMEM((n_streams, tq, HEAD_DIM), F32)]


def _flash_init(scratch):
    _, _, m_ref, l_ref, _, acc_ref = scratch
    m_ref[...] = jnp.full(m_ref.shape, NEG, F32)
    l_ref[...] = jnp.zeros(l_ref.shape, F32)
    acc_ref[...] = jnp.zeros(acc_ref.shape, F32)


def _flash_chunk(n_streams, j, raw_scores, values, to_logits, scratch, *, diagonal):
    t_ref, p_ref, m_ref, l_ref, alpha_ref, acc_ref = scratch
    strip = min(STRIP, t_ref.shape[1])
    row = lax.broadcasted_iota(I32, (strip, LANES), 0)
    col = lax.broadcasted_iota(I32, (strip, LANES), 1)
    for a in range(n_streams):
        t_ref[a] = raw_scores(a, j)
    for a in range(n_streams):
        def adjust(r0, c, x, a=a):
            t = to_logits(a, j, r0, c, x)
            if not diagonal or c * LANES + LANES - 1 <= r0:
                return t
            if c * LANES > r0 + strip - 1:
                return None
            return jnp.where(col + c * LANES <= row + r0, t, NEG)

        _softmax_strips(t_ref.at[a], p_ref.at[a], m_ref.at[a], l_ref.at[a], alpha_ref.at[a],
                        adjust)
    for a in range(n_streams):
        acc_ref[a] = alpha_ref[a] * acc_ref[a] + jnp.dot(p_ref[a], values(a, j),
                                                         preferred_element_type=F32)


def _flash_finish(n_streams, scratch):
    _, _, _, l_ref, _, acc_ref = scratch
    return [acc_ref[a] / l_ref[a] for a in range(n_streams)]


def _causal_flash(i, n_streams, raw_scores, values, to_logits, scratch):
    _flash_init(scratch)

    def body(j, _):
        _flash_chunk(n_streams, j, raw_scores, values, to_logits, scratch, diagonal=False)
        return 0

    lax.fori_loop(0, i, body, 0)
    _flash_chunk(n_streams, i, raw_scores, values, to_logits, scratch, diagonal=True)
    return _flash_finish(n_streams, scratch)


def _flash_t_scratch(n_streams, tq):
    return [pltpu.VMEM((2, n_streams, tq, tq), F32), pltpu.VMEM((2, n_streams, tq, tq), BF16),
            pltpu.VMEM((n_streams, 1, tq), F32), pltpu.VMEM((n_streams, 1, tq), F32),
            pltpu.VMEM((n_streams, 1, tq), F32), pltpu.VMEM((n_streams, HEAD_DIM, tq), F32)]


def _flash_t_chunk(n_streams, j, raw_scores_t, values_t, to_logits_t, scratch, *, diagonal):
    t2_ref, p2_ref, m_ref, l_ref, alpha_ref, acc_ref = scratch
    t_ref, p_ref = t2_ref.at[j % 2], p2_ref.at[j % 2]
    tk, tq = t_ref.shape[1:]
    for a in range(n_streams):
        t_ref[a] = raw_scores_t(a, j)
    for a in range(n_streams):
        for c in range(tq // LANES):
            cs = pl.ds(c * LANES, LANES)
            parts = [(0, tk, False)]
            if diagonal:
                parts = ([(0, c * LANES, False)] if c else []) + [(c * LANES, LANES, True)]
            groups = [(r0 + g, min(RG, nr - g), band) for r0, nr, band in parts
                      for g in range(0, nr, RG)]
            mx8 = None
            for g0, ng, band in groups:
                raw = t_ref[a, pl.ds(g0, ng), cs]
                x = to_logits_t(a, j, g0, ng, raw)
                if band:
                    kk = lax.broadcasted_iota(I32, (ng, LANES), 0) + (g0 - c * LANES)
                    qq = lax.broadcasted_iota(I32, (ng, LANES), 1)
                    x = jnp.where(kk <= qq, x, NEG)
                if x is not raw:
                    t_ref[a, pl.ds(g0, ng), cs] = x
                x8 = jnp.max(x.reshape(ng // 8, 8, LANES), axis=0)
                mx8 = x8 if mx8 is None else jnp.maximum(mx8, x8)
            m_old = m_ref[a, :, cs]
            m_new = jnp.maximum(m_old, jnp.max(mx8, axis=0, keepdims=True))
            alpha = jnp.exp2(m_old - m_new)
            s8 = None
            for g0, ng, _ in groups:
                p = jnp.exp2(t_ref[a, pl.ds(g0, ng), cs] - m_new)
                p_ref[a, pl.ds(g0, ng), cs] = p.astype(BF16)
                p8 = jnp.sum(p.reshape(ng // 8, 8, LANES), axis=0)
                s8 = p8 if s8 is None else s8 + p8
            tot = jnp.sum(s8, axis=0, keepdims=True)
            done = parts[-1][0] + parts[-1][1]
            if done < tk:
                p_ref[a, pl.ds(done, tk - done), cs] = jnp.zeros((tk - done, LANES), BF16)
            l_ref[a, :, cs] = alpha * l_ref[a, :, cs] + tot
            m_ref[a, :, cs] = m_new
            alpha_ref[a, :, cs] = alpha
    for a in range(n_streams):
        acc_ref[a] = alpha_ref[a] * acc_ref[a] + jnp.dot(values_t(a, j), p_ref[a],
                                                         preferred_element_type=F32)


def _causal_flash_t(i, n_tiles, n_streams, raw_scores_t, values_t, to_logits_t, scratch):
    _flash_init(scratch)
    for ii in range(n_tiles):
        @pl.when(i == ii)
        def _(ii=ii):
            for j in range(ii):
                _flash_t_chunk(n_streams, j, raw_scores_t, values_t, to_logits_t, scratch,
                               diagonal=False)
            _flash_t_chunk(n_streams, ii, raw_scores_t, values_t, to_logits_t, scratch,
                           diagonal=True)
    _, _, _, l_ref, _, acc_ref = scratch
    return [(acc_ref[a] / l_ref[a]).T for a in range(n_streams)]


def _transpose_chunks(dst_ref, src_ref, lo, hi, tk):
    for j in range(src_ref.shape[0] // tk):
        dst_ref[j] = src_ref[j * tk:(j + 1) * tk, lo:hi].astype(F32).T.astype(dst_ref.dtype)


def _fox_attn_kernel(q_ref, k_ref, v_ref, c_ref, o_ref, qaug_ref, kaug_ref, vt_ref, *scratch,
                     tq, n_tiles):
    p_idx, i = pl.program_id(1), pl.program_id(2)
    qk_scale = HEAD_DIM ** -0.5 * LOG2E

    @pl.when(i == 0)
    def _():
        lane = lax.broadcasted_iota(I32, c_ref.shape, 1)
        for a in range(2):
            _transpose_chunks(vt_ref.at[a], v_ref, a * HEAD_DIM, (a + 1) * HEAD_DIM, tq)
            c = jnp.sum(jnp.where(lane == 2 * p_idx + a, c_ref[...], 0.0), axis=-1, keepdims=True)
            c1 = c.astype(BF16).astype(F32)
            c2 = (c - c1).astype(BF16).astype(F32)
            c3 = (c - c1 - c2).astype(BF16).astype(F32)
            aug = jnp.where(lane == 0, -c1, jnp.where(lane == 1, -c2,
                                                       jnp.where(lane == 2, -c3, 0.0)))
            kaug_ref[a, :, :HEAD_DIM] = k_ref[:, a * HEAD_DIM:(a + 1) * HEAD_DIM]
            kaug_ref[a, :, HEAD_DIM:] = aug.astype(BF16)

    ones = jnp.where(lax.broadcasted_iota(I32, (tq, LANES), 1) < 3, 1.0, 0.0).astype(BF16)
    for a in range(2):
        qa = q_ref[:, a * HEAD_DIM:(a + 1) * HEAD_DIM].astype(F32) * qk_scale
        qaug_ref[a, :, :HEAD_DIM] = qa.astype(BF16)
        qaug_ref[a, :, HEAD_DIM:] = ones

    def raw_scores_t(a, j):
        return lax.dot_general(kaug_ref[a, j * tq:(j + 1) * tq, :], qaug_ref[a], NT_DIMS,
                               preferred_element_type=F32)

    def to_logits_t(a, j, r0, nr, x):
        return x

    def values_t(a, j):
        return vt_ref[a, j]

    outs = _causal_flash_t(i, n_tiles, 2, raw_scores_t, values_t, to_logits_t, scratch)
    for a in range(2):
        o_ref[:, a * HEAD_DIM:(a + 1) * HEAD_DIM] = outs[a].astype(o_ref.dtype)


def _fox_attn(u3, c_nat, *, q_blk, k_blk, v_blk):
    b, s, _ = u3.shape
    tq = min(512, s)
    nq = s // tq
    pw = 2 * HEAD_DIM
    return pl.pallas_call(
        functools.partial(_fox_attn_kernel, tq=tq, n_tiles=nq),
        grid=(b, N_HEADS // 2, nq),
        in_specs=[pl.BlockSpec((None, tq, pw), lambda bi, p, i: (bi, i, q_blk + p)),
                  pl.BlockSpec((None, s, pw), lambda bi, p, i: (bi, 0, k_blk + p)),
                  pl.BlockSpec((None, s, pw), lambda bi, p, i: (bi, 0, v_blk + p)),
                  pl.BlockSpec((None, s, LANES), lambda bi, p, i: (bi, 0, 0))],
        out_specs=pl.BlockSpec((None, tq, pw), lambda bi, p, i: (bi, i, p)),
        out_shape=jax.ShapeDtypeStruct((b, s, N_HEADS * HEAD_DIM), BF16),
        scratch_shapes=[pltpu.VMEM((2, tq, 2 * LANES), BF16), pltpu.VMEM((2, s, 2 * LANES), BF16),
                        pltpu.VMEM((2, nq, HEAD_DIM, tq), BF16)] + _flash_t_scratch(2, tq),
        compiler_params=_cparams(3),
        name="fox_attn",
    )(u3, u3, u3, c_nat)


def _dsa_kprep_kernel(k_ref, mk_ref, c128_ref, s128_ref, c64_ref, s64_ref,
                      kr_ref, klo_ref, khi_ref):
    kr_ref[...] = _rope128(k_ref[...].astype(F32), c128_ref[...], s128_ref[...]).astype(BF16)
    ki = _rope64(mk_ref[...].astype(F32), c64_ref[...], s64_ref[...])
    lane = lax.broadcasted_iota(I32, ki.shape, 1)
    lo = jnp.where(lane < IDX_DIM, ki, 0.0)
    klo_ref[...] = lo.astype(BF16)
    khi_ref[...] = pltpu.roll(lo, IDX_DIM, 1).astype(BF16)


def _dsa_kprep(u3, tabs, *, k_blk, misc_blk):
    b, s, _ = u3.shape
    ts = min(512, s)
    c128, s128, c64, s64 = tabs

    def spec(blk):
        return pl.BlockSpec((None, ts, LANES), lambda bi, i: (bi, i, blk))

    return pl.pallas_call(
        _dsa_kprep_kernel,
        grid=(b, s // ts),
        in_specs=[spec(k_blk), spec(misc_blk), spec(0), spec(0), spec(0), spec(0)],
        out_specs=[spec(0)] * 3,
        out_shape=[jax.ShapeDtypeStruct((b, s, LANES), BF16)] * 3,
        compiler_params=_cparams(2),
        name="dsa_kprep",
    )(u3, u3, c128, s128, c64, s64)


def _dsa_kernel(q_ref, qi_ref, mq_ref, v_ref, kr_ref, klo_ref, khi_ref,
                cq128_ref, sq128_ref, cq64_ref, sq64_ref, y_in_ref,
                o_ref,
                qr_ref, qir_ref, w_ref, sc_ref, t4_ref, yo_ref, *flash,
                tq, kl, q_off, n_sel):
    del y_in_ref
    t0 = q_off + pl.program_id(1) * tq
    qk_scale = HEAD_DIM ** -0.5 * LOG2E
    idx_scale = (IDX_DIM ** -0.5) * (IDX_HEADS ** -0.5)
    ck = min(512, kl)
    n_chunks = kl // ck
    strip = min(STRIP, tq)

    cq128, sq128 = cq128_ref[...], sq128_ref[...]
    for h in range(N_HEADS):
        qh = q_ref[:, h * HEAD_DIM:(h + 1) * HEAD_DIM].astype(F32)
        qr_ref[h] = (_rope128(qh, cq128, sq128) * qk_scale).astype(BF16)
    cq64, sq64 = cq64_ref[...], sq64_ref[...]
    for a in range(IDX_HEADS // 2):
        qir_ref[a] = _rope64(qi_ref[:, a * LANES:(a + 1) * LANES].astype(F32),
                             cq64, sq64).astype(BF16)
    w_ref[...] = mq_ref[...].astype(F32) * idx_scale

    row = lax.broadcasted_iota(I32, (strip, LANES), 0)
    col = lax.broadcasted_iota(I32, (strip, LANES), 1)
    group = 4

    def idx_chunk(j, _):
        off = pl.multiple_of(j * ck, ck)
        k_lo, k_hi = klo_ref[pl.ds(off, ck), :], khi_ref[pl.ds(off, ck), :]
        for g in range(IDX_HEADS // group):
            for a2 in range(group // 2):
                x = qir_ref[g * (group // 2) + a2]
                t4_ref[2 * a2] = lax.dot_general(x, k_lo, NT_DIMS, preferred_element_type=F32)
                t4_ref[2 * a2 + 1] = lax.dot_general(x, k_hi, NT_DIMS,
                                                     preferred_element_type=F32)
            last = g == IDX_HEADS // group - 1
            for r0 in range(0, tq, strip):
                rs = pl.ds(r0, strip)
                w_rows = w_ref[rs, :]
                ws = [jnp.broadcast_to(w_rows[:, IDX_DIM + g * group + u:IDX_DIM + g * group + u + 1],
                                       (strip, LANES)) for u in range(group)]
                for c in range(ck // LANES):
                    cs = pl.ds(c * LANES, LANES)
                    acc = ws[0] * jnp.maximum(t4_ref[0, rs, cs], 0.0)
                    for u in range(1, group):
                        acc = acc + ws[u] * jnp.maximum(t4_ref[u, rs, cs], 0.0)
                    if g > 0:
                        acc = acc + sc_ref[j, rs, cs]
                    if last:
                        causal = col + (off + c * LANES) <= row + (t0 + r0)
                        acc = jnp.where(causal, acc, -jnp.inf)
                    sc_ref[j, rs, cs] = acc
        return 0

    lax.fori_loop(0, n_chunks, idx_chunk, 0)

    shape3 = (n_chunks, tq, ck)
    col3 = lax.broadcasted_iota(I32, shape3, 0) * ck + lax.broadcasted_iota(I32, shape3, 2)
    causal3 = col3 <= lax.broadcasted_iota(I32, shape3, 1) + t0

    def count(x):
        return jnp.sum(jnp.sum(x, axis=0), axis=-1, keepdims=True)

    def key_to_f32(key):
        return pltpu.bitcast(jnp.where(key < 0, key ^ jnp.int32(0x7FFFFFFF), key), F32)

    def thr_body(it, key):
        cand = key + (jnp.int32(1) << (31 - it))
        c = count(jnp.where(sc_ref[...] >= key_to_f32(cand), 1.0, 0.0))
        return jnp.where(c >= n_sel, cand, key)

    thr_key = lax.fori_loop(0, 32, thr_body, jnp.full((tq, 1), INT_MIN, I32))
    take_all = thr_key == INT_MIN
    thr = key_to_f32(thr_key)
    score = sc_ref[...]
    need = n_sel - count(jnp.where(score > thr, 1.0, 0.0))
    surplus = jnp.where(take_all, 0.0, count(jnp.where(score == thr, 1.0, 0.0)) - need)
    has_surplus = jnp.max(surplus) > 0.0

    n_bits = kl.bit_length()

    def tie_body(it, jm):
        cand = jm + (jnp.int32(1) << (n_bits - 1 - it))
        f = count(jnp.where(sc_ref[...] == thr, jnp.where(col3 < cand, 1.0, 0.0), 0.0))
        return jnp.where(jnp.logical_and(f < need, cand <= kl), cand, jm)

    jm = lax.fori_loop(0, jnp.where(has_surplus, n_bits, 0), tie_body,
                       jnp.broadcast_to(jnp.where(has_surplus, 0, kl), (tq, 1)).astype(I32))
    keep = jnp.where(jnp.logical_or(score > thr, take_all), 1.0,
                     jnp.where(score == thr, jnp.where(col3 <= jm, 1.0, 0.0), 0.0))
    sc_ref[...] = jnp.where(causal3, jnp.where(keep > 0.0, 0.0, NEG), NEG)

    def raw_scores(h0):
        def f(a, j):
            return lax.dot_general(qr_ref[h0 + a], kr_ref[j * ck:(j + 1) * ck, :], NT_DIMS,
                                   preferred_element_type=F32)
        return f

    def values(a, j):
        return v_ref[j * ck:(j + 1) * ck, :]

    def to_logits(a, j, r0, c, x):
        return x + sc_ref[j, r0:r0 + strip, c * LANES:(c + 1) * LANES]

    def pair_body(pp, _):
        h0 = 2 * pp
        _flash_init(flash)
        for j in range(n_chunks):
            _flash_chunk(2, j, raw_scores(h0), values, to_logits, flash, diagonal=False)
        outs = _flash_finish(2, flash)
        for a in range(2):
            yo_ref[h0 + a] = outs[a].astype(yo_ref.dtype)
        return 0

    lax.fori_loop(0, N_HEADS // 2, pair_body, 0)
    for h in range(N_HEADS):
        o_ref[:, h * HEAD_DIM:(h + 1) * HEAD_DIM] = yo_ref[h]


def _dsa_group(u3, kprep, tabs, y, *, q_off, rows, kl, n_sel, q_blk, v_blk, qi_blk, misc_blk):
    b, s, _ = u3.shape
    tq = rows
    assert kl % min(512, kl) == 0 and tq == min(512, kl)
    qb0 = q_off // tq
    c128, s128, c64, s64 = tabs
    kr, klo, khi = kprep
    qw = N_HEADS * HEAD_DIM
    iw = IDX_HEADS * IDX_DIM
    ck = min(512, kl)

    def qspec(width, blk):
        return pl.BlockSpec((None, tq, width), lambda bi, i: (bi, qb0 + i, blk))

    def kspec(blk):
        return pl.BlockSpec((None, kl, LANES), lambda bi, i: (bi, 0, blk))

    return pl.pallas_call(
        functools.partial(_dsa_kernel, tq=tq, kl=kl, q_off=q_off, n_sel=n_sel),
        grid=(b, rows // tq),
        in_specs=[qspec(qw, q_blk), qspec(iw, qi_blk), qspec(LANES, misc_blk),
                  kspec(v_blk), kspec(0), kspec(0), kspec(0),
                  qspec(LANES, 0), qspec(LANES, 0), qspec(LANES, 0), qspec(LANES, 0),
                  pl.BlockSpec(memory_space=pl.ANY)],
        out_specs=pl.BlockSpec((None, tq, qw), lambda bi, i: (bi, qb0 + i, 0)),
        out_shape=jax.ShapeDtypeStruct((b, s, qw), BF16),
        input_output_aliases={11: 0},
        scratch_shapes=[pltpu.VMEM((N_HEADS, tq, HEAD_DIM), BF16),
                        pltpu.VMEM((IDX_HEADS // 2, tq, LANES), BF16),
                        pltpu.VMEM((tq, LANES), F32),
                        pltpu.VMEM((kl // ck, tq, ck), F32),
                        pltpu.VMEM((4, tq, ck), F32),
                        pltpu.VMEM((N_HEADS, tq, HEAD_DIM), BF16)] + _flash_scratch(2, tq),
        compiler_params=_cparams(2),
        name="dsa_attn",
    )(u3, u3, u3, u3, kr, klo, khi, c128, s128, c64, s64, y)


def _dsa_attn(u3, tabs, *, k_blk, misc_blk, **blks):
    b, s, _ = u3.shape
    n_sel = min(TOPK_MAX, s // 4)
    rows = min(512, s)
    kprep = _dsa_kprep(u3, tabs, k_blk=k_blk, misc_blk=misc_blk)
    y = jnp.zeros((b, s, N_HEADS * HEAD_DIM), BF16)
    for q_off in range(0, s, rows):
        y = _dsa_group(u3, kprep, tabs, y, q_off=q_off, rows=rows, kl=q_off + rows, n_sel=n_sel,
                       misc_blk=misc_blk, **blks)
    return y


def _dil_kernel(q0_ref, k0_ref, v0_ref, q1_ref, k1_ref, v1_ref, q2_ref, k2_ref, v2_ref,
                cos_ref, sin_ref, y_ref, qr_ref, kr_ref, o_ref, lse_ref, *, seq):
    qk_scale = HEAD_DIM ** -0.5 * LOG2E
    cos_f, sin_s = cos_ref[...], sin_ref[...]
    groups = ((q0_ref, k0_ref, v0_ref), (q1_ref, k1_ref, v1_ref), (q2_ref, k2_ref, v2_ref))
    for g, (q_ref, k_ref, _) in enumerate(groups):
        qr_ref[g] = _rope128(q_ref[...].astype(F32), cos_f, sin_s)
        kr_ref[g] = _rope128(k_ref[...].astype(F32), cos_f, sin_s)

    for g, (window, dil) in enumerate(DILATED_PAIRS):
        v_ref = groups[g][2]
        sub = seq // dil
        qb = min(window // dil, sub)
        row = lax.broadcasted_iota(I32, (qb, qb), 0)
        col = lax.broadcasted_iota(I32, (qb, qb), 1)
        mask_cur = col <= row
        mask_prev = col >= row
        for r in range(dil):
            for i in range(sub // qb):
                def rows(blk, r=r, dil=dil, qb=qb):
                    start = r + dil * qb * blk
                    return pl.ds(start, qb) if dil == 1 else pl.ds(start, qb, stride=dil)

                q = qr_ref[g, rows(i), :].astype(BF16)
                k_cur = kr_ref[g, rows(i), :].astype(BF16)
                t_cur = lax.dot_general(q, k_cur, NT_DIMS, preferred_element_type=F32) * qk_scale
                t_cur = jnp.where(mask_cur, t_cur, NEG)
                m = jnp.max(t_cur, axis=-1, keepdims=True)
                if i > 0:
                    k_prev = kr_ref[g, rows(i - 1), :].astype(BF16)
                    t_prev = lax.dot_general(q, k_prev, NT_DIMS,
                                             preferred_element_type=F32) * qk_scale
                    t_prev = jnp.where(mask_prev, t_prev, NEG)
                    m = jnp.maximum(m, jnp.max(t_prev, axis=-1, keepdims=True))
                p_cur = jnp.exp2(t_cur - m)
                l = jnp.sum(p_cur, axis=-1, keepdims=True)
                acc = jnp.dot(p_cur.astype(BF16), v_ref[rows(i), :].astype(BF16),
                              preferred_element_type=F32)
                if i > 0:
                    p_prev = jnp.exp2(t_prev - m)
                    l = l + jnp.sum(p_prev, axis=-1, keepdims=True)
                    acc = acc + jnp.dot(p_prev.astype(BF16), v_ref[rows(i - 1), :].astype(BF16),
                                        preferred_element_type=F32)
                o_ref[g, rows(i), :] = acc / l
                lse_ref[g, rows(i), :] = jnp.broadcast_to(m + jnp.log2(l), (qb, HEAD_DIM))

    l0, l1, l2 = lse_ref[0], lse_ref[1], lse_ref[2]
    m = jnp.maximum(jnp.maximum(l0, l1), l2)
    e0, e1, e2 = jnp.exp2(l0 - m), jnp.exp2(l1 - m), jnp.exp2(l2 - m)
    y = (e0 * o_ref[0] + e1 * o_ref[1] + e2 * o_ref[2]) / (e0 + e1 + e2)
    y_ref[...] = y.astype(y_ref.dtype)


def _dil_attn(ua3, ub3, tabs128, *, q0_blk, k0_blk, v0_blk):
    b, s, _ = ua3.shape
    for window, dil in DILATED_PAIRS:
        assert s % dil == 0 and (s // dil) % min(window // dil, s // dil) == 0
    cos_t, sin_t = tabs128

    def spec(blk0):
        return pl.BlockSpec((None, s, HEAD_DIM), lambda bi, a: (bi, 0, blk0 + a))

    tab = pl.BlockSpec((None, s, LANES), lambda bi, a: (bi, 0, 0))
    return pl.pallas_call(
        functools.partial(_dil_kernel, seq=s),
        grid=(b, DIL_HEADS),
        in_specs=[spec(q0_blk), spec(k0_blk), spec(v0_blk)]
        + [spec(j * DIL_HEADS) for j in range(6)] + [tab, tab],
        out_specs=pl.BlockSpec((None, s, HEAD_DIM), lambda bi, a: (bi, 0, a)),
        out_shape=jax.ShapeDtypeStruct((b, s, DIL_WIDTH), BF16),
        scratch_shapes=[pltpu.VMEM((3, s, HEAD_DIM), F32)] * 4,
        compiler_params=_cparams(2),
        name="dilated_attn",
    )(ua3, ua3, ua3, ub3, ub3, ub3, ub3, ub3, ub3, cos_t, sin_t)


def _mla_kernel(qn_ref, qr_ref, kn_ref, kr_ref, v_ref, cq_ref, sq_ref, ck_ref, sk_ref,
                o_ref, qcat_ref, kcat_ref, vt_ref, *scratch, tq, n_tiles):
    i = pl.program_id(2)
    qk_scale = (QK_NOPE + QK_ROPE) ** -0.5 * LOG2E
    cat = QK_NOPE + QK_ROPE

    @pl.when(i == 0)
    def _():
        k_rope = _rope64(kr_ref[...].astype(F32), ck_ref[...],
                         sk_ref[...])[:, :QK_ROPE].astype(BF16)
        for a in range(2):
            kcat_ref[a, :, :QK_NOPE] = kn_ref[:, a * HEAD_DIM:(a + 1) * HEAD_DIM]
            kcat_ref[a, :, QK_NOPE:cat] = k_rope
            kcat_ref[a, :, cat:] = jnp.zeros((kcat_ref.shape[1], 2 * LANES - cat), BF16)
            _transpose_chunks(vt_ref.at[a], v_ref, a * HEAD_DIM, (a + 1) * HEAD_DIM, tq)

    q_rope = (_rope64(qr_ref[...].astype(F32), cq_ref[...], sq_ref[...]) * qk_scale).astype(BF16)
    for a in range(2):
        q_nope = qn_ref[:, a * HEAD_DIM:(a + 1) * HEAD_DIM].astype(F32) * qk_scale
        qcat_ref[a, :, :QK_NOPE] = q_nope.astype(BF16)
        qcat_ref[a, :, QK_NOPE:cat] = q_rope[:, a * QK_ROPE:(a + 1) * QK_ROPE]
        qcat_ref[a, :, cat:] = jnp.zeros((tq, 2 * LANES - cat), BF16)

    def raw_scores_t(a, j):
        return lax.dot_general(kcat_ref[a, j * tq:(j + 1) * tq, :], qcat_ref[a], NT_DIMS,
                               preferred_element_type=F32)

    def to_logits_t(a, j, r0, nr, x):
        return x

    def values_t(a, j):
        return vt_ref[a, j]

    outs = _causal_flash_t(i, n_tiles, 2, raw_scores_t, values_t, to_logits_t, scratch)
    for a in range(2):
        o_ref[:, a * HEAD_DIM:(a + 1) * HEAD_DIM] = outs[a].astype(o_ref.dtype)


def _mla_attn(qf3, kvf3, u3, tabs64, *, kr_blk):
    b, s, _ = qf3.shape
    tq = min(512, s)
    pw = 2 * HEAD_DIM
    n_pairs = N_HEADS // 2
    cos_t, sin_t = tabs64
    return pl.pallas_call(
        functools.partial(_mla_kernel, tq=tq, n_tiles=s // tq),
        grid=(b, n_pairs, s // tq),
        in_specs=[pl.BlockSpec((None, tq, pw), lambda bi, p, i: (bi, i, p)),
                  pl.BlockSpec((None, tq, LANES), lambda bi, p, i: (bi, i, N_HEADS + p)),
                  pl.BlockSpec((None, s, pw), lambda bi, p, i: (bi, 0, p)),
                  pl.BlockSpec((None, s, LANES), lambda bi, p, i: (bi, 0, kr_blk)),
                  pl.BlockSpec((None, s, pw), lambda bi, p, i: (bi, 0, n_pairs + p)),
                  pl.BlockSpec((None, tq, LANES), lambda bi, p, i: (bi, i, 0)),
                  pl.BlockSpec((None, tq, LANES), lambda bi, p, i: (bi, i, 0)),
                  pl.BlockSpec((None, s, LANES), lambda bi, p, i: (bi, 0, 0)),
                  pl.BlockSpec((None, s, LANES), lambda bi, p, i: (bi, 0, 0))],
        out_specs=pl.BlockSpec((None, tq, pw), lambda bi, p, i: (bi, i, p)),
        out_shape=jax.ShapeDtypeStruct((b, s, N_HEADS * HEAD_DIM), BF16),
        scratch_shapes=[pltpu.VMEM((2, tq, 2 * LANES), BF16),
                        pltpu.VMEM((2, s, 2 * LANES), BF16),
                        pltpu.VMEM((2, s // tq, HEAD_DIM, tq), BF16)] + _flash_t_scratch(2, tq),
        compiler_params=_cparams(3),
        name="mla_attn",
    )(qf3, qf3, kvf3, u3, kvf3, cos_t, sin_t, cos_t, sin_t)


def _pack(parts, multiple=512):
    k = next(p.shape[0] for p, _ in parts if p is not None)
    cols = [jnp.zeros((k, w), BF16) if p is None else p.astype(BF16) for p, w in parts]
    n = sum(w for _, w in parts)
    pad = (-n) % multiple
    if pad:
        cols.append(jnp.zeros((k, pad), BF16))
    return jnp.concatenate(cols, axis=1)


def _cols(w, sizes):
    out, acc = [], 0
    for sz in sizes:
        out.append(w[:, acc:acc + sz])
        acc += sz
    return out


def kernel(x, mem, positions, l0_norm, l0_w_in, l0_forget_bias, l0_mem_norm, l0_w_mem_kv, l0_w_out, l1_norm, l1_w_in, l1_mem_norm, l1_w_mem_kv, l1_w_out, l2_norm, l2_w_in, l2_mem_norm, l2_w_mem_kv, l2_w_out, l3_norm, l3_w_in, l3_q_norm, l3_w_uq, l3_kv_norm, l3_w_ukv, l3_mem_norm, l3_w_mem_kv, l3_w_out, final_norm):
    b, s, d = x.shape
    n_mem = mem.shape[1]
    m = b * s
    mw = N_HEADS * HEAD_DIM
    x2 = x.reshape(m, d)
    mem2 = mem.reshape(b * n_mem, d)

    pos_b = jnp.broadcast_to(positions.astype(F32).reshape(m, 1), (m, LANES))
    tabs128 = tuple(t.reshape(b, s, LANES) for t in _rope_tables(pos_b, HEAD_DIM))
    tabs64 = tuple(t.reshape(b, s, LANES) for t in _rope_tables(pos_b, IDX_DIM))

    def mem_kv(g, w):
        return _norm_proj(mem2, g, w.astype(BF16))

    def finish(y, u, x_in, g_mem, w_mem_kv, w_out, *, wy, zy_blk, zm_blk, qm_blk, final_g=None):
        return _out_proj(y, u, mem_kv(g_mem, w_mem_kv), x_in, w_out.astype(BF16), wy=wy,
                         zy_blk=zy_blk, zm_blk=zm_blk, qm_blk=qm_blk, seq=s, n_mem=n_mem,
                         final_g=final_g)

    q_w, k_w, v_w, f_w, qm_w, z_w = _cols(l0_w_in, (mw, mw, mw, N_HEADS, MEM_WIDTH, mw + MEM_WIDTH))
    w0 = _pack([(z_w[:, :mw], mw), (z_w[:, mw:], MEM_WIDTH), (qm_w, MEM_WIDTH), (q_w, mw),
                (k_w, mw), (v_w, mw)])
    u, f_arr = _norm_proj(x2, l0_norm, w0, w_extra=_pack([(f_w, N_HEADS)], multiple=LANES),
                          extra_dtype=F32)
    u3 = u.reshape(b, s, -1)
    base = (mw + 2 * MEM_WIDTH) // LANES
    c_nat = _fox_gate(f_arr.reshape(b, s, LANES), l0_forget_bias)
    y = _fox_attn(u3, c_nat, q_blk=base // 2, k_blk=(base + N_HEADS) // 2,
                  v_blk=(base + 2 * N_HEADS) // 2)
    x2 = finish(y.reshape(m, mw), u, x2, l0_mem_norm, l0_w_mem_kv, l0_w_out, wy=mw, zy_blk=0,
                zm_blk=mw // MEM_WIDTH, qm_blk=mw // MEM_WIDTH + 1)

    iw = IDX_HEADS * IDX_DIM
    q_w, k_w, v_w, qi_w, ki_w, wi_w, qm_w, z_w = _cols(
        l1_w_in, (mw, HEAD_DIM, HEAD_DIM, iw, IDX_DIM, IDX_HEADS, MEM_WIDTH, mw + MEM_WIDTH))
    w1 = _pack([(z_w[:, :mw], mw), (q_w, mw), (qi_w, iw), (z_w[:, mw:], MEM_WIDTH),
                (qm_w, MEM_WIDTH), (k_w, HEAD_DIM), (v_w, HEAD_DIM), (ki_w, IDX_DIM),
                (wi_w, IDX_HEADS), (None, LANES - IDX_DIM - IDX_HEADS)])
    u = _norm_proj(x2, l1_norm, w1)
    u3 = u.reshape(b, s, -1)
    kv0 = (2 * mw + iw + 2 * MEM_WIDTH) // LANES
    y = _dsa_attn(u3, tabs128 + tabs64, q_blk=1, qi_blk=2 * mw // iw, k_blk=kv0, v_blk=kv0 + 1,
                  misc_blk=kv0 + 2)
    zoff = (2 * mw + iw) // MEM_WIDTH
    x2 = finish(y.reshape(m, mw), u, x2, l1_mem_norm, l1_w_mem_kv, l1_w_out, wy=mw, zy_blk=0,
                zm_blk=zoff, qm_blk=zoff + 1)

    gw = len(DILATED_PAIRS) * DIL_WIDTH
    q_w, k_w, v_w, qm_w, z_w = _cols(l2_w_in, (gw, gw, gw, MEM_WIDTH, DIL_WIDTH + MEM_WIDTH))

    def grp(w, g):
        return (w[:, g * DIL_WIDTH:(g + 1) * DIL_WIDTH], DIL_WIDTH)

    wa = _pack([grp(q_w, 0), (z_w[:, :DIL_WIDTH], DIL_WIDTH), grp(k_w, 0), grp(v_w, 0),
                (z_w[:, DIL_WIDTH:], MEM_WIDTH), (qm_w, MEM_WIDTH)])
    wb = _pack([grp(q_w, 1), grp(k_w, 1), grp(v_w, 1), grp(q_w, 2), grp(k_w, 2), grp(v_w, 2)])
    u = _norm_proj(x2, l2_norm, wa)
    ub = _norm_proj(x2, l2_norm, wb, out_dtype=F32)
    y = _dil_attn(u.reshape(b, s, -1), ub.reshape(b, s, -1), tabs128, q0_blk=0,
                  k0_blk=2 * DIL_HEADS, v0_blk=3 * DIL_HEADS)
    zoff = 4 * DIL_WIDTH // MEM_WIDTH
    x2 = finish(y.reshape(m, DIL_WIDTH), u, x2, l2_mem_norm, l2_w_mem_kv, l2_w_out, wy=DIL_WIDTH,
                zy_blk=1, zm_blk=zoff, qm_blk=zoff + 1)

    q_lora, kv_lora = l3_w_uq.shape[0], l3_w_ukv.shape[0]
    cq_w, ckv_w, kr_w, qm_w, z_w = _cols(l3_w_in, (q_lora, kv_lora, QK_ROPE, MEM_WIDTH,
                                                  mw + MEM_WIDTH))
    assert q_lora == MEM_WIDTH and kv_lora == MEM_WIDTH
    w3 = _pack([(z_w[:, :mw], mw), (z_w[:, mw:], MEM_WIDTH), (qm_w, MEM_WIDTH), (cq_w, q_lora),
                (ckv_w, kv_lora)])
    u, kr_arr = _norm_proj(x2, l3_norm, w3, w_extra=_pack([(kr_w, QK_ROPE)], multiple=LANES))
    cq_blk = (mw + 2 * MEM_WIDTH) // MEM_WIDTH
    uq = l3_w_uq.reshape(q_lora, N_HEADS, QK_NOPE + QK_ROPE)
    w_uq = _pack([(uq[:, :, :QK_NOPE].reshape(q_lora, -1), N_HEADS * QK_NOPE),
                  (uq[:, :, QK_NOPE:].reshape(q_lora, -1), N_HEADS * QK_ROPE)])
    ukv = l3_w_ukv.reshape(kv_lora, N_HEADS, QK_NOPE + HEAD_DIM)
    w_ukv = _pack([(ukv[:, :, :QK_NOPE].reshape(kv_lora, -1), N_HEADS * QK_NOPE),
                   (ukv[:, :, QK_NOPE:].reshape(kv_lora, -1), N_HEADS * HEAD_DIM)])
    qf = _norm_proj(u, l3_q_norm, w_uq, x_col_block=cq_blk)
    kvf = _norm_proj(u, l3_kv_norm, w_ukv, x_col_block=cq_blk + 1)
    y = _mla_attn(qf.reshape(b, s, -1), kvf.reshape(b, s, -1), kr_arr.reshape(b, s, LANES), tabs64,
                  kr_blk=0)
    x2 = finish(y.reshape(m, mw), u, x2, l3_mem_norm, l3_w_mem_kv, l3_w_out, wy=mw, zy_blk=0,
                zm_blk=mw // MEM_WIDTH, qm_blk=mw // MEM_WIDTH + 1, final_g=final_norm)
    return x2.reshape(b, s, d)
```

```python
import functools

import jax
import jax.numpy as jnp
from jax import lax
from jax.experimental import pallas as pl
from jax.experimental.pallas import tpu as pltpu

F32 = jnp.float32
BF16 = jnp.bfloat16
I32 = jnp.int32

EPS = 1e-6
ROPE_THETA = 10000.0
HEAD_DIM = 128
N_HEADS = 16
MEM_HEADS = 4
MEM_WIDTH = MEM_HEADS * HEAD_DIM
IDX_HEADS = 16
IDX_DIM = 64
TOPK_MAX = 256
DILATED_PAIRS = ((128, 1), (512, 4), (2048, 16))
DIL_HEADS = 6
DIL_WIDTH = DIL_HEADS * HEAD_DIM
QK_NOPE = 128
QK_ROPE = 64
LANES = 128
NEG = -1e30
INT_MIN = -(2 ** 31)
VMEM_LIMIT = 56 * 1024 * 1024

NT_DIMS = (((1,), (1,)), ((), ()))
LOG2E = 1.4426950408889634


def _cparams(n_axes):
    return pltpu.CompilerParams(
        dimension_semantics=("arbitrary",) * n_axes, vmem_limit_bytes=VMEM_LIMIT)


def _rope128(x, cos_f, sin_s):
    return x * cos_f + pltpu.roll(x, 64, 1) * sin_s


def _rope64(x, cos_f, sin_s):
    lane = lax.broadcasted_iota(I32, x.shape, 1)
    partner = jnp.where((lane & 32) == 0, pltpu.roll(x, 96, 1), pltpu.roll(x, 32, 1))
    return x * cos_f + partner * sin_s


def _rope_table_kernel(pos_ref, inv_ref, sgn_ref, cos_ref, sin_ref):
    ang = pos_ref[...] * inv_ref[...]
    cos_ref[...] = jnp.cos(ang)
    sin_ref[...] = jnp.sin(ang) * sgn_ref[...]


def _rope_tables(pos_b, dh):
    m = pos_b.shape[0]
    half = dh // 2
    inv = jnp.power(ROPE_THETA, -jnp.arange(half, dtype=F32) * 2.0 / dh)
    reps = LANES // half
    inv_l = jnp.tile(inv, reps).reshape(1, LANES)
    sgn = jnp.tile(jnp.concatenate([-jnp.ones((half,), F32), jnp.ones((half,), F32)]),
                   reps // 2).reshape(1, LANES)
    ts = min(512, m)
    return pl.pallas_call(
        _rope_table_kernel,
        grid=(m // ts,),
        in_specs=[pl.BlockSpec((ts, LANES), lambda i: (i, 0)),
                  pl.BlockSpec((1, LANES), lambda i: (0, 0)),
                  pl.BlockSpec((1, LANES), lambda i: (0, 0))],
        out_specs=[pl.BlockSpec((ts, LANES), lambda i: (i, 0))] * 2,
        out_shape=[jax.ShapeDtypeStruct((m, LANES), F32)] * 2,
        compiler_params=_cparams(1),
        name="rope_tables",
    )(pos_b, inv_l, sgn)


def _norm_proj_kernel(x_ref, g_ref, w_ref, *rest, has_extra):
    if has_extra:
        wx_ref, o_ref, ox_ref, h_ref = rest
    else:
        o_ref, h_ref = rest

    @pl.when(pl.program_id(1) == 0)
    def _():
        x = x_ref[...].astype(F32)
        ms = jnp.mean(x * x, axis=-1, keepdims=True)
        h_ref[...] = (x * lax.rsqrt(ms + EPS) * g_ref[...]).astype(BF16)
        if has_extra:
            ox_ref[...] = jnp.dot(h_ref[...], wx_ref[...],
                                  preferred_element_type=F32).astype(ox_ref.dtype)

    o_ref[...] = jnp.dot(h_ref[...], w_ref[...], preferred_element_type=F32).astype(o_ref.dtype)


def _norm_proj(x, g, w, *, x_col_block=0, out_dtype=BF16, w_extra=None, extra_dtype=BF16):
    m = x.shape[0]
    k, n = w.shape
    tm = min(1024, m)
    tn = 1024 if n % 1024 == 0 else 512
    assert m % tm == 0 and n % tn == 0
    has_extra = w_extra is not None
    in_specs = [pl.BlockSpec((tm, k), lambda i, j: (i, x_col_block)),
                pl.BlockSpec((1, k), lambda i, j: (0, 0)),
                pl.BlockSpec((k, tn), lambda i, j: (0, j))]
    out_specs = [pl.BlockSpec((tm, tn), lambda i, j: (i, j))]
    out_shape = [jax.ShapeDtypeStruct((m, n), out_dtype)]
    args = [x, g.reshape(1, k).astype(F32), w]
    if has_extra:
        in_specs.append(pl.BlockSpec((k, LANES), lambda i, j: (0, 0)))
        out_specs.append(pl.BlockSpec((tm, LANES), lambda i, j: (i, 0)))
        out_shape.append(jax.ShapeDtypeStruct((m, LANES), extra_dtype))
        args.append(w_extra)
    outs = pl.pallas_call(
        functools.partial(_norm_proj_kernel, has_extra=has_extra),
        grid=(m // tm, n // tn),
        in_specs=in_specs,
        out_specs=out_specs,
        out_shape=out_shape,
        scratch_shapes=[pltpu.VMEM((tm, k), BF16)],
        compiler_params=_cparams(2),
        name="norm_proj",
    )(*args)
    return outs if has_extra else outs[0]


def _out_proj_kernel(y_ref, zy_ref, zm_ref, qm_ref, mkv_ref, x_ref, w_ref, *rest, wy, final):
    if final:
        gf_ref, o_ref, gated_ref = rest
    else:
        o_ref, gated_ref = rest
    zy = zy_ref[...].astype(F32)
    gated_ref[:, :wy] = (y_ref[...].astype(F32) * (zy * jax.nn.sigmoid(zy))).astype(BF16)
    scale = HEAD_DIM ** -0.5
    for h in range(MEM_HEADS):
        lo, hi = h * HEAD_DIM, (h + 1) * HEAD_DIM
        s = lax.dot_general(qm_ref[:, lo:hi], mkv_ref[:, lo:hi], NT_DIMS,
                            preferred_element_type=F32) * scale
        m = jnp.max(s, axis=-1, keepdims=True)
        p = jnp.exp(s - m)
        l = jnp.sum(p, axis=-1, keepdims=True)
        o = jnp.dot(p.astype(BF16), mkv_ref[:, MEM_WIDTH + lo:MEM_WIDTH + hi],
                    preferred_element_type=F32) / l
        zm = zm_ref[:, lo:hi].astype(F32)
        gated_ref[:, wy + lo:wy + hi] = (o * (zm * jax.nn.sigmoid(zm))).astype(BF16)
    out = x_ref[...] + jnp.dot(gated_ref[...], w_ref[...], preferred_element_type=F32)
    if final:
        ms = jnp.mean(out * out, axis=-1, keepdims=True)
        out = out * lax.rsqrt(ms + EPS) * gf_ref[...]
    o_ref[...] = out


def _out_proj(y, u, mkv, x, w_out, *, wy, zy_blk, zm_blk, qm_blk, seq, n_mem, final_g=None):
    m, d = x.shape
    tm = min(512, seq)
    final = final_g is not None
    in_specs = [
        pl.BlockSpec((tm, wy), lambda i: (i, 0)),
        pl.BlockSpec((tm, wy), lambda i: (i, zy_blk)),
        pl.BlockSpec((tm, MEM_WIDTH), lambda i: (i, zm_blk)),
        pl.BlockSpec((tm, MEM_WIDTH), lambda i: (i, qm_blk)),
        pl.BlockSpec((n_mem, 2 * MEM_WIDTH), lambda i: ((i * tm) // seq, 0)),
        pl.BlockSpec((tm, d), lambda i: (i, 0)),
        pl.BlockSpec((wy + MEM_WIDTH, d), lambda i: (0, 0), pipeline_mode=pl.Buffered(1)),
    ]
    args = [y, u, u, u, mkv, x, w_out]
    if final:
        in_specs.append(pl.BlockSpec((1, d), lambda i: (0, 0)))
        args.append(final_g.reshape(1, d).astype(F32))
    return pl.pallas_call(
        functools.partial(_out_proj_kernel, wy=wy, final=final),
        grid=(m // tm,),
        in_specs=in_specs,
        out_specs=pl.BlockSpec((tm, d), lambda i: (i, 0)),
        out_shape=jax.ShapeDtypeStruct((m, d), F32),
        scratch_shapes=[pltpu.VMEM((tm, wy + MEM_WIDTH), BF16)],
        compiler_params=_cparams(1),
        name="out_proj",
    )(*args)


def _fox_gate_kernel(f_ref, b_ref, c_ref, carry_ref):
    @pl.when(pl.program_id(1) == 0)
    def _():
        carry_ref[...] = jnp.zeros_like(carry_ref)

    f_t = f_ref[...].astype(F32).T[:N_HEADS, :] + b_ref[...]
    log_f = jnp.minimum(f_t, 0.0) - jnp.log(1.0 + jnp.exp(-jnp.abs(f_t)))
    ts = log_f.shape[1]
    r = lax.broadcasted_iota(I32, (ts, ts), 0)
    c = lax.broadcasted_iota(I32, (ts, ts), 1)
    upper = jnp.where(r <= c, 1.0, 0.0).astype(F32)
    cs = jnp.dot(log_f, upper, precision=lax.Precision.HIGHEST,
                 preferred_element_type=F32) + carry_ref[...]
    c_ref[...] = cs * LOG2E
    carry_ref[...] = cs[:, ts - 1:ts]


def _fox_gate(u3, bias, f_blk):
    b, s, _ = u3.shape
    ts = min(256, s)
    return pl.pallas_call(
        _fox_gate_kernel,
        grid=(b, s // ts),
        in_specs=[pl.BlockSpec((None, ts, LANES), lambda bi, j: (bi, j, f_blk)),
                  pl.BlockSpec((N_HEADS, 1), lambda bi, j: (0, 0))],
        out_specs=pl.BlockSpec((None, N_HEADS, ts), lambda bi, j: (bi, 0, j)),
        out_shape=jax.ShapeDtypeStruct((b, N_HEADS, s), F32),
        scratch_shapes=[pltpu.VMEM((N_HEADS, 1), F32)],
        compiler_params=_cparams(2),
        name="fox_gate",
    )(u3, bias.reshape(N_HEADS, 1).astype(F32))


STRIP = 32


def _softmax_strips(t_ref, p_ref, m_ref, l_ref, alpha_ref, adjust):
    tq, tk = t_ref.shape
    for r0 in range(0, tq, min(STRIP, tq)):
        rs = pl.ds(r0, min(STRIP, tq))
        blocks = [adjust(r0, c, t_ref[rs, c * LANES:(c + 1) * LANES]) for c in range(tk // LANES)]
        live = [x for x in blocks if x is not None]
        mx = live[0]
        for x in live[1:]:
            mx = jnp.maximum(mx, x)
        m_old = m_ref[rs, :]
        m_new = jnp.maximum(m_old, jnp.max(mx, axis=-1, keepdims=True))
        alpha = jnp.exp2(m_old - m_new)
        ps = [None if x is None else jnp.exp2(x - m_new) for x in blocks]
        live = [x for x in ps if x is not None]
        sm = live[0]
        for x in live[1:]:
            sm = sm + x
        l_ref[rs, :] = alpha * l_ref[rs, :] + jnp.sum(sm, axis=-1, keepdims=True)
        m_ref[rs, :] = m_new
        alpha_ref[rs, :] = alpha
        for c, x in enumerate(ps):
            p_ref[rs, c * LANES:(c + 1) * LANES] = (
                jnp.zeros((min(STRIP, tq), LANES), BF16) if x is None else x.astype(BF16))


def _flash_scratch(n_streams, tq):
    return [pltpu.VMEM((n_streams, tq, tq), F32), pltpu.VMEM((n_streams, tq, tq), BF16),
            pltpu.VMEM((n_streams, tq, LANES), F32), pltpu.VMEM((n_streams, tq, LANES), F32),
            pltpu.VMEM((n_streams, tq, LANES), F32), pltpu.VMEM((n_streams, tq, HEAD_DIM), F32)]


def _flash_init(scratch):
    _, _, m_ref, l_ref, _, acc_ref = scratch
    m_ref[...] = jnp.full(m_ref.shape, NEG, F32)
    l_ref[...] = jnp.zeros(l_ref.shape, F32)
    acc_ref[...] = jnp.zeros(acc_ref.shape, F32)


def _flash_chunk(n_streams, j, raw_scores, values, to_logits, scratch, *, diagonal):
    t_ref, p_ref, m_ref, l_ref, alpha_ref, acc_ref = scratch
    strip = min(STRIP, t_ref.shape[1])
    row = lax.broadcasted_iota(I32, (strip, LANES), 0)
    col = lax.broadcasted_iota(I32, (strip, LANES), 1)
    for a in range(n_streams):
        t_ref[a] = raw_scores(a, j)
    for a in range(n_streams):
        def adjust(r0, c, x, a=a):
            t = to_logits(a, j, r0, c, x)
            if not diagonal or c * LANES + LANES - 1 <= r0:
                return t
            if c * LANES > r0 + strip - 1:
                return None
            return jnp.where(col + c * LANES <= row + r0, t, NEG)

        _softmax_strips(t_ref.at[a], p_ref.at[a], m_ref.at[a], l_ref.at[a], alpha_ref.at[a],
                        adjust)
    for a in range(n_streams):
        acc_ref[a] = alpha_ref[a] * acc_ref[a] + jnp.dot(p_ref[a], values(a, j),
                                                         preferred_element_type=F32)


def _flash_finish(n_streams, scratch):
    _, _, _, l_ref, _, acc_ref = scratch
    return [acc_ref[a] / l_ref[a] for a in range(n_streams)]


def _causal_flash(i, n_streams, raw_scores, values, to_logits, scratch):
    _flash_init(scratch)

    def body(j, _):
        _flash_chunk(n_streams, j, raw_scores, values, to_logits, scratch, diagonal=False)
        return 0

    lax.fori_loop(0, i, body, 0)
    _flash_chunk(n_streams, i, raw_scores, values, to_logits, scratch, diagonal=True)
    return _flash_finish(n_streams, scratch)


def _fox_attn_kernel(q_ref, k_ref, v_ref, c_ref, o_ref, qs_ref, *scratch, tq):
    i = pl.program_id(2)
    qk_scale = HEAD_DIM ** -0.5 * LOG2E
    qs_ref[...] = (q_ref[...].astype(F32) * qk_scale).astype(BF16)

    def raw_scores(a, j):
        off = pl.multiple_of(j * tq, tq)
        lo, hi = a * HEAD_DIM, (a + 1) * HEAD_DIM
        return lax.dot_general(qs_ref[:, lo:hi], k_ref[pl.ds(off, tq), lo:hi], NT_DIMS,
                               preferred_element_type=F32)

    def to_logits(a, j, r0, c, x):
        return x - c_ref[a, j, :, c * LANES:(c + 1) * LANES]

    def values(a, j):
        off = pl.multiple_of(j * tq, tq)
        return v_ref[pl.ds(off, tq), a * HEAD_DIM:(a + 1) * HEAD_DIM]

    outs = _causal_flash(i, 2, raw_scores, values, to_logits, scratch)
    for a in range(2):
        o_ref[:, a * HEAD_DIM:(a + 1) * HEAD_DIM] = outs[a].astype(o_ref.dtype)


def _fox_attn(u3, c_t, *, q_blk, k_blk, v_blk):
    b, s, _ = u3.shape
    tq = min(512, s)
    nq = s // tq
    pw = 2 * HEAD_DIM
    c5 = c_t.reshape(b, N_HEADS, nq, 1, tq)
    return pl.pallas_call(
        functools.partial(_fox_attn_kernel, tq=tq),
        grid=(b, N_HEADS // 2, nq),
        in_specs=[pl.BlockSpec((None, tq, pw), lambda bi, p, i: (bi, i, q_blk + p)),
                  pl.BlockSpec((None, s, pw), lambda bi, p, i: (bi, 0, k_blk + p)),
                  pl.BlockSpec((None, s, pw), lambda bi, p, i: (bi, 0, v_blk + p)),
                  pl.BlockSpec((None, 2, nq, 1, tq), lambda bi, p, i: (bi, p, 0, 0, 0))],
        out_specs=pl.BlockSpec((None, tq, pw), lambda bi, p, i: (bi, i, p)),
        out_shape=jax.ShapeDtypeStruct((b, s, N_HEADS * HEAD_DIM), BF16),
        scratch_shapes=[pltpu.VMEM((tq, pw), BF16)] + _flash_scratch(2, tq),
        compiler_params=_cparams(3),
        name="fox_attn",
    )(u3, u3, u3, c5)


def _dsa_kprep_kernel(k_ref, mk_ref, c128_ref, s128_ref, c64_ref, s64_ref,
                      kr_ref, klo_ref, khi_ref):
    kr_ref[...] = _rope128(k_ref[...].astype(F32), c128_ref[...], s128_ref[...]).astype(BF16)
    ki = _rope64(mk_ref[...].astype(F32), c64_ref[...], s64_ref[...])
    lane = lax.broadcasted_iota(I32, ki.shape, 1)
    lo = jnp.where(lane < IDX_DIM, ki, 0.0)
    klo_ref[...] = lo.astype(BF16)
    khi_ref[...] = pltpu.roll(lo, IDX_DIM, 1).astype(BF16)


def _dsa_kprep(u3, tabs, *, k_blk, misc_blk):
    b, s, _ = u3.shape
    ts = min(512, s)
    c128, s128, c64, s64 = tabs

    def spec(blk):
        return pl.BlockSpec((None, ts, LANES), lambda bi, i: (bi, i, blk))

    return pl.pallas_call(
        _dsa_kprep_kernel,
        grid=(b, s // ts),
        in_specs=[spec(k_blk), spec(misc_blk), spec(0), spec(0), spec(0), spec(0)],
        out_specs=[spec(0)] * 3,
        out_shape=[jax.ShapeDtypeStruct((b, s, LANES), BF16)] * 3,
        compiler_params=_cparams(2),
        name="dsa_kprep",
    )(u3, u3, c128, s128, c64, s64)


def _dsa_kernel(q_ref, qi_ref, mq_ref, v_ref, kr_ref, klo_ref, khi_ref,
                cq128_ref, sq128_ref, cq64_ref, sq64_ref, y_in_ref,
                o_ref,
                qr_ref, qir_ref, w_ref, sc_ref, t4_ref, yo_ref, *flash,
                tq, kl, q_off, n_sel):
    del y_in_ref
    t0 = q_off + pl.program_id(1) * tq
    qk_scale = HEAD_DIM ** -0.5 * LOG2E
    idx_scale = (IDX_DIM ** -0.5) * (IDX_HEADS ** -0.5)
    ck = min(512, kl)
    n_chunks = kl // ck
    strip = min(STRIP, tq)

    cq128, sq128 = cq128_ref[...], sq128_ref[...]
    for h in range(N_HEADS):
        qh = q_ref[:, h * HEAD_DIM:(h + 1) * HEAD_DIM].astype(F32)
        qr_ref[h] = (_rope128(qh, cq128, sq128) * qk_scale).astype(BF16)
    cq64, sq64 = cq64_ref[...], sq64_ref[...]
    for a in range(IDX_HEADS // 2):
        qir_ref[a] = _rope64(qi_ref[:, a * LANES:(a + 1) * LANES].astype(F32),
                             cq64, sq64).astype(BF16)
    w_ref[...] = mq_ref[...].astype(F32) * idx_scale

    row = lax.broadcasted_iota(I32, (strip, LANES), 0)
    col = lax.broadcasted_iota(I32, (strip, LANES), 1)
    group = 4

    def idx_chunk(j, _):
        off = pl.multiple_of(j * ck, ck)
        k_lo, k_hi = klo_ref[pl.ds(off, ck), :], khi_ref[pl.ds(off, ck), :]
        for g in range(IDX_HEADS // group):
            for a2 in range(group // 2):
                x = qir_ref[g * (group // 2) + a2]
                t4_ref[2 * a2] = lax.dot_general(x, k_lo, NT_DIMS, preferred_element_type=F32)
                t4_ref[2 * a2 + 1] = lax.dot_general(x, k_hi, NT_DIMS,
                                                     preferred_element_type=F32)
            last = g == IDX_HEADS // group - 1
            for r0 in range(0, tq, strip):
                rs = pl.ds(r0, strip)
                w_rows = w_ref[rs, :]
                ws = [jnp.broadcast_to(w_rows[:, IDX_DIM + g * group + u:IDX_DIM + g * group + u + 1],
                                       (strip, LANES)) for u in range(group)]
                for c in range(ck // LANES):
                    cs = pl.ds(c * LANES, LANES)
                    acc = ws[0] * jnp.maximum(t4_ref[0, rs, cs], 0.0)
                    for u in range(1, group):
                        acc = acc + ws[u] * jnp.maximum(t4_ref[u, rs, cs], 0.0)
                    if g > 0:
                        acc = acc + sc_ref[j, rs, cs]
                    if last:
                        causal = col + (off + c * LANES) <= row + (t0 + r0)
                        acc = jnp.where(causal, acc, -jnp.inf)
                    sc_ref[j, rs, cs] = acc
        return 0

    lax.fori_loop(0, n_chunks, idx_chunk, 0)

    shape3 = (n_chunks, tq, ck)
    col3 = lax.broadcasted_iota(I32, shape3, 0) * ck + lax.broadcasted_iota(I32, shape3, 2)
    causal3 = col3 <= lax.broadcasted_iota(I32, shape3, 1) + t0

    def count(x):
        return jnp.sum(jnp.sum(x, axis=0), axis=-1, keepdims=True)

    def key_to_f32(key):
        return pltpu.bitcast(jnp.where(key < 0, key ^ jnp.int32(0x7FFFFFFF), key), F32)

    def thr_body(it, key):
        cand = key + (jnp.int32(1) << (31 - it))
        c = count(jnp.where(sc_ref[...] >= key_to_f32(cand), 1.0, 0.0))
        return jnp.where(c >= n_sel, cand, key)

    thr_key = lax.fori_loop(0, 32, thr_body, jnp.full((tq, 1), INT_MIN, I32))
    take_all = thr_key == INT_MIN
    thr = key_to_f32(thr_key)
    score = sc_ref[...]
    need = n_sel - count(jnp.where(score > thr, 1.0, 0.0))
    surplus = jnp.where(take_all, 0.0, count(jnp.where(score == thr, 1.0, 0.0)) - need)
    has_surplus = jnp.max(surplus) > 0.0

    n_bits = kl.bit_length()

    def tie_body(it, jm):
        cand = jm + (jnp.int32(1) << (n_bits - 1 - it))
        f = count(jnp.where(sc_ref[...] == thr, jnp.where(col3 < cand, 1.0, 0.0), 0.0))
        return jnp.where(jnp.logical_and(f < need, cand <= kl), cand, jm)

    jm = lax.fori_loop(0, jnp.where(has_surplus, n_bits, 0), tie_body,
                       jnp.broadcast_to(jnp.where(has_surplus, 0, kl), (tq, 1)).astype(I32))
    keep = jnp.where(jnp.logical_or(score > thr, take_all), 1.0,
                     jnp.where(score == thr, jnp.where(col3 <= jm, 1.0, 0.0), 0.0))
    sc_ref[...] = jnp.where(causal3, jnp.where(keep > 0.0, 0.0, NEG), NEG)

    def raw_scores(h0):
        def f(a, j):
            return lax.dot_general(qr_ref[h0 + a], kr_ref[j * ck:(j + 1) * ck, :], NT_DIMS,
                                   preferred_element_type=F32)
        return f

    def values(a, j):
        return v_ref[j * ck:(j + 1) * ck, :]

    def to_logits(a, j, r0, c, x):
        return x + sc_ref[j, r0:r0 + strip, c * LANES:(c + 1) * LANES]

    def pair_body(pp, _):
        h0 = 2 * pp
        _flash_init(flash)
        for j in range(n_chunks):
            _flash_chunk(2, j, raw_scores(h0), values, to_logits, flash, diagonal=False)
        outs = _flash_finish(2, flash)
        for a in range(2):
            yo_ref[h0 + a] = outs[a].astype(yo_ref.dtype)
        return 0

    lax.fori_loop(0, N_HEADS // 2, pair_body, 0)
    for h in range(N_HEADS):
        o_ref[:, h * HEAD_DIM:(h + 1) * HEAD_DIM] = yo_ref[h]


def _dsa_group(u3, kprep, tabs, y, *, q_off, rows, kl, n_sel, q_blk, v_blk, qi_blk, misc_blk):
    b, s, _ = u3.shape
    tq = rows
    assert kl % min(512, kl) == 0 and tq == min(512, kl)
    qb0 = q_off // tq
    c128, s128, c64, s64 = tabs
    kr, klo, khi = kprep
    qw = N_HEADS * HEAD_DIM
    iw = IDX_HEADS * IDX_DIM
    ck = min(512, kl)

    def qspec(width, blk):
        return pl.BlockSpec((None, tq, width), lambda bi, i: (bi, qb0 + i, blk))

    def kspec(blk):
        return pl.BlockSpec((None, kl, LANES), lambda bi, i: (bi, 0, blk))

    return pl.pallas_call(
        functools.partial(_dsa_kernel, tq=tq, kl=kl, q_off=q_off, n_sel=n_sel),
        grid=(b, rows // tq),
        in_specs=[qspec(qw, q_blk), qspec(iw, qi_blk), qspec(LANES, misc_blk),
                  kspec(v_blk), kspec(0), kspec(0), kspec(0),
                  qspec(LANES, 0), qspec(LANES, 0), qspec(LANES, 0), qspec(LANES, 0),
                  pl.BlockSpec(memory_space=pl.ANY)],
        out_specs=pl.BlockSpec((None, tq, qw), lambda bi, i: (bi, qb0 + i, 0)),
        out_shape=jax.ShapeDtypeStruct((b, s, qw), BF16),
        input_output_aliases={11: 0},
        scratch_shapes=[pltpu.VMEM((N_HEADS, tq, HEAD_DIM), BF16),
                        pltpu.VMEM((IDX_HEADS // 2, tq, LANES), BF16),
                        pltpu.VMEM((tq, LANES), F32),
                        pltpu.VMEM((kl // ck, tq, ck), F32),
                        pltpu.VMEM((4, tq, ck), F32),
                        pltpu.VMEM((N_HEADS, tq, HEAD_DIM), BF16)] + _flash_scratch(2, tq),
        compiler_params=_cparams(2),
        name="dsa_attn",
    )(u3, u3, u3, u3, kr, klo, khi, c128, s128, c64, s64, y)


def _dsa_attn(u3, tabs, *, k_blk, misc_blk, **blks):
    b, s, _ = u3.shape
    n_sel = min(TOPK_MAX, s // 4)
    rows = min(512, s)
    kprep = _dsa_kprep(u3, tabs, k_blk=k_blk, misc_blk=misc_blk)
    y = jnp.zeros((b, s, N_HEADS * HEAD_DIM), BF16)
    for q_off in range(0, s, rows):
        y = _dsa_group(u3, kprep, tabs, y, q_off=q_off, rows=rows, kl=q_off + rows, n_sel=n_sel,
                       misc_blk=misc_blk, **blks)
    return y


def _dil_kernel(q0_ref, k0_ref, v0_ref, q1_ref, k1_ref, v1_ref, q2_ref, k2_ref, v2_ref,
                cos_ref, sin_ref, y_ref, qr_ref, kr_ref, o_ref, lse_ref, t_ref, p_ref, *, seq):
    qk_scale = HEAD_DIM ** -0.5 * LOG2E
    cos_f, sin_s = cos_ref[...], sin_ref[...]
    groups = ((q0_ref, k0_ref, v0_ref), (q1_ref, k1_ref, v1_ref), (q2_ref, k2_ref, v2_ref))
    for g, (q_ref, k_ref, _) in enumerate(groups):
        qr_ref[g] = _rope128(q_ref[...].astype(F32), cos_f, sin_s) * qk_scale
        kr_ref[g] = _rope128(k_ref[...].astype(F32), cos_f, sin_s)

    for g, (window, dil) in enumerate(DILATED_PAIRS):
        v_ref = groups[g][2]
        sub = seq // dil
        qb = min(window // dil, sub)
        tiles = [(r, i) for r in range(dil) for i in range(sub // qb)]

        def rows(r, blk, dil=dil, qb=qb):
            start = r + dil * qb * blk
            return pl.ds(start, qb) if dil == 1 else pl.ds(start, qb, stride=dil)

        for n, (r, i) in enumerate(tiles):
            q = qr_ref[g, rows(r, i), :].astype(BF16)
            k_cur = kr_ref[g, rows(r, i), :].astype(BF16)
            t_ref[n, :qb, qb:2 * qb] = lax.dot_general(q, k_cur, NT_DIMS,
                                                       preferred_element_type=F32)
            if i > 0:
                k_prev = kr_ref[g, rows(r, i - 1), :].astype(BF16)
                t_ref[n, :qb, :qb] = lax.dot_general(q, k_prev, NT_DIMS,
                                                     preferred_element_type=F32)
        row = lax.broadcasted_iota(I32, (qb, qb), 0)
        col = lax.broadcasted_iota(I32, (qb, qb), 1)
        for n, (r, i) in enumerate(tiles):
            t_cur = jnp.where(col <= row, t_ref[n, :qb, qb:2 * qb], NEG)
            m = jnp.max(t_cur, axis=-1, keepdims=True)
            if i > 0:
                t_prev = jnp.where(col >= row, t_ref[n, :qb, :qb], NEG)
                m = jnp.maximum(m, jnp.max(t_prev, axis=-1, keepdims=True))
            p_cur = jnp.exp2(t_cur - m)
            l = jnp.sum(p_cur, axis=-1, keepdims=True)
            p_ref[n, :qb, qb:2 * qb] = p_cur.astype(BF16)
            if i > 0:
                p_prev = jnp.exp2(t_prev - m)
                l = l + jnp.sum(p_prev, axis=-1, keepdims=True)
                p_ref[n, :qb, :qb] = p_prev.astype(BF16)
            t_ref[n, :qb, :LANES] = jnp.broadcast_to(l, (qb, LANES))
            t_ref[n, :qb, LANES:2 * LANES] = jnp.broadcast_to(m, (qb, LANES))
        for n, (r, i) in enumerate(tiles):
            acc = jnp.dot(p_ref[n, :qb, qb:2 * qb], v_ref[rows(r, i), :].astype(BF16),
                          preferred_element_type=F32)
            if i > 0:
                acc = acc + jnp.dot(p_ref[n, :qb, :qb], v_ref[rows(r, i - 1), :].astype(BF16),
                                    preferred_element_type=F32)
            l = t_ref[n, :qb, :LANES]
            o_ref[g, rows(r, i), :] = acc / l
            lse_ref[g, rows(r, i), :] = t_ref[n, :qb, LANES:2 * LANES] + jnp.log2(l)

    l0, l1, l2 = lse_ref[0], lse_ref[1], lse_ref[2]
    m = jnp.maximum(jnp.maximum(l0, l1), l2)
    e0, e1, e2 = jnp.exp2(l0 - m), jnp.exp2(l1 - m), jnp.exp2(l2 - m)
    y = (e0 * o_ref[0] + e1 * o_ref[1] + e2 * o_ref[2]) / (e0 + e1 + e2)
    y_ref[...] = y.astype(y_ref.dtype)


def _dil_attn(ua3, ub3, tabs128, *, q0_blk, k0_blk, v0_blk):
    b, s, _ = ua3.shape
    for window, dil in DILATED_PAIRS:
        assert s % dil == 0 and (s // dil) % min(window // dil, s // dil) == 0
    cos_t, sin_t = tabs128
    tile = min(DILATED_PAIRS[0][0], s)
    n_tiles = s // min(min(w // d, s // d) for w, d in DILATED_PAIRS)

    def spec(blk0):
        return pl.BlockSpec((None, s, HEAD_DIM), lambda bi, a: (bi, 0, blk0 + a))

    tab = pl.BlockSpec((None, s, LANES), lambda bi, a: (bi, 0, 0))
    return pl.pallas_call(
        functools.partial(_dil_kernel, seq=s),
        grid=(b, DIL_HEADS),
        in_specs=[spec(q0_blk), spec(k0_blk), spec(v0_blk)]
        + [spec(j * DIL_HEADS) for j in range(6)] + [tab, tab],
        out_specs=pl.BlockSpec((None, s, HEAD_DIM), lambda bi, a: (bi, 0, a)),
        out_shape=jax.ShapeDtypeStruct((b, s, DIL_WIDTH), BF16),
        scratch_shapes=[pltpu.VMEM((3, s, HEAD_DIM), F32)] * 4
        + [pltpu.VMEM((n_tiles, tile, 2 * LANES), F32), pltpu.VMEM((n_tiles, tile, 2 * LANES), BF16)],
        compiler_params=_cparams(2),
        name="dilated_attn",
    )(ua3, ua3, ua3, ub3, ub3, ub3, ub3, ub3, ub3, cos_t, sin_t)


def _mla_kernel(qn_ref, qr_ref, kn_ref, kr_ref, v_ref, cq_ref, sq_ref, ck_ref, sk_ref,
                o_ref, qcat_ref, kcat_ref, *scratch, tq):
    i = pl.program_id(2)
    qk_scale = (QK_NOPE + QK_ROPE) ** -0.5 * LOG2E
    cat = QK_NOPE + QK_ROPE

    @pl.when(i == 0)
    def _():
        k_rope = _rope64(kr_ref[...].astype(F32), ck_ref[...],
                         sk_ref[...])[:, :QK_ROPE].astype(BF16)
        for a in range(2):
            kcat_ref[a, :, :QK_NOPE] = kn_ref[:, a * HEAD_DIM:(a + 1) * HEAD_DIM]
            kcat_ref[a, :, QK_NOPE:cat] = k_rope
            kcat_ref[a, :, cat:] = jnp.zeros((kcat_ref.shape[1], 2 * LANES - cat), BF16)

    q_rope = (_rope64(qr_ref[...].astype(F32), cq_ref[...], sq_ref[...]) * qk_scale).astype(BF16)
    for a in range(2):
        q_nope = qn_ref[:, a * HEAD_DIM:(a + 1) * HEAD_DIM].astype(F32) * qk_scale
        qcat_ref[a, :, :QK_NOPE] = q_nope.astype(BF16)
        qcat_ref[a, :, QK_NOPE:cat] = q_rope[:, a * QK_ROPE:(a + 1) * QK_ROPE]
        qcat_ref[a, :, cat:] = jnp.zeros((tq, 2 * LANES - cat), BF16)

    def raw_scores(a, j):
        off = pl.multiple_of(j * tq, tq)
        return lax.dot_general(qcat_ref[a], kcat_ref[a, pl.ds(off, tq), :], NT_DIMS,
                               preferred_element_type=F32)

    def to_logits(a, j, r0, c, x):
        return x

    def values(a, j):
        off = pl.multiple_of(j * tq, tq)
        return v_ref[pl.ds(off, tq), a * HEAD_DIM:(a + 1) * HEAD_DIM]

    outs = _causal_flash(i, 2, raw_scores, values, to_logits, scratch)
    for a in range(2):
        o_ref[:, a * HEAD_DIM:(a + 1) * HEAD_DIM] = outs[a].astype(o_ref.dtype)


def _mla_attn(qf3, kvf3, u3, tabs64, *, kr_blk):
    b, s, _ = qf3.shape
    tq = min(512, s)
    pw = 2 * HEAD_DIM
    n_pairs = N_HEADS // 2
    cos_t, sin_t = tabs64
    return pl.pallas_call(
        functools.partial(_mla_kernel, tq=tq),
        grid=(b, n_pairs, s // tq),
        in_specs=[pl.BlockSpec((None, tq, pw), lambda bi, p, i: (bi, i, p)),
                  pl.BlockSpec((None, tq, LANES), lambda bi, p, i: (bi, i, N_HEADS + p)),
                  pl.BlockSpec((None, s, pw), lambda bi, p, i: (bi, 0, p)),
                  pl.BlockSpec((None, s, LANES), lambda bi, p, i: (bi, 0, kr_blk)),
                  pl.BlockSpec((None, s, pw), lambda bi, p, i: (bi, 0, n_pairs + p)),
                  pl.BlockSpec((None, tq, LANES), lambda bi, p, i: (bi, i, 0)),
                  pl.BlockSpec((None, tq, LANES), lambda bi, p, i: (bi, i, 0)),
                  pl.BlockSpec((None, s, LANES), lambda bi, p, i: (bi, 0, 0)),
                  pl.BlockSpec((None, s, LANES), lambda bi, p, i: (bi, 0, 0))],
        out_specs=pl.BlockSpec((None, tq, pw), lambda bi, p, i: (bi, i, p)),
        out_shape=jax.ShapeDtypeStruct((b, s, N_HEADS * HEAD_DIM), BF16),
        scratch_shapes=[pltpu.VMEM((2, tq, 2 * LANES), BF16),
                        pltpu.VMEM((2, s, 2 * LANES), BF16)] + _flash_scratch(2, tq),
        compiler_params=_cparams(3),
        name="mla_attn",
    )(qf3, qf3, kvf3, u3, kvf3, cos_t, sin_t, cos_t, sin_t)


def _pack(parts, multiple=512):
    k = next(p.shape[0] for p, _ in parts if p is not None)
    cols = [jnp.zeros((k, w), BF16) if p is None else p.astype(BF16) for p, w in parts]
    n = sum(w for _, w in parts)
    pad = (-n) % multiple
    if pad:
        cols.append(jnp.zeros((k, pad), BF16))
    return jnp.concatenate(cols, axis=1)


def _cols(w, sizes):
    out, acc = [], 0
    for sz in sizes:
        out.append(w[:, acc:acc + sz])
        acc += sz
    return out


def kernel(x, mem, positions, l0_norm, l0_w_in, l0_forget_bias, l0_mem_norm, l0_w_mem_kv, l0_w_out, l1_norm, l1_w_in, l1_mem_norm, l1_w_mem_kv, l1_w_out, l2_norm, l2_w_in, l2_mem_norm, l2_w_mem_kv, l2_w_out, l3_norm, l3_w_in, l3_q_norm, l3_w_uq, l3_kv_norm, l3_w_ukv, l3_mem_norm, l3_w_mem_kv, l3_w_out, final_norm):
    b, s, d = x.shape
    n_mem = mem.shape[1]
    m = b * s
    mw = N_HEADS * HEAD_DIM
    x2 = x.reshape(m, d)
    mem2 = mem.reshape(b * n_mem, d)

    pos_b = jnp.broadcast_to(positions.astype(F32).reshape(m, 1), (m, LANES))
    tabs128 = tuple(t.reshape(b, s, LANES) for t in _rope_tables(pos_b, HEAD_DIM))
    tabs64 = tuple(t.reshape(b, s, LANES) for t in _rope_tables(pos_b, IDX_DIM))

    def mem_kv(g, w):
        return _norm_proj(mem2, g, w.astype(BF16))

    def finish(y, u, x_in, g_mem, w_mem_kv, w_out, *, wy, zy_blk, zm_blk, qm_blk, final_g=None):
        return _out_proj(y, u, mem_kv(g_mem, w_mem_kv), x_in, w_out.astype(BF16), wy=wy,
                         zy_blk=zy_blk, zm_blk=zm_blk, qm_blk=qm_blk, seq=s, n_mem=n_mem,
                         final_g=final_g)

    q_w, k_w, v_w, f_w, qm_w, z_w = _cols(l0_w_in, (mw, mw, mw, N_HEADS, MEM_WIDTH, mw + MEM_WIDTH))
    w0 = _pack([(z_w[:, :mw], mw), (z_w[:, mw:], MEM_WIDTH), (qm_w, MEM_WIDTH), (q_w, mw),
                (k_w, mw), (v_w, mw)])
    u, f_arr = _norm_proj(x2, l0_norm, w0, w_extra=_pack([(f_w, N_HEADS)], multiple=LANES),
                          extra_dtype=F32)
    u3 = u.reshape(b, s, -1)
    base = (mw + 2 * MEM_WIDTH) // LANES
    c_t = _fox_gate(f_arr.reshape(b, s, LANES), l0_forget_bias, 0)
    y = _fox_attn(u3, c_t, q_blk=base // 2, k_blk=(base + N_HEADS) // 2,
                  v_blk=(base + 2 * N_HEADS) // 2)
    x2 = finish(y.reshape(m, mw), u, x2, l0_mem_norm, l0_w_mem_kv, l0_w_out, wy=mw, zy_blk=0,
                zm_blk=mw // MEM_WIDTH, qm_blk=mw // MEM_WIDTH + 1)

    iw = IDX_HEADS * IDX_DIM
    q_w, k_w, v_w, qi_w, ki_w, wi_w, qm_w, z_w = _cols(
        l1_w_in, (mw, HEAD_DIM, HEAD_DIM, iw, IDX_DIM, IDX_HEADS, MEM_WIDTH, mw + MEM_WIDTH))
    w1 = _pack([(z_w[:, :mw], mw), (q_w, mw), (qi_w, iw), (z_w[:, mw:], MEM_WIDTH),
                (qm_w, MEM_WIDTH), (k_w, HEAD_DIM), (v_w, HEAD_DIM), (ki_w, IDX_DIM),
                (wi_w, IDX_HEADS), (None, LANES - IDX_DIM - IDX_HEADS)])
    u = _norm_proj(x2, l1_norm, w1)
    u3 = u.reshape(b, s, -1)
    kv0 = (2 * mw + iw + 2 * MEM_WIDTH) // LANES
    y = _dsa_attn(u3, tabs128 + tabs64, q_blk=1, qi_blk=2 * mw // iw, k_blk=kv0, v_blk=kv0 + 1,
                  misc_blk=kv0 + 2)
    zoff = (2 * mw + iw) // MEM_WIDTH
    x2 = finish(y.reshape(m, mw), u, x2, l1_mem_norm, l1_w_mem_kv, l1_w_out, wy=mw, zy_blk=0,
                zm_blk=zoff, qm_blk=zoff + 1)

    gw = len(DILATED_PAIRS) * DIL_WIDTH
    q_w, k_w, v_w, qm_w, z_w = _cols(l2_w_in, (gw, gw, gw, MEM_WIDTH, DIL_WIDTH + MEM_WIDTH))

    def grp(w, g):
        return (w[:, g * DIL_WIDTH:(g + 1) * DIL_WIDTH], DIL_WIDTH)

    wa = _pack([grp(q_w, 0), (z_w[:, :DIL_WIDTH], DIL_WIDTH), grp(k_w, 0), grp(v_w, 0),
                (z_w[:, DIL_WIDTH:], MEM_WIDTH), (qm_w, MEM_WIDTH)])
    wb = _pack([grp(q_w, 1), grp(k_w, 1), grp(v_w, 1), grp(q_w, 2), grp(k_w, 2), grp(v_w, 2)])
    u = _norm_proj(x2, l2_norm, wa)
    ub = _norm_proj(x2, l2_norm, wb, out_dtype=F32)
    y = _dil_attn(u.reshape(b, s, -1), ub.reshape(b, s, -1), tabs128, q0_blk=0,
                  k0_blk=2 * DIL_HEADS, v0_blk=3 * DIL_HEADS)
    zoff = 4 * DIL_WIDTH // MEM_WIDTH
    x2 = finish(y.reshape(m, DIL_WIDTH), u, x2, l2_mem_norm, l2_w_mem_kv, l2_w_out, wy=DIL_WIDTH,
                zy_blk=1, zm_blk=zoff, qm_blk=zoff + 1)

    q_lora, kv_lora = l3_w_uq.shape[0], l3_w_ukv.shape[0]
    cq_w, ckv_w, kr_w, qm_w, z_w = _cols(l3_w_in, (q_lora, kv_lora, QK_ROPE, MEM_WIDTH,
                                                  mw + MEM_WIDTH))
    assert q_lora == MEM_WIDTH and kv_lora == MEM_WIDTH
    w3 = _pack([(z_w[:, :mw], mw), (z_w[:, mw:], MEM_WIDTH), (qm_w, MEM_WIDTH), (cq_w, q_lora),
                (ckv_w, kv_lora)])
    u, kr_arr = _norm_proj(x2, l3_norm, w3, w_extra=_pack([(kr_w, QK_ROPE)], multiple=LANES))
    cq_blk = (mw + 2 * MEM_WIDTH) // MEM_WIDTH
    uq = l3_w_uq.reshape(q_lora, N_HEADS, QK_NOPE + QK_ROPE)
    w_uq = _pack([(uq[:, :, :QK_NOPE].reshape(q_lora, -1), N_HEADS * QK_NOPE),
                  (uq[:, :, QK_NOPE:].reshape(q_lora, -1), N_HEADS * QK_ROPE)])
    ukv = l3_w_ukv.reshape(kv_lora, N_HEADS, QK_NOPE + HEAD_DIM)
    w_ukv = _pack([(ukv[:, :, :QK_NOPE].reshape(kv_lora, -1), N_HEADS * QK_NOPE),
                   (ukv[:, :, QK_NOPE:].reshape(kv_lora, -1), N_HEADS * HEAD_DIM)])
    qf = _norm_proj(u, l3_q_norm, w_uq, x_col_block=cq_blk)
    kvf = _norm_proj(u, l3_kv_norm, w_ukv, x_col_block=cq_blk + 1)
    y = _mla_attn(qf.reshape(b, s, -1), kvf.reshape(b, s, -1), kr_arr.reshape(b, s, LANES), tabs64,
                  kr_blk=0)
    x2 = finish(y.reshape(m, mw), u, x2, l3_mem_norm, l3_w_mem_kv, l3_w_out, wy=mw, zy_blk=0,
                zm_blk=mw // MEM_WIDTH, qm_blk=mw // MEM_WIDTH + 1, final_g=final_norm)
    return x2.reshape(b, s, d)
```

```python
import functools

import jax
import jax.numpy as jnp
from jax import lax
from jax.experimental import pallas as pl
from jax.experimental.pallas import tpu as pltpu

F32 = jnp.float32
BF16 = jnp.bfloat16
I32 = jnp.int32

EPS = 1e-6
ROPE_THETA = 10000.0
HEAD_DIM = 128
N_HEADS = 16
MEM_HEADS = 4
MEM_WIDTH = MEM_HEADS * HEAD_DIM
IDX_HEADS = 16
IDX_DIM = 64
TOPK_MAX = 256
DILATED_PAIRS = ((128, 1), (512, 4), (2048, 16))
DIL_HEADS = 6
DIL_WIDTH = DIL_HEADS * HEAD_DIM
QK_NOPE = 128
QK_ROPE = 64
LANES = 128
NEG = -1e30
INT_MIN = -(2 ** 31)
VMEM_LIMIT = 56 * 1024 * 1024

NT_DIMS = (((1,), (1,)), ((), ()))
LOG2E = 1.4426950408889634


def _cparams(n_axes):
    return pltpu.CompilerParams(
        dimension_semantics=("arbitrary",) * n_axes, vmem_limit_bytes=VMEM_LIMIT)


def _rope128(x, cos_f, sin_s):
    return x * cos_f + pltpu.roll(x, 64, 1) * sin_s


def _rope64(x, cos_f, sin_s):
    lane = lax.broadcasted_iota(I32, x.shape, 1)
    partner = jnp.where((lane & 32) == 0, pltpu.roll(x, 96, 1), pltpu.roll(x, 32, 1))
    return x * cos_f + partner * sin_s


def _rope_table_kernel(pos_ref, inv_ref, sgn_ref, cos_ref, sin_ref):
    ang = pos_ref[...] * inv_ref[...]
    cos_ref[...] = jnp.cos(ang)
    sin_ref[...] = jnp.sin(ang) * sgn_ref[...]


def _rope_tables(pos_b, dh):
    m = pos_b.shape[0]
    half = dh // 2
    inv = jnp.power(ROPE_THETA, -jnp.arange(half, dtype=F32) * 2.0 / dh)
    reps = LANES // half
    inv_l = jnp.tile(inv, reps).reshape(1, LANES)
    sgn = jnp.tile(jnp.concatenate([-jnp.ones((half,), F32), jnp.ones((half,), F32)]),
                   reps // 2).reshape(1, LANES)
    ts = min(512, m)
    return pl.pallas_call(
        _rope_table_kernel,
        grid=(m // ts,),
        in_specs=[pl.BlockSpec((ts, LANES), lambda i: (i, 0)),
                  pl.BlockSpec((1, LANES), lambda i: (0, 0)),
                  pl.BlockSpec((1, LANES), lambda i: (0, 0))],
        out_specs=[pl.BlockSpec((ts, LANES), lambda i: (i, 0))] * 2,
        out_shape=[jax.ShapeDtypeStruct((m, LANES), F32)] * 2,
        compiler_params=_cparams(1),
        name="rope_tables",
    )(pos_b, inv_l, sgn)


def _norm_proj_kernel(x_ref, g_ref, w_ref, *rest, has_extra):
    if has_extra:
        wx_ref, o_ref, ox_ref, h_ref = rest
    else:
        o_ref, h_ref = rest

    @pl.when(pl.program_id(1) == 0)
    def _():
        x = x_ref[...].astype(F32)
        ms = jnp.mean(x * x, axis=-1, keepdims=True)
        h_ref[...] = (x * lax.rsqrt(ms + EPS) * g_ref[...]).astype(BF16)
        if has_extra:
            ox_ref[...] = jnp.dot(h_ref[...], wx_ref[...],
                                  preferred_element_type=F32).astype(ox_ref.dtype)

    o_ref[...] = jnp.dot(h_ref[...], w_ref[...], preferred_element_type=F32).astype(o_ref.dtype)


def _norm_proj(x, g, w, *, x_col_block=0, out_dtype=BF16, w_extra=None, extra_dtype=BF16):
    m = x.shape[0]
    k, n = w.shape
    tm = min(1024, m)
    tn = 1024 if n % 1024 == 0 else 512
    assert m % tm == 0 and n % tn == 0
    has_extra = w_extra is not None
    in_specs = [pl.BlockSpec((tm, k), lambda i, j: (i, x_col_block)),
                pl.BlockSpec((1, k), lambda i, j: (0, 0)),
                pl.BlockSpec((k, tn), lambda i, j: (0, j))]
    out_specs = [pl.BlockSpec((tm, tn), lambda i, j: (i, j))]
    out_shape = [jax.ShapeDtypeStruct((m, n), out_dtype)]
    args = [x, g.reshape(1, k).astype(F32), w]
    if has_extra:
        in_specs.append(pl.BlockSpec((k, LANES), lambda i, j: (0, 0)))
        out_specs.append(pl.BlockSpec((tm, LANES), lambda i, j: (i, 0)))
        out_shape.append(jax.ShapeDtypeStruct((m, LANES), extra_dtype))
        args.append(w_extra)
    outs = pl.pallas_call(
        functools.partial(_norm_proj_kernel, has_extra=has_extra),
        grid=(m // tm, n // tn),
        in_specs=in_specs,
        out_specs=out_specs,
        out_shape=out_shape,
        scratch_shapes=[pltpu.VMEM((tm, k), BF16)],
        compiler_params=_cparams(2),
        name="norm_proj",
    )(*args)
    return outs if has_extra else outs[0]


def _out_proj_kernel(y_ref, zy_ref, zm_ref, qm_ref, mkv_ref, x_ref, w_ref, *rest, wy, final):
    if final:
        gf_ref, o_ref, gated_ref = rest
    else:
        o_ref, gated_ref = rest
    zy = zy_ref[...].astype(F32)
    gated_ref[:, :wy] = (y_ref[...].astype(F32) * (zy * jax.nn.sigmoid(zy))).astype(BF16)
    scale = HEAD_DIM ** -0.5
    for h in range(MEM_HEADS):
        lo, hi = h * HEAD_DIM, (h + 1) * HEAD_DIM
        s = lax.dot_general(qm_ref[:, lo:hi], mkv_ref[:, lo:hi], NT_DIMS,
                            preferred_element_type=F32) * scale
        m = jnp.max(s, axis=-1, keepdims=True)
        p = jnp.exp(s - m)
        l = jnp.sum(p, axis=-1, keepdims=True)
        o = jnp.dot(p.astype(BF16), mkv_ref[:, MEM_WIDTH + lo:MEM_WIDTH + hi],
                    preferred_element_type=F32) / l
        zm = zm_ref[:, lo:hi].astype(F32)
        gated_ref[:, wy + lo:wy + hi] = (o * (zm * jax.nn.sigmoid(zm))).astype(BF16)
    out = x_ref[...] + jnp.dot(gated_ref[...], w_ref[...], preferred_element_type=F32)
    if final:
        ms = jnp.mean(out * out, axis=-1, keepdims=True)
        out = out * lax.rsqrt(ms + EPS) * gf_ref[...]
    o_ref[...] = out


def _out_proj(y, u, mkv, x, w_out, *, wy, zy_blk, zm_blk, qm_blk, seq, n_mem, final_g=None):
    m, d = x.shape
    tm = min(512, seq)
    final = final_g is not None
    in_specs = [
        pl.BlockSpec((tm, wy), lambda i: (i, 0)),
        pl.BlockSpec((tm, wy), lambda i: (i, zy_blk)),
        pl.BlockSpec((tm, MEM_WIDTH), lambda i: (i, zm_blk)),
        pl.BlockSpec((tm, MEM_WIDTH), lambda i: (i, qm_blk)),
        pl.BlockSpec((n_mem, 2 * MEM_WIDTH), lambda i: ((i * tm) // seq, 0)),
        pl.BlockSpec((tm, d), lambda i: (i, 0)),
        pl.BlockSpec((wy + MEM_WIDTH, d), lambda i: (0, 0), pipeline_mode=pl.Buffered(1)),
    ]
    args = [y, u, u, u, mkv, x, w_out]
    if final:
        in_specs.append(pl.BlockSpec((1, d), lambda i: (0, 0)))
        args.append(final_g.reshape(1, d).astype(F32))
    return pl.pallas_call(
        functools.partial(_out_proj_kernel, wy=wy, final=final),
        grid=(m // tm,),
        in_specs=in_specs,
        out_specs=pl.BlockSpec((tm, d), lambda i: (i, 0)),
        out_shape=jax.ShapeDtypeStruct((m, d), F32),
        scratch_shapes=[pltpu.VMEM((tm, wy + MEM_WIDTH), BF16)],
        compiler_params=_cparams(1),
        name="out_proj",
    )(*args)


def _fox_gate_kernel(f_ref, b_ref, c_ref, carry_ref):
    @pl.when(pl.program_id(1) == 0)
    def _():
        carry_ref[...] = jnp.zeros_like(carry_ref)

    f_t = f_ref[...].astype(F32).T[:N_HEADS, :] + b_ref[...]
    log_f = jnp.minimum(f_t, 0.0) - jnp.log(1.0 + jnp.exp(-jnp.abs(f_t)))
    ts = log_f.shape[1]
    r = lax.broadcasted_iota(I32, (ts, ts), 0)
    c = lax.broadcasted_iota(I32, (ts, ts), 1)
    upper = jnp.where(r <= c, 1.0, 0.0).astype(F32)
    cs = jnp.dot(log_f, upper, precision=lax.Precision.HIGHEST,
                 preferred_element_type=F32) + carry_ref[...]
    c_ref[...] = cs * LOG2E
    carry_ref[...] = cs[:, ts - 1:ts]


def _fox_gate(u3, bias, f_blk):
    b, s, _ = u3.shape
    ts = min(256, s)
    return pl.pallas_call(
        _fox_gate_kernel,
        grid=(b, s // ts),
        in_specs=[pl.BlockSpec((None, ts, LANES), lambda bi, j: (bi, j, f_blk)),
                  pl.BlockSpec((N_HEADS, 1), lambda bi, j: (0, 0))],
        out_specs=pl.BlockSpec((None, N_HEADS, ts), lambda bi, j: (bi, 0, j)),
        out_shape=jax.ShapeDtypeStruct((b, N_HEADS, s), F32),
        scratch_shapes=[pltpu.VMEM((N_HEADS, 1), F32)],
        compiler_params=_cparams(2),
        name="fox_gate",
    )(u3, bias.reshape(N_HEADS, 1).astype(F32))


STRIP = 32


def _softmax_strips(t_ref, p_ref, m_ref, l_ref, alpha_ref, adjust):
    tq, tk = t_ref.shape
    for r0 in range(0, tq, min(STRIP, tq)):
        rs = pl.ds(r0, min(STRIP, tq))
        blocks = [adjust(r0, c, t_ref[rs, c * LANES:(c + 1) * LANES]) for c in range(tk // LANES)]
        live = [x for x in blocks if x is not None]
        mx = live[0]
        for x in live[1:]:
            mx = jnp.maximum(mx, x)
        m_old = m_ref[rs, :]
        m_new = jnp.maximum(m_old, jnp.max(mx, axis=-1, keepdims=True))
        alpha = jnp.exp2(m_old - m_new)
        ps = [None if x is None else jnp.exp2(x - m_new) for x in blocks]
        live = [x for x in ps if x is not None]
        sm = live[0]
        for x in live[1:]:
            sm = sm + x
        l_ref[rs, :] = alpha * l_ref[rs, :] + jnp.sum(sm, axis=-1, keepdims=True)
        m_ref[rs, :] = m_new
        alpha_ref[rs, :] = alpha
        for c, x in enumerate(ps):
            p_ref[rs, c * LANES:(c + 1) * LANES] = (
                jnp.zeros((min(STRIP, tq), LANES), BF16) if x is None else x.astype(BF16))


def _flash_scratch(n_streams, tq):
    return [pltpu.VMEM((2, n_streams, tq, tq), F32), pltpu.VMEM((n_streams, tq, tq), BF16),
            pltpu.VMEM((n_streams, tq, LANES), F32), pltpu.VMEM((n_streams, tq, LANES), F32),
            pltpu.VMEM((n_streams, tq, LANES), F32), pltpu.VMEM((n_streams, tq, HEAD_DIM), F32)]


def _flash_init(scratch):
    _, _, m_ref, l_ref, _, acc_ref = scratch
    m_ref[...] = jnp.full(m_ref.shape, NEG, F32)
    l_ref[...] = jnp.zeros(l_ref.shape, F32)
    acc_ref[...] = jnp.zeros(acc_ref.shape, F32)


def _flash_scores(n_streams, j, raw_scores, scratch):
    t2_ref = scratch[0]
    for a in range(n_streams):
        t2_ref[j % 2, a] = raw_scores(a, j)


def _flash_chunk(n_streams, j, values, to_logits, scratch, *, diagonal):
    t2_ref, p_ref, m_ref, l_ref, alpha_ref, acc_ref = scratch
    strip = min(STRIP, p_ref.shape[1])
    row = lax.broadcasted_iota(I32, (strip, LANES), 0)
    col = lax.broadcasted_iota(I32, (strip, LANES), 1)
    for a in range(n_streams):
        def adjust(r0, c, x, a=a):
            t = to_logits(a, j, r0, c, x)
            if not diagonal or c * LANES + LANES - 1 <= r0:
                return t
            if c * LANES > r0 + strip - 1:
                return None
            return jnp.where(col + c * LANES <= row + r0, t, NEG)

        _softmax_strips(t2_ref.at[j % 2, a], p_ref.at[a], m_ref.at[a], l_ref.at[a],
                        alpha_ref.at[a], adjust)
    for a in range(n_streams):
        acc_ref[a] = alpha_ref[a] * acc_ref[a] + jnp.dot(p_ref[a], values(a, j),
                                                         preferred_element_type=F32)


def _flash_finish(n_streams, scratch):
    _, _, _, l_ref, _, acc_ref = scratch
    return [acc_ref[a] / l_ref[a] for a in range(n_streams)]


def _causal_flash(i, n_tiles, n_streams, raw_scores, values, to_logits, scratch):
    _flash_init(scratch)
    for ii in range(n_tiles):
        @pl.when(i == ii)
        def _(ii=ii):
            _flash_scores(n_streams, 0, raw_scores, scratch)
            for j in range(ii):
                _flash_scores(n_streams, j + 1, raw_scores, scratch)
                _flash_chunk(n_streams, j, values, to_logits, scratch, diagonal=False)
            _flash_chunk(n_streams, ii, values, to_logits, scratch, diagonal=True)
    return _flash_finish(n_streams, scratch)


def _fox_attn_kernel(q_ref, k_ref, v_ref, c_ref, o_ref, qs_ref, *scratch, tq, n_tiles):
    i = pl.program_id(2)
    qk_scale = HEAD_DIM ** -0.5 * LOG2E
    qs_ref[...] = (q_ref[...].astype(F32) * qk_scale).astype(BF16)

    def raw_scores(a, j):
        off = j * tq
        lo, hi = a * HEAD_DIM, (a + 1) * HEAD_DIM
        return lax.dot_general(qs_ref[:, lo:hi], k_ref[pl.ds(off, tq), lo:hi], NT_DIMS,
                               preferred_element_type=F32)

    def to_logits(a, j, r0, c, x):
        return x - c_ref[a, j, :, c * LANES:(c + 1) * LANES]

    def values(a, j):
        off = j * tq
        return v_ref[pl.ds(off, tq), a * HEAD_DIM:(a + 1) * HEAD_DIM]

    outs = _causal_flash(i, n_tiles, 2, raw_scores, values, to_logits, scratch)
    for a in range(2):
        o_ref[:, a * HEAD_DIM:(a + 1) * HEAD_DIM] = outs[a].astype(o_ref.dtype)


def _fox_attn(u3, c_t, *, q_blk, k_blk, v_blk):
    b, s, _ = u3.shape
    tq = min(512, s)
    nq = s // tq
    pw = 2 * HEAD_DIM
    c5 = c_t.reshape(b, N_HEADS, nq, 1, tq)
    return pl.pallas_call(
        functools.partial(_fox_attn_kernel, tq=tq, n_tiles=nq),
        grid=(b, N_HEADS // 2, nq),
        in_specs=[pl.BlockSpec((None, tq, pw), lambda bi, p, i: (bi, i, q_blk + p)),
                  pl.BlockSpec((None, s, pw), lambda bi, p, i: (bi, 0, k_blk + p)),
                  pl.BlockSpec((None, s, pw), lambda bi, p, i: (bi, 0, v_blk + p)),
                  pl.BlockSpec((None, 2, nq, 1, tq), lambda bi, p, i: (bi, p, 0, 0, 0))],
        out_specs=pl.BlockSpec((None, tq, pw), lambda bi, p, i: (bi, i, p)),
        out_shape=jax.ShapeDtypeStruct((b, s, N_HEADS * HEAD_DIM), BF16),
        scratch_shapes=[pltpu.VMEM((tq, pw), BF16)] + _flash_scratch(2, tq),
        compiler_params=_cparams(3),
        name="fox_attn",
    )(u3, u3, u3, c5)


def _dsa_kprep_kernel(k_ref, mk_ref, c128_ref, s128_ref, c64_ref, s64_ref,
                      kr_ref, klo_ref, khi_ref):
    kr_ref[...] = _rope128(k_ref[...].astype(F32), c128_ref[...], s128_ref[...]).astype(BF16)
    ki = _rope64(mk_ref[...].astype(F32), c64_ref[...], s64_ref[...])
    lane = lax.broadcasted_iota(I32, ki.shape, 1)
    lo = jnp.where(lane < IDX_DIM, ki, 0.0)
    klo_ref[...] = lo.astype(BF16)
    khi_ref[...] = pltpu.roll(lo, IDX_DIM, 1).astype(BF16)


def _dsa_kprep(u3, tabs, *, k_blk, misc_blk):
    b, s, _ = u3.shape
    ts = min(512, s)
    c128, s128, c64, s64 = tabs

    def spec(blk):
        return pl.BlockSpec((None, ts, LANES), lambda bi, i: (bi, i, blk))

    return pl.pallas_call(
        _dsa_kprep_kernel,
        grid=(b, s // ts),
        in_specs=[spec(k_blk), spec(misc_blk), spec(0), spec(0), spec(0), spec(0)],
        out_specs=[spec(0)] * 3,
        out_shape=[jax.ShapeDtypeStruct((b, s, LANES), BF16)] * 3,
        compiler_params=_cparams(2),
        name="dsa_kprep",
    )(u3, u3, c128, s128, c64, s64)


def _dsa_kernel(q_ref, qi_ref, mq_ref, v_ref, kr_ref, klo_ref, khi_ref,
                cq128_ref, sq128_ref, cq64_ref, sq64_ref, y_in_ref,
                o_ref,
                qr_ref, qir_ref, w_ref, sc_ref, t4_ref, yo_ref, *flash,
                tq, kl, q_off, n_sel):
    del y_in_ref
    t0 = q_off + pl.program_id(1) * tq
    qk_scale = HEAD_DIM ** -0.5 * LOG2E
    idx_scale = (IDX_DIM ** -0.5) * (IDX_HEADS ** -0.5)
    ck = min(512, kl)
    n_chunks = kl // ck
    strip = min(STRIP, tq)

    cq128, sq128 = cq128_ref[...], sq128_ref[...]
    for h in range(N_HEADS):
        qh = q_ref[:, h * HEAD_DIM:(h + 1) * HEAD_DIM].astype(F32)
        qr_ref[h] = (_rope128(qh, cq128, sq128) * qk_scale).astype(BF16)
    cq64, sq64 = cq64_ref[...], sq64_ref[...]
    for a in range(IDX_HEADS // 2):
        qir_ref[a] = _rope64(qi_ref[:, a * LANES:(a + 1) * LANES].astype(F32),
                             cq64, sq64).astype(BF16)
    w_ref[...] = mq_ref[...].astype(F32) * idx_scale

    row = lax.broadcasted_iota(I32, (strip, LANES), 0)
    col = lax.broadcasted_iota(I32, (strip, LANES), 1)
    group = 4

    def idx_chunk(j, _):
        off = pl.multiple_of(j * ck, ck)
        k_lo, k_hi = klo_ref[pl.ds(off, ck), :], khi_ref[pl.ds(off, ck), :]
        for g in range(IDX_HEADS // group):
            for a2 in range(group // 2):
                x = qir_ref[g * (group // 2) + a2]
                t4_ref[2 * a2] = lax.dot_general(x, k_lo, NT_DIMS, preferred_element_type=F32)
                t4_ref[2 * a2 + 1] = lax.dot_general(x, k_hi, NT_DIMS,
                                                     preferred_element_type=F32)
            last = g == IDX_HEADS // group - 1
            for r0 in range(0, tq, strip):
                rs = pl.ds(r0, strip)
                w_rows = w_ref[rs, :]
                ws = [jnp.broadcast_to(w_rows[:, IDX_DIM + g * group + u:IDX_DIM + g * group + u + 1],
                                       (strip, LANES)) for u in range(group)]
                for c in range(ck // LANES):
                    cs = pl.ds(c * LANES, LANES)
                    acc = ws[0] * jnp.maximum(t4_ref[0, rs, cs], 0.0)
                    for u in range(1, group):
                        acc = acc + ws[u] * jnp.maximum(t4_ref[u, rs, cs], 0.0)
                    if g > 0:
                        acc = acc + sc_ref[j, rs, cs]
                    if last:
                        causal = col + (off + c * LANES) <= row + (t0 + r0)
                        acc = jnp.where(causal, acc, -jnp.inf)
                    sc_ref[j, rs, cs] = acc
        return 0

    lax.fori_loop(0, n_chunks, idx_chunk, 0)

    shape3 = (n_chunks, tq, ck)
    col3 = lax.broadcasted_iota(I32, shape3, 0) * ck + lax.broadcasted_iota(I32, shape3, 2)
    causal3 = col3 <= lax.broadcasted_iota(I32, shape3, 1) + t0

    def count(x):
        return jnp.sum(jnp.sum(x, axis=0), axis=-1, keepdims=True)

    def key_to_f32(key):
        return pltpu.bitcast(jnp.where(key < 0, key ^ jnp.int32(0x7FFFFFFF), key), F32)

    def thr_body(it, key):
        cand = key + (jnp.int32(1) << (31 - it))
        c = count(jnp.where(sc_ref[...] >= key_to_f32(cand), 1.0, 0.0))
        return jnp.where(c >= n_sel, cand, key)

    thr_key = lax.fori_loop(0, 32, thr_body, jnp.full((tq, 1), INT_MIN, I32))
    take_all = thr_key == INT_MIN
    thr = key_to_f32(thr_key)
    score = sc_ref[...]
    need = n_sel - count(jnp.where(score > thr, 1.0, 0.0))
    surplus = jnp.where(take_all, 0.0, count(jnp.where(score == thr, 1.0, 0.0)) - need)
    has_surplus = jnp.max(surplus) > 0.0

    n_bits = kl.bit_length()

    def tie_body(it, jm):
        cand = jm + (jnp.int32(1) << (n_bits - 1 - it))
        f = count(jnp.where(sc_ref[...] == thr, jnp.where(col3 < cand, 1.0, 0.0), 0.0))
        return jnp.where(jnp.logical_and(f < need, cand <= kl), cand, jm)

    jm = lax.fori_loop(0, jnp.where(has_surplus, n_bits, 0), tie_body,
                       jnp.broadcast_to(jnp.where(has_surplus, 0, kl), (tq, 1)).astype(I32))
    keep = jnp.where(jnp.logical_or(score > thr, take_all), 1.0,
                     jnp.where(score == thr, jnp.where(col3 <= jm, 1.0, 0.0), 0.0))
    sc_ref[...] = jnp.where(causal3, jnp.where(keep > 0.0, 0.0, NEG), NEG)

    def raw_scores(h0):
        def f(a, j):
            return lax.dot_general(qr_ref[h0 + a], kr_ref[j * ck:(j + 1) * ck, :], NT_DIMS,
                                   preferred_element_type=F32)
        return f

    def values(a, j):
        return v_ref[j * ck:(j + 1) * ck, :]

    def to_logits(a, j, r0, c, x):
        return x + sc_ref[j, r0:r0 + strip, c * LANES:(c + 1) * LANES]

    def pair_body(pp, _):
        h0 = 2 * pp
        _flash_init(flash)
        _flash_scores(2, 0, raw_scores(h0), flash)
        for j in range(n_chunks):
            if j + 1 < n_chunks:
                _flash_scores(2, j + 1, raw_scores(h0), flash)
            _flash_chunk(2, j, values, to_logits, flash, diagonal=False)
        outs = _flash_finish(2, flash)
        for a in range(2):
            yo_ref[h0 + a] = outs[a].astype(yo_ref.dtype)
        return 0

    lax.fori_loop(0, N_HEADS // 2, pair_body, 0)
    for h in range(N_HEADS):
        o_ref[:, h * HEAD_DIM:(h + 1) * HEAD_DIM] = yo_ref[h]


def _dsa_group(u3, kprep, tabs, y, *, q_off, rows, kl, n_sel, q_blk, v_blk, qi_blk, misc_blk):
    b, s, _ = u3.shape
    tq = rows
    assert kl % min(512, kl) == 0 and tq == min(512, kl)
    qb0 = q_off // tq
    c128, s128, c64, s64 = tabs
    kr, klo, khi = kprep
    qw = N_HEADS * HEAD_DIM
    iw = IDX_HEADS * IDX_DIM
    ck = min(512, kl)

    def qspec(width, blk):
        return pl.BlockSpec((None, tq, width), lambda bi, i: (bi, qb0 + i, blk))

    def kspec(blk):
        return pl.BlockSpec((None, kl, LANES), lambda bi, i: (bi, 0, blk))

    return pl.pallas_call(
        functools.partial(_dsa_kernel, tq=tq, kl=kl, q_off=q_off, n_sel=n_sel),
        grid=(b, rows // tq),
        in_specs=[qspec(qw, q_blk), qspec(iw, qi_blk), qspec(LANES, misc_blk),
                  kspec(v_blk), kspec(0), kspec(0), kspec(0),
                  qspec(LANES, 0), qspec(LANES, 0), qspec(LANES, 0), qspec(LANES, 0),
                  pl.BlockSpec(memory_space=pl.ANY)],
        out_specs=pl.BlockSpec((None, tq, qw), lambda bi, i: (bi, qb0 + i, 0)),
        out_shape=jax.ShapeDtypeStruct((b, s, qw), BF16),
        input_output_aliases={11: 0},
        scratch_shapes=[pltpu.VMEM((N_HEADS, tq, HEAD_DIM), BF16),
                        pltpu.VMEM((IDX_HEADS // 2, tq, LANES), BF16),
                        pltpu.VMEM((tq, LANES), F32),
                        pltpu.VMEM((kl // ck, tq, ck), F32),
                        pltpu.VMEM((4, tq, ck), F32),
                        pltpu.VMEM((N_HEADS, tq, HEAD_DIM), BF16)] + _flash_scratch(2, tq),
        compiler_params=_cparams(2),
        name="dsa_attn",
    )(u3, u3, u3, u3, kr, klo, khi, c128, s128, c64, s64, y)


def _dsa_attn(u3, tabs, *, k_blk, misc_blk, **blks):
    b, s, _ = u3.shape
    n_sel = min(TOPK_MAX, s // 4)
    rows = min(512, s)
    kprep = _dsa_kprep(u3, tabs, k_blk=k_blk, misc_blk=misc_blk)
    y = jnp.zeros((b, s, N_HEADS * HEAD_DIM), BF16)
    for q_off in range(0, s, rows):
        y = _dsa_group(u3, kprep, tabs, y, q_off=q_off, rows=rows, kl=q_off + rows, n_sel=n_sel,
                       misc_blk=misc_blk, **blks)
    return y


def _dil_kernel(q0_ref, k0_ref, v0_ref, q1_ref, k1_ref, v1_ref, q2_ref, k2_ref, v2_ref,
                cos_ref, sin_ref, y_ref, qr_ref, kr_ref, o_ref, lse_ref, t_ref, p_ref, *, seq):
    qk_scale = HEAD_DIM ** -0.5 * LOG2E
    cos_f, sin_s = cos_ref[...], sin_ref[...]
    groups = ((q0_ref, k0_ref, v0_ref), (q1_ref, k1_ref, v1_ref), (q2_ref, k2_ref, v2_ref))
    for g, (q_ref, k_ref, _) in enumerate(groups):
        qr_ref[g] = _rope128(q_ref[...].astype(F32), cos_f, sin_s) * qk_scale
        kr_ref[g] = _rope128(k_ref[...].astype(F32), cos_f, sin_s)

    for g, (window, dil) in enumerate(DILATED_PAIRS):
        v_ref = groups[g][2]
        sub = seq // dil
        qb = min(window // dil, sub)
        tiles = [(r, i) for r in range(dil) for i in range(sub // qb)]

        def rows(r, blk, dil=dil, qb=qb):
            start = r + dil * qb * blk
            return pl.ds(start, qb) if dil == 1 else pl.ds(start, qb, stride=dil)

        for n, (r, i) in enumerate(tiles):
            q = qr_ref[g, rows(r, i), :].astype(BF16)
            k_cur = kr_ref[g, rows(r, i), :].astype(BF16)
            t_ref[n, :qb, qb:2 * qb] = lax.dot_general(q, k_cur, NT_DIMS,
                                                       preferred_element_type=F32)
            if i > 0:
                k_prev = kr_ref[g, rows(r, i - 1), :].astype(BF16)
                t_ref[n, :qb, :qb] = lax.dot_general(q, k_prev, NT_DIMS,
                                                     preferred_element_type=F32)
        row = lax.broadcasted_iota(I32, (qb, qb), 0)
        col = lax.broadcasted_iota(I32, (qb, qb), 1)
        for n, (r, i) in enumerate(tiles):
            t_cur = jnp.where(col <= row, t_ref[n, :qb, qb:2 * qb], NEG)
            m = jnp.max(t_cur, axis=-1, keepdims=True)
            if i > 0:
                t_prev = jnp.where(col >= row, t_ref[n, :qb, :qb], NEG)
                m = jnp.maximum(m, jnp.max(t_prev, axis=-1, keepdims=True))
            p_cur = jnp.exp2(t_cur - m)
            l = jnp.sum(p_cur, axis=-1, keepdims=True)
            p_ref[n, :qb, qb:2 * qb] = p_cur.astype(BF16)
            if i > 0:
                p_prev = jnp.exp2(t_prev - m)
                l = l + jnp.sum(p_prev, axis=-1, keepdims=True)
                p_ref[n, :qb, :qb] = p_prev.astype(BF16)
            t_ref[n, :qb, :LANES] = jnp.broadcast_to(l, (qb, LANES))
            t_ref[n, :qb, LANES:2 * LANES] = jnp.broadcast_to(m, (qb, LANES))
        for n, (r, i) in enumerate(tiles):
            acc = jnp.dot(p_ref[n, :qb, qb:2 * qb], v_ref[rows(r, i), :].astype(BF16),
                          preferred_element_type=F32)
            if i > 0:
                acc = acc + jnp.dot(p_ref[n, :qb, :qb], v_ref[rows(r, i - 1), :].astype(BF16),
                                    preferred_element_type=F32)
            l = t_ref[n, :qb, :LANES]
            o_ref[g, rows(r, i), :] = acc / l
            lse_ref[g, rows(r, i), :] = t_ref[n, :qb, LANES:2 * LANES] + jnp.log2(l)

    l0, l1, l2 = lse_ref[0], lse_ref[1], lse_ref[2]
    m = jnp.maximum(jnp.maximum(l0, l1), l2)
    e0, e1, e2 = jnp.exp2(l0 - m), jnp.exp2(l1 - m), jnp.exp2(l2 - m)
    y = (e0 * o_ref[0] + e1 * o_ref[1] + e2 * o_ref[2]) / (e0 + e1 + e2)
    y_ref[...] = y.astype(y_ref.dtype)


def _dil_attn(ua3, ub3, tabs128, *, q0_blk, k0_blk, v0_blk):
    b, s, _ = ua3.shape
    for window, dil in DILATED_PAIRS:
        assert s % dil == 0 and (s // dil) % min(window // dil, s // dil) == 0
    cos_t, sin_t = tabs128
    tile = min(DILATED_PAIRS[0][0], s)
    n_tiles = s // min(min(w // d, s // d) for w, d in DILATED_PAIRS)

    def spec(blk0):
        return pl.BlockSpec((None, s, HEAD_DIM), lambda bi, a: (bi, 0, blk0 + a))

    tab = pl.BlockSpec((None, s, LANES), lambda bi, a: (bi, 0, 0))
    return pl.pallas_call(
        functools.partial(_dil_kernel, seq=s),
        grid=(b, DIL_HEADS),
        in_specs=[spec(q0_blk), spec(k0_blk), spec(v0_blk)]
        + [spec(j * DIL_HEADS) for j in range(6)] + [tab, tab],
        out_specs=pl.BlockSpec((None, s, HEAD_DIM), lambda bi, a: (bi, 0, a)),
        out_shape=jax.ShapeDtypeStruct((b, s, DIL_WIDTH), BF16),
        scratch_shapes=[pltpu.VMEM((3, s, HEAD_DIM), F32)] * 4
        + [pltpu.VMEM((n_tiles, tile, 2 * LANES), F32), pltpu.VMEM((n_tiles, tile, 2 * LANES), BF16)],
        compiler_params=_cparams(2),
        name="dilated_attn",
    )(ua3, ua3, ua3, ub3, ub3, ub3, ub3, ub3, ub3, cos_t, sin_t)


def _mla_kernel(qn_ref, qr_ref, kn_ref, kr_ref, v_ref, cq_ref, sq_ref, ck_ref, sk_ref,
                o_ref, qcat_ref, kcat_ref, *scratch, tq, n_tiles):
    i = pl.program_id(2)
    qk_scale = (QK_NOPE + QK_ROPE) ** -0.5 * LOG2E
    cat = QK_NOPE + QK_ROPE

    @pl.when(i == 0)
    def _():
        k_rope = _rope64(kr_ref[...].astype(F32), ck_ref[...],
                         sk_ref[...])[:, :QK_ROPE].astype(BF16)
        for a in range(2):
            kcat_ref[a, :, :QK_NOPE] = kn_ref[:, a * HEAD_DIM:(a + 1) * HEAD_DIM]
            kcat_ref[a, :, QK_NOPE:cat] = k_rope
            kcat_ref[a, :, cat:] = jnp.zeros((kcat_ref.shape[1], 2 * LANES - cat), BF16)

    q_rope = (_rope64(qr_ref[...].astype(F32), cq_ref[...], sq_ref[...]) * qk_scale).astype(BF16)
    for a in range(2):
        q_nope = qn_ref[:, a * HEAD_DIM:(a + 1) * HEAD_DIM].astype(F32) * qk_scale
        qcat_ref[a, :, :QK_NOPE] = q_nope.astype(BF16)
        qcat_ref[a, :, QK_NOPE:cat] = q_rope[:, a * QK_ROPE:(a + 1) * QK_ROPE]
        qcat_ref[a, :, cat:] = jnp.zeros((tq, 2 * LANES - cat), BF16)

    def raw_scores(a, j):
        off = j * tq
        return lax.dot_general(qcat_ref[a], kcat_ref[a, pl.ds(off, tq), :], NT_DIMS,
                               preferred_element_type=F32)

    def to_logits(a, j, r0, c, x):
        return x

    def values(a, j):
        off = j * tq
        return v_ref[pl.ds(off, tq), a * HEAD_DIM:(a + 1) * HEAD_DIM]

    outs = _causal_flash(i, n_tiles, 2, raw_scores, values, to_logits, scratch)
    for a in range(2):
        o_ref[:, a * HEAD_DIM:(a + 1) * HEAD_DIM] = outs[a].astype(o_ref.dtype)


def _mla_attn(qf3, kvf3, u3, tabs64, *, kr_blk):
    b, s, _ = qf3.shape
    tq = min(512, s)
    pw = 2 * HEAD_DIM
    n_pairs = N_HEADS // 2
    cos_t, sin_t = tabs64
    return pl.pallas_call(
        functools.partial(_mla_kernel, tq=tq, n_tiles=s // tq),
        grid=(b, n_pairs, s // tq),
        in_specs=[pl.BlockSpec((None, tq, pw), lambda bi, p, i: (bi, i, p)),
                  pl.BlockSpec((None, tq, LANES), lambda bi, p, i: (bi, i, N_HEADS + p)),
                  pl.BlockSpec((None, s, pw), lambda bi, p, i: (bi, 0, p)),
                  pl.BlockSpec((None, s, LANES), lambda bi, p, i: (bi, 0, kr_blk)),
                  pl.BlockSpec((None, s, pw), lambda bi, p, i: (bi, 0, n_pairs + p)),
                  pl.BlockSpec((None, tq, LANES), lambda bi, p, i: (bi, i, 0)),
                  pl.BlockSpec((None, tq, LANES), lambda bi, p, i: (bi, i, 0)),
                  pl.BlockSpec((None, s, LANES), lambda bi, p, i: (bi, 0, 0)),
                  pl.BlockSpec((None, s, LANES), lambda bi, p, i: (bi, 0, 0))],
        out_specs=pl.BlockSpec((None, tq, pw), lambda bi, p, i: (bi, i, p)),
        out_shape=jax.ShapeDtypeStruct((b, s, N_HEADS * HEAD_DIM), BF16),
        scratch_shapes=[pltpu.VMEM((2, tq, 2 * LANES), BF16),
                        pltpu.VMEM((2, s, 2 * LANES), BF16)] + _flash_scratch(2, tq),
        compiler_params=_cparams(3),
        name="mla_attn",
    )(qf3, qf3, kvf3, u3, kvf3, cos_t, sin_t, cos_t, sin_t)


def _pack(parts, multiple=512):
    k = next(p.shape[0] for p, _ in parts if p is not None)
    cols = [jnp.zeros((k, w), BF16) if p is None else p.astype(BF16) for p, w in parts]
    n = sum(w for _, w in parts)
    pad = (-n) % multiple
    if pad:
        cols.append(jnp.zeros((k, pad), BF16))
    return jnp.concatenate(cols, axis=1)


def _cols(w, sizes):
    out, acc = [], 0
    for sz in sizes:
        out.append(w[:, acc:acc + sz])
        acc += sz
    return out


def kernel(x, mem, positions, l0_norm, l0_w_in, l0_forget_bias, l0_mem_norm, l0_w_mem_kv, l0_w_out, l1_norm, l1_w_in, l1_mem_norm, l1_w_mem_kv, l1_w_out, l2_norm, l2_w_in, l2_mem_norm, l2_w_mem_kv, l2_w_out, l3_norm, l3_w_in, l3_q_norm, l3_w_uq, l3_kv_norm, l3_w_ukv, l3_mem_norm, l3_w_mem_kv, l3_w_out, final_norm):
    b, s, d = x.shape
    n_mem = mem.shape[1]
    m = b * s
    mw = N_HEADS * HEAD_DIM
    x2 = x.reshape(m, d)
    mem2 = mem.reshape(b * n_mem, d)

    pos_b = jnp.broadcast_to(positions.astype(F32).reshape(m, 1), (m, LANES))
    tabs128 = tuple(t.reshape(b, s, LANES) for t in _rope_tables(pos_b, HEAD_DIM))
    tabs64 = tuple(t.reshape(b, s, LANES) for t in _rope_tables(pos_b, IDX_DIM))

    def mem_kv(g, w):
        return _norm_proj(mem2, g, w.astype(BF16))

    def finish(y, u, x_in, g_mem, w_mem_kv, w_out, *, wy, zy_blk, zm_blk, qm_blk, final_g=None):
        return _out_proj(y, u, mem_kv(g_mem, w_mem_kv), x_in, w_out.astype(BF16), wy=wy,
                         zy_blk=zy_blk, zm_blk=zm_blk, qm_blk=qm_blk, seq=s, n_mem=n_mem,
                         final_g=final_g)

    q_w, k_w, v_w, f_w, qm_w, z_w = _cols(l0_w_in, (mw, mw, mw, N_HEADS, MEM_WIDTH, mw + MEM_WIDTH))
    w0 = _pack([(z_w[:, :mw], mw), (z_w[:, mw:], MEM_WIDTH), (qm_w, MEM_WIDTH), (q_w, mw),
                (k_w, mw), (v_w, mw)])
    u, f_arr = _norm_proj(x2, l0_norm, w0, w_extra=_pack([(f_w, N_HEADS)], multiple=LANES),
                          extra_dtype=F32)
    u3 = u.reshape(b, s, -1)
    base = (mw + 2 * MEM_WIDTH) // LANES
    c_t = _fox_gate(f_arr.reshape(b, s, LANES), l0_forget_bias, 0)
    y = _fox_attn(u3, c_t, q_blk=base // 2, k_blk=(base + N_HEADS) // 2,
                  v_blk=(base + 2 * N_HEADS) // 2)
    x2 = finish(y.reshape(m, mw), u, x2, l0_mem_norm, l0_w_mem_kv, l0_w_out, wy=mw, zy_blk=0,
                zm_blk=mw // MEM_WIDTH, qm_blk=mw // MEM_WIDTH + 1)

    iw = IDX_HEADS * IDX_DIM
    q_w, k_w, v_w, qi_w, ki_w, wi_w, qm_w, z_w = _cols(
        l1_w_in, (mw, HEAD_DIM, HEAD_DIM, iw, IDX_DIM, IDX_HEADS, MEM_WIDTH, mw + MEM_WIDTH))
    w1 = _pack([(z_w[:, :mw], mw), (q_w, mw), (qi_w, iw), (z_w[:, mw:], MEM_WIDTH),
                (qm_w, MEM_WIDTH), (k_w, HEAD_DIM), (v_w, HEAD_DIM), (ki_w, IDX_DIM),
                (wi_w, IDX_HEADS), (None, LANES - IDX_DIM - IDX_HEADS)])
    u = _norm_proj(x2, l1_norm, w1)
    u3 = u.reshape(b, s, -1)
    kv0 = (2 * mw + iw + 2 * MEM_WIDTH) // LANES
    y = _dsa_attn(u3, tabs128 + tabs64, q_blk=1, qi_blk=2 * mw // iw, k_blk=kv0, v_blk=kv0 + 1,
                  misc_blk=kv0 + 2)
    zoff = (2 * mw + iw) // MEM_WIDTH
    x2 = finish(y.reshape(m, mw), u, x2, l1_mem_norm, l1_w_mem_kv, l1_w_out, wy=mw, zy_blk=0,
                zm_blk=zoff, qm_blk=zoff + 1)

    gw = len(DILATED_PAIRS) * DIL_WIDTH
    q_w, k_w, v_w, qm_w, z_w = _cols(l2_w_in, (gw, gw, gw, MEM_WIDTH, DIL_WIDTH + MEM_WIDTH))

    def grp(w, g):
        return (w[:, g * DIL_WIDTH:(g + 1) * DIL_WIDTH], DIL_WIDTH)

    wa = _pack([grp(q_w, 0), (z_w[:, :DIL_WIDTH], DIL_WIDTH), grp(k_w, 0), grp(v_w, 0),
                (z_w[:, DIL_WIDTH:], MEM_WIDTH), (qm_w, MEM_WIDTH)])
    wb = _pack([grp(q_w, 1), grp(k_w, 1), grp(v_w, 1), grp(q_w, 2), grp(k_w, 2), grp(v_w, 2)])
    u = _norm_proj(x2, l2_norm, wa)
    ub = _norm_proj(x2, l2_norm, wb, out_dtype=F32)
    y = _dil_attn(u.reshape(b, s, -1), ub.reshape(b, s, -1), tabs128, q0_blk=0,
                  k0_blk=2 * DIL_HEADS, v0_blk=3 * DIL_HEADS)
    zoff = 4 * DIL_WIDTH // MEM_WIDTH
    x2 = finish(y.reshape(m, DIL_WIDTH), u, x2, l2_mem_norm, l2_w_mem_kv, l2_w_out, wy=DIL_WIDTH,
                zy_blk=1, zm_blk=zoff, qm_blk=zoff + 1)

    q_lora, kv_lora = l3_w_uq.shape[0], l3_w_ukv.shape[0]
    cq_w, ckv_w, kr_w, qm_w, z_w = _cols(l3_w_in, (q_lora, kv_lora, QK_ROPE, MEM_WIDTH,
                                                  mw + MEM_WIDTH))
    assert q_lora == MEM_WIDTH and kv_lora == MEM_WIDTH
    w3 = _pack([(z_w[:, :mw], mw), (z_w[:, mw:], MEM_WIDTH), (qm_w, MEM_WIDTH), (cq_w, q_lora),
                (ckv_w, kv_lora)])
    u, kr_arr = _norm_proj(x2, l3_norm, w3, w_extra=_pack([(kr_w, QK_ROPE)], multiple=LANES))
    cq_blk = (mw + 2 * MEM_WIDTH) // MEM_WIDTH
    uq = l3_w_uq.reshape(q_lora, N_HEADS, QK_NOPE + QK_ROPE)
    w_uq = _pack([(uq[:, :, :QK_NOPE].reshape(q_lora, -1), N_HEADS * QK_NOPE),
                  (uq[:, :, QK_NOPE:].reshape(q_lora, -1), N_HEADS * QK_ROPE)])
    ukv = l3_w_ukv.reshape(kv_lora, N_HEADS, QK_NOPE + HEAD_DIM)
    w_ukv = _pack([(ukv[:, :, :QK_NOPE].reshape(kv_lora, -1), N_HEADS * QK_NOPE),
                   (ukv[:, :, QK_NOPE:].reshape(kv_lora, -1), N_HEADS * HEAD_DIM)])
    qf = _norm_proj(u, l3_q_norm, w_uq, x_col_block=cq_blk)
    kvf = _norm_proj(u, l3_kv_norm, w_ukv, x_col_block=cq_blk + 1)
    y = _mla_attn(qf.reshape(b, s, -1), kvf.reshape(b, s, -1), kr_arr.reshape(b, s, LANES), tabs64,
                  kr_blk=0)
    x2 = finish(y.reshape(m, mw), u, x2, l3_mem_norm, l3_w_mem_kv, l3_w_out, wy=mw, zy_blk=0,
                zm_blk=mw // MEM_WIDTH, qm_blk=mw // MEM_WIDTH + 1, final_g=final_norm)
    return x2.reshape(b, s, d)
```

```python
import functools

import jax
import jax.numpy as jnp
from jax import lax
from jax.experimental import pallas as pl
from jax.experimental.pallas import tpu as pltpu

F32 = jnp.float32
BF16 = jnp.bfloat16
I32 = jnp.int32

EPS = 1e-6
ROPE_THETA = 10000.0
HEAD_DIM = 128
N_HEADS = 16
MEM_HEADS = 4
MEM_WIDTH = MEM_HEADS * HEAD_DIM
IDX_HEADS = 16
IDX_DIM = 64
TOPK_MAX = 256
DILATED_PAIRS = ((128, 1), (512, 4), (2048, 16))
DIL_HEADS = 6
DIL_WIDTH = DIL_HEADS * HEAD_DIM
QK_NOPE = 128
QK_ROPE = 64
LANES = 128
NEG = -1e30
INT_MIN = -(2 ** 31)
VMEM_LIMIT = 56 * 1024 * 1024

NT_DIMS = (((1,), (1,)), ((), ()))
LOG2E = 1.4426950408889634


def _cparams(n_axes):
    return pltpu.CompilerParams(
        dimension_semantics=("arbitrary",) * n_axes, vmem_limit_bytes=VMEM_LIMIT)


def _rope128(x, cos_f, sin_s):
    return x * cos_f + pltpu.roll(x, 64, 1) * sin_s


def _rope64(x, cos_f, sin_s):
    lane = lax.broadcasted_iota(I32, x.shape, 1)
    partner = jnp.where((lane & 32) == 0, pltpu.roll(x, 96, 1), pltpu.roll(x, 32, 1))
    return x * cos_f + partner * sin_s


def _rope_table_kernel(pos_ref, inv_ref, sgn_ref, cos_ref, sin_ref):
    ang = pos_ref[...] * inv_ref[...]
    cos_ref[...] = jnp.cos(ang)
    sin_ref[...] = jnp.sin(ang) * sgn_ref[...]


def _rope_tables(pos_b, dh):
    m = pos_b.shape[0]
    half = dh // 2
    inv = jnp.power(ROPE_THETA, -jnp.arange(half, dtype=F32) * 2.0 / dh)
    reps = LANES // half
    inv_l = jnp.tile(inv, reps).reshape(1, LANES)
    sgn = jnp.tile(jnp.concatenate([-jnp.ones((half,), F32), jnp.ones((half,), F32)]),
                   reps // 2).reshape(1, LANES)
    ts = min(512, m)
    return pl.pallas_call(
        _rope_table_kernel,
        grid=(m // ts,),
        in_specs=[pl.BlockSpec((ts, LANES), lambda i: (i, 0)),
                  pl.BlockSpec((1, LANES), lambda i: (0, 0)),
                  pl.BlockSpec((1, LANES), lambda i: (0, 0))],
        out_specs=[pl.BlockSpec((ts, LANES), lambda i: (i, 0))] * 2,
        out_shape=[jax.ShapeDtypeStruct((m, LANES), F32)] * 2,
        compiler_params=_cparams(1),
        name="rope_tables",
    )(pos_b, inv_l, sgn)


def _norm_proj_kernel(x_ref, g_ref, w_ref, *rest, has_extra):
    if has_extra:
        wx_ref, o_ref, ox_ref, h_ref = rest
    else:
        o_ref, h_ref = rest

    @pl.when(pl.program_id(1) == 0)
    def _():
        x = x_ref[...].astype(F32)
        ms = jnp.mean(x * x, axis=-1, keepdims=True)
        h_ref[...] = (x * lax.rsqrt(ms + EPS) * g_ref[...]).astype(BF16)
        if has_extra:
            ox_ref[...] = jnp.dot(h_ref[...], wx_ref[...],
                                  preferred_element_type=F32).astype(ox_ref.dtype)

    o_ref[...] = jnp.dot(h_ref[...], w_ref[...], preferred_element_type=F32).astype(o_ref.dtype)


def _norm_proj(x, g, w, *, x_col_block=0, out_dtype=BF16, w_extra=None, extra_dtype=BF16):
    m = x.shape[0]
    k, n = w.shape
    tm = min(1024, m)
    tn = 1024 if n % 1024 == 0 else 512
    assert m % tm == 0 and n % tn == 0
    has_extra = w_extra is not None
    in_specs = [pl.BlockSpec((tm, k), lambda i, j: (i, x_col_block)),
                pl.BlockSpec((1, k), lambda i, j: (0, 0)),
                pl.BlockSpec((k, tn), lambda i, j: (0, j))]
    out_specs = [pl.BlockSpec((tm, tn), lambda i, j: (i, j))]
    out_shape = [jax.ShapeDtypeStruct((m, n), out_dtype)]
    args = [x, g.reshape(1, k).astype(F32), w]
    if has_extra:
        in_specs.append(pl.BlockSpec((k, LANES), lambda i, j: (0, 0)))
        out_specs.append(pl.BlockSpec((tm, LANES), lambda i, j: (i, 0)))
        out_shape.append(jax.ShapeDtypeStruct((m, LANES), extra_dtype))
        args.append(w_extra)
    outs = pl.pallas_call(
        functools.partial(_norm_proj_kernel, has_extra=has_extra),
        grid=(m // tm, n // tn),
        in_specs=in_specs,
        out_specs=out_specs,
        out_shape=out_shape,
        scratch_shapes=[pltpu.VMEM((tm, k), BF16)],
        compiler_params=_cparams(2),
        name="norm_proj",
    )(*args)
    return outs if has_extra else outs[0]


def _out_proj_kernel(y_ref, zy_ref, zm_ref, qm_ref, mkv_ref, x_ref, w_ref, *rest, wy, final):
    if final:
        gf_ref, o_ref, gated_ref = rest
    else:
        o_ref, gated_ref = rest
    zy = zy_ref[...].astype(F32)
    gated_ref[:, :wy] = (y_ref[...].astype(F32) * (zy * jax.nn.sigmoid(zy))).astype(BF16)
    scale = HEAD_DIM ** -0.5
    for h in range(MEM_HEADS):
        lo, hi = h * HEAD_DIM, (h + 1) * HEAD_DIM
        s = lax.dot_general(qm_ref[:, lo:hi], mkv_ref[:, lo:hi], NT_DIMS,
                            preferred_element_type=F32) * scale
        m = jnp.max(s, axis=-1, keepdims=True)
        p = jnp.exp(s - m)
        l = jnp.sum(p, axis=-1, keepdims=True)
        o = jnp.dot(p.astype(BF16), mkv_ref[:, MEM_WIDTH + lo:MEM_WIDTH + hi],
                    preferred_element_type=F32) / l
        zm = zm_ref[:, lo:hi].astype(F32)
        gated_ref[:, wy + lo:wy + hi] = (o * (zm * jax.nn.sigmoid(zm))).astype(BF16)
    out = x_ref[...] + jnp.dot(gated_ref[...], w_ref[...], preferred_element_type=F32)
    if final:
        ms = jnp.mean(out * out, axis=-1, keepdims=True)
        out = out * lax.rsqrt(ms + EPS) * gf_ref[...]
    o_ref[...] = out


def _out_proj(y, u, mkv, x, w_out, *, wy, zy_blk, zm_blk, qm_blk, seq, n_mem, final_g=None):
    m, d = x.shape
    tm = min(512, seq)
    final = final_g is not None
    in_specs = [
        pl.BlockSpec((tm, wy), lambda i: (i, 0)),
        pl.BlockSpec((tm, wy), lambda i: (i, zy_blk)),
        pl.BlockSpec((tm, MEM_WIDTH), lambda i: (i, zm_blk)),
        pl.BlockSpec((tm, MEM_WIDTH), lambda i: (i, qm_blk)),
        pl.BlockSpec((n_mem, 2 * MEM_WIDTH), lambda i: ((i * tm) // seq, 0)),
        pl.BlockSpec((tm, d), lambda i: (i, 0)),
        pl.BlockSpec((wy + MEM_WIDTH, d), lambda i: (0, 0), pipeline_mode=pl.Buffered(1)),
    ]
    args = [y, u, u, u, mkv, x, w_out]
    if final:
        in_specs.append(pl.BlockSpec((1, d), lambda i: (0, 0)))
        args.append(final_g.reshape(1, d).astype(F32))
    return pl.pallas_call(
        functools.partial(_out_proj_kernel, wy=wy, final=final),
        grid=(m // tm,),
        in_specs=in_specs,
        out_specs=pl.BlockSpec((tm, d), lambda i: (i, 0)),
        out_shape=jax.ShapeDtypeStruct((m, d), F32),
        scratch_shapes=[pltpu.VMEM((tm, wy + MEM_WIDTH), BF16)],
        compiler_params=_cparams(1),
        name="out_proj",
    )(*args)


def _fox_gate_kernel(f_ref, b_ref, c_ref, carry_ref):
    @pl.when(pl.program_id(1) == 0)
    def _():
        carry_ref[...] = jnp.zeros_like(carry_ref)

    f_t = f_ref[...].astype(F32).T[:N_HEADS, :] + b_ref[...]
    log_f = jnp.minimum(f_t, 0.0) - jnp.log(1.0 + jnp.exp(-jnp.abs(f_t)))
    ts = log_f.shape[1]
    r = lax.broadcasted_iota(I32, (ts, ts), 0)
    c = lax.broadcasted_iota(I32, (ts, ts), 1)
    upper = jnp.where(r <= c, 1.0, 0.0).astype(F32)
    cs = jnp.dot(log_f, upper, precision=lax.Precision.HIGHEST,
                 preferred_element_type=F32) + carry_ref[...]
    c_ref[...] = cs * LOG2E
    carry_ref[...] = cs[:, ts - 1:ts]


def _fox_gate(u3, bias, f_blk):
    b, s, _ = u3.shape
    ts = min(256, s)
    return pl.pallas_call(
        _fox_gate_kernel,
        grid=(b, s // ts),
        in_specs=[pl.BlockSpec((None, ts, LANES), lambda bi, j: (bi, j, f_blk)),
                  pl.BlockSpec((N_HEADS, 1), lambda bi, j: (0, 0))],
        out_specs=pl.BlockSpec((None, N_HEADS, ts), lambda bi, j: (bi, 0, j)),
        out_shape=jax.ShapeDtypeStruct((b, N_HEADS, s), F32),
        scratch_shapes=[pltpu.VMEM((N_HEADS, 1), F32)],
        compiler_params=_cparams(2),
        name="fox_gate",
    )(u3, bias.reshape(N_HEADS, 1).astype(F32))


STRIP = 32
HEADS_PER_STEP = 4


def _softmax_strips(t_ref, p_ref, m_ref, l_ref, alpha_ref, adjust):
    tq, tk = t_ref.shape
    for r0 in range(0, tq, min(STRIP, tq)):
        rs = pl.ds(r0, min(STRIP, tq))
        blocks = [adjust(r0, c, t_ref[rs, c * LANES:(c + 1) * LANES]) for c in range(tk // LANES)]
        live = [x for x in blocks if x is not None]
        mx = live[0]
        for x in live[1:]:
            mx = jnp.maximum(mx, x)
        m_old = m_ref[rs, :]
        m_new = jnp.maximum(m_old, jnp.max(mx, axis=-1, keepdims=True))
        alpha = jnp.exp2(m_old - m_new)
        ps = [None if x is None else jnp.exp2(x - m_new) for x in blocks]
        live = [x for x in ps if x is not None]
        sm = live[0]
        for x in live[1:]:
            sm = sm + x
        l_ref[rs, :] = alpha * l_ref[rs, :] + jnp.sum(sm, axis=-1, keepdims=True)
        m_ref[rs, :] = m_new
        alpha_ref[rs, :] = alpha
        for c, x in enumerate(ps):
            p_ref[rs, c * LANES:(c + 1) * LANES] = (
                jnp.zeros((min(STRIP, tq), LANES), BF16) if x is None else x.astype(BF16))


def _flash_scratch(n_streams, tq):
    return [pltpu.VMEM((n_streams, tq, tq), F32), pltpu.VMEM((n_streams, tq, tq), BF16),
            pltpu.VMEM((n_streams, tq, LANES), F32), pltpu.VMEM((n_streams, tq, LANES), F32),
            pltpu.VMEM((n_streams, tq, LANES), F32), pltpu.VMEM((n_streams, tq, HEAD_DIM), F32)]


def _flash_init(scratch):
    _, _, m_ref, l_ref, _, acc_ref = scratch
    m_ref[...] = jnp.full(m_ref.shape, NEG, F32)
    l_ref[...] = jnp.zeros(l_ref.shape, F32)
    acc_ref[...] = jnp.zeros(acc_ref.shape, F32)


def _flash_chunk(n_streams, j, raw_scores, values, to_logits, scratch, *, diagonal):
    t_ref, p_ref, m_ref, l_ref, alpha_ref, acc_ref = scratch
    strip = min(STRIP, t_ref.shape[1])
    row = lax.broadcasted_iota(I32, (strip, LANES), 0)
    col = lax.broadcasted_iota(I32, (strip, LANES), 1)
    for a in range(n_streams):
        t_ref[a] = raw_scores(a, j)
    for a in range(n_streams):
        def adjust(r0, c, x, a=a):
            t = to_logits(a, j, r0, c, x)
            if not diagonal or c * LANES + LANES - 1 <= r0:
                return t
            if c * LANES > r0 + strip - 1:
                return None
            return jnp.where(col + c * LANES <= row + r0, t, NEG)

        _softmax_strips(t_ref.at[a], p_ref.at[a], m_ref.at[a], l_ref.at[a], alpha_ref.at[a],
                        adjust)
    for a in range(n_streams):
        acc_ref[a] = alpha_ref[a] * acc_ref[a] + jnp.dot(p_ref[a], values(a, j),
                                                         preferred_element_type=F32)


def _flash_finish(n_streams, scratch):
    _, _, _, l_ref, _, acc_ref = scratch
    return [acc_ref[a] / l_ref[a] for a in range(n_streams)]


def _causal_flash(i, n_streams, raw_scores, values, to_logits, scratch):
    _flash_init(scratch)

    def body(j, _):
        _flash_chunk(n_streams, j, raw_scores, values, to_logits, scratch, diagonal=False)
        return 0

    lax.fori_loop(0, i, body, 0)
    _flash_chunk(n_streams, i, raw_scores, values, to_logits, scratch, diagonal=True)
    return _flash_finish(n_streams, scratch)


def _fox_attn_kernel(q_ref, k_ref, v_ref, c_ref, o_ref, qs_ref, *scratch, tq):
    i = pl.program_id(2)
    qk_scale = HEAD_DIM ** -0.5 * LOG2E
    qs_ref[...] = (q_ref[...].astype(F32) * qk_scale).astype(BF16)

    def raw_scores(a, j):
        off = pl.multiple_of(j * tq, tq)
        lo, hi = a * HEAD_DIM, (a + 1) * HEAD_DIM
        return lax.dot_general(qs_ref[:, lo:hi], k_ref[pl.ds(off, tq), lo:hi], NT_DIMS,
                               preferred_element_type=F32)

    def to_logits(a, j, r0, c, x):
        return x - c_ref[a, j, :, c * LANES:(c + 1) * LANES]

    def values(a, j):
        off = pl.multiple_of(j * tq, tq)
        return v_ref[pl.ds(off, tq), a * HEAD_DIM:(a + 1) * HEAD_DIM]

    outs = _causal_flash(i, HEADS_PER_STEP, raw_scores, values, to_logits, scratch)
    for a in range(HEADS_PER_STEP):
        o_ref[:, a * HEAD_DIM:(a + 1) * HEAD_DIM] = outs[a].astype(o_ref.dtype)


def _fox_attn(u3, c_t, *, q_blk, k_blk, v_blk):
    b, s, _ = u3.shape
    tq = min(512, s)
    nq = s // tq
    pw = HEADS_PER_STEP * HEAD_DIM
    c5 = c_t.reshape(b, N_HEADS, nq, 1, tq)
    return pl.pallas_call(
        functools.partial(_fox_attn_kernel, tq=tq),
        grid=(b, N_HEADS // HEADS_PER_STEP, nq),
        in_specs=[pl.BlockSpec((None, tq, pw), lambda bi, p, i: (bi, i, q_blk + p)),
                  pl.BlockSpec((None, s, pw), lambda bi, p, i: (bi, 0, k_blk + p)),
                  pl.BlockSpec((None, s, pw), lambda bi, p, i: (bi, 0, v_blk + p)),
                  pl.BlockSpec((None, HEADS_PER_STEP, nq, 1, tq), lambda bi, p, i: (bi, p, 0, 0, 0))],
        out_specs=pl.BlockSpec((None, tq, pw), lambda bi, p, i: (bi, i, p)),
        out_shape=jax.ShapeDtypeStruct((b, s, N_HEADS * HEAD_DIM), BF16),
        scratch_shapes=[pltpu.VMEM((tq, pw), BF16)] + _flash_scratch(HEADS_PER_STEP, tq),
        compiler_params=_cparams(3),
        name="fox_attn",
    )(u3, u3, u3, c5)


def _dsa_kprep_kernel(k_ref, mk_ref, c128_ref, s128_ref, c64_ref, s64_ref,
                      kr_ref, klo_ref, khi_ref):
    kr_ref[...] = _rope128(k_ref[...].astype(F32), c128_ref[...], s128_ref[...]).astype(BF16)
    ki = _rope64(mk_ref[...].astype(F32), c64_ref[...], s64_ref[...])
    lane = lax.broadcasted_iota(I32, ki.shape, 1)
    lo = jnp.where(lane < IDX_DIM, ki, 0.0)
    klo_ref[...] = lo.astype(BF16)
    khi_ref[...] = pltpu.roll(lo, IDX_DIM, 1).astype(BF16)


def _dsa_kprep(u3, tabs, *, k_blk, misc_blk):
    b, s, _ = u3.shape
    ts = min(512, s)
    c128, s128, c64, s64 = tabs

    def spec(blk):
        return pl.BlockSpec((None, ts, LANES), lambda bi, i: (bi, i, blk))

    return pl.pallas_call(
        _dsa_kprep_kernel,
        grid=(b, s // ts),
        in_specs=[spec(k_blk), spec(misc_blk), spec(0), spec(0), spec(0), spec(0)],
        out_specs=[spec(0)] * 3,
        out_shape=[jax.ShapeDtypeStruct((b, s, LANES), BF16)] * 3,
        compiler_params=_cparams(2),
        name="dsa_kprep",
    )(u3, u3, c128, s128, c64, s64)


def _dsa_kernel(q_ref, qi_ref, mq_ref, v_ref, kr_ref, klo_ref, khi_ref,
                cq128_ref, sq128_ref, cq64_ref, sq64_ref, y_in_ref,
                o_ref,
                qr_ref, qir_ref, w_ref, sc_ref, t4_ref, yo_ref, *flash,
                tq, kl, q_off, n_sel):
    del y_in_ref
    t0 = q_off + pl.program_id(1) * tq
    qk_scale = HEAD_DIM ** -0.5 * LOG2E
    idx_scale = (IDX_DIM ** -0.5) * (IDX_HEADS ** -0.5)
    ck = min(512, kl)
    n_chunks = kl // ck
    strip = min(STRIP, tq)

    cq128, sq128 = cq128_ref[...], sq128_ref[...]
    for h in range(N_HEADS):
        qh = q_ref[:, h * HEAD_DIM:(h + 1) * HEAD_DIM].astype(F32)
        qr_ref[h] = (_rope128(qh, cq128, sq128) * qk_scale).astype(BF16)
    cq64, sq64 = cq64_ref[...], sq64_ref[...]
    for a in range(IDX_HEADS // 2):
        qir_ref[a] = _rope64(qi_ref[:, a * LANES:(a + 1) * LANES].astype(F32),
                             cq64, sq64).astype(BF16)
    w_ref[...] = mq_ref[...].astype(F32) * idx_scale

    row = lax.broadcasted_iota(I32, (strip, LANES), 0)
    col = lax.broadcasted_iota(I32, (strip, LANES), 1)
    group = 4

    def idx_chunk(j, _):
        off = pl.multiple_of(j * ck, ck)
        k_lo, k_hi = klo_ref[pl.ds(off, ck), :], khi_ref[pl.ds(off, ck), :]
        for g in range(IDX_HEADS // group):
            for a2 in range(group // 2):
                x = qir_ref[g * (group // 2) + a2]
                t4_ref[2 * a2] = lax.dot_general(x, k_lo, NT_DIMS, preferred_element_type=F32)
                t4_ref[2 * a2 + 1] = lax.dot_general(x, k_hi, NT_DIMS,
                                                     preferred_element_type=F32)
            last = g == IDX_HEADS // group - 1
            for r0 in range(0, tq, strip):
                rs = pl.ds(r0, strip)
                w_rows = w_ref[rs, :]
                ws = [jnp.broadcast_to(w_rows[:, IDX_DIM + g * group + u:IDX_DIM + g * group + u + 1],
                                       (strip, LANES)) for u in range(group)]
                for c in range(ck // LANES):
                    cs = pl.ds(c * LANES, LANES)
                    acc = ws[0] * jnp.maximum(t4_ref[0, rs, cs], 0.0)
                    for u in range(1, group):
                        acc = acc + ws[u] * jnp.maximum(t4_ref[u, rs, cs], 0.0)
                    if g > 0:
                        acc = acc + sc_ref[j, rs, cs]
                    if last:
                        causal = col + (off + c * LANES) <= row + (t0 + r0)
                        acc = jnp.where(causal, acc, -jnp.inf)
                    sc_ref[j, rs, cs] = acc
        return 0

    lax.fori_loop(0, n_chunks, idx_chunk, 0)

    shape3 = (n_chunks, tq, ck)
    col3 = lax.broadcasted_iota(I32, shape3, 0) * ck + lax.broadcasted_iota(I32, shape3, 2)
    causal3 = col3 <= lax.broadcasted_iota(I32, shape3, 1) + t0

    def count(x):
        return jnp.sum(jnp.sum(x, axis=0), axis=-1, keepdims=True)

    def key_to_f32(key):
        return pltpu.bitcast(jnp.where(key < 0, key ^ jnp.int32(0x7FFFFFFF), key), F32)

    def thr_body(it, key):
        cand = key + (jnp.int32(1) << (31 - it))
        c = count(jnp.where(sc_ref[...] >= key_to_f32(cand), 1.0, 0.0))
        return jnp.where(c >= n_sel, cand, key)

    thr_key = lax.fori_loop(0, 32, thr_body, jnp.full((tq, 1), INT_MIN, I32))
    take_all = thr_key == INT_MIN
    thr = key_to_f32(thr_key)
    score = sc_ref[...]
    need = n_sel - count(jnp.where(score > thr, 1.0, 0.0))
    surplus = jnp.where(take_all, 0.0, count(jnp.where(score == thr, 1.0, 0.0)) - need)
    has_surplus = jnp.max(surplus) > 0.0

    n_bits = kl.bit_length()

    def tie_body(it, jm):
        cand = jm + (jnp.int32(1) << (n_bits - 1 - it))
        f = count(jnp.where(sc_ref[...] == thr, jnp.where(col3 < cand, 1.0, 0.0), 0.0))
        return jnp.where(jnp.logical_and(f < need, cand <= kl), cand, jm)

    jm = lax.fori_loop(0, jnp.where(has_surplus, n_bits, 0), tie_body,
                       jnp.broadcast_to(jnp.where(has_surplus, 0, kl), (tq, 1)).astype(I32))
    keep = jnp.where(jnp.logical_or(score > thr, take_all), 1.0,
                     jnp.where(score == thr, jnp.where(col3 <= jm, 1.0, 0.0), 0.0))
    sc_ref[...] = jnp.where(causal3, jnp.where(keep > 0.0, 0.0, NEG), NEG)

    def raw_scores(h0):
        def f(a, j):
            return lax.dot_general(qr_ref[h0 + a], kr_ref[j * ck:(j + 1) * ck, :], NT_DIMS,
                                   preferred_element_type=F32)
        return f

    def values(a, j):
        return v_ref[j * ck:(j + 1) * ck, :]

    def to_logits(a, j, r0, c, x):
        return x + sc_ref[j, r0:r0 + strip, c * LANES:(c + 1) * LANES]

    def group_body(pp, _):
        h0 = HEADS_PER_STEP * pp
        _flash_init(flash)
        for j in range(n_chunks):
            _flash_chunk(HEADS_PER_STEP, j, raw_scores(h0), values, to_logits, flash,
                         diagonal=False)
        outs = _flash_finish(HEADS_PER_STEP, flash)
        for a in range(HEADS_PER_STEP):
            yo_ref[h0 + a] = outs[a].astype(yo_ref.dtype)
        return 0

    lax.fori_loop(0, N_HEADS // HEADS_PER_STEP, group_body, 0)
    for h in range(N_HEADS):
        o_ref[:, h * HEAD_DIM:(h + 1) * HEAD_DIM] = yo_ref[h]


def _dsa_group(u3, kprep, tabs, y, *, q_off, rows, kl, n_sel, q_blk, v_blk, qi_blk, misc_blk):
    b, s, _ = u3.shape
    tq = rows
    assert kl % min(512, kl) == 0 and tq == min(512, kl)
    qb0 = q_off // tq
    c128, s128, c64, s64 = tabs
    kr, klo, khi = kprep
    qw = N_HEADS * HEAD_DIM
    iw = IDX_HEADS * IDX_DIM
    ck = min(512, kl)

    def qspec(width, blk):
        return pl.BlockSpec((None, tq, width), lambda bi, i: (bi, qb0 + i, blk))

    def kspec(blk):
        return pl.BlockSpec((None, kl, LANES), lambda bi, i: (bi, 0, blk))

    return pl.pallas_call(
        functools.partial(_dsa_kernel, tq=tq, kl=kl, q_off=q_off, n_sel=n_sel),
        grid=(b, rows // tq),
        in_specs=[qspec(qw, q_blk), qspec(iw, qi_blk), qspec(LANES, misc_blk),
                  kspec(v_blk), kspec(0), kspec(0), kspec(0),
                  qspec(LANES, 0), qspec(LANES, 0), qspec(LANES, 0), qspec(LANES, 0),
                  pl.BlockSpec(memory_space=pl.ANY)],
        out_specs=pl.BlockSpec((None, tq, qw), lambda bi, i: (bi, qb0 + i, 0)),
        out_shape=jax.ShapeDtypeStruct((b, s, qw), BF16),
        input_output_aliases={11: 0},
        scratch_shapes=[pltpu.VMEM((N_HEADS, tq, HEAD_DIM), BF16),
                        pltpu.VMEM((IDX_HEADS // 2, tq, LANES), BF16),
                        pltpu.VMEM((tq, LANES), F32),
                        pltpu.VMEM((kl // ck, tq, ck), F32),
                        pltpu.VMEM((4, tq, ck), F32),
                        pltpu.VMEM((N_HEADS, tq, HEAD_DIM), BF16)]
        + _flash_scratch(HEADS_PER_STEP, tq),
        compiler_params=_cparams(2),
        name="dsa_attn",
    )(u3, u3, u3, u3, kr, klo, khi, c128, s128, c64, s64, y)


def _dsa_attn(u3, tabs, *, k_blk, misc_blk, **blks):
    b, s, _ = u3.shape
    n_sel = min(TOPK_MAX, s // 4)
    rows = min(512, s)
    kprep = _dsa_kprep(u3, tabs, k_blk=k_blk, misc_blk=misc_blk)
    y = jnp.zeros((b, s, N_HEADS * HEAD_DIM), BF16)
    for q_off in range(0, s, rows):
        y = _dsa_group(u3, kprep, tabs, y, q_off=q_off, rows=rows, kl=q_off + rows, n_sel=n_sel,
                       misc_blk=misc_blk, **blks)
    return y


def _dil_kernel(q0_ref, k0_ref, v0_ref, q1_ref, k1_ref, v1_ref, q2_ref, k2_ref, v2_ref,
                cos_ref, sin_ref, y_ref, qr_ref, kr_ref, o_ref, lse_ref, t_ref, p_ref, *, seq):
    qk_scale = HEAD_DIM ** -0.5 * LOG2E
    cos_f, sin_s = cos_ref[...], sin_ref[...]
    groups = ((q0_ref, k0_ref, v0_ref), (q1_ref, k1_ref, v1_ref), (q2_ref, k2_ref, v2_ref))
    for g, (q_ref, k_ref, _) in enumerate(groups):
        qr_ref[g] = _rope128(q_ref[...].astype(F32), cos_f, sin_s) * qk_scale
        kr_ref[g] = _rope128(k_ref[...].astype(F32), cos_f, sin_s)

    for g, (window, dil) in enumerate(DILATED_PAIRS):
        v_ref = groups[g][2]
        sub = seq // dil
        qb = min(window // dil, sub)
        tiles = [(r, i) for r in range(dil) for i in range(sub // qb)]

        def rows(r, blk, dil=dil, qb=qb):
            start = r + dil * qb * blk
            return pl.ds(start, qb) if dil == 1 else pl.ds(start, qb, stride=dil)

        for n, (r, i) in enumerate(tiles):
            q = qr_ref[g, rows(r, i), :].astype(BF16)
            k_cur = kr_ref[g, rows(r, i), :].astype(BF16)
            t_ref[n, :qb, qb:2 * qb] = lax.dot_general(q, k_cur, NT_DIMS,
                                                       preferred_element_type=F32)
            if i > 0:
                k_prev = kr_ref[g, rows(r, i - 1), :].astype(BF16)
                t_ref[n, :qb, :qb] = lax.dot_general(q, k_prev, NT_DIMS,
                                                     preferred_element_type=F32)
        row = lax.broadcasted_iota(I32, (qb, qb), 0)
        col = lax.broadcasted_iota(I32, (qb, qb), 1)
        for n, (r, i) in enumerate(tiles):
            t_cur = jnp.where(col <= row, t_ref[n, :qb, qb:2 * qb], NEG)
            m = jnp.max(t_cur, axis=-1, keepdims=True)
            if i > 0:
                t_prev = jnp.where(col >= row, t_ref[n, :qb, :qb], NEG)
                m = jnp.maximum(m, jnp.max(t_prev, axis=-1, keepdims=True))
            p_cur = jnp.exp2(t_cur - m)
            l = jnp.sum(p_cur, axis=-1, keepdims=True)
            p_ref[n, :qb, qb:2 * qb] = p_cur.astype(BF16)
            if i > 0:
                p_prev = jnp.exp2(t_prev - m)
                l = l + jnp.sum(p_prev, axis=-1, keepdims=True)
                p_ref[n, :qb, :qb] = p_prev.astype(BF16)
            t_ref[n, :qb, :LANES] = jnp.broadcast_to(l, (qb, LANES))
            t_ref[n, :qb, LANES:2 * LANES] = jnp.broadcast_to(m, (qb, LANES))
        for n, (r, i) in enumerate(tiles):
            acc = jnp.dot(p_ref[n, :qb, qb:2 * qb], v_ref[rows(r, i), :].astype(BF16),
                          preferred_element_type=F32)
            if i > 0:
                acc = acc + jnp.dot(p_ref[n, :qb, :qb], v_ref[rows(r, i - 1), :].astype(BF16),
                                    preferred_element_type=F32)
            l = t_ref[n, :qb, :LANES]
            o_ref[g, rows(r, i), :] = acc / l
            lse_ref[g, rows(r, i), :] = t_ref[n, :qb, LANES:2 * LANES] + jnp.log2(l)

    l0, l1, l2 = lse_ref[0], lse_ref[1], lse_ref[2]
    m = jnp.maximum(jnp.maximum(l0, l1), l2)
    e0, e1, e2 = jnp.exp2(l0 - m), jnp.exp2(l1 - m), jnp.exp2(l2 - m)
    y = (e0 * o_ref[0] + e1 * o_ref[1] + e2 * o_ref[2]) / (e0 + e1 + e2)
    y_ref[...] = y.astype(y_ref.dtype)


def _dil_attn(ua3, ub3, tabs128, *, q0_blk, k0_blk, v0_blk):
    b, s, _ = ua3.shape
    for window, dil in DILATED_PAIRS:
        assert s % dil == 0 and (s // dil) % min(window // dil, s // dil) == 0
    cos_t, sin_t = tabs128
    tile = min(DILATED_PAIRS[0][0], s)
    n_tiles = s // min(min(w // d, s // d) for w, d in DILATED_PAIRS)

    def spec(blk0):
        return pl.BlockSpec((None, s, HEAD_DIM), lambda bi, a: (bi, 0, blk0 + a))

    tab = pl.BlockSpec((None, s, LANES), lambda bi, a: (bi, 0, 0))
    return pl.pallas_call(
        functools.partial(_dil_kernel, seq=s),
        grid=(b, DIL_HEADS),
        in_specs=[spec(q0_blk), spec(k0_blk), spec(v0_blk)]
        + [spec(j * DIL_HEADS) for j in range(6)] + [tab, tab],
        out_specs=pl.BlockSpec((None, s, HEAD_DIM), lambda bi, a: (bi, 0, a)),
        out_shape=jax.ShapeDtypeStruct((b, s, DIL_WIDTH), BF16),
        scratch_shapes=[pltpu.VMEM((3, s, HEAD_DIM), F32)] * 4
        + [pltpu.VMEM((n_tiles, tile, 2 * LANES), F32), pltpu.VMEM((n_tiles, tile, 2 * LANES), BF16)],
        compiler_params=_cparams(2),
        name="dilated_attn",
    )(ua3, ua3, ua3, ub3, ub3, ub3, ub3, ub3, ub3, cos_t, sin_t)


def _mla_kernel(qn_ref, qr_ref, kn_ref, kr_ref, v_ref, cq_ref, sq_ref, ck_ref, sk_ref,
                o_ref, qcat_ref, kcat_ref, *scratch, tq):
    i = pl.program_id(2)
    qk_scale = (QK_NOPE + QK_ROPE) ** -0.5 * LOG2E
    cat = QK_NOPE + QK_ROPE

    @pl.when(i == 0)
    def _():
        k_rope = _rope64(kr_ref[...].astype(F32), ck_ref[...],
                         sk_ref[...])[:, :QK_ROPE].astype(BF16)
        for a in range(2):
            kcat_ref[a, :, :QK_NOPE] = kn_ref[:, a * HEAD_DIM:(a + 1) * HEAD_DIM]
            kcat_ref[a, :, QK_NOPE:cat] = k_rope
            kcat_ref[a, :, cat:] = jnp.zeros((kcat_ref.shape[1], 2 * LANES - cat), BF16)

    q_rope = (_rope64(qr_ref[...].astype(F32), cq_ref[...], sq_ref[...]) * qk_scale).astype(BF16)
    for a in range(2):
        q_nope = qn_ref[:, a * HEAD_DIM:(a + 1) * HEAD_DIM].astype(F32) * qk_scale
        qcat_ref[a, :, :QK_NOPE] = q_nope.astype(BF16)
        qcat_ref[a, :, QK_NOPE:cat] = q_rope[:, a * QK_ROPE:(a + 1) * QK_ROPE]
        qcat_ref[a, :, cat:] = jnp.zeros((tq, 2 * LANES - cat), BF16)

    def raw_scores(a, j):
        off = pl.multiple_of(j * tq, tq)
        return lax.dot_general(qcat_ref[a], kcat_ref[a, pl.ds(off, tq), :], NT_DIMS,
                               preferred_element_type=F32)

    def to_logits(a, j, r0, c, x):
        return x

    def values(a, j):
        off = pl.multiple_of(j * tq, tq)
        return v_ref[pl.ds(off, tq), a * HEAD_DIM:(a + 1) * HEAD_DIM]

    outs = _causal_flash(i, 2, raw_scores, values, to_logits, scratch)
    for a in range(2):
        o_ref[:, a * HEAD_DIM:(a + 1) * HEAD_DIM] = outs[a].astype(o_ref.dtype)


def _mla_attn(qf3, kvf3, u3, tabs64, *, kr_blk):
    b, s, _ = qf3.shape
    tq = min(512, s)
    pw = 2 * HEAD_DIM
    n_pairs = N_HEADS // 2
    cos_t, sin_t = tabs64
    return pl.pallas_call(
        functools.partial(_mla_kernel, tq=tq),
        grid=(b, n_pairs, s // tq),
        in_specs=[pl.BlockSpec((None, tq, pw), lambda bi, p, i: (bi, i, p)),
                  pl.BlockSpec((None, tq, LANES), lambda bi, p, i: (bi, i, N_HEADS + p)),
                  pl.BlockSpec((None, s, pw), lambda bi, p, i: (bi, 0, p)),
                  pl.BlockSpec((None, s, LANES), lambda bi, p, i: (bi, 0, kr_blk)),
                  pl.BlockSpec((None, s, pw), lambda bi, p, i: (bi, 0, n_pairs + p)),
                  pl.BlockSpec((None, tq, LANES), lambda bi, p, i: (bi, i, 0)),
                  pl.BlockSpec((None, tq, LANES), lambda bi, p, i: (bi, i, 0)),
                  pl.BlockSpec((None, s, LANES), lambda bi, p, i: (bi, 0, 0)),
                  pl.BlockSpec((None, s, LANES), lambda bi, p, i: (bi, 0, 0))],
        out_specs=pl.BlockSpec((None, tq, pw), lambda bi, p, i: (bi, i, p)),
        out_shape=jax.ShapeDtypeStruct((b, s, N_HEADS * HEAD_DIM), BF16),
        scratch_shapes=[pltpu.VMEM((2, tq, 2 * LANES), BF16),
                        pltpu.VMEM((2, s, 2 * LANES), BF16)] + _flash_scratch(2, tq),
        compiler_params=_cparams(3),
        name="mla_attn",
    )(qf3, qf3, kvf3, u3, kvf3, cos_t, sin_t, cos_t, sin_t)


def _pack(parts, multiple=512):
    k = next(p.shape[0] for p, _ in parts if p is not None)
    cols = [jnp.zeros((k, w), BF16) if p is None else p.astype(BF16) for p, w in parts]
    n = sum(w for _, w in parts)
    pad = (-n) % multiple
    if pad:
        cols.append(jnp.zeros((k, pad), BF16))
    return jnp.concatenate(cols, axis=1)


def _cols(w, sizes):
    out, acc = [], 0
    for sz in sizes:
        out.append(w[:, acc:acc + sz])
        acc += sz
    return out


def kernel(x, mem, positions, l0_norm, l0_w_in, l0_forget_bias, l0_mem_norm, l0_w_mem_kv, l0_w_out, l1_norm, l1_w_in, l1_mem_norm, l1_w_mem_kv, l1_w_out, l2_norm, l2_w_in, l2_mem_norm, l2_w_mem_kv, l2_w_out, l3_norm, l3_w_in, l3_q_norm, l3_w_uq, l3_kv_norm, l3_w_ukv, l3_mem_norm, l3_w_mem_kv, l3_w_out, final_norm):
    b, s, d = x.shape
    n_mem = mem.shape[1]
    m = b * s
    mw = N_HEADS * HEAD_DIM
    x2 = x.reshape(m, d)
    mem2 = mem.reshape(b * n_mem, d)

    pos_b = jnp.broadcast_to(positions.astype(F32).reshape(m, 1), (m, LANES))
    tabs128 = tuple(t.reshape(b, s, LANES) for t in _rope_tables(pos_b, HEAD_DIM))
    tabs64 = tuple(t.reshape(b, s, LANES) for t in _rope_tables(pos_b, IDX_DIM))

    def mem_kv(g, w):
        return _norm_proj(mem2, g, w.astype(BF16))

    def finish(y, u, x_in, g_mem, w_mem_kv, w_out, *, wy, zy_blk, zm_blk, qm_blk, final_g=None):
        return _out_proj(y, u, mem_kv(g_mem, w_mem_kv), x_in, w_out.astype(BF16), wy=wy,
                         zy_blk=zy_blk, zm_blk=zm_blk, qm_blk=qm_blk, seq=s, n_mem=n_mem,
                         final_g=final_g)

    q_w, k_w, v_w, f_w, qm_w, z_w = _cols(l0_w_in, (mw, mw, mw, N_HEADS, MEM_WIDTH, mw + MEM_WIDTH))
    w0 = _pack([(z_w[:, :mw], mw), (z_w[:, mw:], MEM_WIDTH), (qm_w, MEM_WIDTH), (q_w, mw),
                (k_w, mw), (v_w, mw)])
    u, f_arr = _norm_proj(x2, l0_norm, w0, w_extra=_pack([(f_w, N_HEADS)], multiple=LANES),
                          extra_dtype=F32)
    u3 = u.reshape(b, s, -1)
    base = (mw + 2 * MEM_WIDTH) // LANES
    c_t = _fox_gate(f_arr.reshape(b, s, LANES), l0_forget_bias, 0)
    y = _fox_attn(u3, c_t, q_blk=base // HEADS_PER_STEP, k_blk=(base + N_HEADS) // HEADS_PER_STEP,
                  v_blk=(base + 2 * N_HEADS) // HEADS_PER_STEP)
    x2 = finish(y.reshape(m, mw), u, x2, l0_mem_norm, l0_w_mem_kv, l0_w_out, wy=mw, zy_blk=0,
                zm_blk=mw // MEM_WIDTH, qm_blk=mw // MEM_WIDTH + 1)

    iw = IDX_HEADS * IDX_DIM
    q_w, k_w, v_w, qi_w, ki_w, wi_w, qm_w, z_w = _cols(
        l1_w_in, (mw, HEAD_DIM, HEAD_DIM, iw, IDX_DIM, IDX_HEADS, MEM_WIDTH, mw + MEM_WIDTH))
    w1 = _pack([(z_w[:, :mw], mw), (q_w, mw), (qi_w, iw), (z_w[:, mw:], MEM_WIDTH),
                (qm_w, MEM_WIDTH), (k_w, HEAD_DIM), (v_w, HEAD_DIM), (ki_w, IDX_DIM),
                (wi_w, IDX_HEADS), (None, LANES - IDX_DIM - IDX_HEADS)])
    u = _norm_proj(x2, l1_norm, w1)
    u3 = u.reshape(b, s, -1)
    kv0 = (2 * mw + iw + 2 * MEM_WIDTH) // LANES
    y = _dsa_attn(u3, tabs128 + tabs64, q_blk=1, qi_blk=2 * mw // iw, k_blk=kv0, v_blk=kv0 + 1,
                  misc_blk=kv0 + 2)
    zoff = (2 * mw + iw) // MEM_WIDTH
    x2 = finish(y.reshape(m, mw), u, x2, l1_mem_norm, l1_w_mem_kv, l1_w_out, wy=mw, zy_blk=0,
                zm_blk=zoff, qm_blk=zoff + 1)

    gw = len(DILATED_PAIRS) * DIL_WIDTH
    q_w, k_w, v_w, qm_w, z_w = _cols(l2_w_in, (gw, gw, gw, MEM_WIDTH, DIL_WIDTH + MEM_WIDTH))

    def grp(w, g):
        return (w[:, g * DIL_WIDTH:(g + 1) * DIL_WIDTH], DIL_WIDTH)

    wa = _pack([grp(q_w, 0), (z_w[:, :DIL_WIDTH], DIL_WIDTH), grp(k_w, 0), grp(v_w, 0),
                (z_w[:, DIL_WIDTH:], MEM_WIDTH), (qm_w, MEM_WIDTH)])
    wb = _pack([grp(q_w, 1), grp(k_w, 1), grp(v_w, 1), grp(q_w, 2), grp(k_w, 2), grp(v_w, 2)])
    u = _norm_proj(x2, l2_norm, wa)
    ub = _norm_proj(x2, l2_norm, wb, out_dtype=F32)
    y = _dil_attn(u.reshape(b, s, -1), ub.reshape(b, s, -1), tabs128, q0_blk=0,
                  k0_blk=2 * DIL_HEADS, v0_blk=3 * DIL_HEADS)
    zoff = 4 * DIL_WIDTH // MEM_WIDTH
    x2 = finish(y.reshape(m, DIL_WIDTH), u, x2, l2_mem_norm, l2_w_mem_kv, l2_w_out, wy=DIL_WIDTH,
                zy_blk=1, zm_blk=zoff, qm_blk=zoff + 1)

    q_lora, kv_lora = l3_w_uq.shape[0], l3_w_ukv.shape[0]
    cq_w, ckv_w, kr_w, qm_w, z_w = _cols(l3_w_in, (q_lora, kv_lora, QK_ROPE, MEM_WIDTH,
                                                  mw + MEM_WIDTH))
    assert q_lora == MEM_WIDTH and kv_lora == MEM_WIDTH
    w3 = _pack([(z_w[:, :mw], mw), (z_w[:, mw:], MEM_WIDTH), (qm_w, MEM_WIDTH), (cq_w, q_lora),
                (ckv_w, kv_lora)])
    u, kr_arr = _norm_proj(x2, l3_norm, w3, w_extra=_pack([(kr_w, QK_ROPE)], multiple=LANES))
    cq_blk = (mw + 2 * MEM_WIDTH) // MEM_WIDTH
    uq = l3_w_uq.reshape(q_lora, N_HEADS, QK_NOPE + QK_ROPE)
    w_uq = _pack([(uq[:, :, :QK_NOPE].reshape(q_lora, -1), N_HEADS * QK_NOPE),
                  (uq[:, :, QK_NOPE:].reshape(q_lora, -1), N_HEADS * QK_ROPE)])
    ukv = l3_w_ukv.reshape(kv_lora, N_HEADS, QK_NOPE + HEAD_DIM)
    w_ukv = _pack([(ukv[:, :, :QK_NOPE].reshape(kv_lora, -1), N_HEADS * QK_NOPE),
                   (ukv[:, :, QK_NOPE:].reshape(kv_lora, -1), N_HEADS * HEAD_DIM)])
    qf = _norm_proj(u, l3_q_norm, w_uq, x_col_block=cq_blk)
    kvf = _norm_proj(u, l3_kv_norm, w_ukv, x_col_block=cq_blk + 1)
    y = _mla_attn(qf.reshape(b, s, -1), kvf.reshape(b, s, -1), kr_arr.reshape(b, s, LANES), tabs64,
                  kr_blk=0)
    x2 = finish(y.reshape(m, mw), u, x2, l3_mem_norm, l3_w_mem_kv, l3_w_out, wy=mw, zy_blk=0,
                zm_blk=mw // MEM_WIDTH, qm_blk=mw // MEM_WIDTH + 1, final_g=final_norm)
    return x2.reshape(b, s, d)
```

```python
import functools

import jax
import jax.numpy as jnp
from jax import lax
from jax.experimental import pallas as pl
from jax.experimental.pallas import tpu as pltpu

F32 = jnp.float32
BF16 = jnp.bfloat16
I32 = jnp.int32

EPS = 1e-6
ROPE_THETA = 10000.0
HEAD_DIM = 128
N_HEADS = 16
MEM_HEADS = 4
MEM_WIDTH = MEM_HEADS * HEAD_DIM
IDX_HEADS = 16
IDX_DIM = 64
TOPK_MAX = 256
DILATED_PAIRS = ((128, 1), (512, 4), (2048, 16))
DIL_HEADS = 6
DIL_WIDTH = DIL_HEADS * HEAD_DIM
QK_NOPE = 128
QK_ROPE = 64
LANES = 128
NEG = -1e30
INT_MIN = -(2 ** 31)
VMEM_LIMIT = 56 * 1024 * 1024

NT_DIMS = (((1,), (1,)), ((), ()))
LOG2E = 1.4426950408889634


def _cparams(n_axes):
    return pltpu.CompilerParams(
        dimension_semantics=("arbitrary",) * n_axes, vmem_limit_bytes=VMEM_LIMIT)


def _rope128(x, cos_f, sin_s):
    return x * cos_f + pltpu.roll(x, 64, 1) * sin_s


def _rope64(x, cos_f, sin_s):
    lane = lax.broadcasted_iota(I32, x.shape, 1)
    partner = jnp.where((lane & 32) == 0, pltpu.roll(x, 96, 1), pltpu.roll(x, 32, 1))
    return x * cos_f + partner * sin_s


def _rope_table_kernel(pos_ref, inv_ref, sgn_ref, cos_ref, sin_ref):
    ang = pos_ref[...] * inv_ref[...]
    cos_ref[...] = jnp.cos(ang)
    sin_ref[...] = jnp.sin(ang) * sgn_ref[...]


def _rope_tables(pos_b, dh):
    m = pos_b.shape[0]
    half = dh // 2
    inv = jnp.power(ROPE_THETA, -jnp.arange(half, dtype=F32) * 2.0 / dh)
    reps = LANES // half
    inv_l = jnp.tile(inv, reps).reshape(1, LANES)
    sgn = jnp.tile(jnp.concatenate([-jnp.ones((half,), F32), jnp.ones((half,), F32)]),
                   reps // 2).reshape(1, LANES)
    ts = min(512, m)
    return pl.pallas_call(
        _rope_table_kernel,
        grid=(m // ts,),
        in_specs=[pl.BlockSpec((ts, LANES), lambda i: (i, 0)),
                  pl.BlockSpec((1, LANES), lambda i: (0, 0)),
                  pl.BlockSpec((1, LANES), lambda i: (0, 0))],
        out_specs=[pl.BlockSpec((ts, LANES), lambda i: (i, 0))] * 2,
        out_shape=[jax.ShapeDtypeStruct((m, LANES), F32)] * 2,
        compiler_params=_cparams(1),
        name="rope_tables",
    )(pos_b, inv_l, sgn)


def _norm_proj_kernel(x_ref, g_ref, w_ref, *rest, has_extra):
    if has_extra:
        wx_ref, o_ref, ox_ref, h_ref = rest
    else:
        o_ref, h_ref = rest

    @pl.when(pl.program_id(1) == 0)
    def _():
        x = x_ref[...].astype(F32)
        ms = jnp.mean(x * x, axis=-1, keepdims=True)
        h_ref[...] = (x * lax.rsqrt(ms + EPS) * g_ref[...]).astype(BF16)
        if has_extra:
            ox_ref[...] = jnp.dot(h_ref[...], wx_ref[...],
                                  preferred_element_type=F32).astype(ox_ref.dtype)

    o_ref[...] = jnp.dot(h_ref[...], w_ref[...], preferred_element_type=F32).astype(o_ref.dtype)


def _norm_proj(x, g, w, *, x_col_block=0, out_dtype=BF16, w_extra=None, extra_dtype=BF16):
    m = x.shape[0]
    k, n = w.shape
    tm = min(1024, m)
    tn = 1024 if n % 1024 == 0 else 512
    assert m % tm == 0 and n % tn == 0
    has_extra = w_extra is not None
    in_specs = [pl.BlockSpec((tm, k), lambda i, j: (i, x_col_block)),
                pl.BlockSpec((1, k), lambda i, j: (0, 0)),
                pl.BlockSpec((k, tn), lambda i, j: (0, j))]
    out_specs = [pl.BlockSpec((tm, tn), lambda i, j: (i, j))]
    out_shape = [jax.ShapeDtypeStruct((m, n), out_dtype)]
    args = [x, g.reshape(1, k).astype(F32), w]
    if has_extra:
        in_specs.append(pl.BlockSpec((k, LANES), lambda i, j: (0, 0)))
        out_specs.append(pl.BlockSpec((tm, LANES), lambda i, j: (i, 0)))
        out_shape.append(jax.ShapeDtypeStruct((m, LANES), extra_dtype))
        args.append(w_extra)
    outs = pl.pallas_call(
        functools.partial(_norm_proj_kernel, has_extra=has_extra),
        grid=(m // tm, n // tn),
        in_specs=in_specs,
        out_specs=out_specs,
        out_shape=out_shape,
        scratch_shapes=[pltpu.VMEM((tm, k), BF16)],
        compiler_params=_cparams(2),
        name="norm_proj",
    )(*args)
    return outs if has_extra else outs[0]


def _out_proj_kernel(y_ref, zy_ref, zm_ref, qm_ref, mkv_ref, x_ref, w_ref, *rest, wy, final):
    if final:
        gf_ref, o_ref, gated_ref = rest
    else:
        o_ref, gated_ref = rest
    zy = zy_ref[...].astype(F32)
    gated_ref[:, :wy] = (y_ref[...].astype(F32) * (zy * jax.nn.sigmoid(zy))).astype(BF16)
    scale = HEAD_DIM ** -0.5
    for h in range(MEM_HEADS):
        lo, hi = h * HEAD_DIM, (h + 1) * HEAD_DIM
        s = lax.dot_general(qm_ref[:, lo:hi], mkv_ref[:, lo:hi], NT_DIMS,
                            preferred_element_type=F32) * scale
        m = jnp.max(s, axis=-1, keepdims=True)
        p = jnp.exp(s - m)
        l = jnp.sum(p, axis=-1, keepdims=True)
        o = jnp.dot(p.astype(BF16), mkv_ref[:, MEM_WIDTH + lo:MEM_WIDTH + hi],
                    preferred_element_type=F32) / l
        zm = zm_ref[:, lo:hi].astype(F32)
        gated_ref[:, wy + lo:wy + hi] = (o * (zm * jax.nn.sigmoid(zm))).astype(BF16)
    out = x_ref[...] + jnp.dot(gated_ref[...], w_ref[...], preferred_element_type=F32)
    if final:
        ms = jnp.mean(out * out, axis=-1, keepdims=True)
        out = out * lax.rsqrt(ms + EPS) * gf_ref[...]
    o_ref[...] = out


def _out_proj(y, u, mkv, x, w_out, *, wy, zy_blk, zm_blk, qm_blk, seq, n_mem, final_g=None):
    m, d = x.shape
    tm = min(512, seq)
    final = final_g is not None
    in_specs = [
        pl.BlockSpec((tm, wy), lambda i: (i, 0)),
        pl.BlockSpec((tm, wy), lambda i: (i, zy_blk)),
        pl.BlockSpec((tm, MEM_WIDTH), lambda i: (i, zm_blk)),
        pl.BlockSpec((tm, MEM_WIDTH), lambda i: (i, qm_blk)),
        pl.BlockSpec((n_mem, 2 * MEM_WIDTH), lambda i: ((i * tm) // seq, 0)),
        pl.BlockSpec((tm, d), lambda i: (i, 0)),
        pl.BlockSpec((wy + MEM_WIDTH, d), lambda i: (0, 0), pipeline_mode=pl.Buffered(1)),
    ]
    args = [y, u, u, u, mkv, x, w_out]
    if final:
        in_specs.append(pl.BlockSpec((1, d), lambda i: (0, 0)))
        args.append(final_g.reshape(1, d).astype(F32))
    return pl.pallas_call(
        functools.partial(_out_proj_kernel, wy=wy, final=final),
        grid=(m // tm,),
        in_specs=in_specs,
        out_specs=pl.BlockSpec((tm, d), lambda i: (i, 0)),
        out_shape=jax.ShapeDtypeStruct((m, d), F32),
        scratch_shapes=[pltpu.VMEM((tm, wy + MEM_WIDTH), BF16)],
        compiler_params=_cparams(1),
        name="out_proj",
    )(*args)


def _fox_gate_kernel(f_ref, b_ref, c_ref, carry_ref):
    @pl.when(pl.program_id(1) == 0)
    def _():
        carry_ref[...] = jnp.zeros_like(carry_ref)

    f_t = f_ref[...].astype(F32).T[:N_HEADS, :] + b_ref[...]
    log_f = jnp.minimum(f_t, 0.0) - jnp.log(1.0 + jnp.exp(-jnp.abs(f_t)))
    ts = log_f.shape[1]
    r = lax.broadcasted_iota(I32, (ts, ts), 0)
    c = lax.broadcasted_iota(I32, (ts, ts), 1)
    upper = jnp.where(r <= c, 1.0, 0.0).astype(F32)
    cs = jnp.dot(log_f, upper, precision=lax.Precision.HIGHEST,
                 preferred_element_type=F32) + carry_ref[...]
    c_ref[...] = cs * LOG2E
    carry_ref[...] = cs[:, ts - 1:ts]


def _fox_gate(u3, bias, f_blk):
    b, s, _ = u3.shape
    ts = min(256, s)
    return pl.pallas_call(
        _fox_gate_kernel,
        grid=(b, s // ts),
        in_specs=[pl.BlockSpec((None, ts, LANES), lambda bi, j: (bi, j, f_blk)),
                  pl.BlockSpec((N_HEADS, 1), lambda bi, j: (0, 0))],
        out_specs=pl.BlockSpec((None, N_HEADS, ts), lambda bi, j: (bi, 0, j)),
        out_shape=jax.ShapeDtypeStruct((b, N_HEADS, s), F32),
        scratch_shapes=[pltpu.VMEM((N_HEADS, 1), F32)],
        compiler_params=_cparams(2),
        name="fox_gate",
    )(u3, bias.reshape(N_HEADS, 1).astype(F32))


STRIP = 32
HEADS_PER_STEP = 4


def _softmax_strips(t_ref, p_ref, m_ref, l_ref, alpha_ref, adjust):
    tq, tk = t_ref.shape
    for r0 in range(0, tq, min(STRIP, tq)):
        rs = pl.ds(r0, min(STRIP, tq))
        blocks = [adjust(r0, c, t_ref[rs, c * LANES:(c + 1) * LANES]) for c in range(tk // LANES)]
        live = [x for x in blocks if x is not None]
        mx = live[0]
        for x in live[1:]:
            mx = jnp.maximum(mx, x)
        m_old = m_ref[rs, :]
        m_new = jnp.maximum(m_old, jnp.max(mx, axis=-1, keepdims=True))
        alpha = jnp.exp2(m_old - m_new)
        ps = [None if x is None else jnp.exp2(x - m_new) for x in blocks]
        live = [x for x in ps if x is not None]
        sm = live[0]
        for x in live[1:]:
            sm = sm + x
        l_ref[rs, :] = alpha * l_ref[rs, :] + jnp.sum(sm, axis=-1, keepdims=True)
        m_ref[rs, :] = m_new
        alpha_ref[rs, :] = alpha
        for c, x in enumerate(ps):
            p_ref[rs, c * LANES:(c + 1) * LANES] = (
                jnp.zeros((min(STRIP, tq), LANES), BF16) if x is None else x.astype(BF16))


def _flash_scratch(n_streams, tq):
    return [pltpu.VMEM((n_streams, tq, tq), F32), pltpu.VMEM((n_streams, tq, tq), BF16),
            pltpu.VMEM((n_streams, tq, LANES), F32), pltpu.VMEM((n_streams, tq, LANES), F32),
            pltpu.VMEM((n_streams, tq, LANES), F32), pltpu.VMEM((n_streams, tq, HEAD_DIM), F32)]


def _flash_init(scratch):
    _, _, m_ref, l_ref, _, acc_ref = scratch
    m_ref[...] = jnp.full(m_ref.shape, NEG, F32)
    l_ref[...] = jnp.zeros(l_ref.shape, F32)
    acc_ref[...] = jnp.zeros(acc_ref.shape, F32)


def _flash_chunk(n_streams, j, raw_scores, values, to_logits, scratch, *, diagonal):
    t_ref, p_ref, m_ref, l_ref, alpha_ref, acc_ref = scratch
    strip = min(STRIP, t_ref.shape[1])
    row = lax.broadcasted_iota(I32, (strip, LANES), 0)
    col = lax.broadcasted_iota(I32, (strip, LANES), 1)
    for a in range(n_streams):
        t_ref[a] = raw_scores(a, j)
    for a in range(n_streams):
        def adjust(r0, c, x, a=a):
            t = to_logits(a, j, r0, c, x)
            if not diagonal or c * LANES + LANES - 1 <= r0:
                return t
            if c * LANES > r0 + strip - 1:
                return None
            return jnp.where(col + c * LANES <= row + r0, t, NEG)

        _softmax_strips(t_ref.at[a], p_ref.at[a], m_ref.at[a], l_ref.at[a], alpha_ref.at[a],
                        adjust)
    for a in range(n_streams):
        acc_ref[a] = alpha_ref[a] * acc_ref[a] + jnp.dot(p_ref[a], values(a, j),
                                                         preferred_element_type=F32)


def _flash_finish(n_streams, scratch):
    _, _, _, l_ref, _, acc_ref = scratch
    return [acc_ref[a] / l_ref[a] for a in range(n_streams)]


def _causal_flash(i, n_streams, raw_scores, values, to_logits, scratch):
    _flash_init(scratch)

    def body(j, _):
        _flash_chunk(n_streams, j, raw_scores, values, to_logits, scratch, diagonal=False)
        return 0

    lax.fori_loop(0, i, body, 0)
    _flash_chunk(n_streams, i, raw_scores, values, to_logits, scratch, diagonal=True)
    return _flash_finish(n_streams, scratch)


def _fox_attn_kernel(q_ref, k_ref, v_ref, c_ref, o_ref, qs_ref, *scratch, tq):
    i = pl.program_id(2)
    qk_scale = HEAD_DIM ** -0.5 * LOG2E
    qs_ref[...] = (q_ref[...].astype(F32) * qk_scale).astype(BF16)

    def raw_scores(a, j):
        off = pl.multiple_of(j * tq, tq)
        lo, hi = a * HEAD_DIM, (a + 1) * HEAD_DIM
        return lax.dot_general(qs_ref[:, lo:hi], k_ref[pl.ds(off, tq), lo:hi], NT_DIMS,
                               preferred_element_type=F32)

    def to_logits(a, j, r0, c, x):
        return x - c_ref[a, j, :, c * LANES:(c + 1) * LANES]

    def values(a, j):
        off = pl.multiple_of(j * tq, tq)
        return v_ref[pl.ds(off, tq), a * HEAD_DIM:(a + 1) * HEAD_DIM]

    outs = _causal_flash(i, HEADS_PER_STEP, raw_scores, values, to_logits, scratch)
    for a in range(HEADS_PER_STEP):
        o_ref[:, a * HEAD_DIM:(a + 1) * HEAD_DIM] = outs[a].astype(o_ref.dtype)


def _fox_attn(u3, c_t, *, q_blk, k_blk, v_blk):
    b, s, _ = u3.shape
    tq = min(512, s)
    nq = s // tq
    pw = HEADS_PER_STEP * HEAD_DIM
    c5 = c_t.reshape(b, N_HEADS, nq, 1, tq)
    return pl.pallas_call(
        functools.partial(_fox_attn_kernel, tq=tq),
        grid=(b, N_HEADS // HEADS_PER_STEP, nq),
        in_specs=[pl.BlockSpec((None, tq, pw), lambda bi, p, i: (bi, i, q_blk + p)),
                  pl.BlockSpec((None, s, pw), lambda bi, p, i: (bi, 0, k_blk + p)),
                  pl.BlockSpec((None, s, pw), lambda bi, p, i: (bi, 0, v_blk + p)),
                  pl.BlockSpec((None, HEADS_PER_STEP, nq, 1, tq), lambda bi, p, i: (bi, p, 0, 0, 0))],
        out_specs=pl.BlockSpec((None, tq, pw), lambda bi, p, i: (bi, i, p)),
        out_shape=jax.ShapeDtypeStruct((b, s, N_HEADS * HEAD_DIM), BF16),
        scratch_shapes=[pltpu.VMEM((tq, pw), BF16)] + _flash_scratch(HEADS_PER_STEP, tq),
        compiler_params=_cparams(3),
        name="fox_attn",
    )(u3, u3, u3, c5)


def _dsa_kprep_kernel(k_ref, mk_ref, c128_ref, s128_ref, c64_ref, s64_ref,
                      kr_ref, klo_ref, khi_ref):
    kr_ref[...] = _rope128(k_ref[...].astype(F32), c128_ref[...], s128_ref[...]).astype(BF16)
    ki = _rope64(mk_ref[...].astype(F32), c64_ref[...], s64_ref[...])
    lane = lax.broadcasted_iota(I32, ki.shape, 1)
    lo = jnp.where(lane < IDX_DIM, ki, 0.0)
    klo_ref[...] = lo.astype(BF16)
    khi_ref[...] = pltpu.roll(lo, IDX_DIM, 1).astype(BF16)


def _dsa_kprep(u3, tabs, *, k_blk, misc_blk):
    b, s, _ = u3.shape
    ts = min(512, s)
    c128, s128, c64, s64 = tabs

    def spec(blk):
        return pl.BlockSpec((None, ts, LANES), lambda bi, i: (bi, i, blk))

    return pl.pallas_call(
        _dsa_kprep_kernel,
        grid=(b, s // ts),
        in_specs=[spec(k_blk), spec(misc_blk), spec(0), spec(0), spec(0), spec(0)],
        out_specs=[spec(0)] * 3,
        out_shape=[jax.ShapeDtypeStruct((b, s, LANES), BF16)] * 3,
        compiler_params=_cparams(2),
        name="dsa_kprep",
    )(u3, u3, c128, s128, c64, s64)


def _dsa_kernel(q_ref, qi_ref, mq_ref, v_ref, kr_ref, klo_ref, khi_ref,
                cq128_ref, sq128_ref, cq64_ref, sq64_ref, y_in_ref,
                o_ref,
                qr_ref, qir_ref, w_ref, sc_ref, t4_ref, yo_ref, *flash,
                tq, kl, q_off, n_sel):
    del y_in_ref
    t0 = q_off + pl.program_id(1) * tq
    qk_scale = HEAD_DIM ** -0.5 * LOG2E
    idx_scale = (IDX_DIM ** -0.5) * (IDX_HEADS ** -0.5)
    ck = min(512, kl)
    n_chunks = kl // ck
    strip = min(STRIP, tq)

    cq128, sq128 = cq128_ref[...], sq128_ref[...]
    for h in range(N_HEADS):
        qh = q_ref[:, h * HEAD_DIM:(h + 1) * HEAD_DIM].astype(F32)
        qr_ref[h] = (_rope128(qh, cq128, sq128) * qk_scale).astype(BF16)
    cq64, sq64 = cq64_ref[...], sq64_ref[...]
    for a in range(IDX_HEADS // 2):
        qir_ref[a] = _rope64(qi_ref[:, a * LANES:(a + 1) * LANES].astype(F32),
                             cq64, sq64).astype(BF16)
    w_ref[...] = mq_ref[...].astype(F32) * idx_scale

    row = lax.broadcasted_iota(I32, (strip, LANES), 0)
    col = lax.broadcasted_iota(I32, (strip, LANES), 1)
    group = 4

    def idx_chunk(j, _):
        off = pl.multiple_of(j * ck, ck)
        k_lo, k_hi = klo_ref[pl.ds(off, ck), :], khi_ref[pl.ds(off, ck), :]
        for g in range(IDX_HEADS // group):
            for a2 in range(group // 2):
                x = qir_ref[g * (group // 2) + a2]
                t4_ref[2 * a2] = lax.dot_general(x, k_lo, NT_DIMS, preferred_element_type=F32)
                t4_ref[2 * a2 + 1] = lax.dot_general(x, k_hi, NT_DIMS,
                                                     preferred_element_type=F32)
            last = g == IDX_HEADS // group - 1
            for r0 in range(0, tq, strip):
                rs = pl.ds(r0, strip)
                w_rows = w_ref[rs, :]
                ws = [jnp.broadcast_to(w_rows[:, IDX_DIM + g * group + u:IDX_DIM + g * group + u + 1],
                                       (strip, LANES)) for u in range(group)]
                for c in range(ck // LANES):
                    cs = pl.ds(c * LANES, LANES)
                    acc = ws[0] * jnp.maximum(t4_ref[0, rs, cs], 0.0)
                    for u in range(1, group):
                        acc = acc + ws[u] * jnp.maximum(t4_ref[u, rs, cs], 0.0)
                    if g > 0:
                        acc = acc + sc_ref[j, rs, cs]
                    if last:
                        causal = col + (off + c * LANES) <= row + (t0 + r0)
                        acc = jnp.where(causal, acc, -jnp.inf)
                    sc_ref[j, rs, cs] = acc
        return 0

    lax.fori_loop(0, n_chunks, idx_chunk, 0)

    shape3 = (n_chunks, tq, ck)
    col3 = lax.broadcasted_iota(I32, shape3, 0) * ck + lax.broadcasted_iota(I32, shape3, 2)
    causal3 = col3 <= lax.broadcasted_iota(I32, shape3, 1) + t0

    def count(x):
        return jnp.sum(jnp.sum(x, axis=0), axis=-1, keepdims=True)

    def key_to_f32(key):
        return pltpu.bitcast(jnp.where(key < 0, key ^ jnp.int32(0x7FFFFFFF), key), F32)

    def thr_body(it, key):
        cand = key + (jnp.int32(1) << (31 - it))
        c = count(jnp.where(sc_ref[...] >= key_to_f32(cand), 1.0, 0.0))
        return jnp.where(c >= n_sel, cand, key)

    thr_key = lax.fori_loop(0, 32, thr_body, jnp.full((tq, 1), INT_MIN, I32))
    take_all = thr_key == INT_MIN
    thr = key_to_f32(thr_key)
    score = sc_ref[...]
    need = n_sel - count(jnp.where(score > thr, 1.0, 0.0))
    surplus = jnp.where(take_all, 0.0, count(jnp.where(score == thr, 1.0, 0.0)) - need)
    has_surplus = jnp.max(surplus) > 0.0

    n_bits = kl.bit_length()

    def tie_body(it, jm):
        cand = jm + (jnp.int32(1) << (n_bits - 1 - it))
        f = count(jnp.where(sc_ref[...] == thr, jnp.where(col3 < cand, 1.0, 0.0), 0.0))
        return jnp.where(jnp.logical_and(f < need, cand <= kl), cand, jm)

    jm = lax.fori_loop(0, jnp.where(has_surplus, n_bits, 0), tie_body,
                       jnp.broadcast_to(jnp.where(has_surplus, 0, kl), (tq, 1)).astype(I32))
    keep = jnp.where(jnp.logical_or(score > thr, take_all), 1.0,
                     jnp.where(score == thr, jnp.where(col3 <= jm, 1.0, 0.0), 0.0))
    sc_ref[...] = jnp.where(causal3, jnp.where(keep > 0.0, 0.0, NEG), NEG)

    def raw_scores(h0):
        def f(a, j):
            return lax.dot_general(qr_ref[h0 + a], kr_ref[j * ck:(j + 1) * ck, :], NT_DIMS,
                                   preferred_element_type=F32)
        return f

    def values(a, j):
        return v_ref[j * ck:(j + 1) * ck, :]

    def to_logits(a, j, r0, c, x):
        return x + sc_ref[j, r0:r0 + strip, c * LANES:(c + 1) * LANES]

    def group_body(pp, _):
        h0 = HEADS_PER_STEP * pp
        _flash_init(flash)
        for j in range(n_chunks):
            _flash_chunk(HEADS_PER_STEP, j, raw_scores(h0), values, to_logits, flash,
                         diagonal=False)
        outs = _flash_finish(HEADS_PER_STEP, flash)
        for a in range(HEADS_PER_STEP):
            yo_ref[h0 + a] = outs[a].astype(yo_ref.dtype)
        return 0

    lax.fori_loop(0, N_HEADS // HEADS_PER_STEP, group_body, 0)
    for h in range(N_HEADS):
        o_ref[:, h * HEAD_DIM:(h + 1) * HEAD_DIM] = yo_ref[h]


def _dsa_group(u3, kprep, tabs, y, *, q_off, rows, kl, n_sel, q_blk, v_blk, qi_blk, misc_blk):
    b, s, _ = u3.shape
    tq = rows
    assert kl % min(512, kl) == 0 and tq == min(512, kl)
    qb0 = q_off // tq
    c128, s128, c64, s64 = tabs
    kr, klo, khi = kprep
    qw = N_HEADS * HEAD_DIM
    iw = IDX_HEADS * IDX_DIM
    ck = min(512, kl)

    def qspec(width, blk):
        return pl.BlockSpec((None, tq, width), lambda bi, i: (bi, qb0 + i, blk))

    def kspec(blk):
        return pl.BlockSpec((None, kl, LANES), lambda bi, i: (bi, 0, blk))

    return pl.pallas_call(
        functools.partial(_dsa_kernel, tq=tq, kl=kl, q_off=q_off, n_sel=n_sel),
        grid=(b, rows // tq),
        in_specs=[qspec(qw, q_blk), qspec(iw, qi_blk), qspec(LANES, misc_blk),
                  kspec(v_blk), kspec(0), kspec(0), kspec(0),
                  qspec(LANES, 0), qspec(LANES, 0), qspec(LANES, 0), qspec(LANES, 0),
                  pl.BlockSpec(memory_space=pl.ANY)],
        out_specs=pl.BlockSpec((None, tq, qw), lambda bi, i: (bi, qb0 + i, 0)),
        out_shape=jax.ShapeDtypeStruct((b, s, qw), BF16),
        input_output_aliases={11: 0},
        scratch_shapes=[pltpu.VMEM((N_HEADS, tq, HEAD_DIM), BF16),
                        pltpu.VMEM((IDX_HEADS // 2, tq, LANES), BF16),
                        pltpu.VMEM((tq, LANES), F32),
                        pltpu.VMEM((kl // ck, tq, ck), F32),
                        pltpu.VMEM((4, tq, ck), F32),
                        pltpu.VMEM((N_HEADS, tq, HEAD_DIM), BF16)]
        + _flash_scratch(HEADS_PER_STEP, tq),
        compiler_params=_cparams(2),
        name="dsa_attn",
    )(u3, u3, u3, u3, kr, klo, khi, c128, s128, c64, s64, y)


def _dsa_attn(u3, tabs, *, k_blk, misc_blk, **blks):
    b, s, _ = u3.shape
    n_sel = min(TOPK_MAX, s // 4)
    rows = min(512, s)
    kprep = _dsa_kprep(u3, tabs, k_blk=k_blk, misc_blk=misc_blk)
    y = jnp.zeros((b, s, N_HEADS * HEAD_DIM), BF16)
    for q_off in range(0, s, rows):
        y = _dsa_group(u3, kprep, tabs, y, q_off=q_off, rows=rows, kl=q_off + rows, n_sel=n_sel,
                       misc_blk=misc_blk, **blks)
    return y


def _dil_kernel(q0_ref, k0_ref, v0_ref, q1_ref, k1_ref, v1_ref, q2_ref, k2_ref, v2_ref,
                cos_ref, sin_ref, y_ref, qr_ref, kr_ref, o_ref, lse_ref, t_ref, p_ref, *, seq):
    qk_scale = HEAD_DIM ** -0.5 * LOG2E
    cos_f, sin_s = cos_ref[...], sin_ref[...]
    groups = ((q0_ref, k0_ref, v0_ref), (q1_ref, k1_ref, v1_ref), (q2_ref, k2_ref, v2_ref))
    for g, (q_ref, k_ref, _) in enumerate(groups):
        qr_ref[g] = _rope128(q_ref[...].astype(F32), cos_f, sin_s) * qk_scale
        kr_ref[g] = _rope128(k_ref[...].astype(F32), cos_f, sin_s)

    for g, (window, dil) in enumerate(DILATED_PAIRS):
        v_ref = groups[g][2]
        sub = seq // dil
        qb = min(window // dil, sub)
        tiles = [(r, i) for r in range(dil) for i in range(sub // qb)]

        def rows(r, blk, dil=dil, qb=qb):
            start = r + dil * qb * blk
            return pl.ds(start, qb) if dil == 1 else pl.ds(start, qb, stride=dil)

        for n, (r, i) in enumerate(tiles):
            q = qr_ref[g, rows(r, i), :].astype(BF16)
            k_cur = kr_ref[g, rows(r, i), :].astype(BF16)
            t_ref[n, :qb, qb:2 * qb] = lax.dot_general(q, k_cur, NT_DIMS,
                                                       preferred_element_type=F32)
            if i > 0:
                k_prev = kr_ref[g, rows(r, i - 1), :].astype(BF16)
                t_ref[n, :qb, :qb] = lax.dot_general(q, k_prev, NT_DIMS,
                                                     preferred_element_type=F32)
        row = lax.broadcasted_iota(I32, (qb, qb), 0)
        col = lax.broadcasted_iota(I32, (qb, qb), 1)
        for n, (r, i) in enumerate(tiles):
            t_cur = jnp.where(col <= row, t_ref[n, :qb, qb:2 * qb], NEG)
            m = jnp.max(t_cur, axis=-1, keepdims=True)
            if i > 0:
                t_prev = jnp.where(col >= row, t_ref[n, :qb, :qb], NEG)
                m = jnp.maximum(m, jnp.max(t_prev, axis=-1, keepdims=True))
            p_cur = jnp.exp2(t_cur - m)
            l = jnp.sum(p_cur, axis=-1, keepdims=True)
            p_ref[n, :qb, qb:2 * qb] = p_cur.astype(BF16)
            if i > 0:
                p_prev = jnp.exp2(t_prev - m)
                l = l + jnp.sum(p_prev, axis=-1, keepdims=True)
                p_ref[n, :qb, :qb] = p_prev.astype(BF16)
            t_ref[n, :qb, :LANES] = jnp.broadcast_to(l, (qb, LANES))
            t_ref[n, :qb, LANES:2 * LANES] = jnp.broadcast_to(m, (qb, LANES))
        for n, (r, i) in enumerate(tiles):
            acc = jnp.dot(p_ref[n, :qb, qb:2 * qb], v_ref[rows(r, i), :].astype(BF16),
                          preferred_element_type=F32)
            if i > 0:
                acc = acc + jnp.dot(p_ref[n, :qb, :qb], v_ref[rows(r, i - 1), :].astype(BF16),
                                    preferred_element_type=F32)
            l = t_ref[n, :qb, :LANES]
            o_ref[g, rows(r, i), :] = acc / l
            lse_ref[g, rows(r, i), :] = t_ref[n, :qb, LANES:2 * LANES] + jnp.log2(l)

    l0, l1, l2 = lse_ref[0], lse_ref[1], lse_ref[2]
    m = jnp.maximum(jnp.maximum(l0, l1), l2)
    e0, e1, e2 = jnp.exp2(l0 - m), jnp.exp2(l1 - m), jnp.exp2(l2 - m)
    y = (e0 * o_ref[0] + e1 * o_ref[1] + e2 * o_ref[2]) / (e0 + e1 + e2)
    y_ref[...] = y.astype(y_ref.dtype)


def _dil_attn(ua3, ub3, tabs128, *, q0_blk, k0_blk, v0_blk):
    b, s, _ = ua3.shape
    for window, dil in DILATED_PAIRS:
        assert s % dil == 0 and (s // dil) % min(window // dil, s // dil) == 0
    cos_t, sin_t = tabs128
    tile = min(DILATED_PAIRS[0][0], s)
    n_tiles = s // min(min(w // d, s // d) for w, d in DILATED_PAIRS)

    def spec(blk0):
        return pl.BlockSpec((None, s, HEAD_DIM), lambda bi, a: (bi, 0, blk0 + a))

    tab = pl.BlockSpec((None, s, LANES), lambda bi, a: (bi, 0, 0))
    return pl.pallas_call(
        functools.partial(_dil_kernel, seq=s),
        grid=(b, DIL_HEADS),
        in_specs=[spec(q0_blk), spec(k0_blk), spec(v0_blk)]
        + [spec(j * DIL_HEADS) for j in range(6)] + [tab, tab],
        out_specs=pl.BlockSpec((None, s, HEAD_DIM), lambda bi, a: (bi, 0, a)),
        out_shape=jax.ShapeDtypeStruct((b, s, DIL_WIDTH), BF16),
        scratch_shapes=[pltpu.VMEM((3, s, HEAD_DIM), F32)] * 4
        + [pltpu.VMEM((n_tiles, tile, 2 * LANES), F32), pltpu.VMEM((n_tiles, tile, 2 * LANES), BF16)],
        compiler_params=_cparams(2),
        name="dilated_attn",
    )(ua3, ua3, ua3, ub3, ub3, ub3, ub3, ub3, ub3, cos_t, sin_t)


def _mla_kernel(qn_ref, qr_ref, kn_ref, kr_ref, v_ref, cq_ref, sq_ref, ck_ref, sk_ref,
                o_ref, qcat_ref, kcat_ref, *scratch, tq):
    i = pl.program_id(2)
    qk_scale = (QK_NOPE + QK_ROPE) ** -0.5 * LOG2E
    cat = QK_NOPE + QK_ROPE

    @pl.when(i == 0)
    def _():
        k_rope = _rope64(kr_ref[...].astype(F32), ck_ref[...],
                         sk_ref[...])[:, :QK_ROPE].astype(BF16)
        for a in range(HEADS_PER_STEP):
            kcat_ref[a, :, :QK_NOPE] = kn_ref[:, a * HEAD_DIM:(a + 1) * HEAD_DIM]
            kcat_ref[a, :, QK_NOPE:cat] = k_rope
            kcat_ref[a, :, cat:] = jnp.zeros((kcat_ref.shape[1], 2 * LANES - cat), BF16)

    cos_q, sin_q = cq_ref[...], sq_ref[...]
    for a2 in range(HEADS_PER_STEP // 2):
        q_rope = (_rope64(qr_ref[:, a2 * LANES:(a2 + 1) * LANES].astype(F32), cos_q, sin_q)
                  * qk_scale).astype(BF16)
        for a in (2 * a2, 2 * a2 + 1):
            q_nope = qn_ref[:, a * HEAD_DIM:(a + 1) * HEAD_DIM].astype(F32) * qk_scale
            qcat_ref[a, :, :QK_NOPE] = q_nope.astype(BF16)
            qcat_ref[a, :, QK_NOPE:cat] = q_rope[:, (a % 2) * QK_ROPE:(a % 2 + 1) * QK_ROPE]
            qcat_ref[a, :, cat:] = jnp.zeros((tq, 2 * LANES - cat), BF16)

    def raw_scores(a, j):
        off = pl.multiple_of(j * tq, tq)
        return lax.dot_general(qcat_ref[a], kcat_ref[a, pl.ds(off, tq), :], NT_DIMS,
                               preferred_element_type=F32)

    def to_logits(a, j, r0, c, x):
        return x

    def values(a, j):
        off = pl.multiple_of(j * tq, tq)
        return v_ref[pl.ds(off, tq), a * HEAD_DIM:(a + 1) * HEAD_DIM]

    outs = _causal_flash(i, HEADS_PER_STEP, raw_scores, values, to_logits, scratch)
    for a in range(HEADS_PER_STEP):
        o_ref[:, a * HEAD_DIM:(a + 1) * HEAD_DIM] = outs[a].astype(o_ref.dtype)


def _mla_attn(qf3, kvf3, u3, tabs64, *, kr_blk):
    b, s, _ = qf3.shape
    tq = min(512, s)
    gw = HEADS_PER_STEP * HEAD_DIM
    rw = HEADS_PER_STEP * QK_ROPE
    n_groups = N_HEADS // HEADS_PER_STEP
    cos_t, sin_t = tabs64
    return pl.pallas_call(
        functools.partial(_mla_kernel, tq=tq),
        grid=(b, n_groups, s // tq),
        in_specs=[pl.BlockSpec((None, tq, gw), lambda bi, p, i: (bi, i, p)),
                  pl.BlockSpec((None, tq, rw),
                               lambda bi, p, i: (bi, i, N_HEADS * QK_NOPE // rw + p)),
                  pl.BlockSpec((None, s, gw), lambda bi, p, i: (bi, 0, p)),
                  pl.BlockSpec((None, s, LANES), lambda bi, p, i: (bi, 0, kr_blk)),
                  pl.BlockSpec((None, s, gw), lambda bi, p, i: (bi, 0, n_groups + p)),
                  pl.BlockSpec((None, tq, LANES), lambda bi, p, i: (bi, i, 0)),
                  pl.BlockSpec((None, tq, LANES), lambda bi, p, i: (bi, i, 0)),
                  pl.BlockSpec((None, s, LANES), lambda bi, p, i: (bi, 0, 0)),
                  pl.BlockSpec((None, s, LANES), lambda bi, p, i: (bi, 0, 0))],
        out_specs=pl.BlockSpec((None, tq, gw), lambda bi, p, i: (bi, i, p)),
        out_shape=jax.ShapeDtypeStruct((b, s, N_HEADS * HEAD_DIM), BF16),
        scratch_shapes=[pltpu.VMEM((HEADS_PER_STEP, tq, 2 * LANES), BF16),
                        pltpu.VMEM((HEADS_PER_STEP, s, 2 * LANES), BF16)]
        + _flash_scratch(HEADS_PER_STEP, tq),
        compiler_params=_cparams(3),
        name="mla_attn",
    )(qf3, qf3, kvf3, u3, kvf3, cos_t, sin_t, cos_t, sin_t)


def _pack(parts, multiple=512):
    k = next(p.shape[0] for p, _ in parts if p is not None)
    cols = [jnp.zeros((k, w), BF16) if p is None else p.astype(BF16) for p, w in parts]
    n = sum(w for _, w in parts)
    pad = (-n) % multiple
    if pad:
        cols.append(jnp.zeros((k, pad), BF16))
    return jnp.concatenate(cols, axis=1)


def _cols(w, sizes):
    out, acc = [], 0
    for sz in sizes:
        out.append(w[:, acc:acc + sz])
        acc += sz
    return out


def kernel(x, mem, positions, l0_norm, l0_w_in, l0_forget_bias, l0_mem_norm, l0_w_mem_kv, l0_w_out, l1_norm, l1_w_in, l1_mem_norm, l1_w_mem_kv, l1_w_out, l2_norm, l2_w_in, l2_mem_norm, l2_w_mem_kv, l2_w_out, l3_norm, l3_w_in, l3_q_norm, l3_w_uq, l3_kv_norm, l3_w_ukv, l3_mem_norm, l3_w_mem_kv, l3_w_out, final_norm):
    b, s, d = x.shape
    n_mem = mem.shape[1]
    m = b * s
    mw = N_HEADS * HEAD_DIM
    x2 = x.reshape(m, d)
    mem2 = mem.reshape(b * n_mem, d)

    pos_b = jnp.broadcast_to(positions.astype(F32).reshape(m, 1), (m, LANES))
    tabs128 = tuple(t.reshape(b, s, LANES) for t in _rope_tables(pos_b, HEAD_DIM))
    tabs64 = tuple(t.reshape(b, s, LANES) for t in _rope_tables(pos_b, IDX_DIM))

    def mem_kv(g, w):
        return _norm_proj(mem2, g, w.astype(BF16))

    def finish(y, u, x_in, g_mem, w_mem_kv, w_out, *, wy, zy_blk, zm_blk, qm_blk, final_g=None):
        return _out_proj(y, u, mem_kv(g_mem, w_mem_kv), x_in, w_out.astype(BF16), wy=wy,
                         zy_blk=zy_blk, zm_blk=zm_blk, qm_blk=qm_blk, seq=s, n_mem=n_mem,
                         final_g=final_g)

    q_w, k_w, v_w, f_w, qm_w, z_w = _cols(l0_w_in, (mw, mw, mw, N_HEADS, MEM_WIDTH, mw + MEM_WIDTH))
    w0 = _pack([(z_w[:, :mw], mw), (z_w[:, mw:], MEM_WIDTH), (qm_w, MEM_WIDTH), (q_w, mw),
                (k_w, mw), (v_w, mw)])
    u, f_arr = _norm_proj(x2, l0_norm, w0, w_extra=_pack([(f_w, N_HEADS)], multiple=LANES),
                          extra_dtype=F32)
    u3 = u.reshape(b, s, -1)
    base = (mw + 2 * MEM_WIDTH) // LANES
    c_t = _fox_gate(f_arr.reshape(b, s, LANES), l0_forget_bias, 0)
    y = _fox_attn(u3, c_t, q_blk=base // HEADS_PER_STEP, k_blk=(base + N_HEADS) // HEADS_PER_STEP,
                  v_blk=(base + 2 * N_HEADS) // HEADS_PER_STEP)
    x2 = finish(y.reshape(m, mw), u, x2, l0_mem_norm, l0_w_mem_kv, l0_w_out, wy=mw, zy_blk=0,
                zm_blk=mw // MEM_WIDTH, qm_blk=mw // MEM_WIDTH + 1)

    iw = IDX_HEADS * IDX_DIM
    q_w, k_w, v_w, qi_w, ki_w, wi_w, qm_w, z_w = _cols(
        l1_w_in, (mw, HEAD_DIM, HEAD_DIM, iw, IDX_DIM, IDX_HEADS, MEM_WIDTH, mw + MEM_WIDTH))
    w1 = _pack([(z_w[:, :mw], mw), (q_w, mw), (qi_w, iw), (z_w[:, mw:], MEM_WIDTH),
                (qm_w, MEM_WIDTH), (k_w, HEAD_DIM), (v_w, HEAD_DIM), (ki_w, IDX_DIM),
                (wi_w, IDX_HEADS), (None, LANES - IDX_DIM - IDX_HEADS)])
    u = _norm_proj(x2, l1_norm, w1)
    u3 = u.reshape(b, s, -1)
    kv0 = (2 * mw + iw + 2 * MEM_WIDTH) // LANES
    y = _dsa_attn(u3, tabs128 + tabs64, q_blk=1, qi_blk=2 * mw // iw, k_blk=kv0, v_blk=kv0 + 1,
                  misc_blk=kv0 + 2)
    zoff = (2 * mw + iw) // MEM_WIDTH
    x2 = finish(y.reshape(m, mw), u, x2, l1_mem_norm, l1_w_mem_kv, l1_w_out, wy=mw, zy_blk=0,
                zm_blk=zoff, qm_blk=zoff + 1)

    gw = len(DILATED_PAIRS) * DIL_WIDTH
    q_w, k_w, v_w, qm_w, z_w = _cols(l2_w_in, (gw, gw, gw, MEM_WIDTH, DIL_WIDTH + MEM_WIDTH))

    def grp(w, g):
        return (w[:, g * DIL_WIDTH:(g + 1) * DIL_WIDTH], DIL_WIDTH)

    wa = _pack([grp(q_w, 0), (z_w[:, :DIL_WIDTH], DIL_WIDTH), grp(k_w, 0), grp(v_w, 0),
                (z_w[:, DIL_WIDTH:], MEM_WIDTH), (qm_w, MEM_WIDTH)])
    wb = _pack([grp(q_w, 1), grp(k_w, 1), grp(v_w, 1), grp(q_w, 2), grp(k_w, 2), grp(v_w, 2)])
    u = _norm_proj(x2, l2_norm, wa)
    ub = _norm_proj(x2, l2_norm, wb, out_dtype=F32)
    y = _dil_attn(u.reshape(b, s, -1), ub.reshape(b, s, -1), tabs128, q0_blk=0,
                  k0_blk=2 * DIL_HEADS, v0_blk=3 * DIL_HEADS)
    zoff = 4 * DIL_WIDTH // MEM_WIDTH
    x2 = finish(y.reshape(m, DIL_WIDTH), u, x2, l2_mem_norm, l2_w_mem_kv, l2_w_out, wy=DIL_WIDTH,
                zy_blk=1, zm_blk=zoff, qm_blk=zoff + 1)

    q_lora, kv_lora = l3_w_uq.shape[0], l3_w_ukv.shape[0]
    cq_w, ckv_w, kr_w, qm_w, z_w = _cols(l3_w_in, (q_lora, kv_lora, QK_ROPE, MEM_WIDTH,
                                                  mw + MEM_WIDTH))
    assert q_lora == MEM_WIDTH and kv_lora == MEM_WIDTH
    w3 = _pack([(z_w[:, :mw], mw), (z_w[:, mw:], MEM_WIDTH), (qm_w, MEM_WIDTH), (cq_w, q_lora),
                (ckv_w, kv_lora)])
    u, kr_arr = _norm_proj(x2, l3_norm, w3, w_extra=_pack([(kr_w, QK_ROPE)], multiple=LANES))
    cq_blk = (mw + 2 * MEM_WIDTH) // MEM_WIDTH
    uq = l3_w_uq.reshape(q_lora, N_HEADS, QK_NOPE + QK_ROPE)
    w_uq = _pack([(uq[:, :, :QK_NOPE].reshape(q_lora, -1), N_HEADS * QK_NOPE),
                  (uq[:, :, QK_NOPE:].reshape(q_lora, -1), N_HEADS * QK_ROPE)])
    ukv = l3_w_ukv.reshape(kv_lora, N_HEADS, QK_NOPE + HEAD_DIM)
    w_ukv = _pack([(ukv[:, :, :QK_NOPE].reshape(kv_lora, -1), N_HEADS * QK_NOPE),
                   (ukv[:, :, QK_NOPE:].reshape(kv_lora, -1), N_HEADS * HEAD_DIM)])
    qf = _norm_proj(u, l3_q_norm, w_uq, x_col_block=cq_blk)
    kvf = _norm_proj(u, l3_kv_norm, w_ukv, x_col_block=cq_blk + 1)
    y = _mla_attn(qf.reshape(b, s, -1), kvf.reshape(b, s, -1), kr_arr.reshape(b, s, LANES), tabs64,
                  kr_blk=0)
    x2 = finish(y.reshape(m, mw), u, x2, l3_mem_norm, l3_w_mem_kv, l3_w_out, wy=mw, zy_blk=0,
                zm_blk=mw // MEM_WIDTH, qm_blk=mw // MEM_WIDTH + 1, final_g=final_norm)
    return x2.reshape(b, s, d)
```

```python
import functools

import jax
import jax.numpy as jnp
from jax import lax
from jax.experimental import pallas as pl
from jax.experimental.pallas import tpu as pltpu

F32 = jnp.float32
BF16 = jnp.bfloat16
I32 = jnp.int32

EPS = 1e-6
ROPE_THETA = 10000.0
HEAD_DIM = 128
N_HEADS = 16
MEM_HEADS = 4
MEM_WIDTH = MEM_HEADS * HEAD_DIM
IDX_HEADS = 16
IDX_DIM = 64
TOPK_MAX = 256
DILATED_PAIRS = ((128, 1), (512, 4), (2048, 16))
DIL_HEADS = 6
DIL_WIDTH = DIL_HEADS * HEAD_DIM
QK_NOPE = 128
QK_ROPE = 64
LANES = 128
NEG = -1e30
INT_MIN = -(2 ** 31)
VMEM_LIMIT = 56 * 1024 * 1024

NT_DIMS = (((1,), (1,)), ((), ()))
LOG2E = 1.4426950408889634


def _cparams(n_axes):
    return pltpu.CompilerParams(
        dimension_semantics=("arbitrary",) * n_axes, vmem_limit_bytes=VMEM_LIMIT)


def _rope128(x, cos_f, sin_s):
    return x * cos_f + pltpu.roll(x, 64, 1) * sin_s


def _rope64(x, cos_f, sin_s):
    lane = lax.broadcasted_iota(I32, x.shape, 1)
    partner = jnp.where((lane & 32) == 0, pltpu.roll(x, 96, 1), pltpu.roll(x, 32, 1))
    return x * cos_f + partner * sin_s


def _rope_table_kernel(pos_ref, inv_ref, sgn_ref, cos_ref, sin_ref):
    ang = pos_ref[...] * inv_ref[...]
    cos_ref[...] = jnp.cos(ang)
    sin_ref[...] = jnp.sin(ang) * sgn_ref[...]


def _rope_tables(pos_b, dh):
    m = pos_b.shape[0]
    half = dh // 2
    inv = jnp.power(ROPE_THETA, -jnp.arange(half, dtype=F32) * 2.0 / dh)
    reps = LANES // half
    inv_l = jnp.tile(inv, reps).reshape(1, LANES)
    sgn = jnp.tile(jnp.concatenate([-jnp.ones((half,), F32), jnp.ones((half,), F32)]),
                   reps // 2).reshape(1, LANES)
    ts = min(512, m)
    return pl.pallas_call(
        _rope_table_kernel,
        grid=(m // ts,),
        in_specs=[pl.BlockSpec((ts, LANES), lambda i: (i, 0)),
                  pl.BlockSpec((1, LANES), lambda i: (0, 0)),
                  pl.BlockSpec((1, LANES), lambda i: (0, 0))],
        out_specs=[pl.BlockSpec((ts, LANES), lambda i: (i, 0))] * 2,
        out_shape=[jax.ShapeDtypeStruct((m, LANES), F32)] * 2,
        compiler_params=_cparams(1),
        name="rope_tables",
    )(pos_b, inv_l, sgn)


def _norm_proj_kernel(x_ref, g_ref, w_ref, *rest, has_extra):
    if has_extra:
        wx_ref, o_ref, ox_ref, h_ref = rest
    else:
        o_ref, h_ref = rest

    first = pl.program_id(1) == 0

    @pl.when(first)
    def _():
        tm = o_ref.shape[0]
        parts = 2 if tm % 32 == 0 else 1
        for part in range(parts):
            rows = pl.ds(part * (tm // parts), tm // parts)
            x = x_ref[rows, :].astype(F32)
            ms = jnp.mean(x * x, axis=-1, keepdims=True)
            h_ref[rows, :] = (x * lax.rsqrt(ms + EPS) * g_ref[...]).astype(BF16)
            o_ref[rows, :] = jnp.dot(h_ref[rows, :], w_ref[...],
                                     preferred_element_type=F32).astype(o_ref.dtype)
            if has_extra:
                ox_ref[rows, :] = jnp.dot(h_ref[rows, :], wx_ref[...],
                                          preferred_element_type=F32).astype(ox_ref.dtype)

    @pl.when(jnp.logical_not(first))
    def _():
        o_ref[...] = jnp.dot(h_ref[...], w_ref[...],
                             preferred_element_type=F32).astype(o_ref.dtype)


def _norm_proj(x, g, w, *, x_col_block=0, out_dtype=BF16, w_extra=None, extra_dtype=BF16):
    m = x.shape[0]
    k, n = w.shape
    tm = min(1024, m)
    tn = 1024 if n % 1024 == 0 else 512
    assert m % tm == 0 and n % tn == 0
    has_extra = w_extra is not None
    in_specs = [pl.BlockSpec((tm, k), lambda i, j: (i, x_col_block)),
                pl.BlockSpec((1, k), lambda i, j: (0, 0)),
                pl.BlockSpec((k, tn), lambda i, j: (0, j))]
    out_specs = [pl.BlockSpec((tm, tn), lambda i, j: (i, j))]
    out_shape = [jax.ShapeDtypeStruct((m, n), out_dtype)]
    args = [x, g.reshape(1, k).astype(F32), w]
    if has_extra:
        in_specs.append(pl.BlockSpec((k, LANES), lambda i, j: (0, 0)))
        out_specs.append(pl.BlockSpec((tm, LANES), lambda i, j: (i, 0)))
        out_shape.append(jax.ShapeDtypeStruct((m, LANES), extra_dtype))
        args.append(w_extra)
    outs = pl.pallas_call(
        functools.partial(_norm_proj_kernel, has_extra=has_extra),
        grid=(m // tm, n // tn),
        in_specs=in_specs,
        out_specs=out_specs,
        out_shape=out_shape,
        scratch_shapes=[pltpu.VMEM((tm, k), BF16)],
        compiler_params=_cparams(2),
        name="norm_proj",
    )(*args)
    return outs if has_extra else outs[0]


def _out_proj_kernel(y_ref, zy_ref, zm_ref, qm_ref, mkv_ref, x_ref, w_ref, *rest, wy, final):
    if final:
        gf_ref, o_ref, gated_ref = rest
    else:
        o_ref, gated_ref = rest
    scale = HEAD_DIM ** -0.5
    for h in range(MEM_HEADS):
        lo, hi = h * HEAD_DIM, (h + 1) * HEAD_DIM
        s = lax.dot_general(qm_ref[:, lo:hi], mkv_ref[:, lo:hi], NT_DIMS,
                            preferred_element_type=F32) * scale
        m = jnp.max(s, axis=-1, keepdims=True)
        p = jnp.exp(s - m)
        l = jnp.sum(p, axis=-1, keepdims=True)
        o = jnp.dot(p.astype(BF16), mkv_ref[:, MEM_WIDTH + lo:MEM_WIDTH + hi],
                    preferred_element_type=F32) / l
        zm = zm_ref[:, lo:hi].astype(F32)
        gated_ref[:, wy + lo:wy + hi] = (o * (zm * jax.nn.sigmoid(zm))).astype(BF16)
    tm = o_ref.shape[0]
    parts = 2 if tm % 32 == 0 else 1
    for part in range(parts):
        rows = pl.ds(part * (tm // parts), tm // parts)
        zy = zy_ref[rows, :].astype(F32)
        gated_ref[rows, :wy] = (y_ref[rows, :].astype(F32)
                                * (zy * jax.nn.sigmoid(zy))).astype(BF16)
        out = x_ref[rows, :] + jnp.dot(gated_ref[rows, :], w_ref[...],
                                       preferred_element_type=F32)
        if final:
            ms = jnp.mean(out * out, axis=-1, keepdims=True)
            out = out * lax.rsqrt(ms + EPS) * gf_ref[...]
        o_ref[rows, :] = out


def _out_proj(y, u, mkv, x, w_out, *, wy, zy_blk, zm_blk, qm_blk, seq, n_mem, final_g=None):
    m, d = x.shape
    tm = min(512, seq)
    final = final_g is not None
    in_specs = [
        pl.BlockSpec((tm, wy), lambda i: (i, 0)),
        pl.BlockSpec((tm, wy), lambda i: (i, zy_blk)),
        pl.BlockSpec((tm, MEM_WIDTH), lambda i: (i, zm_blk)),
        pl.BlockSpec((tm, MEM_WIDTH), lambda i: (i, qm_blk)),
        pl.BlockSpec((n_mem, 2 * MEM_WIDTH), lambda i: ((i * tm) // seq, 0)),
        pl.BlockSpec((tm, d), lambda i: (i, 0)),
        pl.BlockSpec((wy + MEM_WIDTH, d), lambda i: (0, 0), pipeline_mode=pl.Buffered(1)),
    ]
    args = [y, u, u, u, mkv, x, w_out]
    if final:
        in_specs.append(pl.BlockSpec((1, d), lambda i: (0, 0)))
        args.append(final_g.reshape(1, d).astype(F32))
    return pl.pallas_call(
        functools.partial(_out_proj_kernel, wy=wy, final=final),
        grid=(m // tm,),
        in_specs=in_specs,
        out_specs=pl.BlockSpec((tm, d), lambda i: (i, 0)),
        out_shape=jax.ShapeDtypeStruct((m, d), F32),
        scratch_shapes=[pltpu.VMEM((tm, wy + MEM_WIDTH), BF16)],
        compiler_params=_cparams(1),
        name="out_proj",
    )(*args)


def _fox_gate_kernel(f_ref, b_ref, c_ref, carry_ref):
    @pl.when(pl.program_id(1) == 0)
    def _():
        carry_ref[...] = jnp.zeros_like(carry_ref)

    f_t = f_ref[...].astype(F32).T[:N_HEADS, :] + b_ref[...]
    log_f = jnp.minimum(f_t, 0.0) - jnp.log(1.0 + jnp.exp(-jnp.abs(f_t)))
    ts = log_f.shape[1]
    r = lax.broadcasted_iota(I32, (ts, ts), 0)
    c = lax.broadcasted_iota(I32, (ts, ts), 1)
    upper = jnp.where(r <= c, 1.0, 0.0).astype(F32)
    cs = jnp.dot(log_f, upper, precision=lax.Precision.HIGHEST,
                 preferred_element_type=F32) + carry_ref[...]
    c_ref[...] = cs * LOG2E
    carry_ref[...] = cs[:, ts - 1:ts]


def _fox_gate(u3, bias, f_blk):
    b, s, _ = u3.shape
    ts = min(256, s)
    return pl.pallas_call(
        _fox_gate_kernel,
        grid=(b, s // ts),
        in_specs=[pl.BlockSpec((None, ts, LANES), lambda bi, j: (bi, j, f_blk)),
                  pl.BlockSpec((N_HEADS, 1), lambda bi, j: (0, 0))],
        out_specs=pl.BlockSpec((None, N_HEADS, ts), lambda bi, j: (bi, 0, j)),
        out_shape=jax.ShapeDtypeStruct((b, N_HEADS, s), F32),
        scratch_shapes=[pltpu.VMEM((N_HEADS, 1), F32)],
        compiler_params=_cparams(2),
        name="fox_gate",
    )(u3, bias.reshape(N_HEADS, 1).astype(F32))


STRIP = 32
HEADS_PER_STEP = 4


def _softmax_strips(t_ref, p_ref, m_ref, l_ref, alpha_ref, adjust):
    tq, tk = t_ref.shape
    for r0 in range(0, tq, min(STRIP, tq)):
        rs = pl.ds(r0, min(STRIP, tq))
        blocks = [adjust(r0, c, t_ref[rs, c * LANES:(c + 1) * LANES]) for c in range(tk // LANES)]
        live = [x for x in blocks if x is not None]
        mx = live[0]
        for x in live[1:]:
            mx = jnp.maximum(mx, x)
        m_old = m_ref[rs, :]
        m_new = jnp.maximum(m_old, jnp.max(mx, axis=-1, keepdims=True))
        alpha = jnp.exp2(m_old - m_new)
        ps = [None if x is None else jnp.exp2(x - m_new) for x in blocks]
        live = [x for x in ps if x is not None]
        sm = live[0]
        for x in live[1:]:
            sm = sm + x
        l_ref[rs, :] = alpha * l_ref[rs, :] + jnp.sum(sm, axis=-1, keepdims=True)
        m_ref[rs, :] = m_new
        alpha_ref[rs, :] = alpha
        for c, x in enumerate(ps):
            p_ref[rs, c * LANES:(c + 1) * LANES] = (
                jnp.zeros((min(STRIP, tq), LANES), BF16) if x is None else x.astype(BF16))


def _flash_scratch(n_streams, tq):
    return [pltpu.VMEM((n_streams, tq, tq), F32), pltpu.VMEM((n_streams, tq, tq), BF16),
            pltpu.VMEM((n_streams, tq, LANES), F32), pltpu.VMEM((n_streams, tq, LANES), F32),
            pltpu.VMEM((n_streams, tq, LANES), F32), pltpu.VMEM((n_streams, tq, HEAD_DIM), F32)]


def _flash_init(scratch):
    _, _, m_ref, l_ref, _, acc_ref = scratch
    m_ref[...] = jnp.full(m_ref.shape, NEG, F32)
    l_ref[...] = jnp.zeros(l_ref.shape, F32)
    acc_ref[...] = jnp.zeros(acc_ref.shape, F32)


def _flash_chunk(n_streams, j, raw_scores, values, to_logits, scratch, *, diagonal):
    t_ref, p_ref, m_ref, l_ref, alpha_ref, acc_ref = scratch
    strip = min(STRIP, t_ref.shape[1])
    row = lax.broadcasted_iota(I32, (strip, LANES), 0)
    col = lax.broadcasted_iota(I32, (strip, LANES), 1)
    for a in range(n_streams):
        t_ref[a] = raw_scores(a, j)
    for a in range(n_streams):
        def adjust(r0, c, x, a=a):
            t = to_logits(a, j, r0, c, x)
            if not diagonal or c * LANES + LANES - 1 <= r0:
                return t
            if c * LANES > r0 + strip - 1:
                return None
            return jnp.where(col + c * LANES <= row + r0, t, NEG)

        _softmax_strips(t_ref.at[a], p_ref.at[a], m_ref.at[a], l_ref.at[a], alpha_ref.at[a],
                        adjust)
    for a in range(n_streams):
        acc_ref[a] = alpha_ref[a] * acc_ref[a] + jnp.dot(p_ref[a], values(a, j),
                                                         preferred_element_type=F32)


def _flash_finish(n_streams, scratch):
    _, _, _, l_ref, _, acc_ref = scratch
    return [acc_ref[a] / l_ref[a] for a in range(n_streams)]


def _causal_flash(i, n_streams, raw_scores, values, to_logits, scratch):
    _flash_init(scratch)

    def body(j, _):
        _flash_chunk(n_streams, j, raw_scores, values, to_logits, scratch, diagonal=False)
        return 0

    lax.fori_loop(0, i, body, 0)
    _flash_chunk(n_streams, i, raw_scores, values, to_logits, scratch, diagonal=True)
    return _flash_finish(n_streams, scratch)


def _fox_attn_kernel(q_ref, k_ref, v_ref, c_ref, o_ref, qs_ref, *scratch, tq):
    i = pl.program_id(2)
    qk_scale = HEAD_DIM ** -0.5 * LOG2E
    qs_ref[...] = (q_ref[...].astype(F32) * qk_scale).astype(BF16)

    def raw_scores(a, j):
        off = pl.multiple_of(j * tq, tq)
        lo, hi = a * HEAD_DIM, (a + 1) * HEAD_DIM
        return lax.dot_general(qs_ref[:, lo:hi], k_ref[pl.ds(off, tq), lo:hi], NT_DIMS,
                               preferred_element_type=F32)

    def to_logits(a, j, r0, c, x):
        return x - c_ref[a, j, :, c * LANES:(c + 1) * LANES]

    def values(a, j):
        off = pl.multiple_of(j * tq, tq)
        return v_ref[pl.ds(off, tq), a * HEAD_DIM:(a + 1) * HEAD_DIM]

    outs = _causal_flash(i, HEADS_PER_STEP, raw_scores, values, to_logits, scratch)
    for a in range(HEADS_PER_STEP):
        o_ref[:, a * HEAD_DIM:(a + 1) * HEAD_DIM] = outs[a].astype(o_ref.dtype)


def _fox_attn(u3, c_t, *, q_blk, k_blk, v_blk):
    b, s, _ = u3.shape
    tq = min(512, s)
    nq = s // tq
    pw = HEADS_PER_STEP * HEAD_DIM
    c5 = c_t.reshape(b, N_HEADS, nq, 1, tq)
    return pl.pallas_call(
        functools.partial(_fox_attn_kernel, tq=tq),
        grid=(b, N_HEADS // HEADS_PER_STEP, nq),
        in_specs=[pl.BlockSpec((None, tq, pw), lambda bi, p, i: (bi, i, q_blk + p)),
                  pl.BlockSpec((None, s, pw), lambda bi, p, i: (bi, 0, k_blk + p)),
                  pl.BlockSpec((None, s, pw), lambda bi, p, i: (bi, 0, v_blk + p)),
                  pl.BlockSpec((None, HEADS_PER_STEP, nq, 1, tq), lambda bi, p, i: (bi, p, 0, 0, 0))],
        out_specs=pl.BlockSpec((None, tq, pw), lambda bi, p, i: (bi, i, p)),
        out_shape=jax.ShapeDtypeStruct((b, s, N_HEADS * HEAD_DIM), BF16),
        scratch_shapes=[pltpu.VMEM((tq, pw), BF16)] + _flash_scratch(HEADS_PER_STEP, tq),
        compiler_params=_cparams(3),
        name="fox_attn",
    )(u3, u3, u3, c5)


def _dsa_kprep_kernel(k_ref, mk_ref, c128_ref, s128_ref, c64_ref, s64_ref,
                      kr_ref, klo_ref, khi_ref):
    kr_ref[...] = _rope128(k_ref[...].astype(F32), c128_ref[...], s128_ref[...]).astype(BF16)
    ki = _rope64(mk_ref[...].astype(F32), c64_ref[...], s64_ref[...])
    lane = lax.broadcasted_iota(I32, ki.shape, 1)
    lo = jnp.where(lane < IDX_DIM, ki, 0.0)
    klo_ref[...] = lo.astype(BF16)
    khi_ref[...] = pltpu.roll(lo, IDX_DIM, 1).astype(BF16)


def _dsa_kprep(u3, tabs, *, k_blk, misc_blk):
    b, s, _ = u3.shape
    ts = min(512, s)
    c128, s128, c64, s64 = tabs

    def spec(blk):
        return pl.BlockSpec((None, ts, LANES), lambda bi, i: (bi, i, blk))

    return pl.pallas_call(
        _dsa_kprep_kernel,
        grid=(b, s // ts),
        in_specs=[spec(k_blk), spec(misc_blk), spec(0), spec(0), spec(0), spec(0)],
        out_specs=[spec(0)] * 3,
        out_shape=[jax.ShapeDtypeStruct((b, s, LANES), BF16)] * 3,
        compiler_params=_cparams(2),
        name="dsa_kprep",
    )(u3, u3, c128, s128, c64, s64)


def _dsa_kernel(q_ref, qi_ref, mq_ref, v_ref, kr_ref, klo_ref, khi_ref,
                cq128_ref, sq128_ref, cq64_ref, sq64_ref, y_in_ref,
                o_ref,
                qr_ref, qir_ref, w_ref, sc_ref, t4_ref, yo_ref, *flash,
                tq, kl, q_off, n_sel):
    del y_in_ref
    t0 = q_off + pl.program_id(1) * tq
    qk_scale = HEAD_DIM ** -0.5 * LOG2E
    idx_scale = (IDX_DIM ** -0.5) * (IDX_HEADS ** -0.5)
    ck = min(512, kl)
    n_chunks = kl // ck
    strip = min(STRIP, tq)

    cq128, sq128 = cq128_ref[...], sq128_ref[...]
    for h in range(N_HEADS):
        qh = q_ref[:, h * HEAD_DIM:(h + 1) * HEAD_DIM].astype(F32)
        qr_ref[h] = (_rope128(qh, cq128, sq128) * qk_scale).astype(BF16)
    cq64, sq64 = cq64_ref[...], sq64_ref[...]
    for a in range(IDX_HEADS // 2):
        qir_ref[a] = _rope64(qi_ref[:, a * LANES:(a + 1) * LANES].astype(F32),
                             cq64, sq64).astype(BF16)
    w_ref[...] = mq_ref[...].astype(F32) * idx_scale

    row = lax.broadcasted_iota(I32, (strip, LANES), 0)
    col = lax.broadcasted_iota(I32, (strip, LANES), 1)
    group = 4

    def idx_chunk(j, _):
        off = pl.multiple_of(j * ck, ck)
        k_lo, k_hi = klo_ref[pl.ds(off, ck), :], khi_ref[pl.ds(off, ck), :]
        for g in range(IDX_HEADS // group):
            for a2 in range(group // 2):
                x = qir_ref[g * (group // 2) + a2]
                t4_ref[2 * a2] = lax.dot_general(x, k_lo, NT_DIMS, preferred_element_type=F32)
                t4_ref[2 * a2 + 1] = lax.dot_general(x, k_hi, NT_DIMS,
                                                     preferred_element_type=F32)
            last = g == IDX_HEADS // group - 1
            for r0 in range(0, tq, strip):
                rs = pl.ds(r0, strip)
                w_rows = w_ref[rs, :]
                ws = [jnp.broadcast_to(w_rows[:, IDX_DIM + g * group + u:IDX_DIM + g * group + u + 1],
                                       (strip, LANES)) for u in range(group)]
                for c in range(ck // LANES):
                    cs = pl.ds(c * LANES, LANES)
                    acc = ws[0] * jnp.maximum(t4_ref[0, rs, cs], 0.0)
                    for u in range(1, group):
                        acc = acc + ws[u] * jnp.maximum(t4_ref[u, rs, cs], 0.0)
                    if g > 0:
                        acc = acc + sc_ref[j, rs, cs]
                    if last:
                        causal = col + (off + c * LANES) <= row + (t0 + r0)
                        acc = jnp.where(causal, acc, -jnp.inf)
                    sc_ref[j, rs, cs] = acc
        return 0

    lax.fori_loop(0, n_chunks, idx_chunk, 0)

    shape3 = (n_chunks, tq, ck)
    col3 = lax.broadcasted_iota(I32, shape3, 0) * ck + lax.broadcasted_iota(I32, shape3, 2)
    causal3 = col3 <= lax.broadcasted_iota(I32, shape3, 1) + t0

    def count(x):
        return jnp.sum(jnp.sum(x, axis=0), axis=-1, keepdims=True)

    def key_to_f32(key):
        return pltpu.bitcast(jnp.where(key < 0, key ^ jnp.int32(0x7FFFFFFF), key), F32)

    def thr_body(it, key):
        cand = key + (jnp.int32(1) << (31 - it))
        c = count(jnp.where(sc_ref[...] >= key_to_f32(cand), 1.0, 0.0))
        return jnp.where(c >= n_sel, cand, key)

    thr_key = lax.fori_loop(0, 32, thr_body, jnp.full((tq, 1), INT_MIN, I32))
    take_all = thr_key == INT_MIN
    thr = key_to_f32(thr_key)
    score = sc_ref[...]
    need = n_sel - count(jnp.where(score > thr, 1.0, 0.0))
    surplus = jnp.where(take_all, 0.0, count(jnp.where(score == thr, 1.0, 0.0)) - need)
    has_surplus = jnp.max(surplus) > 0.0

    n_bits = kl.bit_length()

    def tie_body(it, jm):
        cand = jm + (jnp.int32(1) << (n_bits - 1 - it))
        f = count(jnp.where(sc_ref[...] == thr, jnp.where(col3 < cand, 1.0, 0.0), 0.0))
        return jnp.where(jnp.logical_and(f < need, cand <= kl), cand, jm)

    jm = lax.fori_loop(0, jnp.where(has_surplus, n_bits, 0), tie_body,
                       jnp.broadcast_to(jnp.where(has_surplus, 0, kl), (tq, 1)).astype(I32))
    keep = jnp.where(jnp.logical_or(score > thr, take_all), 1.0,
                     jnp.where(score == thr, jnp.where(col3 <= jm, 1.0, 0.0), 0.0))
    sc_ref[...] = jnp.where(causal3, jnp.where(keep > 0.0, 0.0, NEG), NEG)

    def raw_scores(h0):
        def f(a, j):
            return lax.dot_general(qr_ref[h0 + a], kr_ref[j * ck:(j + 1) * ck, :], NT_DIMS,
                                   preferred_element_type=F32)
        return f

    def values(a, j):
        return v_ref[j * ck:(j + 1) * ck, :]

    def to_logits(a, j, r0, c, x):
        return x + sc_ref[j, r0:r0 + strip, c * LANES:(c + 1) * LANES]

    def group_body(pp, _):
        h0 = HEADS_PER_STEP * pp
        _flash_init(flash)
        for j in range(n_chunks):
            _flash_chunk(HEADS_PER_STEP, j, raw_scores(h0), values, to_logits, flash,
                         diagonal=False)
        outs = _flash_finish(HEADS_PER_STEP, flash)
        for a in range(HEADS_PER_STEP):
            yo_ref[h0 + a] = outs[a].astype(yo_ref.dtype)
        return 0

    lax.fori_loop(0, N_HEADS // HEADS_PER_STEP, group_body, 0)
    for h in range(N_HEADS):
        o_ref[:, h * HEAD_DIM:(h + 1) * HEAD_DIM] = yo_ref[h]


def _dsa_group(u3, kprep, tabs, y, *, q_off, rows, kl, n_sel, q_blk, v_blk, qi_blk, misc_blk):
    b, s, _ = u3.shape
    tq = rows
    assert kl % min(512, kl) == 0 and tq == min(512, kl)
    qb0 = q_off // tq
    c128, s128, c64, s64 = tabs
    kr, klo, khi = kprep
    qw = N_HEADS * HEAD_DIM
    iw = IDX_HEADS * IDX_DIM
    ck = min(512, kl)

    def qspec(width, blk):
        return pl.BlockSpec((None, tq, width), lambda bi, i: (bi, qb0 + i, blk))

    def kspec(blk):
        return pl.BlockSpec((None, kl, LANES), lambda bi, i: (bi, 0, blk))

    return pl.pallas_call(
        functools.partial(_dsa_kernel, tq=tq, kl=kl, q_off=q_off, n_sel=n_sel),
        grid=(b, rows // tq),
        in_specs=[qspec(qw, q_blk), qspec(iw, qi_blk), qspec(LANES, misc_blk),
                  kspec(v_blk), kspec(0), kspec(0), kspec(0),
                  qspec(LANES, 0), qspec(LANES, 0), qspec(LANES, 0), qspec(LANES, 0),
                  pl.BlockSpec(memory_space=pl.ANY)],
        out_specs=pl.BlockSpec((None, tq, qw), lambda bi, i: (bi, qb0 + i, 0)),
        out_shape=jax.ShapeDtypeStruct((b, s, qw), BF16),
        input_output_aliases={11: 0},
        scratch_shapes=[pltpu.VMEM((N_HEADS, tq, HEAD_DIM), BF16),
                        pltpu.VMEM((IDX_HEADS // 2, tq, LANES), BF16),
                        pltpu.VMEM((tq, LANES), F32),
                        pltpu.VMEM((kl // ck, tq, ck), F32),
                        pltpu.VMEM((4, tq, ck), F32),
                        pltpu.VMEM((N_HEADS, tq, HEAD_DIM), BF16)]
        + _flash_scratch(HEADS_PER_STEP, tq),
        compiler_params=_cparams(2),
        name="dsa_attn",
    )(u3, u3, u3, u3, kr, klo, khi, c128, s128, c64, s64, y)


def _dsa_attn(u3, tabs, *, k_blk, misc_blk, **blks):
    b, s, _ = u3.shape
    n_sel = min(TOPK_MAX, s // 4)
    rows = min(512, s)
    kprep = _dsa_kprep(u3, tabs, k_blk=k_blk, misc_blk=misc_blk)
    y = jnp.zeros((b, s, N_HEADS * HEAD_DIM), BF16)
    for q_off in range(0, s, rows):
        y = _dsa_group(u3, kprep, tabs, y, q_off=q_off, rows=rows, kl=q_off + rows, n_sel=n_sel,
                       misc_blk=misc_blk, **blks)
    return y


def _dil_kernel(q0_ref, k0_ref, v0_ref, q1_ref, k1_ref, v1_ref, q2_ref, k2_ref, v2_ref,
                cos_ref, sin_ref, y_ref, qr_ref, kr_ref, o_ref, lse_ref, t_ref, p_ref, *, seq):
    qk_scale = HEAD_DIM ** -0.5 * LOG2E
    cos_f, sin_s = cos_ref[...], sin_ref[...]
    groups = ((q0_ref, k0_ref, v0_ref), (q1_ref, k1_ref, v1_ref), (q2_ref, k2_ref, v2_ref))
    for g, (q_ref, k_ref, _) in enumerate(groups):
        qr_ref[g] = _rope128(q_ref[...].astype(F32), cos_f, sin_s) * qk_scale
        kr_ref[g] = _rope128(k_ref[...].astype(F32), cos_f, sin_s)

    for g, (window, dil) in enumerate(DILATED_PAIRS):
        v_ref = groups[g][2]
        sub = seq // dil
        qb = min(window // dil, sub)
        tiles = [(r, i) for r in range(dil) for i in range(sub // qb)]

        def rows(r, blk, dil=dil, qb=qb):
            start = r + dil * qb * blk
            return pl.ds(start, qb) if dil == 1 else pl.ds(start, qb, stride=dil)

        for n, (r, i) in enumerate(tiles):
            q = qr_ref[g, rows(r, i), :].astype(BF16)
            k_cur = kr_ref[g, rows(r, i), :].astype(BF16)
            t_ref[n, :qb, qb:2 * qb] = lax.dot_general(q, k_cur, NT_DIMS,
                                                       preferred_element_type=F32)
            if i > 0:
                k_prev = kr_ref[g, rows(r, i - 1), :].astype(BF16)
                t_ref[n, :qb, :qb] = lax.dot_general(q, k_prev, NT_DIMS,
                                                     preferred_element_type=F32)
        row = lax.broadcasted_iota(I32, (qb, qb), 0)
        col = lax.broadcasted_iota(I32, (qb, qb), 1)
        for n, (r, i) in enumerate(tiles):
            t_cur = jnp.where(col <= row, t_ref[n, :qb, qb:2 * qb], NEG)
            m = jnp.max(t_cur, axis=-1, keepdims=True)
            if i > 0:
                t_prev = jnp.where(col >= row, t_ref[n, :qb, :qb], NEG)
                m = jnp.maximum(m, jnp.max(t_prev, axis=-1, keepdims=True))
            p_cur = jnp.exp2(t_cur - m)
            l = jnp.sum(p_cur, axis=-1, keepdims=True)
            p_ref[n, :qb, qb:2 * qb] = p_cur.astype(BF16)
            if i > 0:
                p_prev = jnp.exp2(t_prev - m)
                l = l + jnp.sum(p_prev, axis=-1, keepdims=True)
                p_ref[n, :qb, :qb] = p_prev.astype(BF16)
            t_ref[n, :qb, :LANES] = jnp.broadcast_to(l, (qb, LANES))
            t_ref[n, :qb, LANES:2 * LANES] = jnp.broadcast_to(m, (qb, LANES))
        for n, (r, i) in enumerate(tiles):
            acc = jnp.dot(p_ref[n, :qb, qb:2 * qb], v_ref[rows(r, i), :].astype(BF16),
                          preferred_element_type=F32)
            if i > 0:
                acc = acc + jnp.dot(p_ref[n, :qb, :qb], v_ref[rows(r, i - 1), :].astype(BF16),
                                    preferred_element_type=F32)
            l = t_ref[n, :qb, :LANES]
            o_ref[g, rows(r, i), :] = acc / l
            lse_ref[g, rows(r, i), :] = t_ref[n, :qb, LANES:2 * LANES] + jnp.log2(l)

    l0, l1, l2 = lse_ref[0], lse_ref[1], lse_ref[2]
    m = jnp.maximum(jnp.maximum(l0, l1), l2)
    e0, e1, e2 = jnp.exp2(l0 - m), jnp.exp2(l1 - m), jnp.exp2(l2 - m)
    y = (e0 * o_ref[0] + e1 * o_ref[1] + e2 * o_ref[2]) / (e0 + e1 + e2)
    y_ref[...] = y.astype(y_ref.dtype)


def _dil_attn(ua3, ub3, tabs128, *, q0_blk, k0_blk, v0_blk):
    b, s, _ = ua3.shape
    for window, dil in DILATED_PAIRS:
        assert s % dil == 0 and (s // dil) % min(window // dil, s // dil) == 0
    cos_t, sin_t = tabs128
    tile = min(DILATED_PAIRS[0][0], s)
    n_tiles = s // min(min(w // d, s // d) for w, d in DILATED_PAIRS)

    def spec(blk0):
        return pl.BlockSpec((None, s, HEAD_DIM), lambda bi, a: (bi, 0, blk0 + a))

    tab = pl.BlockSpec((None, s, LANES), lambda bi, a: (bi, 0, 0))
    return pl.pallas_call(
        functools.partial(_dil_kernel, seq=s),
        grid=(b, DIL_HEADS),
        in_specs=[spec(q0_blk), spec(k0_blk), spec(v0_blk)]
        + [spec(j * DIL_HEADS) for j in range(6)] + [tab, tab],
        out_specs=pl.BlockSpec((None, s, HEAD_DIM), lambda bi, a: (bi, 0, a)),
        out_shape=jax.ShapeDtypeStruct((b, s, DIL_WIDTH), BF16),
        scratch_shapes=[pltpu.VMEM((3, s, HEAD_DIM), F32)] * 4
        + [pltpu.VMEM((n_tiles, tile, 2 * LANES), F32), pltpu.VMEM((n_tiles, tile, 2 * LANES), BF16)],
        compiler_params=_cparams(2),
        name="dilated_attn",
    )(ua3, ua3, ua3, ub3, ub3, ub3, ub3, ub3, ub3, cos_t, sin_t)


def _mla_kernel(qn_ref, qr_ref, kn_ref, kr_ref, v_ref, cq_ref, sq_ref, ck_ref, sk_ref,
                o_ref, qcat_ref, kcat_ref, *scratch, tq):
    i = pl.program_id(2)
    qk_scale = (QK_NOPE + QK_ROPE) ** -0.5 * LOG2E
    cat = QK_NOPE + QK_ROPE

    @pl.when(i == 0)
    def _():
        k_rope = _rope64(kr_ref[...].astype(F32), ck_ref[...],
                         sk_ref[...])[:, :QK_ROPE].astype(BF16)
        for a in range(HEADS_PER_STEP):
            kcat_ref[a, :, :QK_NOPE] = kn_ref[:, a * HEAD_DIM:(a + 1) * HEAD_DIM]
            kcat_ref[a, :, QK_NOPE:cat] = k_rope
            kcat_ref[a, :, cat:] = jnp.zeros((kcat_ref.shape[1], 2 * LANES - cat), BF16)

    cos_q, sin_q = cq_ref[...], sq_ref[...]
    for a2 in range(HEADS_PER_STEP // 2):
        q_rope = (_rope64(qr_ref[:, a2 * LANES:(a2 + 1) * LANES].astype(F32), cos_q, sin_q)
                  * qk_scale).astype(BF16)
        for a in (2 * a2, 2 * a2 + 1):
            q_nope = qn_ref[:, a * HEAD_DIM:(a + 1) * HEAD_DIM].astype(F32) * qk_scale
            qcat_ref[a, :, :QK_NOPE] = q_nope.astype(BF16)
            qcat_ref[a, :, QK_NOPE:cat] = q_rope[:, (a % 2) * QK_ROPE:(a % 2 + 1) * QK_ROPE]
            qcat_ref[a, :, cat:] = jnp.zeros((tq, 2 * LANES - cat), BF16)

    def raw_scores(a, j):
        off = pl.multiple_of(j * tq, tq)
        return lax.dot_general(qcat_ref[a], kcat_ref[a, pl.ds(off, tq), :], NT_DIMS,
                               preferred_element_type=F32)

    def to_logits(a, j, r0, c, x):
        return x

    def values(a, j):
        off = pl.multiple_of(j * tq, tq)
        return v_ref[pl.ds(off, tq), a * HEAD_DIM:(a + 1) * HEAD_DIM]

    outs = _causal_flash(i, HEADS_PER_STEP, raw_scores, values, to_logits, scratch)
    for a in range(HEADS_PER_STEP):
        o_ref[:, a * HEAD_DIM:(a + 1) * HEAD_DIM] = outs[a].astype(o_ref.dtype)


def _mla_attn(qf3, kvf3, u3, tabs64, *, kr_blk):
    b, s, _ = qf3.shape
    tq = min(512, s)
    gw = HEADS_PER_STEP * HEAD_DIM
    rw = HEADS_PER_STEP * QK_ROPE
    n_groups = N_HEADS // HEADS_PER_STEP
    cos_t, sin_t = tabs64
    return pl.pallas_call(
        functools.partial(_mla_kernel, tq=tq),
        grid=(b, n_groups, s // tq),
        in_specs=[pl.BlockSpec((None, tq, gw), lambda bi, p, i: (bi, i, p)),
                  pl.BlockSpec((None, tq, rw),
                               lambda bi, p, i: (bi, i, N_HEADS * QK_NOPE // rw + p)),
                  pl.BlockSpec((None, s, gw), lambda bi, p, i: (bi, 0, p)),
                  pl.BlockSpec((None, s, LANES), lambda bi, p, i: (bi, 0, kr_blk)),
                  pl.BlockSpec((None, s, gw), lambda bi, p, i: (bi, 0, n_groups + p)),
                  pl.BlockSpec((None, tq, LANES), lambda bi, p, i: (bi, i, 0)),
                  pl.BlockSpec((None, tq, LANES), lambda bi, p, i: (bi, i, 0)),
                  pl.BlockSpec((None, s, LANES), lambda bi, p, i: (bi, 0, 0)),
                  pl.BlockSpec((None, s, LANES), lambda bi, p, i: (bi, 0, 0))],
        out_specs=pl.BlockSpec((None, tq, gw), lambda bi, p, i: (bi, i, p)),
        out_shape=jax.ShapeDtypeStruct((b, s, N_HEADS * HEAD_DIM), BF16),
        scratch_shapes=[pltpu.VMEM((HEADS_PER_STEP, tq, 2 * LANES), BF16),
                        pltpu.VMEM((HEADS_PER_STEP, s, 2 * LANES), BF16)]
        + _flash_scratch(HEADS_PER_STEP, tq),
        compiler_params=_cparams(3),
        name="mla_attn",
    )(qf3, qf3, kvf3, u3, kvf3, cos_t, sin_t, cos_t, sin_t)


def _pack(parts, multiple=512):
    k = next(p.shape[0] for p, _ in parts if p is not None)
    cols = [jnp.zeros((k, w), BF16) if p is None else p.astype(BF16) for p, w in parts]
    n = sum(w for _, w in parts)
    pad = (-n) % multiple
    if pad:
        cols.append(jnp.zeros((k, pad), BF16))
    return jnp.concatenate(cols, axis=1)


def _cols(w, sizes):
    out, acc = [], 0
    for sz in sizes:
        out.append(w[:, acc:acc + sz])
        acc += sz
    return out


def kernel(x, mem, positions, l0_norm, l0_w_in, l0_forget_bias, l0_mem_norm, l0_w_mem_kv, l0_w_out, l1_norm, l1_w_in, l1_mem_norm, l1_w_mem_kv, l1_w_out, l2_norm, l2_w_in, l2_mem_norm, l2_w_mem_kv, l2_w_out, l3_norm, l3_w_in, l3_q_norm, l3_w_uq, l3_kv_norm, l3_w_ukv, l3_mem_norm, l3_w_mem_kv, l3_w_out, final_norm):
    b, s, d = x.shape
    n_mem = mem.shape[1]
    m = b * s
    mw = N_HEADS * HEAD_DIM
    x2 = x.reshape(m, d)
    mem2 = mem.reshape(b * n_mem, d)

    pos_b = jnp.broadcast_to(positions.astype(F32).reshape(m, 1), (m, LANES))
    tabs128 = tuple(t.reshape(b, s, LANES) for t in _rope_tables(pos_b, HEAD_DIM))
    tabs64 = tuple(t.reshape(b, s, LANES) for t in _rope_tables(pos_b, IDX_DIM))

    def mem_kv(g, w):
        return _norm_proj(mem2, g, w.astype(BF16))

    def finish(y, u, x_in, g_mem, w_mem_kv, w_out, *, wy, zy_blk, zm_blk, qm_blk, final_g=None):
        return _out_proj(y, u, mem_kv(g_mem, w_mem_kv), x_in, w_out.astype(BF16), wy=wy,
                         zy_blk=zy_blk, zm_blk=zm_blk, qm_blk=qm_blk, seq=s, n_mem=n_mem,
                         final_g=final_g)

    q_w, k_w, v_w, f_w, qm_w, z_w = _cols(l0_w_in, (mw, mw, mw, N_HEADS, MEM_WIDTH, mw + MEM_WIDTH))
    w0 = _pack([(z_w[:, :mw], mw), (z_w[:, mw:], MEM_WIDTH), (qm_w, MEM_WIDTH), (q_w, mw),
                (k_w, mw), (v_w, mw)])
    u, f_arr = _norm_proj(x2, l0_norm, w0, w_extra=_pack([(f_w, N_HEADS)], multiple=LANES),
                          extra_dtype=F32)
    u3 = u.reshape(b, s, -1)
    base = (mw + 2 * MEM_WIDTH) // LANES
    c_t = _fox_gate(f_arr.reshape(b, s, LANES), l0_forget_bias, 0)
    y = _fox_attn(u3, c_t, q_blk=base // HEADS_PER_STEP, k_blk=(base + N_HEADS) // HEADS_PER_STEP,
                  v_blk=(base + 2 * N_HEADS) // HEADS_PER_STEP)
    x2 = finish(y.reshape(m, mw), u, x2, l0_mem_norm, l0_w_mem_kv, l0_w_out, wy=mw, zy_blk=0,
                zm_blk=mw // MEM_WIDTH, qm_blk=mw // MEM_WIDTH + 1)

    iw = IDX_HEADS * IDX_DIM
    q_w, k_w, v_w, qi_w, ki_w, wi_w, qm_w, z_w = _cols(
        l1_w_in, (mw, HEAD_DIM, HEAD_DIM, iw, IDX_DIM, IDX_HEADS, MEM_WIDTH, mw + MEM_WIDTH))
    w1 = _pack([(z_w[:, :mw], mw), (q_w, mw), (qi_w, iw), (z_w[:, mw:], MEM_WIDTH),
                (qm_w, MEM_WIDTH), (k_w, HEAD_DIM), (v_w, HEAD_DIM), (ki_w, IDX_DIM),
                (wi_w, IDX_HEADS), (None, LANES - IDX_DIM - IDX_HEADS)])
    u = _norm_proj(x2, l1_norm, w1)
    u3 = u.reshape(b, s, -1)
    kv0 = (2 * mw + iw + 2 * MEM_WIDTH) // LANES
    y = _dsa_attn(u3, tabs128 + tabs64, q_blk=1, qi_blk=2 * mw // iw, k_blk=kv0, v_blk=kv0 + 1,
                  misc_blk=kv0 + 2)
    zoff = (2 * mw + iw) // MEM_WIDTH
    x2 = finish(y.reshape(m, mw), u, x2, l1_mem_norm, l1_w_mem_kv, l1_w_out, wy=mw, zy_blk=0,
                zm_blk=zoff, qm_blk=zoff + 1)

    gw = len(DILATED_PAIRS) * DIL_WIDTH
    q_w, k_w, v_w, qm_w, z_w = _cols(l2_w_in, (gw, gw, gw, MEM_WIDTH, DIL_WIDTH + MEM_WIDTH))

    def grp(w, g):
        return (w[:, g * DIL_WIDTH:(g + 1) * DIL_WIDTH], DIL_WIDTH)

    wa = _pack([grp(q_w, 0), (z_w[:, :DIL_WIDTH], DIL_WIDTH), grp(k_w, 0), grp(v_w, 0),
                (z_w[:, DIL_WIDTH:], MEM_WIDTH), (qm_w, MEM_WIDTH)])
    wb = _pack([grp(q_w, 1), grp(k_w, 1), grp(v_w, 1), grp(q_w, 2), grp(k_w, 2), grp(v_w, 2)])
    u = _norm_proj(x2, l2_norm, wa)
    ub = _norm_proj(x2, l2_norm, wb, out_dtype=F32)
    y = _dil_attn(u.reshape(b, s, -1), ub.reshape(b, s, -1), tabs128, q0_blk=0,
                  k0_blk=2 * DIL_HEADS, v0_blk=3 * DIL_HEADS)
    zoff = 4 * DIL_WIDTH // MEM_WIDTH
    x2 = finish(y.reshape(m, DIL_WIDTH), u, x2, l2_mem_norm, l2_w_mem_kv, l2_w_out, wy=DIL_WIDTH,
                zy_blk=1, zm_blk=zoff, qm_blk=zoff + 1)

    q_lora, kv_lora = l3_w_uq.shape[0], l3_w_ukv.shape[0]
    cq_w, ckv_w, kr_w, qm_w, z_w = _cols(l3_w_in, (q_lora, kv_lora, QK_ROPE, MEM_WIDTH,
                                                  mw + MEM_WIDTH))
    assert q_lora == MEM_WIDTH and kv_lora == MEM_WIDTH
    w3 = _pack([(z_w[:, :mw], mw), (z_w[:, mw:], MEM_WIDTH), (qm_w, MEM_WIDTH), (cq_w, q_lora),
                (ckv_w, kv_lora)])
    u, kr_arr = _norm_proj(x2, l3_norm, w3, w_extra=_pack([(kr_w, QK_ROPE)], multiple=LANES))
    cq_blk = (mw + 2 * MEM_WIDTH) // MEM_WIDTH
    uq = l3_w_uq.reshape(q_lora, N_HEADS, QK_NOPE + QK_ROPE)
    w_uq = _pack([(uq[:, :, :QK_NOPE].reshape(q_lora, -1), N_HEADS * QK_NOPE),
                  (uq[:, :, QK_NOPE:].reshape(q_lora, -1), N_HEADS * QK_ROPE)])
    ukv = l3_w_ukv.reshape(kv_lora, N_HEADS, QK_NOPE + HEAD_DIM)
    w_ukv = _pack([(ukv[:, :, :QK_NOPE].reshape(kv_lora, -1), N_HEADS * QK_NOPE),
                   (ukv[:, :, QK_NOPE:].reshape(kv_lora, -1), N_HEADS * HEAD_DIM)])
    qf = _norm_proj(u, l3_q_norm, w_uq, x_col_block=cq_blk)
    kvf = _norm_proj(u, l3_kv_norm, w_ukv, x_col_block=cq_blk + 1)
    y = _mla_attn(qf.reshape(b, s, -1), kvf.reshape(b, s, -1), kr_arr.reshape(b, s, LANES), tabs64,
                  kr_blk=0)
    x2 = finish(y.reshape(m, mw), u, x2, l3_mem_norm, l3_w_mem_kv, l3_w_out, wy=mw, zy_blk=0,
                zm_blk=mw // MEM_WIDTH, qm_blk=mw // MEM_WIDTH + 1, final_g=final_norm)
    return x2.reshape(b, s, d)
```

```python
import functools

import jax
import jax.numpy as jnp
from jax import lax
from jax.experimental import pallas as pl
from jax.experimental.pallas import tpu as pltpu

F32 = jnp.float32
BF16 = jnp.bfloat16
I32 = jnp.int32

EPS = 1e-6
ROPE_THETA = 10000.0
HEAD_DIM = 128
N_HEADS = 16
MEM_HEADS = 4
MEM_WIDTH = MEM_HEADS * HEAD_DIM
IDX_HEADS = 16
IDX_DIM = 64
TOPK_MAX = 256
DILATED_PAIRS = ((128, 1), (512, 4), (2048, 16))
DIL_HEADS = 6
DIL_WIDTH = DIL_HEADS * HEAD_DIM
QK_NOPE = 128
QK_ROPE = 64
LANES = 128
NEG = -1e30
INT_MIN = -(2 ** 31)
VMEM_LIMIT = 56 * 1024 * 1024

NT_DIMS = (((1,), (1,)), ((), ()))
LOG2E = 1.4426950408889634


def _cparams(n_axes):
    return pltpu.CompilerParams(
        dimension_semantics=("arbitrary",) * n_axes, vmem_limit_bytes=VMEM_LIMIT)


def _rope128(x, cos_f, sin_s):
    return x * cos_f + pltpu.roll(x, 64, 1) * sin_s


def _rope64(x, cos_f, sin_s):
    lane = lax.broadcasted_iota(I32, x.shape, 1)
    partner = jnp.where((lane & 32) == 0, pltpu.roll(x, 96, 1), pltpu.roll(x, 32, 1))
    return x * cos_f + partner * sin_s


def _rope_table_kernel(pos_ref, inv_ref, sgn_ref, cos_ref, sin_ref):
    ang = pos_ref[...] * inv_ref[...]
    cos_ref[...] = jnp.cos(ang)
    sin_ref[...] = jnp.sin(ang) * sgn_ref[...]


def _rope_tables(pos_b, dh):
    m = pos_b.shape[0]
    half = dh // 2
    inv = jnp.power(ROPE_THETA, -jnp.arange(half, dtype=F32) * 2.0 / dh)
    reps = LANES // half
    inv_l = jnp.tile(inv, reps).reshape(1, LANES)
    sgn = jnp.tile(jnp.concatenate([-jnp.ones((half,), F32), jnp.ones((half,), F32)]),
                   reps // 2).reshape(1, LANES)
    ts = min(512, m)
    return pl.pallas_call(
        _rope_table_kernel,
        grid=(m // ts,),
        in_specs=[pl.BlockSpec((ts, LANES), lambda i: (i, 0)),
                  pl.BlockSpec((1, LANES), lambda i: (0, 0)),
                  pl.BlockSpec((1, LANES), lambda i: (0, 0))],
        out_specs=[pl.BlockSpec((ts, LANES), lambda i: (i, 0))] * 2,
        out_shape=[jax.ShapeDtypeStruct((m, LANES), F32)] * 2,
        compiler_params=_cparams(1),
        name="rope_tables",
    )(pos_b, inv_l, sgn)


def _norm_proj_kernel(x_ref, g_ref, w_ref, *rest, has_extra):
    if has_extra:
        wx_ref, o_ref, ox_ref, h_ref = rest
    else:
        o_ref, h_ref = rest

    first = pl.program_id(1) == 0

    @pl.when(first)
    def _():
        tm = o_ref.shape[0]
        parts = 2 if tm % 32 == 0 else 1
        for part in range(parts):
            rows = pl.ds(part * (tm // parts), tm // parts)
            x = x_ref[rows, :].astype(F32)
            ms = jnp.mean(x * x, axis=-1, keepdims=True)
            h_ref[rows, :] = (x * lax.rsqrt(ms + EPS) * g_ref[...]).astype(BF16)
            o_ref[rows, :] = jnp.dot(h_ref[rows, :], w_ref[...],
                                     preferred_element_type=F32).astype(o_ref.dtype)
            if has_extra:
                ox_ref[rows, :] = jnp.dot(h_ref[rows, :], wx_ref[...],
                                          preferred_element_type=F32).astype(ox_ref.dtype)

    @pl.when(jnp.logical_not(first))
    def _():
        o_ref[...] = jnp.dot(h_ref[...], w_ref[...],
                             preferred_element_type=F32).astype(o_ref.dtype)


def _norm_proj(x, g, w, *, x_col_block=0, out_dtype=BF16, w_extra=None, extra_dtype=BF16):
    m = x.shape[0]
    k, n = w.shape
    tm = min(1024, m)
    tn = 1024 if n % 1024 == 0 else 512
    assert m % tm == 0 and n % tn == 0
    has_extra = w_extra is not None
    in_specs = [pl.BlockSpec((tm, k), lambda i, j: (i, x_col_block)),
                pl.BlockSpec((1, k), lambda i, j: (0, 0)),
                pl.BlockSpec((k, tn), lambda i, j: (0, j))]
    out_specs = [pl.BlockSpec((tm, tn), lambda i, j: (i, j))]
    out_shape = [jax.ShapeDtypeStruct((m, n), out_dtype)]
    args = [x, g.reshape(1, k).astype(F32), w]
    if has_extra:
        in_specs.append(pl.BlockSpec((k, LANES), lambda i, j: (0, 0)))
        out_specs.append(pl.BlockSpec((tm, LANES), lambda i, j: (i, 0)))
        out_shape.append(jax.ShapeDtypeStruct((m, LANES), extra_dtype))
        args.append(w_extra)
    outs = pl.pallas_call(
        functools.partial(_norm_proj_kernel, has_extra=has_extra),
        grid=(m // tm, n // tn),
        in_specs=in_specs,
        out_specs=out_specs,
        out_shape=out_shape,
        scratch_shapes=[pltpu.VMEM((tm, k), BF16)],
        compiler_params=_cparams(2),
        name="norm_proj",
    )(*args)
    return outs if has_extra else outs[0]


def _out_proj_kernel(y_ref, zy_ref, zm_ref, qm_ref, mkv_ref, x_ref, w_ref, *rest, wy, final):
    if final:
        gf_ref, o_ref, gated_ref = rest
    else:
        o_ref, gated_ref = rest
    scale = HEAD_DIM ** -0.5
    for h in range(MEM_HEADS):
        lo, hi = h * HEAD_DIM, (h + 1) * HEAD_DIM
        s = lax.dot_general(qm_ref[:, lo:hi], mkv_ref[:, lo:hi], NT_DIMS,
                            preferred_element_type=F32) * scale
        m = jnp.max(s, axis=-1, keepdims=True)
        p = jnp.exp(s - m)
        l = jnp.sum(p, axis=-1, keepdims=True)
        o = jnp.dot(p.astype(BF16), mkv_ref[:, MEM_WIDTH + lo:MEM_WIDTH + hi],
                    preferred_element_type=F32) / l
        zm = zm_ref[:, lo:hi].astype(F32)
        gated_ref[:, wy + lo:wy + hi] = (o * (zm * jax.nn.sigmoid(zm))).astype(BF16)
    tm = o_ref.shape[0]
    parts = 2 if tm % 32 == 0 else 1
    for part in range(parts):
        rows = pl.ds(part * (tm // parts), tm // parts)
        zy = zy_ref[rows, :].astype(F32)
        gated_ref[rows, :wy] = (y_ref[rows, :].astype(F32)
                                * (zy * jax.nn.sigmoid(zy))).astype(BF16)
        out = x_ref[rows, :] + jnp.dot(gated_ref[rows, :], w_ref[...],
                                       preferred_element_type=F32)
        if final:
            ms = jnp.mean(out * out, axis=-1, keepdims=True)
            out = out * lax.rsqrt(ms + EPS) * gf_ref[...]
        o_ref[rows, :] = out


def _out_proj(y, u, mkv, x, w_out, *, wy, zy_blk, zm_blk, qm_blk, seq, n_mem, final_g=None):
    m, d = x.shape
    tm = min(512, seq)
    final = final_g is not None
    in_specs = [
        pl.BlockSpec((tm, wy), lambda i: (i, 0)),
        pl.BlockSpec((tm, wy), lambda i: (i, zy_blk)),
        pl.BlockSpec((tm, MEM_WIDTH), lambda i: (i, zm_blk)),
        pl.BlockSpec((tm, MEM_WIDTH), lambda i: (i, qm_blk)),
        pl.BlockSpec((n_mem, 2 * MEM_WIDTH), lambda i: ((i * tm) // seq, 0)),
        pl.BlockSpec((tm, d), lambda i: (i, 0)),
        pl.BlockSpec((wy + MEM_WIDTH, d), lambda i: (0, 0), pipeline_mode=pl.Buffered(1)),
    ]
    args = [y, u, u, u, mkv, x, w_out]
    if final:
        in_specs.append(pl.BlockSpec((1, d), lambda i: (0, 0)))
        args.append(final_g.reshape(1, d).astype(F32))
    return pl.pallas_call(
        functools.partial(_out_proj_kernel, wy=wy, final=final),
        grid=(m // tm,),
        in_specs=in_specs,
        out_specs=pl.BlockSpec((tm, d), lambda i: (i, 0)),
        out_shape=jax.ShapeDtypeStruct((m, d), F32),
        scratch_shapes=[pltpu.VMEM((tm, wy + MEM_WIDTH), BF16)],
        compiler_params=_cparams(1),
        name="out_proj",
    )(*args)


def _fox_gate_kernel(f_ref, b_ref, c_ref, carry_ref):
    @pl.when(pl.program_id(1) == 0)
    def _():
        carry_ref[...] = jnp.zeros_like(carry_ref)

    f_t = f_ref[...].astype(F32).T[:N_HEADS, :] + b_ref[...]
    log_f = jnp.minimum(f_t, 0.0) - jnp.log(1.0 + jnp.exp(-jnp.abs(f_t)))
    ts = log_f.shape[1]
    r = lax.broadcasted_iota(I32, (ts, ts), 0)
    c = lax.broadcasted_iota(I32, (ts, ts), 1)
    upper = jnp.where(r <= c, 1.0, 0.0).astype(F32)
    cs = jnp.dot(log_f, upper, precision=lax.Precision.HIGHEST,
                 preferred_element_type=F32) + carry_ref[...]
    c_ref[...] = cs * LOG2E
    carry_ref[...] = cs[:, ts - 1:ts]


def _fox_gate(u3, bias, f_blk):
    b, s, _ = u3.shape
    ts = min(256, s)
    return pl.pallas_call(
        _fox_gate_kernel,
        grid=(b, s // ts),
        in_specs=[pl.BlockSpec((None, ts, LANES), lambda bi, j: (bi, j, f_blk)),
                  pl.BlockSpec((N_HEADS, 1), lambda bi, j: (0, 0))],
        out_specs=pl.BlockSpec((None, N_HEADS, ts), lambda bi, j: (bi, 0, j)),
        out_shape=jax.ShapeDtypeStruct((b, N_HEADS, s), F32),
        scratch_shapes=[pltpu.VMEM((N_HEADS, 1), F32)],
        compiler_params=_cparams(2),
        name="fox_gate",
    )(u3, bias.reshape(N_HEADS, 1).astype(F32))


STRIP = 32
HEADS_PER_STEP = 4


def _softmax_strips(t_ref, p_ref, m_ref, l_ref, alpha_ref, adjust):
    tq, tk = t_ref.shape
    for r0 in range(0, tq, min(STRIP, tq)):
        rs = pl.ds(r0, min(STRIP, tq))
        blocks = [adjust(r0, c, t_ref[rs, c * LANES:(c + 1) * LANES]) for c in range(tk // LANES)]
        live = [x for x in blocks if x is not None]
        mx = live[0]
        for x in live[1:]:
            mx = jnp.maximum(mx, x)
        m_old = m_ref[rs, :]
        m_new = jnp.maximum(m_old, jnp.max(mx, axis=-1, keepdims=True))
        alpha = jnp.exp2(m_old - m_new)
        ps = [None if x is None else jnp.exp2(x - m_new) for x in blocks]
        live = [x for x in ps if x is not None]
        sm = live[0]
        for x in live[1:]:
            sm = sm + x
        l_ref[rs, :] = alpha * l_ref[rs, :] + jnp.sum(sm, axis=-1, keepdims=True)
        m_ref[rs, :] = m_new
        alpha_ref[rs, :] = alpha
        for c, x in enumerate(ps):
            p_ref[rs, c * LANES:(c + 1) * LANES] = (
                jnp.zeros((min(STRIP, tq), LANES), BF16) if x is None else x.astype(BF16))


def _flash_scratch(n_streams, tq):
    return [pltpu.VMEM((n_streams, tq, tq), F32), pltpu.VMEM((n_streams, tq, tq), BF16),
            pltpu.VMEM((n_streams, tq, LANES), F32), pltpu.VMEM((n_streams, tq, LANES), F32),
            pltpu.VMEM((n_streams, tq, LANES), F32), pltpu.VMEM((n_streams, tq, HEAD_DIM), F32)]


def _flash_init(scratch):
    _, _, m_ref, l_ref, _, acc_ref = scratch
    m_ref[...] = jnp.full(m_ref.shape, NEG, F32)
    l_ref[...] = jnp.zeros(l_ref.shape, F32)
    acc_ref[...] = jnp.zeros(acc_ref.shape, F32)


def _flash_chunk(n_streams, j, raw_scores, values, to_logits, scratch, *, diagonal):
    t_ref, p_ref, m_ref, l_ref, alpha_ref, acc_ref = scratch
    strip = min(STRIP, t_ref.shape[1])
    row = lax.broadcasted_iota(I32, (strip, LANES), 0)
    col = lax.broadcasted_iota(I32, (strip, LANES), 1)
    for a in range(n_streams):
        t_ref[a] = raw_scores(a, j)
    for a in range(n_streams):
        def adjust(r0, c, x, a=a):
            t = to_logits(a, j, r0, c, x)
            if not diagonal or c * LANES + LANES - 1 <= r0:
                return t
            if c * LANES > r0 + strip - 1:
                return None
            return jnp.where(col + c * LANES <= row + r0, t, NEG)

        _softmax_strips(t_ref.at[a], p_ref.at[a], m_ref.at[a], l_ref.at[a], alpha_ref.at[a],
                        adjust)
    for a in range(n_streams):
        acc_ref[a] = alpha_ref[a] * acc_ref[a] + jnp.dot(p_ref[a], values(a, j),
                                                         preferred_element_type=F32)


def _flash_finish(n_streams, scratch):
    _, _, _, l_ref, _, acc_ref = scratch
    return [acc_ref[a] / l_ref[a] for a in range(n_streams)]


def _causal_flash(i, n_streams, raw_scores, values, to_logits, scratch):
    _flash_init(scratch)

    def body(j, _):
        _flash_chunk(n_streams, j, raw_scores, values, to_logits, scratch, diagonal=False)
        return 0

    lax.fori_loop(0, i, body, 0)
    _flash_chunk(n_streams, i, raw_scores, values, to_logits, scratch, diagonal=True)
    return _flash_finish(n_streams, scratch)


def _fox_attn_kernel(q_ref, k_ref, v_ref, c_ref, o_ref, qs_ref, *scratch, tq):
    i = pl.program_id(2)
    qk_scale = HEAD_DIM ** -0.5 * LOG2E
    qs_ref[...] = (q_ref[...].astype(F32) * qk_scale).astype(BF16)

    def raw_scores(a, j):
        off = pl.multiple_of(j * tq, tq)
        lo, hi = a * HEAD_DIM, (a + 1) * HEAD_DIM
        return lax.dot_general(qs_ref[:, lo:hi], k_ref[pl.ds(off, tq), lo:hi], NT_DIMS,
                               preferred_element_type=F32)

    def to_logits(a, j, r0, c, x):
        return x - c_ref[a, j, :, c * LANES:(c + 1) * LANES]

    def values(a, j):
        off = pl.multiple_of(j * tq, tq)
        return v_ref[pl.ds(off, tq), a * HEAD_DIM:(a + 1) * HEAD_DIM]

    outs = _causal_flash(i, HEADS_PER_STEP, raw_scores, values, to_logits, scratch)
    for a in range(HEADS_PER_STEP):
        o_ref[:, a * HEAD_DIM:(a + 1) * HEAD_DIM] = outs[a].astype(o_ref.dtype)


def _fox_attn(u3, c_t, *, q_blk, k_blk, v_blk):
    b, s, _ = u3.shape
    tq = min(512, s)
    nq = s // tq
    pw = HEADS_PER_STEP * HEAD_DIM
    c5 = c_t.reshape(b, N_HEADS, nq, 1, tq)
    return pl.pallas_call(
        functools.partial(_fox_attn_kernel, tq=tq),
        grid=(b, N_HEADS // HEADS_PER_STEP, nq),
        in_specs=[pl.BlockSpec((None, tq, pw), lambda bi, p, i: (bi, i, q_blk + p)),
                  pl.BlockSpec((None, s, pw), lambda bi, p, i: (bi, 0, k_blk + p)),
                  pl.BlockSpec((None, s, pw), lambda bi, p, i: (bi, 0, v_blk + p)),
                  pl.BlockSpec((None, HEADS_PER_STEP, nq, 1, tq), lambda bi, p, i: (bi, p, 0, 0, 0))],
        out_specs=pl.BlockSpec((None, tq, pw), lambda bi, p, i: (bi, i, p)),
        out_shape=jax.ShapeDtypeStruct((b, s, N_HEADS * HEAD_DIM), BF16),
        scratch_shapes=[pltpu.VMEM((tq, pw), BF16)] + _flash_scratch(HEADS_PER_STEP, tq),
        compiler_params=_cparams(3),
        name="fox_attn",
    )(u3, u3, u3, c5)


def _dsa_kprep_kernel(k_ref, mk_ref, c128_ref, s128_ref, c64_ref, s64_ref,
                      kr_ref, klo_ref, khi_ref):
    kr_ref[...] = _rope128(k_ref[...].astype(F32), c128_ref[...], s128_ref[...]).astype(BF16)
    ki = _rope64(mk_ref[...].astype(F32), c64_ref[...], s64_ref[...])
    lane = lax.broadcasted_iota(I32, ki.shape, 1)
    lo = jnp.where(lane < IDX_DIM, ki, 0.0)
    klo_ref[...] = lo.astype(BF16)
    khi_ref[...] = pltpu.roll(lo, IDX_DIM, 1).astype(BF16)


def _dsa_kprep(u3, tabs, *, k_blk, misc_blk):
    b, s, _ = u3.shape
    ts = min(512, s)
    c128, s128, c64, s64 = tabs

    def spec(blk):
        return pl.BlockSpec((None, ts, LANES), lambda bi, i: (bi, i, blk))

    return pl.pallas_call(
        _dsa_kprep_kernel,
        grid=(b, s // ts),
        in_specs=[spec(k_blk), spec(misc_blk), spec(0), spec(0), spec(0), spec(0)],
        out_specs=[spec(0)] * 3,
        out_shape=[jax.ShapeDtypeStruct((b, s, LANES), BF16)] * 3,
        compiler_params=_cparams(2),
        name="dsa_kprep",
    )(u3, u3, c128, s128, c64, s64)


def _dsa_kernel(q_ref, qi_ref, mq_ref, v_ref, kr_ref, klo_ref, khi_ref,
                cq128_ref, sq128_ref, cq64_ref, sq64_ref, y_in_ref,
                o_ref,
                qr_ref, qir_ref, w_ref, sc_ref, t4_ref, yo_ref, *flash,
                tq, kl, q_off, n_sel):
    del y_in_ref
    t0 = q_off + pl.program_id(1) * tq
    qk_scale = HEAD_DIM ** -0.5 * LOG2E
    idx_scale = (IDX_DIM ** -0.5) * (IDX_HEADS ** -0.5)
    ck = min(512, kl)
    n_chunks = kl // ck
    strip = min(STRIP, tq)

    cq128, sq128 = cq128_ref[...], sq128_ref[...]
    for h in range(N_HEADS):
        qh = q_ref[:, h * HEAD_DIM:(h + 1) * HEAD_DIM].astype(F32)
        qr_ref[h] = (_rope128(qh, cq128, sq128) * qk_scale).astype(BF16)
    cq64, sq64 = cq64_ref[...], sq64_ref[...]
    for a in range(IDX_HEADS // 2):
        qir_ref[a] = _rope64(qi_ref[:, a * LANES:(a + 1) * LANES].astype(F32),
                             cq64, sq64).astype(BF16)
    w_ref[...] = mq_ref[...].astype(F32) * idx_scale

    row = lax.broadcasted_iota(I32, (strip, LANES), 0)
    col = lax.broadcasted_iota(I32, (strip, LANES), 1)
    group = 4

    def idx_chunk(j, _):
        off = pl.multiple_of(j * ck, ck)
        k_lo, k_hi = klo_ref[pl.ds(off, ck), :], khi_ref[pl.ds(off, ck), :]
        for g in range(IDX_HEADS // group):
            for a2 in range(group // 2):
                x = qir_ref[g * (group // 2) + a2]
                t4_ref[2 * a2] = lax.dot_general(x, k_lo, NT_DIMS, preferred_element_type=F32)
                t4_ref[2 * a2 + 1] = lax.dot_general(x, k_hi, NT_DIMS,
                                                     preferred_element_type=F32)
            last = g == IDX_HEADS // group - 1
            for r0 in range(0, tq, strip):
                rs = pl.ds(r0, strip)
                w_rows = w_ref[rs, :]
                ws = [jnp.broadcast_to(w_rows[:, IDX_DIM + g * group + u:IDX_DIM + g * group + u + 1],
                                       (strip, LANES)) for u in range(group)]
                for c in range(ck // LANES):
                    cs = pl.ds(c * LANES, LANES)
                    acc = ws[0] * jnp.maximum(t4_ref[0, rs, cs], 0.0)
                    for u in range(1, group):
                        acc = acc + ws[u] * jnp.maximum(t4_ref[u, rs, cs], 0.0)
                    if g > 0:
                        acc = acc + sc_ref[j, rs, cs]
                    if last:
                        causal = col + (off + c * LANES) <= row + (t0 + r0)
                        acc = jnp.where(causal, acc, -jnp.inf)
                    sc_ref[j, rs, cs] = acc
        return 0

    lax.fori_loop(0, n_chunks, idx_chunk, 0)

    shape3 = (n_chunks, tq, ck)
    col3 = lax.broadcasted_iota(I32, shape3, 0) * ck + lax.broadcasted_iota(I32, shape3, 2)
    causal3 = col3 <= lax.broadcasted_iota(I32, shape3, 1) + t0

    def count(x):
        return jnp.sum(jnp.sum(x, axis=0), axis=-1, keepdims=True)

    def key_to_f32(key):
        return pltpu.bitcast(jnp.where(key < 0, key ^ jnp.int32(0x7FFFFFFF), key), F32)

    def thr_body(it, key):
        cand = key + (jnp.int32(1) << (31 - it))
        c = count(jnp.where(sc_ref[...] >= key_to_f32(cand), 1.0, 0.0))
        return jnp.where(c >= n_sel, cand, key)

    thr_key = lax.fori_loop(0, 32, thr_body, jnp.full((tq, 1), INT_MIN, I32))
    take_all = thr_key == INT_MIN
    thr = key_to_f32(thr_key)
    score = sc_ref[...]
    need = n_sel - count(jnp.where(score > thr, 1.0, 0.0))
    surplus = jnp.where(take_all, 0.0, count(jnp.where(score == thr, 1.0, 0.0)) - need)
    has_surplus = jnp.max(surplus) > 0.0

    n_bits = kl.bit_length()

    def tie_body(it, jm):
        cand = jm + (jnp.int32(1) << (n_bits - 1 - it))
        f = count(jnp.where(sc_ref[...] == thr, jnp.where(col3 < cand, 1.0, 0.0), 0.0))
        return jnp.where(jnp.logical_and(f < need, cand <= kl), cand, jm)

    jm = lax.fori_loop(0, jnp.where(has_surplus, n_bits, 0), tie_body,
                       jnp.broadcast_to(jnp.where(has_surplus, 0, kl), (tq, 1)).astype(I32))
    keep = jnp.where(jnp.logical_or(score > thr, take_all), 1.0,
                     jnp.where(score == thr, jnp.where(col3 <= jm, 1.0, 0.0), 0.0))
    sc_ref[...] = jnp.where(causal3, jnp.where(keep > 0.0, 0.0, NEG), NEG)

    def raw_scores(h0):
        def f(a, j):
            return lax.dot_general(qr_ref[h0 + a], kr_ref[j * ck:(j + 1) * ck, :], NT_DIMS,
                                   preferred_element_type=F32)
        return f

    def values(a, j):
        return v_ref[j * ck:(j + 1) * ck, :]

    def to_logits(a, j, r0, c, x):
        return x + sc_ref[j, r0:r0 + strip, c * LANES:(c + 1) * LANES]

    def group_body(pp, _):
        h0 = HEADS_PER_STEP * pp
        _flash_init(flash)
        for j in range(n_chunks):
            _flash_chunk(HEADS_PER_STEP, j, raw_scores(h0), values, to_logits, flash,
                         diagonal=False)
        outs = _flash_finish(HEADS_PER_STEP, flash)
        for a in range(HEADS_PER_STEP):
            yo_ref[h0 + a] = outs[a].astype(yo_ref.dtype)
        return 0

    lax.fori_loop(0, N_HEADS // HEADS_PER_STEP, group_body, 0)
    for h in range(N_HEADS):
        o_ref[:, h * HEAD_DIM:(h + 1) * HEAD_DIM] = yo_ref[h]


def _dsa_group(u3, kprep, tabs, y, *, q_off, rows, kl, n_sel, q_blk, v_blk, qi_blk, misc_blk):
    b, s, _ = u3.shape
    tq = rows
    assert kl % min(512, kl) == 0 and tq == min(512, kl)
    qb0 = q_off // tq
    c128, s128, c64, s64 = tabs
    kr, klo, khi = kprep
    qw = N_HEADS * HEAD_DIM
    iw = IDX_HEADS * IDX_DIM
    ck = min(512, kl)

    def qspec(width, blk):
        return pl.BlockSpec((None, tq, width), lambda bi, i: (bi, qb0 + i, blk))

    def kspec(blk):
        return pl.BlockSpec((None, kl, LANES), lambda bi, i: (bi, 0, blk))

    return pl.pallas_call(
        functools.partial(_dsa_kernel, tq=tq, kl=kl, q_off=q_off, n_sel=n_sel),
        grid=(b, rows // tq),
        in_specs=[qspec(qw, q_blk), qspec(iw, qi_blk), qspec(LANES, misc_blk),
                  kspec(v_blk), kspec(0), kspec(0), kspec(0),
                  qspec(LANES, 0), qspec(LANES, 0), qspec(LANES, 0), qspec(LANES, 0),
                  pl.BlockSpec(memory_space=pl.ANY)],
        out_specs=pl.BlockSpec((None, tq, qw), lambda bi, i: (bi, qb0 + i, 0)),
        out_shape=jax.ShapeDtypeStruct((b, s, qw), BF16),
        input_output_aliases={11: 0},
        scratch_shapes=[pltpu.VMEM((N_HEADS, tq, HEAD_DIM), BF16),
                        pltpu.VMEM((IDX_HEADS // 2, tq, LANES), BF16),
                        pltpu.VMEM((tq, LANES), F32),
                        pltpu.VMEM((kl // ck, tq, ck), F32),
                        pltpu.VMEM((4, tq, ck), F32),
                        pltpu.VMEM((N_HEADS, tq, HEAD_DIM), BF16)]
        + _flash_scratch(HEADS_PER_STEP, tq),
        compiler_params=_cparams(2),
        name="dsa_attn",
    )(u3, u3, u3, u3, kr, klo, khi, c128, s128, c64, s64, y)


def _dsa_attn(u3, tabs, *, k_blk, misc_blk, **blks):
    b, s, _ = u3.shape
    n_sel = min(TOPK_MAX, s // 4)
    rows = min(512, s)
    kprep = _dsa_kprep(u3, tabs, k_blk=k_blk, misc_blk=misc_blk)
    y = jnp.zeros((b, s, N_HEADS * HEAD_DIM), BF16)
    for q_off in range(0, s, rows):
        y = _dsa_group(u3, kprep, tabs, y, q_off=q_off, rows=rows, kl=q_off + rows, n_sel=n_sel,
                       misc_blk=misc_blk, **blks)
    return y


def _dil_kernel(q0_ref, k0_ref, v0_ref, q1_ref, k1_ref, v1_ref, q2_ref, k2_ref, v2_ref,
                cos_ref, sin_ref, y_ref, qr_ref, kr_ref, o_ref, lse_ref, t_ref, p_ref, *, seq):
    qk_scale = HEAD_DIM ** -0.5 * LOG2E
    cos_f, sin_s = cos_ref[...], sin_ref[...]
    groups = ((q0_ref, k0_ref, v0_ref), (q1_ref, k1_ref, v1_ref), (q2_ref, k2_ref, v2_ref))
    for g, (q_ref, k_ref, _) in enumerate(groups):
        qr_ref[g] = _rope128(q_ref[...].astype(F32), cos_f, sin_s) * qk_scale
        kr_ref[g] = _rope128(k_ref[...].astype(F32), cos_f, sin_s)

    for g, (window, dil) in enumerate(DILATED_PAIRS):
        v_ref = groups[g][2]
        sub = seq // dil
        qb = min(window // dil, sub)
        tiles = [(r, i) for r in range(dil) for i in range(sub // qb)]

        def rows(r, blk, dil=dil, qb=qb):
            start = r + dil * qb * blk
            return pl.ds(start, qb) if dil == 1 else pl.ds(start, qb, stride=dil)

        for n, (r, i) in enumerate(tiles):
            q = qr_ref[g, rows(r, i), :].astype(BF16)
            k_cur = kr_ref[g, rows(r, i), :].astype(BF16)
            t_ref[n, :qb, qb:2 * qb] = lax.dot_general(q, k_cur, NT_DIMS,
                                                       preferred_element_type=F32)
            if i > 0:
                k_prev = kr_ref[g, rows(r, i - 1), :].astype(BF16)
                t_ref[n, :qb, :qb] = lax.dot_general(q, k_prev, NT_DIMS,
                                                     preferred_element_type=F32)
        row = lax.broadcasted_iota(I32, (qb, qb), 0)
        col = lax.broadcasted_iota(I32, (qb, qb), 1)
        for n, (r, i) in enumerate(tiles):
            t_cur = jnp.where(col <= row, t_ref[n, :qb, qb:2 * qb], NEG)
            m = jnp.max(t_cur, axis=-1, keepdims=True)
            if i > 0:
                t_prev = jnp.where(col >= row, t_ref[n, :qb, :qb], NEG)
                m = jnp.maximum(m, jnp.max(t_prev, axis=-1, keepdims=True))
            p_cur = jnp.exp2(t_cur - m)
            l = jnp.sum(p_cur, axis=-1, keepdims=True)
            p_ref[n, :qb, qb:2 * qb] = p_cur.astype(BF16)
            if i > 0:
                p_prev = jnp.exp2(t_prev - m)
                l = l + jnp.sum(p_prev, axis=-1, keepdims=True)
                p_ref[n, :qb, :qb] = p_prev.astype(BF16)
            t_ref[n, :qb, :LANES] = jnp.broadcast_to(l, (qb, LANES))
            t_ref[n, :qb, LANES:2 * LANES] = jnp.broadcast_to(m, (qb, LANES))
        for n, (r, i) in enumerate(tiles):
            acc = jnp.dot(p_ref[n, :qb, qb:2 * qb], v_ref[rows(r, i), :].astype(BF16),
                          preferred_element_type=F32)
            if i > 0:
                acc = acc + jnp.dot(p_ref[n, :qb, :qb], v_ref[rows(r, i - 1), :].astype(BF16),
                                    preferred_element_type=F32)
            l = t_ref[n, :qb, :LANES]
            o_ref[g, rows(r, i), :] = acc / l
            lse_ref[g, rows(r, i), :] = t_ref[n, :qb, LANES:2 * LANES] + jnp.log2(l)

    l0, l1, l2 = lse_ref[0], lse_ref[1], lse_ref[2]
    m = jnp.maximum(jnp.maximum(l0, l1), l2)
    e0, e1, e2 = jnp.exp2(l0 - m), jnp.exp2(l1 - m), jnp.exp2(l2 - m)
    y = (e0 * o_ref[0] + e1 * o_ref[1] + e2 * o_ref[2]) / (e0 + e1 + e2)
    y_ref[...] = y.astype(y_ref.dtype)


def _dil_attn(ua3, ub3, tabs128, *, q0_blk, k0_blk, v0_blk):
    b, s, _ = ua3.shape
    for window, dil in DILATED_PAIRS:
        assert s % dil == 0 and (s // dil) % min(window // dil, s // dil) == 0
    cos_t, sin_t = tabs128
    tile = min(DILATED_PAIRS[0][0], s)
    n_tiles = s // min(min(w // d, s // d) for w, d in DILATED_PAIRS)

    def spec(blk0):
        return pl.BlockSpec((None, s, HEAD_DIM), lambda bi, a: (bi, 0, blk0 + a))

    tab = pl.BlockSpec((None, s, LANES), lambda bi, a: (bi, 0, 0))
    return pl.pallas_call(
        functools.partial(_dil_kernel, seq=s),
        grid=(b, DIL_HEADS),
        in_specs=[spec(q0_blk), spec(k0_blk), spec(v0_blk)]
        + [spec(j * DIL_HEADS) for j in range(6)] + [tab, tab],
        out_specs=pl.BlockSpec((None, s, HEAD_DIM), lambda bi, a: (bi, 0, a)),
        out_shape=jax.ShapeDtypeStruct((b, s, DIL_WIDTH), BF16),
        scratch_shapes=[pltpu.VMEM((3, s, HEAD_DIM), F32)] * 4
        + [pltpu.VMEM((n_tiles, tile, 2 * LANES), F32), pltpu.VMEM((n_tiles, tile, 2 * LANES), BF16)],
        compiler_params=_cparams(2),
        name="dilated_attn",
    )(ua3, ua3, ua3, ub3, ub3, ub3, ub3, ub3, ub3, cos_t, sin_t)


def _mla_kernel(qn_ref, qr_ref, kn_ref, kr_ref, v_ref, cq_ref, sq_ref, ck_ref, sk_ref,
                o_ref, qcat_ref, kcat_ref, *scratch, tq):
    i = pl.program_id(2)
    qk_scale = (QK_NOPE + QK_ROPE) ** -0.5 * LOG2E
    cat = QK_NOPE + QK_ROPE

    @pl.when(i == 0)
    def _():
        k_rope = _rope64(kr_ref[...].astype(F32), ck_ref[...],
                         sk_ref[...])[:, :QK_ROPE].astype(BF16)
        for a in range(HEADS_PER_STEP):
            kcat_ref[a, :, :QK_NOPE] = kn_ref[:, a * HEAD_DIM:(a + 1) * HEAD_DIM]
            kcat_ref[a, :, QK_NOPE:cat] = k_rope
            kcat_ref[a, :, cat:] = jnp.zeros((kcat_ref.shape[1], 2 * LANES - cat), BF16)

    cos_q, sin_q = cq_ref[...], sq_ref[...]
    for a2 in range(HEADS_PER_STEP // 2):
        q_rope = (_rope64(qr_ref[:, a2 * LANES:(a2 + 1) * LANES].astype(F32), cos_q, sin_q)
                  * qk_scale).astype(BF16)
        for a in (2 * a2, 2 * a2 + 1):
            q_nope = qn_ref[:, a * HEAD_DIM:(a + 1) * HEAD_DIM].astype(F32) * qk_scale
            qcat_ref[a, :, :QK_NOPE] = q_nope.astype(BF16)
            qcat_ref[a, :, QK_NOPE:cat] = q_rope[:, (a % 2) * QK_ROPE:(a % 2 + 1) * QK_ROPE]
            qcat_ref[a, :, cat:] = jnp.zeros((tq, 2 * LANES - cat), BF16)

    def raw_scores(a, j):
        off = pl.multiple_of(j * tq, tq)
        return lax.dot_general(qcat_ref[a], kcat_ref[a, pl.ds(off, tq), :], NT_DIMS,
                               preferred_element_type=F32)

    def to_logits(a, j, r0, c, x):
        return x

    def values(a, j):
        off = pl.multiple_of(j * tq, tq)
        return v_ref[pl.ds(off, tq), a * HEAD_DIM:(a + 1) * HEAD_DIM]

    outs = _causal_flash(i, HEADS_PER_STEP, raw_scores, values, to_logits, scratch)
    for a in range(HEADS_PER_STEP):
        o_ref[:, a * HEAD_DIM:(a + 1) * HEAD_DIM] = outs[a].astype(o_ref.dtype)


def _mla_attn(qf3, kvf3, u3, tabs64, *, kr_blk):
    b, s, _ = qf3.shape
    tq = min(512, s)
    gw = HEADS_PER_STEP * HEAD_DIM
    rw = HEADS_PER_STEP * QK_ROPE
    n_groups = N_HEADS // HEADS_PER_STEP
    cos_t, sin_t = tabs64
    return pl.pallas_call(
        functools.partial(_mla_kernel, tq=tq),
        grid=(b, n_groups, s // tq),
        in_specs=[pl.BlockSpec((None, tq, gw), lambda bi, p, i: (bi, i, p)),
                  pl.BlockSpec((None, tq, rw),
                               lambda bi, p, i: (bi, i, N_HEADS * QK_NOPE // rw + p)),
                  pl.BlockSpec((None, s, gw), lambda bi, p, i: (bi, 0, p)),
                  pl.BlockSpec((None, s, LANES), lambda bi, p, i: (bi, 0, kr_blk)),
                  pl.BlockSpec((None, s, gw), lambda bi, p, i: (bi, 0, n_groups + p)),
                  pl.BlockSpec((None, tq, LANES), lambda bi, p, i: (bi, i, 0)),
                  pl.BlockSpec((None, tq, LANES), lambda bi, p, i: (bi, i, 0)),
                  pl.BlockSpec((None, s, LANES), lambda bi, p, i: (bi, 0, 0)),
                  pl.BlockSpec((None, s, LANES), lambda bi, p, i: (bi, 0, 0))],
        out_specs=pl.BlockSpec((None, tq, gw), lambda bi, p, i: (bi, i, p)),
        out_shape=jax.ShapeDtypeStruct((b, s, N_HEADS * HEAD_DIM), BF16),
        scratch_shapes=[pltpu.VMEM((HEADS_PER_STEP, tq, 2 * LANES), BF16),
                        pltpu.VMEM((HEADS_PER_STEP, s, 2 * LANES), BF16)]
        + _flash_scratch(HEADS_PER_STEP, tq),
        compiler_params=_cparams(3),
        name="mla_attn",
    )(qf3, qf3, kvf3, u3, kvf3, cos_t, sin_t, cos_t, sin_t)


def _shift_cols_kernel(w_ref, o_ref, *, src_off, width):
    n = w_ref.shape[1]
    a, r = divmod(src_off, LANES)
    rows = o_ref.shape[0]
    lane = lax.broadcasted_iota(I32, (rows, LANES), 1)
    for c in range(width // LANES):
        lo = w_ref[:, (a + c) * LANES:(a + c + 1) * LANES]
        start = (a + c + 1) * LANES
        have = min(LANES, n - start)
        hi = w_ref[:, start:start + have]
        if have < LANES:
            hi = jnp.concatenate([hi, jnp.zeros((rows, LANES - have), hi.dtype)], axis=1)
        blk = jnp.where(lane < LANES - r, pltpu.roll(lo, LANES - r, 1),
                        pltpu.roll(hi, LANES - r, 1))
        o_ref[:, c * LANES:(c + 1) * LANES] = blk.astype(o_ref.dtype)


def _shift_cols(w, src_off, width):
    k, n = w.shape
    assert src_off % LANES and width % LANES == 0
    assert n - (src_off // LANES + width // LANES) * LANES >= src_off % LANES
    tr = min(256, k)
    return pl.pallas_call(
        functools.partial(_shift_cols_kernel, src_off=src_off, width=width),
        grid=(k // tr,),
        in_specs=[pl.BlockSpec((tr, n), lambda i: (i, 0))],
        out_specs=pl.BlockSpec((tr, width), lambda i: (i, 0)),
        out_shape=jax.ShapeDtypeStruct((k, width), BF16),
        compiler_params=_cparams(1),
        name="shift_cols",
    )(w)


def _pack(parts, multiple=512):
    k = next(p.shape[0] for p, _ in parts if p is not None)
    cols = [jnp.zeros((k, w), BF16) if p is None else p.astype(BF16) for p, w in parts]
    n = sum(w for _, w in parts)
    pad = (-n) % multiple
    if pad:
        cols.append(jnp.zeros((k, pad), BF16))
    return jnp.concatenate(cols, axis=1)


def _cols(w, sizes):
    out, acc = [], 0
    for sz in sizes:
        out.append(w[:, acc:acc + sz])
        acc += sz
    return out


def kernel(x, mem, positions, l0_norm, l0_w_in, l0_forget_bias, l0_mem_norm, l0_w_mem_kv, l0_w_out, l1_norm, l1_w_in, l1_mem_norm, l1_w_mem_kv, l1_w_out, l2_norm, l2_w_in, l2_mem_norm, l2_w_mem_kv, l2_w_out, l3_norm, l3_w_in, l3_q_norm, l3_w_uq, l3_kv_norm, l3_w_ukv, l3_mem_norm, l3_w_mem_kv, l3_w_out, final_norm):
    b, s, d = x.shape
    n_mem = mem.shape[1]
    m = b * s
    mw = N_HEADS * HEAD_DIM
    x2 = x.reshape(m, d)
    mem2 = mem.reshape(b * n_mem, d)

    pos_b = jnp.broadcast_to(positions.astype(F32).reshape(m, 1), (m, LANES))
    tabs128 = tuple(t.reshape(b, s, LANES) for t in _rope_tables(pos_b, HEAD_DIM))
    tabs64 = tuple(t.reshape(b, s, LANES) for t in _rope_tables(pos_b, IDX_DIM))

    def mem_kv(g, w):
        return _norm_proj(mem2, g, w.astype(BF16))

    def finish(y, u, x_in, g_mem, w_mem_kv, w_out, *, wy, zy_blk, zm_blk, qm_blk, final_g=None):
        return _out_proj(y, u, mem_kv(g_mem, w_mem_kv), x_in, w_out.astype(BF16), wy=wy,
                         zy_blk=zy_blk, zm_blk=zm_blk, qm_blk=qm_blk, seq=s, n_mem=n_mem,
                         final_g=final_g)

    tail = _shift_cols(l0_w_in, 3 * mw + N_HEADS, mw + 2 * MEM_WIDTH)
    f_w = l0_w_in[:, 3 * mw:3 * mw + N_HEADS]
    w0 = _pack([(tail[:, MEM_WIDTH:MEM_WIDTH + mw], mw), (tail[:, MEM_WIDTH + mw:], MEM_WIDTH),
                (tail[:, :MEM_WIDTH], MEM_WIDTH), (l0_w_in[:, :3 * mw], 3 * mw)])
    u, f_arr = _norm_proj(x2, l0_norm, w0, w_extra=_pack([(f_w, N_HEADS)], multiple=LANES),
                          extra_dtype=F32)
    u3 = u.reshape(b, s, -1)
    base = (mw + 2 * MEM_WIDTH) // LANES
    c_t = _fox_gate(f_arr.reshape(b, s, LANES), l0_forget_bias, 0)
    y = _fox_attn(u3, c_t, q_blk=base // HEADS_PER_STEP, k_blk=(base + N_HEADS) // HEADS_PER_STEP,
                  v_blk=(base + 2 * N_HEADS) // HEADS_PER_STEP)
    x2 = finish(y.reshape(m, mw), u, x2, l0_mem_norm, l0_w_mem_kv, l0_w_out, wy=mw, zy_blk=0,
                zm_blk=mw // MEM_WIDTH, qm_blk=mw // MEM_WIDTH + 1)

    iw = IDX_HEADS * IDX_DIM
    t1 = mw + 2 * HEAD_DIM + iw
    tail = _shift_cols(l1_w_in, t1 + IDX_DIM + IDX_HEADS, mw + 2 * MEM_WIDTH)
    w1 = _pack([(tail[:, MEM_WIDTH:MEM_WIDTH + mw], mw), (l1_w_in[:, :mw], mw),
                (l1_w_in[:, mw + 2 * HEAD_DIM:t1], iw), (tail[:, MEM_WIDTH + mw:], MEM_WIDTH),
                (tail[:, :MEM_WIDTH], MEM_WIDTH), (l1_w_in[:, mw:mw + 2 * HEAD_DIM], 2 * HEAD_DIM),
                (l1_w_in[:, t1:t1 + IDX_DIM + IDX_HEADS], IDX_DIM + IDX_HEADS),
                (None, LANES - IDX_DIM - IDX_HEADS)])
    u = _norm_proj(x2, l1_norm, w1)
    u3 = u.reshape(b, s, -1)
    kv0 = (2 * mw + iw + 2 * MEM_WIDTH) // LANES
    y = _dsa_attn(u3, tabs128 + tabs64, q_blk=1, qi_blk=2 * mw // iw, k_blk=kv0, v_blk=kv0 + 1,
                  misc_blk=kv0 + 2)
    zoff = (2 * mw + iw) // MEM_WIDTH
    x2 = finish(y.reshape(m, mw), u, x2, l1_mem_norm, l1_w_mem_kv, l1_w_out, wy=mw, zy_blk=0,
                zm_blk=zoff, qm_blk=zoff + 1)

    gw = len(DILATED_PAIRS) * DIL_WIDTH
    q_w, k_w, v_w, qm_w, z_w = _cols(l2_w_in, (gw, gw, gw, MEM_WIDTH, DIL_WIDTH + MEM_WIDTH))

    def grp(w, g):
        return (w[:, g * DIL_WIDTH:(g + 1) * DIL_WIDTH], DIL_WIDTH)

    wa = _pack([grp(q_w, 0), (z_w[:, :DIL_WIDTH], DIL_WIDTH), grp(k_w, 0), grp(v_w, 0),
                (z_w[:, DIL_WIDTH:], MEM_WIDTH), (qm_w, MEM_WIDTH)])
    wb = _pack([grp(q_w, 1), grp(k_w, 1), grp(v_w, 1), grp(q_w, 2), grp(k_w, 2), grp(v_w, 2)])
    u = _norm_proj(x2, l2_norm, wa)
    ub = _norm_proj(x2, l2_norm, wb, out_dtype=F32)
    y = _dil_attn(u.reshape(b, s, -1), ub.reshape(b, s, -1), tabs128, q0_blk=0,
                  k0_blk=2 * DIL_HEADS, v0_blk=3 * DIL_HEADS)
    zoff = 4 * DIL_WIDTH // MEM_WIDTH
    x2 = finish(y.reshape(m, DIL_WIDTH), u, x2, l2_mem_norm, l2_w_mem_kv, l2_w_out, wy=DIL_WIDTH,
                zy_blk=1, zm_blk=zoff, qm_blk=zoff + 1)

    q_lora, kv_lora = l3_w_uq.shape[0], l3_w_ukv.shape[0]
    assert q_lora == MEM_WIDTH and kv_lora == MEM_WIDTH
    lat = q_lora + kv_lora
    tail = _shift_cols(l3_w_in, lat + QK_ROPE, mw + 2 * MEM_WIDTH)
    kr_w = l3_w_in[:, lat:lat + QK_ROPE]
    w3 = _pack([(tail[:, MEM_WIDTH:MEM_WIDTH + mw], mw), (tail[:, MEM_WIDTH + mw:], MEM_WIDTH),
                (tail[:, :MEM_WIDTH], MEM_WIDTH), (l3_w_in[:, :lat], lat)])
    u, kr_arr = _norm_proj(x2, l3_norm, w3, w_extra=_pack([(kr_w, QK_ROPE)], multiple=LANES))
    cq_blk = (mw + 2 * MEM_WIDTH) // MEM_WIDTH
    uq = l3_w_uq.reshape(q_lora, N_HEADS, QK_NOPE + QK_ROPE)
    w_uq = _pack([(uq[:, :, :QK_NOPE].reshape(q_lora, -1), N_HEADS * QK_NOPE),
                  (uq[:, :, QK_NOPE:].reshape(q_lora, -1), N_HEADS * QK_ROPE)])
    ukv = l3_w_ukv.reshape(kv_lora, N_HEADS, QK_NOPE + HEAD_DIM)
    w_ukv = _pack([(ukv[:, :, :QK_NOPE].reshape(kv_lora, -1), N_HEADS * QK_NOPE),
                   (ukv[:, :, QK_NOPE:].reshape(kv_lora, -1), N_HEADS * HEAD_DIM)])
    qf = _norm_proj(u, l3_q_norm, w_uq, x_col_block=cq_blk)
    kvf = _norm_proj(u, l3_kv_norm, w_ukv, x_col_block=cq_blk + 1)
    y = _mla_attn(qf.reshape(b, s, -1), kvf.reshape(b, s, -1), kr_arr.reshape(b, s, LANES), tabs64,
                  kr_blk=0)
    x2 = finish(y.reshape(m, mw), u, x2, l3_mem_norm, l3_w_mem_kv, l3_w_out, wy=mw, zy_blk=0,
                zm_blk=mw // MEM_WIDTH, qm_blk=mw // MEM_WIDTH + 1, final_g=final_norm)
    return x2.reshape(b, s, d)
```

```python
import functools

import jax
import jax.numpy as jnp
from jax import lax
from jax.experimental import pallas as pl
from jax.experimental.pallas import tpu as pltpu

F32 = jnp.float32
BF16 = jnp.bfloat16
I32 = jnp.int32

EPS = 1e-6
ROPE_THETA = 10000.0
HEAD_DIM = 128
N_HEADS = 16
MEM_HEADS = 4
MEM_WIDTH = MEM_HEADS * HEAD_DIM
IDX_HEADS = 16
IDX_DIM = 64
TOPK_MAX = 256
DILATED_PAIRS = ((128, 1), (512, 4), (2048, 16))
DIL_HEADS = 6
DIL_WIDTH = DIL_HEADS * HEAD_DIM
QK_NOPE = 128
QK_ROPE = 64
LANES = 128
NEG = -1e30
INT_MIN = -(2 ** 31)
VMEM_LIMIT = 56 * 1024 * 1024

NT_DIMS = (((1,), (1,)), ((), ()))
LOG2E = 1.4426950408889634


def _cparams(n_axes):
    return pltpu.CompilerParams(
        dimension_semantics=("arbitrary",) * n_axes, vmem_limit_bytes=VMEM_LIMIT)


def _rope128(x, cos_f, sin_s):
    return x * cos_f + pltpu.roll(x, 64, 1) * sin_s


def _rope64(x, cos_f, sin_s):
    lane = lax.broadcasted_iota(I32, x.shape, 1)
    partner = jnp.where((lane & 32) == 0, pltpu.roll(x, 96, 1), pltpu.roll(x, 32, 1))
    return x * cos_f + partner * sin_s


def _rope_table_kernel(pos_ref, inv_ref, sgn_ref, cos_ref, sin_ref):
    ang = pos_ref[...] * inv_ref[...]
    cos_ref[...] = jnp.cos(ang)
    sin_ref[...] = jnp.sin(ang) * sgn_ref[...]


def _rope_tables(pos_b, dh):
    m = pos_b.shape[0]
    half = dh // 2
    inv = jnp.power(ROPE_THETA, -jnp.arange(half, dtype=F32) * 2.0 / dh)
    reps = LANES // half
    inv_l = jnp.tile(inv, reps).reshape(1, LANES)
    sgn = jnp.tile(jnp.concatenate([-jnp.ones((half,), F32), jnp.ones((half,), F32)]),
                   reps // 2).reshape(1, LANES)
    ts = min(512, m)
    return pl.pallas_call(
        _rope_table_kernel,
        grid=(m // ts,),
        in_specs=[pl.BlockSpec((ts, LANES), lambda i: (i, 0)),
                  pl.BlockSpec((1, LANES), lambda i: (0, 0)),
                  pl.BlockSpec((1, LANES), lambda i: (0, 0))],
        out_specs=[pl.BlockSpec((ts, LANES), lambda i: (i, 0))] * 2,
        out_shape=[jax.ShapeDtypeStruct((m, LANES), F32)] * 2,
        compiler_params=_cparams(1),
        name="rope_tables",
    )(pos_b, inv_l, sgn)


def _norm_proj_kernel(x_ref, g_ref, w_ref, *rest, has_extra):
    if has_extra:
        wx_ref, o_ref, ox_ref, h_ref = rest
    else:
        o_ref, h_ref = rest

    first = pl.program_id(1) == 0

    @pl.when(first)
    def _():
        tm = o_ref.shape[0]
        parts = 2 if tm % 32 == 0 else 1
        for part in range(parts):
            rows = pl.ds(part * (tm // parts), tm // parts)
            x = x_ref[rows, :].astype(F32)
            ms = jnp.mean(x * x, axis=-1, keepdims=True)
            h_ref[rows, :] = (x * lax.rsqrt(ms + EPS) * g_ref[...]).astype(BF16)
            o_ref[rows, :] = jnp.dot(h_ref[rows, :], w_ref[...],
                                     preferred_element_type=F32).astype(o_ref.dtype)
            if has_extra:
                ox_ref[rows, :] = jnp.dot(h_ref[rows, :], wx_ref[...],
                                          preferred_element_type=F32).astype(ox_ref.dtype)

    @pl.when(jnp.logical_not(first))
    def _():
        o_ref[...] = jnp.dot(h_ref[...], w_ref[...],
                             preferred_element_type=F32).astype(o_ref.dtype)


def _norm_proj(x, g, w, *, x_col_block=0, out_dtype=BF16, w_extra=None, extra_dtype=BF16):
    m = x.shape[0]
    k, n = w.shape
    tm = min(1024, m)
    tn = 1024 if n % 1024 == 0 else 512
    assert m % tm == 0 and n % tn == 0
    has_extra = w_extra is not None
    in_specs = [pl.BlockSpec((tm, k), lambda i, j: (i, x_col_block)),
                pl.BlockSpec((1, k), lambda i, j: (0, 0)),
                pl.BlockSpec((k, tn), lambda i, j: (0, j))]
    out_specs = [pl.BlockSpec((tm, tn), lambda i, j: (i, j))]
    out_shape = [jax.ShapeDtypeStruct((m, n), out_dtype)]
    args = [x, g.reshape(1, k).astype(F32), w]
    if has_extra:
        in_specs.append(pl.BlockSpec((k, LANES), lambda i, j: (0, 0)))
        out_specs.append(pl.BlockSpec((tm, LANES), lambda i, j: (i, 0)))
        out_shape.append(jax.ShapeDtypeStruct((m, LANES), extra_dtype))
        args.append(w_extra)
    outs = pl.pallas_call(
        functools.partial(_norm_proj_kernel, has_extra=has_extra),
        grid=(m // tm, n // tn),
        in_specs=in_specs,
        out_specs=out_specs,
        out_shape=out_shape,
        scratch_shapes=[pltpu.VMEM((tm, k), BF16)],
        compiler_params=_cparams(2),
        name="norm_proj",
    )(*args)
    return outs if has_extra else outs[0]


def _out_proj_kernel(y_ref, zy_ref, zm_ref, qm_ref, mkv_ref, x_ref, w_ref, *rest, wy, final):
    if final:
        gf_ref, o_ref, gated_ref = rest
    else:
        o_ref, gated_ref = rest
    scale = HEAD_DIM ** -0.5
    for h in range(MEM_HEADS):
        lo, hi = h * HEAD_DIM, (h + 1) * HEAD_DIM
        s = lax.dot_general(qm_ref[:, lo:hi], mkv_ref[:, lo:hi], NT_DIMS,
                            preferred_element_type=F32) * scale
        m = jnp.max(s, axis=-1, keepdims=True)
        p = jnp.exp(s - m)
        l = jnp.sum(p, axis=-1, keepdims=True)
        o = jnp.dot(p.astype(BF16), mkv_ref[:, MEM_WIDTH + lo:MEM_WIDTH + hi],
                    preferred_element_type=F32) / l
        zm = zm_ref[:, lo:hi].astype(F32)
        gated_ref[:, wy + lo:wy + hi] = (o * (zm * jax.nn.sigmoid(zm))).astype(BF16)
    tm = o_ref.shape[0]
    parts = 2 if tm % 32 == 0 else 1
    for part in range(parts):
        rows = pl.ds(part * (tm // parts), tm // parts)
        zy = zy_ref[rows, :].astype(F32)
        gated_ref[rows, :wy] = (y_ref[rows, :].astype(F32)
                                * (zy * jax.nn.sigmoid(zy))).astype(BF16)
        out = x_ref[rows, :] + jnp.dot(gated_ref[rows, :], w_ref[...],
                                       preferred_element_type=F32)
        if final:
            ms = jnp.mean(out * out, axis=-1, keepdims=True)
            out = out * lax.rsqrt(ms + EPS) * gf_ref[...]
        o_ref[rows, :] = out


def _out_proj(y, u, mkv, x, w_out, *, wy, zy_blk, zm_blk, qm_blk, seq, n_mem, final_g=None):
    m, d = x.shape
    tm = min(512, seq)
    final = final_g is not None
    in_specs = [
        pl.BlockSpec((tm, wy), lambda i: (i, 0)),
        pl.BlockSpec((tm, wy), lambda i: (i, zy_blk)),
        pl.BlockSpec((tm, MEM_WIDTH), lambda i: (i, zm_blk)),
        pl.BlockSpec((tm, MEM_WIDTH), lambda i: (i, qm_blk)),
        pl.BlockSpec((n_mem, 2 * MEM_WIDTH), lambda i: ((i * tm) // seq, 0)),
        pl.BlockSpec((tm, d), lambda i: (i, 0)),
        pl.BlockSpec((wy + MEM_WIDTH, d), lambda i: (0, 0), pipeline_mode=pl.Buffered(1)),
    ]
    args = [y, u, u, u, mkv, x, w_out]
    if final:
        in_specs.append(pl.BlockSpec((1, d), lambda i: (0, 0)))
        args.append(final_g.reshape(1, d).astype(F32))
    return pl.pallas_call(
        functools.partial(_out_proj_kernel, wy=wy, final=final),
        grid=(m // tm,),
        in_specs=in_specs,
        out_specs=pl.BlockSpec((tm, d), lambda i: (i, 0)),
        out_shape=jax.ShapeDtypeStruct((m, d), F32),
        scratch_shapes=[pltpu.VMEM((tm, wy + MEM_WIDTH), BF16)],
        compiler_params=_cparams(1),
        name="out_proj",
    )(*args)


def _fox_gate_kernel(f_ref, b_ref, c_ref, carry_ref):
    @pl.when(pl.program_id(1) == 0)
    def _():
        carry_ref[...] = jnp.zeros_like(carry_ref)

    f_t = f_ref[...].astype(F32).T[:N_HEADS, :] + b_ref[...]
    log_f = jnp.minimum(f_t, 0.0) - jnp.log(1.0 + jnp.exp(-jnp.abs(f_t)))
    ts = log_f.shape[1]
    r = lax.broadcasted_iota(I32, (ts, ts), 0)
    c = lax.broadcasted_iota(I32, (ts, ts), 1)
    upper = jnp.where(r <= c, 1.0, 0.0).astype(F32)
    cs = jnp.dot(log_f, upper, precision=lax.Precision.HIGHEST,
                 preferred_element_type=F32) + carry_ref[...]
    c_ref[...] = cs * LOG2E
    carry_ref[...] = cs[:, ts - 1:ts]


def _fox_gate(u3, bias, f_blk):
    b, s, _ = u3.shape
    ts = min(256, s)
    return pl.pallas_call(
        _fox_gate_kernel,
        grid=(b, s // ts),
        in_specs=[pl.BlockSpec((None, ts, LANES), lambda bi, j: (bi, j, f_blk)),
                  pl.BlockSpec((N_HEADS, 1), lambda bi, j: (0, 0))],
        out_specs=pl.BlockSpec((None, N_HEADS, ts), lambda bi, j: (bi, 0, j)),
        out_shape=jax.ShapeDtypeStruct((b, N_HEADS, s), F32),
        scratch_shapes=[pltpu.VMEM((N_HEADS, 1), F32)],
        compiler_params=_cparams(2),
        name="fox_gate",
    )(u3, bias.reshape(N_HEADS, 1).astype(F32))


STRIP = 32
HEADS_PER_STEP = 4


def _softmax_strips(t_ref, p_ref, m_ref, l_ref, alpha_ref, adjust):
    tq, tk = t_ref.shape
    for r0 in range(0, tq, min(STRIP, tq)):
        rs = pl.ds(r0, min(STRIP, tq))
        blocks = [adjust(r0, c, t_ref[rs, c * LANES:(c + 1) * LANES]) for c in range(tk // LANES)]
        live = [x for x in blocks if x is not None]
        mx = live[0]
        for x in live[1:]:
            mx = jnp.maximum(mx, x)
        m_old = m_ref[rs, :]
        m_new = jnp.maximum(m_old, jnp.max(mx, axis=-1, keepdims=True))
        alpha = jnp.exp2(m_old - m_new)
        ps = [None if x is None else jnp.exp2(x - m_new) for x in blocks]
        live = [x for x in ps if x is not None]
        sm = live[0]
        for x in live[1:]:
            sm = sm + x
        l_ref[rs, :] = alpha * l_ref[rs, :] + jnp.sum(sm, axis=-1, keepdims=True)
        m_ref[rs, :] = m_new
        alpha_ref[rs, :] = alpha
        for c, x in enumerate(ps):
            p_ref[rs, c * LANES:(c + 1) * LANES] = (
                jnp.zeros((min(STRIP, tq), LANES), BF16) if x is None else x.astype(BF16))


def _flash_scratch(n_streams, tq):
    return [pltpu.VMEM((n_streams, tq, tq), F32), pltpu.VMEM((n_streams, tq, tq), BF16),
            pltpu.VMEM((n_streams, tq, LANES), F32), pltpu.VMEM((n_streams, tq, LANES), F32),
            pltpu.VMEM((n_streams, tq, LANES), F32), pltpu.VMEM((n_streams, tq, HEAD_DIM), F32)]


def _flash_init(scratch):
    _, _, m_ref, l_ref, _, acc_ref = scratch
    m_ref[...] = jnp.full(m_ref.shape, NEG, F32)
    l_ref[...] = jnp.zeros(l_ref.shape, F32)
    acc_ref[...] = jnp.zeros(acc_ref.shape, F32)


def _flash_chunk(n_streams, j, raw_scores, values, to_logits, scratch, *, diagonal):
    t_ref, p_ref, m_ref, l_ref, alpha_ref, acc_ref = scratch
    strip = min(STRIP, t_ref.shape[1])
    row = lax.broadcasted_iota(I32, (strip, LANES), 0)
    col = lax.broadcasted_iota(I32, (strip, LANES), 1)
    for a in range(n_streams):
        t_ref[a] = raw_scores(a, j)
    for a in range(n_streams):
        def adjust(r0, c, x, a=a):
            t = to_logits(a, j, r0, c, x)
            if not diagonal or c * LANES + LANES - 1 <= r0:
                return t
            if c * LANES > r0 + strip - 1:
                return None
            return jnp.where(col + c * LANES <= row + r0, t, NEG)

        _softmax_strips(t_ref.at[a], p_ref.at[a], m_ref.at[a], l_ref.at[a], alpha_ref.at[a],
                        adjust)
    for a in range(n_streams):
        acc_ref[a] = alpha_ref[a] * acc_ref[a] + jnp.dot(p_ref[a], values(a, j),
                                                         preferred_element_type=F32)


def _flash_finish(n_streams, scratch):
    _, _, _, l_ref, _, acc_ref = scratch
    return [acc_ref[a] / l_ref[a] for a in range(n_streams)]


def _causal_flash(i, n_streams, raw_scores, values, to_logits, scratch):
    _flash_init(scratch)

    def body(j, _):
        _flash_chunk(n_streams, j, raw_scores, values, to_logits, scratch, diagonal=False)
        return 0

    lax.fori_loop(0, i, body, 0)
    _flash_chunk(n_streams, i, raw_scores, values, to_logits, scratch, diagonal=True)
    return _flash_finish(n_streams, scratch)


def _fox_attn_kernel(q_ref, k_ref, v_ref, c_ref, o_ref, qs_ref, *scratch, tq):
    i = pl.program_id(2)
    qk_scale = HEAD_DIM ** -0.5 * LOG2E
    qs_ref[...] = (q_ref[...].astype(F32) * qk_scale).astype(BF16)

    def raw_scores(a, j):
        off = pl.multiple_of(j * tq, tq)
        lo, hi = a * HEAD_DIM, (a + 1) * HEAD_DIM
        return lax.dot_general(qs_ref[:, lo:hi], k_ref[pl.ds(off, tq), lo:hi], NT_DIMS,
                               preferred_element_type=F32)

    def to_logits(a, j, r0, c, x):
        return x - c_ref[a, j, :, c * LANES:(c + 1) * LANES]

    def values(a, j):
        off = pl.multiple_of(j * tq, tq)
        return v_ref[pl.ds(off, tq), a * HEAD_DIM:(a + 1) * HEAD_DIM]

    outs = _causal_flash(i, HEADS_PER_STEP, raw_scores, values, to_logits, scratch)
    for a in range(HEADS_PER_STEP):
        o_ref[:, a * HEAD_DIM:(a + 1) * HEAD_DIM] = outs[a].astype(o_ref.dtype)


def _fox_attn(u3, c_t, *, q_blk, k_blk, v_blk):
    b, s, _ = u3.shape
    tq = min(512, s)
    nq = s // tq
    pw = HEADS_PER_STEP * HEAD_DIM
    c5 = c_t.reshape(b, N_HEADS, nq, 1, tq)
    return pl.pallas_call(
        functools.partial(_fox_attn_kernel, tq=tq),
        grid=(b, N_HEADS // HEADS_PER_STEP, nq),
        in_specs=[pl.BlockSpec((None, tq, pw), lambda bi, p, i: (bi, i, q_blk + p)),
                  pl.BlockSpec((None, s, pw), lambda bi, p, i: (bi, 0, k_blk + p)),
                  pl.BlockSpec((None, s, pw), lambda bi, p, i: (bi, 0, v_blk + p)),
                  pl.BlockSpec((None, HEADS_PER_STEP, nq, 1, tq), lambda bi, p, i: (bi, p, 0, 0, 0))],
        out_specs=pl.BlockSpec((None, tq, pw), lambda bi, p, i: (bi, i, p)),
        out_shape=jax.ShapeDtypeStruct((b, s, N_HEADS * HEAD_DIM), BF16),
        scratch_shapes=[pltpu.VMEM((tq, pw), BF16)] + _flash_scratch(HEADS_PER_STEP, tq),
        compiler_params=_cparams(3),
        name="fox_attn",
    )(u3, u3, u3, c5)


def _dsa_kprep_kernel(k_ref, mk_ref, c128_ref, s128_ref, c64_ref, s64_ref,
                      kr_ref, klo_ref, khi_ref):
    kr_ref[...] = _rope128(k_ref[...].astype(F32), c128_ref[...], s128_ref[...]).astype(BF16)
    ki = _rope64(mk_ref[...].astype(F32), c64_ref[...], s64_ref[...])
    lane = lax.broadcasted_iota(I32, ki.shape, 1)
    lo = jnp.where(lane < IDX_DIM, ki, 0.0)
    klo_ref[...] = lo.astype(BF16)
    khi_ref[...] = pltpu.roll(lo, IDX_DIM, 1).astype(BF16)


def _dsa_kprep(u3, tabs, *, k_blk, misc_blk):
    b, s, _ = u3.shape
    ts = min(512, s)
    c128, s128, c64, s64 = tabs

    def spec(blk):
        return pl.BlockSpec((None, ts, LANES), lambda bi, i: (bi, i, blk))

    return pl.pallas_call(
        _dsa_kprep_kernel,
        grid=(b, s // ts),
        in_specs=[spec(k_blk), spec(misc_blk), spec(0), spec(0), spec(0), spec(0)],
        out_specs=[spec(0)] * 3,
        out_shape=[jax.ShapeDtypeStruct((b, s, LANES), BF16)] * 3,
        compiler_params=_cparams(2),
        name="dsa_kprep",
    )(u3, u3, c128, s128, c64, s64)


def _dsa_kernel(q_ref, qi_ref, mq_ref, v_ref, kr_ref, klo_ref, khi_ref,
                cq128_ref, sq128_ref, cq64_ref, sq64_ref, y_in_ref,
                o_ref,
                qr_ref, qir_ref, w_ref, sc_ref, t4_ref, yo_ref, *flash,
                tq, kl, q_off, n_sel):
    del y_in_ref
    t0 = q_off + pl.program_id(1) * tq
    qk_scale = HEAD_DIM ** -0.5 * LOG2E
    idx_scale = (IDX_DIM ** -0.5) * (IDX_HEADS ** -0.5)
    ck = min(512, kl)
    n_chunks = kl // ck
    strip = min(STRIP, tq)

    cq128, sq128 = cq128_ref[...], sq128_ref[...]
    for h in range(N_HEADS):
        qh = q_ref[:, h * HEAD_DIM:(h + 1) * HEAD_DIM].astype(F32)
        qr_ref[h] = (_rope128(qh, cq128, sq128) * qk_scale).astype(BF16)
    cq64, sq64 = cq64_ref[...], sq64_ref[...]
    for a in range(IDX_HEADS // 2):
        qir_ref[a] = _rope64(qi_ref[:, a * LANES:(a + 1) * LANES].astype(F32),
                             cq64, sq64).astype(BF16)
    w_ref[...] = mq_ref[...].astype(F32) * idx_scale

    row = lax.broadcasted_iota(I32, (strip, LANES), 0)
    col = lax.broadcasted_iota(I32, (strip, LANES), 1)
    group = 4

    def idx_chunk(j, _):
        off = pl.multiple_of(j * ck, ck)
        k_lo, k_hi = klo_ref[pl.ds(off, ck), :], khi_ref[pl.ds(off, ck), :]
        for g in range(IDX_HEADS // group):
            for a2 in range(group // 2):
                x = qir_ref[g * (group // 2) + a2]
                t4_ref[2 * a2] = lax.dot_general(x, k_lo, NT_DIMS, preferred_element_type=F32)
                t4_ref[2 * a2 + 1] = lax.dot_general(x, k_hi, NT_DIMS,
                                                     preferred_element_type=F32)
            last = g == IDX_HEADS // group - 1
            for r0 in range(0, tq, strip):
                rs = pl.ds(r0, strip)
                w_rows = w_ref[rs, :]
                ws = [jnp.broadcast_to(w_rows[:, IDX_DIM + g * group + u:IDX_DIM + g * group + u + 1],
                                       (strip, LANES)) for u in range(group)]
                for c in range(ck // LANES):
                    cs = pl.ds(c * LANES, LANES)
                    acc = ws[0] * jnp.maximum(t4_ref[0, rs, cs], 0.0)
                    for u in range(1, group):
                        acc = acc + ws[u] * jnp.maximum(t4_ref[u, rs, cs], 0.0)
                    if g > 0:
                        acc = acc + sc_ref[j, rs, cs]
                    if last:
                        causal = col + (off + c * LANES) <= row + (t0 + r0)
                        acc = jnp.where(causal, acc, -jnp.inf)
                    sc_ref[j, rs, cs] = acc
        return 0

    lax.fori_loop(0, n_chunks, idx_chunk, 0)

    shape3 = (n_chunks, tq, ck)
    col3 = lax.broadcasted_iota(I32, shape3, 0) * ck + lax.broadcasted_iota(I32, shape3, 2)
    causal3 = col3 <= lax.broadcasted_iota(I32, shape3, 1) + t0

    def count(x):
        return jnp.sum(jnp.sum(x, axis=0), axis=-1, keepdims=True)

    def key_to_f32(key):
        return pltpu.bitcast(jnp.where(key < 0, key ^ jnp.int32(0x7FFFFFFF), key), F32)

    def count_ge(cand_f):
        parts = []
        for k in range(tq // LANES):
            rs = slice(k * LANES, (k + 1) * LANES)
            blk = sc_ref[n_chunks - 1, rs, :(k + 1) * LANES]
            parts.append(jnp.sum(jnp.where(blk >= cand_f[rs], 1.0, 0.0), axis=-1, keepdims=True))
        total = jnp.concatenate(parts, axis=0)
        if n_chunks > 1:
            total = total + count(jnp.where(sc_ref[:n_chunks - 1] >= cand_f, 1.0, 0.0))
        return total

    def thr_body(it, key):
        cand = key + (jnp.int32(1) << (31 - it))
        c = count_ge(key_to_f32(cand)) if tq == ck and tq % LANES == 0 else count(
            jnp.where(sc_ref[...] >= key_to_f32(cand), 1.0, 0.0))
        return jnp.where(c >= n_sel, cand, key)

    thr_key = lax.fori_loop(0, 32, thr_body, jnp.full((tq, 1), INT_MIN, I32))
    take_all = thr_key == INT_MIN
    thr = key_to_f32(thr_key)
    score = sc_ref[...]
    need = n_sel - count(jnp.where(score > thr, 1.0, 0.0))
    surplus = jnp.where(take_all, 0.0, count(jnp.where(score == thr, 1.0, 0.0)) - need)
    has_surplus = jnp.max(surplus) > 0.0

    n_bits = kl.bit_length()

    def tie_body(it, jm):
        cand = jm + (jnp.int32(1) << (n_bits - 1 - it))
        f = count(jnp.where(sc_ref[...] == thr, jnp.where(col3 < cand, 1.0, 0.0), 0.0))
        return jnp.where(jnp.logical_and(f < need, cand <= kl), cand, jm)

    jm = lax.fori_loop(0, jnp.where(has_surplus, n_bits, 0), tie_body,
                       jnp.broadcast_to(jnp.where(has_surplus, 0, kl), (tq, 1)).astype(I32))
    keep = jnp.where(jnp.logical_or(score > thr, take_all), 1.0,
                     jnp.where(score == thr, jnp.where(col3 <= jm, 1.0, 0.0), 0.0))
    sc_ref[...] = jnp.where(causal3, jnp.where(keep > 0.0, 0.0, NEG), NEG)

    def raw_scores(h0):
        def f(a, j):
            return lax.dot_general(qr_ref[h0 + a], kr_ref[j * ck:(j + 1) * ck, :], NT_DIMS,
                                   preferred_element_type=F32)
        return f

    def values(a, j):
        return v_ref[j * ck:(j + 1) * ck, :]

    def to_logits(a, j, r0, c, x):
        return x + sc_ref[j, r0:r0 + strip, c * LANES:(c + 1) * LANES]

    def group_body(pp, _):
        h0 = HEADS_PER_STEP * pp
        _flash_init(flash)
        for j in range(n_chunks):
            _flash_chunk(HEADS_PER_STEP, j, raw_scores(h0), values, to_logits, flash,
                         diagonal=j == n_chunks - 1)
        outs = _flash_finish(HEADS_PER_STEP, flash)
        for a in range(HEADS_PER_STEP):
            yo_ref[h0 + a] = outs[a].astype(yo_ref.dtype)
        return 0

    lax.fori_loop(0, N_HEADS // HEADS_PER_STEP, group_body, 0)
    for h in range(N_HEADS):
        o_ref[:, h * HEAD_DIM:(h + 1) * HEAD_DIM] = yo_ref[h]


def _dsa_group(u3, kprep, tabs, y, *, q_off, rows, kl, n_sel, q_blk, v_blk, qi_blk, misc_blk):
    b, s, _ = u3.shape
    tq = rows
    assert kl % min(512, kl) == 0 and tq == min(512, kl)
    qb0 = q_off // tq
    c128, s128, c64, s64 = tabs
    kr, klo, khi = kprep
    qw = N_HEADS * HEAD_DIM
    iw = IDX_HEADS * IDX_DIM
    ck = min(512, kl)

    def qspec(width, blk):
        return pl.BlockSpec((None, tq, width), lambda bi, i: (bi, qb0 + i, blk))

    def kspec(blk):
        return pl.BlockSpec((None, kl, LANES), lambda bi, i: (bi, 0, blk))

    return pl.pallas_call(
        functools.partial(_dsa_kernel, tq=tq, kl=kl, q_off=q_off, n_sel=n_sel),
        grid=(b, rows // tq),
        in_specs=[qspec(qw, q_blk), qspec(iw, qi_blk), qspec(LANES, misc_blk),
                  kspec(v_blk), kspec(0), kspec(0), kspec(0),
                  qspec(LANES, 0), qspec(LANES, 0), qspec(LANES, 0), qspec(LANES, 0),
                  pl.BlockSpec(memory_space=pl.ANY)],
        out_specs=pl.BlockSpec((None, tq, qw), lambda bi, i: (bi, qb0 + i, 0)),
        out_shape=jax.ShapeDtypeStruct((b, s, qw), BF16),
        input_output_aliases={11: 0},
        scratch_shapes=[pltpu.VMEM((N_HEADS, tq, HEAD_DIM), BF16),
                        pltpu.VMEM((IDX_HEADS // 2, tq, LANES), BF16),
                        pltpu.VMEM((tq, LANES), F32),
                        pltpu.VMEM((kl // ck, tq, ck), F32),
                        pltpu.VMEM((4, tq, ck), F32),
                        pltpu.VMEM((N_HEADS, tq, HEAD_DIM), BF16)]
        + _flash_scratch(HEADS_PER_STEP, tq),
        compiler_params=_cparams(2),
        name="dsa_attn",
    )(u3, u3, u3, u3, kr, klo, khi, c128, s128, c64, s64, y)


def _dsa_attn(u3, tabs, *, k_blk, misc_blk, **blks):
    b, s, _ = u3.shape
    n_sel = min(TOPK_MAX, s // 4)
    rows = min(512, s)
    kprep = _dsa_kprep(u3, tabs, k_blk=k_blk, misc_blk=misc_blk)
    y = jnp.zeros((b, s, N_HEADS * HEAD_DIM), BF16)
    for q_off in range(0, s, rows):
        y = _dsa_group(u3, kprep, tabs, y, q_off=q_off, rows=rows, kl=q_off + rows, n_sel=n_sel,
                       misc_blk=misc_blk, **blks)
    return y


def _dil_kernel(q0_ref, k0_ref, v0_ref, q1_ref, k1_ref, v1_ref, q2_ref, k2_ref, v2_ref,
                cos_ref, sin_ref, y_ref, qr_ref, kr_ref, o_ref, lse_ref, t_ref, p_ref, *, seq):
    qk_scale = HEAD_DIM ** -0.5 * LOG2E
    cos_f, sin_s = cos_ref[...], sin_ref[...]
    groups = ((q0_ref, k0_ref, v0_ref), (q1_ref, k1_ref, v1_ref), (q2_ref, k2_ref, v2_ref))
    for g, (q_ref, k_ref, _) in enumerate(groups):
        qr_ref[g] = _rope128(q_ref[...].astype(F32), cos_f, sin_s) * qk_scale
        kr_ref[g] = _rope128(k_ref[...].astype(F32), cos_f, sin_s)

    for g, (window, dil) in enumerate(DILATED_PAIRS):
        v_ref = groups[g][2]
        sub = seq // dil
        qb = min(window // dil, sub)
        tiles = [(r, i) for r in range(dil) for i in range(sub // qb)]

        def rows(r, blk, dil=dil, qb=qb):
            start = r + dil * qb * blk
            return pl.ds(start, qb) if dil == 1 else pl.ds(start, qb, stride=dil)

        for n, (r, i) in enumerate(tiles):
            q = qr_ref[g, rows(r, i), :].astype(BF16)
            k_cur = kr_ref[g, rows(r, i), :].astype(BF16)
            t_ref[n, :qb, qb:2 * qb] = lax.dot_general(q, k_cur, NT_DIMS,
                                                       preferred_element_type=F32)
            if i > 0:
                k_prev = kr_ref[g, rows(r, i - 1), :].astype(BF16)
                t_ref[n, :qb, :qb] = lax.dot_general(q, k_prev, NT_DIMS,
                                                     preferred_element_type=F32)
        row = lax.broadcasted_iota(I32, (qb, qb), 0)
        col = lax.broadcasted_iota(I32, (qb, qb), 1)
        for n, (r, i) in enumerate(tiles):
            t_cur = jnp.where(col <= row, t_ref[n, :qb, qb:2 * qb], NEG)
            m = jnp.max(t_cur, axis=-1, keepdims=True)
            if i > 0:
                t_prev = jnp.where(col >= row, t_ref[n, :qb, :qb], NEG)
                m = jnp.maximum(m, jnp.max(t_prev, axis=-1, keepdims=True))
            p_cur = jnp.exp2(t_cur - m)
            l = jnp.sum(p_cur, axis=-1, keepdims=True)
            p_ref[n, :qb, qb:2 * qb] = p_cur.astype(BF16)
            if i > 0:
                p_prev = jnp.exp2(t_prev - m)
                l = l + jnp.sum(p_prev, axis=-1, keepdims=True)
                p_ref[n, :qb, :qb] = p_prev.astype(BF16)
            t_ref[n, :qb, :LANES] = jnp.broadcast_to(l, (qb, LANES))
            t_ref[n, :qb, LANES:2 * LANES] = jnp.broadcast_to(m, (qb, LANES))
        for n, (r, i) in enumerate(tiles):
            acc = jnp.dot(p_ref[n, :qb, qb:2 * qb], v_ref[rows(r, i), :].astype(BF16),
                          preferred_element_type=F32)
            if i > 0:
                acc = acc + jnp.dot(p_ref[n, :qb, :qb], v_ref[rows(r, i - 1), :].astype(BF16),
                                    preferred_element_type=F32)
            l = t_ref[n, :qb, :LANES]
            o_ref[g, rows(r, i), :] = acc / l
            lse_ref[g, rows(r, i), :] = t_ref[n, :qb, LANES:2 * LANES] + jnp.log2(l)

    l0, l1, l2 = lse_ref[0], lse_ref[1], lse_ref[2]
    m = jnp.maximum(jnp.maximum(l0, l1), l2)
    e0, e1, e2 = jnp.exp2(l0 - m), jnp.exp2(l1 - m), jnp.exp2(l2 - m)
    y = (e0 * o_ref[0] + e1 * o_ref[1] + e2 * o_ref[2]) / (e0 + e1 + e2)
    y_ref[...] = y.astype(y_ref.dtype)


def _dil_attn(ua3, ub3, tabs128, *, q0_blk, k0_blk, v0_blk):
    b, s, _ = ua3.shape
    for window, dil in DILATED_PAIRS:
        assert s % dil == 0 and (s // dil) % min(window // dil, s // dil) == 0
    cos_t, sin_t = tabs128
    tile = min(DILATED_PAIRS[0][0], s)
    n_tiles = s // min(min(w // d, s // d) for w, d in DILATED_PAIRS)

    def spec(blk0):
        return pl.BlockSpec((None, s, HEAD_DIM), lambda bi, a: (bi, 0, blk0 + a))

    tab = pl.BlockSpec((None, s, LANES), lambda bi, a: (bi, 0, 0))
    return pl.pallas_call(
        functools.partial(_dil_kernel, seq=s),
        grid=(b, DIL_HEADS),
        in_specs=[spec(q0_blk), spec(k0_blk), spec(v0_blk)]
        + [spec(j * DIL_HEADS) for j in range(6)] + [tab, tab],
        out_specs=pl.BlockSpec((None, s, HEAD_DIM), lambda bi, a: (bi, 0, a)),
        out_shape=jax.ShapeDtypeStruct((b, s, DIL_WIDTH), BF16),
        scratch_shapes=[pltpu.VMEM((3, s, HEAD_DIM), F32)] * 4
        + [pltpu.VMEM((n_tiles, tile, 2 * LANES), F32), pltpu.VMEM((n_tiles, tile, 2 * LANES), BF16)],
        compiler_params=_cparams(2),
        name="dilated_attn",
    )(ua3, ua3, ua3, ub3, ub3, ub3, ub3, ub3, ub3, cos_t, sin_t)


def _mla_kernel(qn_ref, qr_ref, kn_ref, kr_ref, v_ref, cq_ref, sq_ref, ck_ref, sk_ref,
                o_ref, qcat_ref, kcat_ref, *scratch, tq):
    i = pl.program_id(2)
    qk_scale = (QK_NOPE + QK_ROPE) ** -0.5 * LOG2E
    cat = QK_NOPE + QK_ROPE

    @pl.when(i == 0)
    def _():
        k_rope = _rope64(kr_ref[...].astype(F32), ck_ref[...],
                         sk_ref[...])[:, :QK_ROPE].astype(BF16)
        for a in range(HEADS_PER_STEP):
            kcat_ref[a, :, :QK_NOPE] = kn_ref[:, a * HEAD_DIM:(a + 1) * HEAD_DIM]
            kcat_ref[a, :, QK_NOPE:cat] = k_rope
            kcat_ref[a, :, cat:] = jnp.zeros((kcat_ref.shape[1], 2 * LANES - cat), BF16)

    cos_q, sin_q = cq_ref[...], sq_ref[...]
    for a2 in range(HEADS_PER_STEP // 2):
        q_rope = (_rope64(qr_ref[:, a2 * LANES:(a2 + 1) * LANES].astype(F32), cos_q, sin_q)
                  * qk_scale).astype(BF16)
        for a in (2 * a2, 2 * a2 + 1):
            q_nope = qn_ref[:, a * HEAD_DIM:(a + 1) * HEAD_DIM].astype(F32) * qk_scale
            qcat_ref[a, :, :QK_NOPE] = q_nope.astype(BF16)
            qcat_ref[a, :, QK_NOPE:cat] = q_rope[:, (a % 2) * QK_ROPE:(a % 2 + 1) * QK_ROPE]
            qcat_ref[a, :, cat:] = jnp.zeros((tq, 2 * LANES - cat), BF16)

    def raw_scores(a, j):
        off = pl.multiple_of(j * tq, tq)
        return lax.dot_general(qcat_ref[a], kcat_ref[a, pl.ds(off, tq), :], NT_DIMS,
                               preferred_element_type=F32)

    def to_logits(a, j, r0, c, x):
        return x

    def values(a, j):
        off = pl.multiple_of(j * tq, tq)
        return v_ref[pl.ds(off, tq), a * HEAD_DIM:(a + 1) * HEAD_DIM]

    outs = _causal_flash(i, HEADS_PER_STEP, raw_scores, values, to_logits, scratch)
    for a in range(HEADS_PER_STEP):
        o_ref[:, a * HEAD_DIM:(a + 1) * HEAD_DIM] = outs[a].astype(o_ref.dtype)


def _mla_attn(qf3, kvf3, u3, tabs64, *, kr_blk):
    b, s, _ = qf3.shape
    tq = min(512, s)
    gw = HEADS_PER_STEP * HEAD_DIM
    rw = HEADS_PER_STEP * QK_ROPE
    n_groups = N_HEADS // HEADS_PER_STEP
    cos_t, sin_t = tabs64
    return pl.pallas_call(
        functools.partial(_mla_kernel, tq=tq),
        grid=(b, n_groups, s // tq),
        in_specs=[pl.BlockSpec((None, tq, gw), lambda bi, p, i: (bi, i, p)),
                  pl.BlockSpec((None, tq, rw),
                               lambda bi, p, i: (bi, i, N_HEADS * QK_NOPE // rw + p)),
                  pl.BlockSpec((None, s, gw), lambda bi, p, i: (bi, 0, p)),
                  pl.BlockSpec((None, s, LANES), lambda bi, p, i: (bi, 0, kr_blk)),
                  pl.BlockSpec((None, s, gw), lambda bi, p, i: (bi, 0, n_groups + p)),
                  pl.BlockSpec((None, tq, LANES), lambda bi, p, i: (bi, i, 0)),
                  pl.BlockSpec((None, tq, LANES), lambda bi, p, i: (bi, i, 0)),
                  pl.BlockSpec((None, s, LANES), lambda bi, p, i: (bi, 0, 0)),
                  pl.BlockSpec((None, s, LANES), lambda bi, p, i: (bi, 0, 0))],
        out_specs=pl.BlockSpec((None, tq, gw), lambda bi, p, i: (bi, i, p)),
        out_shape=jax.ShapeDtypeStruct((b, s, N_HEADS * HEAD_DIM), BF16),
        scratch_shapes=[pltpu.VMEM((HEADS_PER_STEP, tq, 2 * LANES), BF16),
                        pltpu.VMEM((HEADS_PER_STEP, s, 2 * LANES), BF16)]
        + _flash_scratch(HEADS_PER_STEP, tq),
        compiler_params=_cparams(3),
        name="mla_attn",
    )(qf3, qf3, kvf3, u3, kvf3, cos_t, sin_t, cos_t, sin_t)


def _pack(parts, multiple=512):
    k = next(p.shape[0] for p, _ in parts if p is not None)
    cols = [jnp.zeros((k, w), BF16) if p is None else p.astype(BF16) for p, w in parts]
    n = sum(w for _, w in parts)
    pad = (-n) % multiple
    if pad:
        cols.append(jnp.zeros((k, pad), BF16))
    return jnp.concatenate(cols, axis=1)


def _cols(w, sizes):
    out, acc = [], 0
    for sz in sizes:
        out.append(w[:, acc:acc + sz])
        acc += sz
    return out


def kernel(x, mem, positions, l0_norm, l0_w_in, l0_forget_bias, l0_mem_norm, l0_w_mem_kv, l0_w_out, l1_norm, l1_w_in, l1_mem_norm, l1_w_mem_kv, l1_w_out, l2_norm, l2_w_in, l2_mem_norm, l2_w_mem_kv, l2_w_out, l3_norm, l3_w_in, l3_q_norm, l3_w_uq, l3_kv_norm, l3_w_ukv, l3_mem_norm, l3_w_mem_kv, l3_w_out, final_norm):
    b, s, d = x.shape
    n_mem = mem.shape[1]
    m = b * s
    mw = N_HEADS * HEAD_DIM
    x2 = x.reshape(m, d)
    mem2 = mem.reshape(b * n_mem, d)

    pos_b = jnp.broadcast_to(positions.astype(F32).reshape(m, 1), (m, LANES))
    tabs128 = tuple(t.reshape(b, s, LANES) for t in _rope_tables(pos_b, HEAD_DIM))
    tabs64 = tuple(t.reshape(b, s, LANES) for t in _rope_tables(pos_b, IDX_DIM))

    def mem_kv(g, w):
        return _norm_proj(mem2, g, w.astype(BF16))

    def finish(y, u, x_in, g_mem, w_mem_kv, w_out, *, wy, zy_blk, zm_blk, qm_blk, final_g=None):
        return _out_proj(y, u, mem_kv(g_mem, w_mem_kv), x_in, w_out.astype(BF16), wy=wy,
                         zy_blk=zy_blk, zm_blk=zm_blk, qm_blk=qm_blk, seq=s, n_mem=n_mem,
                         final_g=final_g)

    q_w, k_w, v_w, f_w, qm_w, z_w = _cols(l0_w_in, (mw, mw, mw, N_HEADS, MEM_WIDTH, mw + MEM_WIDTH))
    w0 = _pack([(z_w[:, :mw], mw), (z_w[:, mw:], MEM_WIDTH), (qm_w, MEM_WIDTH), (q_w, mw),
                (k_w, mw), (v_w, mw)])
    u, f_arr = _norm_proj(x2, l0_norm, w0, w_extra=_pack([(f_w, N_HEADS)], multiple=LANES),
                          extra_dtype=F32)
    u3 = u.reshape(b, s, -1)
    base = (mw + 2 * MEM_WIDTH) // LANES
    c_t = _fox_gate(f_arr.reshape(b, s, LANES), l0_forget_bias, 0)
    y = _fox_attn(u3, c_t, q_blk=base // HEADS_PER_STEP, k_blk=(base + N_HEADS) // HEADS_PER_STEP,
                  v_blk=(base + 2 * N_HEADS) // HEADS_PER_STEP)
    x2 = finish(y.reshape(m, mw), u, x2, l0_mem_norm, l0_w_mem_kv, l0_w_out, wy=mw, zy_blk=0,
                zm_blk=mw // MEM_WIDTH, qm_blk=mw // MEM_WIDTH + 1)

    iw = IDX_HEADS * IDX_DIM
    q_w, k_w, v_w, qi_w, ki_w, wi_w, qm_w, z_w = _cols(
        l1_w_in, (mw, HEAD_DIM, HEAD_DIM, iw, IDX_DIM, IDX_HEADS, MEM_WIDTH, mw + MEM_WIDTH))
    w1 = _pack([(z_w[:, :mw], mw), (q_w, mw), (qi_w, iw), (z_w[:, mw:], MEM_WIDTH),
                (qm_w, MEM_WIDTH), (k_w, HEAD_DIM), (v_w, HEAD_DIM), (ki_w, IDX_DIM),
                (wi_w, IDX_HEADS), (None, LANES - IDX_DIM - IDX_HEADS)])
    u = _norm_proj(x2, l1_norm, w1)
    u3 = u.reshape(b, s, -1)
    kv0 = (2 * mw + iw + 2 * MEM_WIDTH) // LANES
    y = _dsa_attn(u3, tabs128 + tabs64, q_blk=1, qi_blk=2 * mw // iw, k_blk=kv0, v_blk=kv0 + 1,
                  misc_blk=kv0 + 2)
    zoff = (2 * mw + iw) // MEM_WIDTH
    x2 = finish(y.reshape(m, mw), u, x2, l1_mem_norm, l1_w_mem_kv, l1_w_out, wy=mw, zy_blk=0,
                zm_blk=zoff, qm_blk=zoff + 1)

    gw = len(DILATED_PAIRS) * DIL_WIDTH
    q_w, k_w, v_w, qm_w, z_w = _cols(l2_w_in, (gw, gw, gw, MEM_WIDTH, DIL_WIDTH + MEM_WIDTH))

    def grp(w, g):
        return (w[:, g * DIL_WIDTH:(g + 1) * DIL_WIDTH], DIL_WIDTH)

    wa = _pack([grp(q_w, 0), (z_w[:, :DIL_WIDTH], DIL_WIDTH), grp(k_w, 0), grp(v_w, 0),
                (z_w[:, DIL_WIDTH:], MEM_WIDTH), (qm_w, MEM_WIDTH)])
    wb = _pack([grp(q_w, 1), grp(k_w, 1), grp(v_w, 1), grp(q_w, 2), grp(k_w, 2), grp(v_w, 2)])
    u = _norm_proj(x2, l2_norm, wa)
    ub = _norm_proj(x2, l2_norm, wb, out_dtype=F32)
    y = _dil_attn(u.reshape(b, s, -1), ub.reshape(b, s, -1), tabs128, q0_blk=0,
                  k0_blk=2 * DIL_HEADS, v0_blk=3 * DIL_HEADS)
    zoff = 4 * DIL_WIDTH // MEM_WIDTH
    x2 = finish(y.reshape(m, DIL_WIDTH), u, x2, l2_mem_norm, l2_w_mem_kv, l2_w_out, wy=DIL_WIDTH,
                zy_blk=1, zm_blk=zoff, qm_blk=zoff + 1)

    q_lora, kv_lora = l3_w_uq.shape[0], l3_w_ukv.shape[0]
    cq_w, ckv_w, kr_w, qm_w, z_w = _cols(l3_w_in, (q_lora, kv_lora, QK_ROPE, MEM_WIDTH,
                                                  mw + MEM_WIDTH))
    assert q_lora == MEM_WIDTH and kv_lora == MEM_WIDTH
    w3 = _pack([(z_w[:, :mw], mw), (z_w[:, mw:], MEM_WIDTH), (qm_w, MEM_WIDTH), (cq_w, q_lora),
                (ckv_w, kv_lora)])
    u, kr_arr = _norm_proj(x2, l3_norm, w3, w_extra=_pack([(kr_w, QK_ROPE)], multiple=LANES))
    cq_blk = (mw + 2 * MEM_WIDTH) // MEM_WIDTH
    uq = l3_w_uq.reshape(q_lora, N_HEADS, QK_NOPE + QK_ROPE)
    w_uq = _pack([(uq[:, :, :QK_NOPE].reshape(q_lora, -1), N_HEADS * QK_NOPE),
                  (uq[:, :, QK_NOPE:].reshape(q_lora, -1), N_HEADS * QK_ROPE)])
    ukv = l3_w_ukv.reshape(kv_lora, N_HEADS, QK_NOPE + HEAD_DIM)
    w_ukv = _pack([(ukv[:, :, :QK_NOPE].reshape(kv_lora, -1), N_HEADS * QK_NOPE),
                   (ukv[:, :, QK_NOPE:].reshape(kv_lora, -1), N_HEADS * HEAD_DIM)])
    qf = _norm_proj(u, l3_q_norm, w_uq, x_col_block=cq_blk)
    kvf = _norm_proj(u, l3_kv_norm, w_ukv, x_col_block=cq_blk + 1)
    y = _mla_attn(qf.reshape(b, s, -1), kvf.reshape(b, s, -1), kr_arr.reshape(b, s, LANES), tabs64,
                  kr_blk=0)
    x2 = finish(y.reshape(m, mw), u, x2, l3_mem_norm, l3_w_mem_kv, l3_w_out, wy=mw, zy_blk=0,
                zm_blk=mw // MEM_WIDTH, qm_blk=mw // MEM_WIDTH + 1, final_g=final_norm)
    return x2.reshape(b, s, d)
```

```python
import functools

import jax
import jax.numpy as jnp
from jax import lax
from jax.experimental import pallas as pl
from jax.experimental.pallas import tpu as pltpu

F32 = jnp.float32
BF16 = jnp.bfloat16
I32 = jnp.int32

EPS = 1e-6
ROPE_THETA = 10000.0
HEAD_DIM = 128
N_HEADS = 16
MEM_HEADS = 4
MEM_WIDTH = MEM_HEADS * HEAD_DIM
IDX_HEADS = 16
IDX_DIM = 64
TOPK_MAX = 256
DILATED_PAIRS = ((128, 1), (512, 4), (2048, 16))
DIL_HEADS = 6
DIL_WIDTH = DIL_HEADS * HEAD_DIM
QK_NOPE = 128
QK_ROPE = 64
LANES = 128
NEG = -1e30
INT_MIN = -(2 ** 31)
VMEM_LIMIT = 56 * 1024 * 1024

NT_DIMS = (((1,), (1,)), ((), ()))
LOG2E = 1.4426950408889634


def _cparams(n_axes):
    return pltpu.CompilerParams(
        dimension_semantics=("arbitrary",) * n_axes, vmem_limit_bytes=VMEM_LIMIT)


def _rope128(x, cos_f, sin_s):
    return x * cos_f + pltpu.roll(x, 64, 1) * sin_s


def _rope64(x, cos_f, sin_s):
    lane = lax.broadcasted_iota(I32, x.shape, 1)
    partner = jnp.where((lane & 32) == 0, pltpu.roll(x, 96, 1), pltpu.roll(x, 32, 1))
    return x * cos_f + partner * sin_s


def _rope_table_kernel(pos_ref, inv_ref, sgn_ref, cos_ref, sin_ref):
    ang = pos_ref[...] * inv_ref[...]
    cos_ref[...] = jnp.cos(ang)
    sin_ref[...] = jnp.sin(ang) * sgn_ref[...]


def _rope_tables(pos_b, dh):
    m = pos_b.shape[0]
    half = dh // 2
    inv = jnp.power(ROPE_THETA, -jnp.arange(half, dtype=F32) * 2.0 / dh)
    reps = LANES // half
    inv_l = jnp.tile(inv, reps).reshape(1, LANES)
    sgn = jnp.tile(jnp.concatenate([-jnp.ones((half,), F32), jnp.ones((half,), F32)]),
                   reps // 2).reshape(1, LANES)
    ts = min(512, m)
    return pl.pallas_call(
        _rope_table_kernel,
        grid=(m // ts,),
        in_specs=[pl.BlockSpec((ts, LANES), lambda i: (i, 0)),
                  pl.BlockSpec((1, LANES), lambda i: (0, 0)),
                  pl.BlockSpec((1, LANES), lambda i: (0, 0))],
        out_specs=[pl.BlockSpec((ts, LANES), lambda i: (i, 0))] * 2,
        out_shape=[jax.ShapeDtypeStruct((m, LANES), F32)] * 2,
        compiler_params=_cparams(1),
        name="rope_tables",
    )(pos_b, inv_l, sgn)


def _norm_proj_kernel(x_ref, g_ref, w_ref, *rest, has_extra):
    if has_extra:
        wx_ref, o_ref, ox_ref, h_ref = rest
    else:
        o_ref, h_ref = rest

    first = pl.program_id(1) == 0

    @pl.when(first)
    def _():
        tm = o_ref.shape[0]
        parts = 2 if tm % 32 == 0 else 1
        for part in range(parts):
            rows = pl.ds(part * (tm // parts), tm // parts)
            x = x_ref[rows, :].astype(F32)
            ms = jnp.mean(x * x, axis=-1, keepdims=True)
            h_ref[rows, :] = (x * lax.rsqrt(ms + EPS) * g_ref[...]).astype(BF16)
            o_ref[rows, :] = jnp.dot(h_ref[rows, :], w_ref[...],
                                     preferred_element_type=F32).astype(o_ref.dtype)
            if has_extra:
                ox_ref[rows, :] = jnp.dot(h_ref[rows, :], wx_ref[...],
                                          preferred_element_type=F32).astype(ox_ref.dtype)

    @pl.when(jnp.logical_not(first))
    def _():
        o_ref[...] = jnp.dot(h_ref[...], w_ref[...],
                             preferred_element_type=F32).astype(o_ref.dtype)


def _norm_proj(x, g, w, *, x_col_block=0, out_dtype=BF16, w_extra=None, extra_dtype=BF16):
    m = x.shape[0]
    k, n = w.shape
    tm = min(1024, m)
    tn = 1024 if n % 1024 == 0 else 512
    assert m % tm == 0 and n % tn == 0
    has_extra = w_extra is not None
    in_specs = [pl.BlockSpec((tm, k), lambda i, j: (i, x_col_block)),
                pl.BlockSpec((1, k), lambda i, j: (0, 0)),
                pl.BlockSpec((k, tn), lambda i, j: (0, j))]
    out_specs = [pl.BlockSpec((tm, tn), lambda i, j: (i, j))]
    out_shape = [jax.ShapeDtypeStruct((m, n), out_dtype)]
    args = [x, g.reshape(1, k).astype(F32), w]
    if has_extra:
        in_specs.append(pl.BlockSpec((k, LANES), lambda i, j: (0, 0)))
        out_specs.append(pl.BlockSpec((tm, LANES), lambda i, j: (i, 0)))
        out_shape.append(jax.ShapeDtypeStruct((m, LANES), extra_dtype))
        args.append(w_extra)
    outs = pl.pallas_call(
        functools.partial(_norm_proj_kernel, has_extra=has_extra),
        grid=(m // tm, n // tn),
        in_specs=in_specs,
        out_specs=out_specs,
        out_shape=out_shape,
        scratch_shapes=[pltpu.VMEM((tm, k), BF16)],
        compiler_params=_cparams(2),
        name="norm_proj",
    )(*args)
    return outs if has_extra else outs[0]


def _out_proj_kernel(y_ref, zy_ref, zm_ref, qm_ref, mkv_ref, x_ref, w_ref, *rest, wy, final):
    if final:
        gf_ref, o_ref, gated_ref = rest
    else:
        o_ref, gated_ref = rest
    scale = HEAD_DIM ** -0.5
    for h in range(MEM_HEADS):
        lo, hi = h * HEAD_DIM, (h + 1) * HEAD_DIM
        s = lax.dot_general(qm_ref[:, lo:hi], mkv_ref[:, lo:hi], NT_DIMS,
                            preferred_element_type=F32) * scale
        m = jnp.max(s, axis=-1, keepdims=True)
        p = jnp.exp(s - m)
        l = jnp.sum(p, axis=-1, keepdims=True)
        o = jnp.dot(p.astype(BF16), mkv_ref[:, MEM_WIDTH + lo:MEM_WIDTH + hi],
                    preferred_element_type=F32) / l
        zm = zm_ref[:, lo:hi].astype(F32)
        gated_ref[:, wy + lo:wy + hi] = (o * (zm * jax.nn.sigmoid(zm))).astype(BF16)
    tm = o_ref.shape[0]
    parts = 2 if tm % 32 == 0 else 1
    for part in range(parts):
        rows = pl.ds(part * (tm // parts), tm // parts)
        zy = zy_ref[rows, :].astype(F32)
        gated_ref[rows, :wy] = (y_ref[rows, :].astype(F32)
                                * (zy * jax.nn.sigmoid(zy))).astype(BF16)
        out = x_ref[rows, :] + jnp.dot(gated_ref[rows, :], w_ref[...],
                                       preferred_element_type=F32)
        if final:
            ms = jnp.mean(out * out, axis=-1, keepdims=True)
            out = out * lax.rsqrt(ms + EPS) * gf_ref[...]
        o_ref[rows, :] = out


def _out_proj(y, u, mkv, x, w_out, *, wy, zy_blk, zm_blk, qm_blk, seq, n_mem, final_g=None):
    m, d = x.shape
    tm = min(512, seq)
    final = final_g is not None
    in_specs = [
        pl.BlockSpec((tm, wy), lambda i: (i, 0)),
        pl.BlockSpec((tm, wy), lambda i: (i, zy_blk)),
        pl.BlockSpec((tm, MEM_WIDTH), lambda i: (i, zm_blk)),
        pl.BlockSpec((tm, MEM_WIDTH), lambda i: (i, qm_blk)),
        pl.BlockSpec((n_mem, 2 * MEM_WIDTH), lambda i: ((i * tm) // seq, 0)),
        pl.BlockSpec((tm, d), lambda i: (i, 0)),
        pl.BlockSpec((wy + MEM_WIDTH, d), lambda i: (0, 0), pipeline_mode=pl.Buffered(1)),
    ]
    args = [y, u, u, u, mkv, x, w_out]
    if final:
        in_specs.append(pl.BlockSpec((1, d), lambda i: (0, 0)))
        args.append(final_g.reshape(1, d).astype(F32))
    return pl.pallas_call(
        functools.partial(_out_proj_kernel, wy=wy, final=final),
        grid=(m // tm,),
        in_specs=in_specs,
        out_specs=pl.BlockSpec((tm, d), lambda i: (i, 0)),
        out_shape=jax.ShapeDtypeStruct((m, d), F32),
        scratch_shapes=[pltpu.VMEM((tm, wy + MEM_WIDTH), BF16)],
        compiler_params=_cparams(1),
        name="out_proj",
    )(*args)


def _fox_gate_kernel(f_ref, b_ref, c_ref, carry_ref):
    @pl.when(pl.program_id(1) == 0)
    def _():
        carry_ref[...] = jnp.zeros_like(carry_ref)

    f_t = f_ref[...].astype(F32).T[:N_HEADS, :] + b_ref[...]
    log_f = jnp.minimum(f_t, 0.0) - jnp.log(1.0 + jnp.exp(-jnp.abs(f_t)))
    ts = log_f.shape[1]
    r = lax.broadcasted_iota(I32, (ts, ts), 0)
    c = lax.broadcasted_iota(I32, (ts, ts), 1)
    upper = jnp.where(r <= c, 1.0, 0.0).astype(F32)
    cs = jnp.dot(log_f, upper, precision=lax.Precision.HIGHEST,
                 preferred_element_type=F32) + carry_ref[...]
    c_ref[...] = cs * LOG2E
    carry_ref[...] = cs[:, ts - 1:ts]


def _fox_gate(u3, bias, f_blk):
    b, s, _ = u3.shape
    ts = min(256, s)
    return pl.pallas_call(
        _fox_gate_kernel,
        grid=(b, s // ts),
        in_specs=[pl.BlockSpec((None, ts, LANES), lambda bi, j: (bi, j, f_blk)),
                  pl.BlockSpec((N_HEADS, 1), lambda bi, j: (0, 0))],
        out_specs=pl.BlockSpec((None, N_HEADS, ts), lambda bi, j: (bi, 0, j)),
        out_shape=jax.ShapeDtypeStruct((b, N_HEADS, s), F32),
        scratch_shapes=[pltpu.VMEM((N_HEADS, 1), F32)],
        compiler_params=_cparams(2),
        name="fox_gate",
    )(u3, bias.reshape(N_HEADS, 1).astype(F32))


STRIP = 32
HEADS_PER_STEP = 4
FLASH_HEADS = 8
FLASH_TQ = 256


def _softmax_strips(t_ref, p_ref, m_ref, l_ref, alpha_ref, adjust):
    tq, tk = t_ref.shape
    for r0 in range(0, tq, min(STRIP, tq)):
        rs = pl.ds(r0, min(STRIP, tq))
        blocks = [adjust(r0, c, t_ref[rs, c * LANES:(c + 1) * LANES]) for c in range(tk // LANES)]
        live = [x for x in blocks if x is not None]
        mx = live[0]
        for x in live[1:]:
            mx = jnp.maximum(mx, x)
        m_old = m_ref[rs, :]
        m_new = jnp.maximum(m_old, jnp.max(mx, axis=-1, keepdims=True))
        alpha = jnp.exp2(m_old - m_new)
        ps = [None if x is None else jnp.exp2(x - m_new) for x in blocks]
        live = [x for x in ps if x is not None]
        sm = live[0]
        for x in live[1:]:
            sm = sm + x
        l_ref[rs, :] = alpha * l_ref[rs, :] + jnp.sum(sm, axis=-1, keepdims=True)
        m_ref[rs, :] = m_new
        alpha_ref[rs, :] = alpha
        for c, x in enumerate(ps):
            p_ref[rs, c * LANES:(c + 1) * LANES] = (
                jnp.zeros((min(STRIP, tq), LANES), BF16) if x is None else x.astype(BF16))


def _flash_scratch(n_streams, tq):
    return [pltpu.VMEM((n_streams, tq, tq), F32), pltpu.VMEM((n_streams, tq, tq), BF16),
            pltpu.VMEM((n_streams, tq, LANES), F32), pltpu.VMEM((n_streams, tq, LANES), F32),
            pltpu.VMEM((n_streams, tq, LANES), F32), pltpu.VMEM((n_streams, tq, HEAD_DIM), F32)]


def _flash_init(scratch):
    _, _, m_ref, l_ref, _, acc_ref = scratch
    m_ref[...] = jnp.full(m_ref.shape, NEG, F32)
    l_ref[...] = jnp.zeros(l_ref.shape, F32)
    acc_ref[...] = jnp.zeros(acc_ref.shape, F32)


def _flash_chunk(n_streams, j, raw_scores, values, to_logits, scratch, *, diagonal):
    t_ref, p_ref, m_ref, l_ref, alpha_ref, acc_ref = scratch
    strip = min(STRIP, t_ref.shape[1])
    row = lax.broadcasted_iota(I32, (strip, LANES), 0)
    col = lax.broadcasted_iota(I32, (strip, LANES), 1)
    for a in range(n_streams):
        t_ref[a] = raw_scores(a, j)
    for a in range(n_streams):
        def adjust(r0, c, x, a=a):
            t = to_logits(a, j, r0, c, x)
            if not diagonal or c * LANES + LANES - 1 <= r0:
                return t
            if c * LANES > r0 + strip - 1:
                return None
            return jnp.where(col + c * LANES <= row + r0, t, NEG)

        _softmax_strips(t_ref.at[a], p_ref.at[a], m_ref.at[a], l_ref.at[a], alpha_ref.at[a],
                        adjust)
    for a in range(n_streams):
        acc_ref[a] = alpha_ref[a] * acc_ref[a] + jnp.dot(p_ref[a], values(a, j),
                                                         preferred_element_type=F32)


def _flash_finish(n_streams, scratch):
    _, _, _, l_ref, _, acc_ref = scratch
    return [acc_ref[a] / l_ref[a] for a in range(n_streams)]


def _causal_flash(i, n_streams, raw_scores, values, to_logits, scratch):
    _flash_init(scratch)

    def body(j, _):
        _flash_chunk(n_streams, j, raw_scores, values, to_logits, scratch, diagonal=False)
        return 0

    lax.fori_loop(0, i, body, 0)
    _flash_chunk(n_streams, i, raw_scores, values, to_logits, scratch, diagonal=True)
    return _flash_finish(n_streams, scratch)


def _fox_attn_kernel(q_ref, k_ref, v_ref, c_ref, o_ref, qs_ref, *scratch, tq):
    i = pl.program_id(2)
    qk_scale = HEAD_DIM ** -0.5 * LOG2E
    qs_ref[...] = (q_ref[...].astype(F32) * qk_scale).astype(BF16)

    def raw_scores(a, j):
        off = pl.multiple_of(j * tq, tq)
        lo, hi = a * HEAD_DIM, (a + 1) * HEAD_DIM
        return lax.dot_general(qs_ref[:, lo:hi], k_ref[pl.ds(off, tq), lo:hi], NT_DIMS,
                               preferred_element_type=F32)

    def to_logits(a, j, r0, c, x):
        return x - c_ref[a, j, :, c * LANES:(c + 1) * LANES]

    def values(a, j):
        off = pl.multiple_of(j * tq, tq)
        return v_ref[pl.ds(off, tq), a * HEAD_DIM:(a + 1) * HEAD_DIM]

    outs = _causal_flash(i, FLASH_HEADS, raw_scores, values, to_logits, scratch)
    for a in range(FLASH_HEADS):
        o_ref[:, a * HEAD_DIM:(a + 1) * HEAD_DIM] = outs[a].astype(o_ref.dtype)


def _fox_attn(u3, c_t, *, q_blk, k_blk, v_blk):
    b, s, _ = u3.shape
    tq = min(FLASH_TQ, s)
    nq = s // tq
    pw = FLASH_HEADS * HEAD_DIM
    c5 = c_t.reshape(b, N_HEADS, nq, 1, tq)
    return pl.pallas_call(
        functools.partial(_fox_attn_kernel, tq=tq),
        grid=(b, N_HEADS // FLASH_HEADS, nq),
        in_specs=[pl.BlockSpec((None, tq, pw), lambda bi, p, i: (bi, i, q_blk + p)),
                  pl.BlockSpec((None, s, pw), lambda bi, p, i: (bi, 0, k_blk + p)),
                  pl.BlockSpec((None, s, pw), lambda bi, p, i: (bi, 0, v_blk + p)),
                  pl.BlockSpec((None, FLASH_HEADS, nq, 1, tq), lambda bi, p, i: (bi, p, 0, 0, 0))],
        out_specs=pl.BlockSpec((None, tq, pw), lambda bi, p, i: (bi, i, p)),
        out_shape=jax.ShapeDtypeStruct((b, s, N_HEADS * HEAD_DIM), BF16),
        scratch_shapes=[pltpu.VMEM((tq, pw), BF16)] + _flash_scratch(FLASH_HEADS, tq),
        compiler_params=_cparams(3),
        name="fox_attn",
    )(u3, u3, u3, c5)


def _dsa_kprep_kernel(k_ref, mk_ref, c128_ref, s128_ref, c64_ref, s64_ref,
                      kr_ref, klo_ref, khi_ref):
    kr_ref[...] = _rope128(k_ref[...].astype(F32), c128_ref[...], s128_ref[...]).astype(BF16)
    ki = _rope64(mk_ref[...].astype(F32), c64_ref[...], s64_ref[...])
    lane = lax.broadcasted_iota(I32, ki.shape, 1)
    lo = jnp.where(lane < IDX_DIM, ki, 0.0)
    klo_ref[...] = lo.astype(BF16)
    khi_ref[...] = pltpu.roll(lo, IDX_DIM, 1).astype(BF16)


def _dsa_kprep(u3, tabs, *, k_blk, misc_blk):
    b, s, _ = u3.shape
    ts = min(512, s)
    c128, s128, c64, s64 = tabs

    def spec(blk):
        return pl.BlockSpec((None, ts, LANES), lambda bi, i: (bi, i, blk))

    return pl.pallas_call(
        _dsa_kprep_kernel,
        grid=(b, s // ts),
        in_specs=[spec(k_blk), spec(misc_blk), spec(0), spec(0), spec(0), spec(0)],
        out_specs=[spec(0)] * 3,
        out_shape=[jax.ShapeDtypeStruct((b, s, LANES), BF16)] * 3,
        compiler_params=_cparams(2),
        name="dsa_kprep",
    )(u3, u3, c128, s128, c64, s64)


def _dsa_kernel(q_ref, qi_ref, mq_ref, v_ref, kr_ref, klo_ref, khi_ref,
                cq128_ref, sq128_ref, cq64_ref, sq64_ref, y_in_ref,
                o_ref,
                qr_ref, qir_ref, w_ref, sc_ref, t4_ref, yo_ref, *flash,
                tq, kl, q_off, n_sel):
    del y_in_ref
    t0 = q_off + pl.program_id(1) * tq
    qk_scale = HEAD_DIM ** -0.5 * LOG2E
    idx_scale = (IDX_DIM ** -0.5) * (IDX_HEADS ** -0.5)
    ck = min(512, kl)
    n_chunks = kl // ck
    strip = min(STRIP, tq)

    cq128, sq128 = cq128_ref[...], sq128_ref[...]
    for h in range(N_HEADS):
        qh = q_ref[:, h * HEAD_DIM:(h + 1) * HEAD_DIM].astype(F32)
        qr_ref[h] = (_rope128(qh, cq128, sq128) * qk_scale).astype(BF16)
    cq64, sq64 = cq64_ref[...], sq64_ref[...]
    for a in range(IDX_HEADS // 2):
        qir_ref[a] = _rope64(qi_ref[:, a * LANES:(a + 1) * LANES].astype(F32),
                             cq64, sq64).astype(BF16)
    w_ref[...] = mq_ref[...].astype(F32) * idx_scale

    row = lax.broadcasted_iota(I32, (strip, LANES), 0)
    col = lax.broadcasted_iota(I32, (strip, LANES), 1)
    group = 4

    def idx_chunk(j, _):
        off = pl.multiple_of(j * ck, ck)
        k_lo, k_hi = klo_ref[pl.ds(off, ck), :], khi_ref[pl.ds(off, ck), :]
        for g in range(IDX_HEADS // group):
            for a2 in range(group // 2):
                x = qir_ref[g * (group // 2) + a2]
                t4_ref[2 * a2] = lax.dot_general(x, k_lo, NT_DIMS, preferred_element_type=F32)
                t4_ref[2 * a2 + 1] = lax.dot_general(x, k_hi, NT_DIMS,
                                                     preferred_element_type=F32)
            last = g == IDX_HEADS // group - 1
            for r0 in range(0, tq, strip):
                rs = pl.ds(r0, strip)
                w_rows = w_ref[rs, :]
                ws = [jnp.broadcast_to(w_rows[:, IDX_DIM + g * group + u:IDX_DIM + g * group + u + 1],
                                       (strip, LANES)) for u in range(group)]
                for c in range(ck // LANES):
                    cs = pl.ds(c * LANES, LANES)
                    acc = ws[0] * jnp.maximum(t4_ref[0, rs, cs], 0.0)
                    for u in range(1, group):
                        acc = acc + ws[u] * jnp.maximum(t4_ref[u, rs, cs], 0.0)
                    if g > 0:
                        acc = acc + sc_ref[j, rs, cs]
                    if last:
                        causal = col + (off + c * LANES) <= row + (t0 + r0)
                        acc = jnp.where(causal, acc, -jnp.inf)
                    sc_ref[j, rs, cs] = acc
        return 0

    lax.fori_loop(0, n_chunks, idx_chunk, 0)

    shape3 = (n_chunks, tq, ck)
    col3 = lax.broadcasted_iota(I32, shape3, 0) * ck + lax.broadcasted_iota(I32, shape3, 2)
    causal3 = col3 <= lax.broadcasted_iota(I32, shape3, 1) + t0

    def count(x):
        return jnp.sum(jnp.sum(x, axis=0), axis=-1, keepdims=True)

    def key_to_f32(key):
        return pltpu.bitcast(jnp.where(key < 0, key ^ jnp.int32(0x7FFFFFFF), key), F32)

    def thr_body(it, key):
        cand = key + (jnp.int32(1) << (31 - it))
        c = count(jnp.where(sc_ref[...] >= key_to_f32(cand), 1.0, 0.0))
        return jnp.where(c >= n_sel, cand, key)

    thr_key = lax.fori_loop(0, 32, thr_body, jnp.full((tq, 1), INT_MIN, I32))
    take_all = thr_key == INT_MIN
    thr = key_to_f32(thr_key)
    score = sc_ref[...]
    need = n_sel - count(jnp.where(score > thr, 1.0, 0.0))
    surplus = jnp.where(take_all, 0.0, count(jnp.where(score == thr, 1.0, 0.0)) - need)
    has_surplus = jnp.max(surplus) > 0.0

    n_bits = kl.bit_length()

    def tie_body(it, jm):
        cand = jm + (jnp.int32(1) << (n_bits - 1 - it))
        f = count(jnp.where(sc_ref[...] == thr, jnp.where(col3 < cand, 1.0, 0.0), 0.0))
        return jnp.where(jnp.logical_and(f < need, cand <= kl), cand, jm)

    jm = lax.fori_loop(0, jnp.where(has_surplus, n_bits, 0), tie_body,
                       jnp.broadcast_to(jnp.where(has_surplus, 0, kl), (tq, 1)).astype(I32))
    keep = jnp.where(jnp.logical_or(score > thr, take_all), 1.0,
                     jnp.where(score == thr, jnp.where(col3 <= jm, 1.0, 0.0), 0.0))
    sc_ref[...] = jnp.where(causal3, jnp.where(keep > 0.0, 0.0, NEG), NEG)

    def raw_scores(h0):
        def f(a, j):
            return lax.dot_general(qr_ref[h0 + a], kr_ref[j * ck:(j + 1) * ck, :], NT_DIMS,
                                   preferred_element_type=F32)
        return f

    def values(a, j):
        return v_ref[j * ck:(j + 1) * ck, :]

    def to_logits(a, j, r0, c, x):
        return x + sc_ref[j, r0:r0 + strip, c * LANES:(c + 1) * LANES]

    def group_body(pp, _):
        h0 = HEADS_PER_STEP * pp
        _flash_init(flash)
        for j in range(n_chunks):
            _flash_chunk(HEADS_PER_STEP, j, raw_scores(h0), values, to_logits, flash,
                         diagonal=False)
        outs = _flash_finish(HEADS_PER_STEP, flash)
        for a in range(HEADS_PER_STEP):
            yo_ref[h0 + a] = outs[a].astype(yo_ref.dtype)
        return 0

    lax.fori_loop(0, N_HEADS // HEADS_PER_STEP, group_body, 0)
    for h in range(N_HEADS):
        o_ref[:, h * HEAD_DIM:(h + 1) * HEAD_DIM] = yo_ref[h]


def _dsa_group(u3, kprep, tabs, y, *, q_off, rows, kl, n_sel, q_blk, v_blk, qi_blk, misc_blk):
    b, s, _ = u3.shape
    tq = rows
    assert kl % min(512, kl) == 0 and tq == min(512, kl)
    qb0 = q_off // tq
    c128, s128, c64, s64 = tabs
    kr, klo, khi = kprep
    qw = N_HEADS * HEAD_DIM
    iw = IDX_HEADS * IDX_DIM
    ck = min(512, kl)

    def qspec(width, blk):
        return pl.BlockSpec((None, tq, width), lambda bi, i: (bi, qb0 + i, blk))

    def kspec(blk):
        return pl.BlockSpec((None, kl, LANES), lambda bi, i: (bi, 0, blk))

    return pl.pallas_call(
        functools.partial(_dsa_kernel, tq=tq, kl=kl, q_off=q_off, n_sel=n_sel),
        grid=(b, rows // tq),
        in_specs=[qspec(qw, q_blk), qspec(iw, qi_blk), qspec(LANES, misc_blk),
                  kspec(v_blk), kspec(0), kspec(0), kspec(0),
                  qspec(LANES, 0), qspec(LANES, 0), qspec(LANES, 0), qspec(LANES, 0),
                  pl.BlockSpec(memory_space=pl.ANY)],
        out_specs=pl.BlockSpec((None, tq, qw), lambda bi, i: (bi, qb0 + i, 0)),
        out_shape=jax.ShapeDtypeStruct((b, s, qw), BF16),
        input_output_aliases={11: 0},
        scratch_shapes=[pltpu.VMEM((N_HEADS, tq, HEAD_DIM), BF16),
                        pltpu.VMEM((IDX_HEADS // 2, tq, LANES), BF16),
                        pltpu.VMEM((tq, LANES), F32),
                        pltpu.VMEM((kl // ck, tq, ck), F32),
                        pltpu.VMEM((4, tq, ck), F32),
                        pltpu.VMEM((N_HEADS, tq, HEAD_DIM), BF16)]
        + _flash_scratch(HEADS_PER_STEP, tq),
        compiler_params=_cparams(2),
        name="dsa_attn",
    )(u3, u3, u3, u3, kr, klo, khi, c128, s128, c64, s64, y)


def _dsa_attn(u3, tabs, *, k_blk, misc_blk, **blks):
    b, s, _ = u3.shape
    n_sel = min(TOPK_MAX, s // 4)
    rows = min(512, s)
    kprep = _dsa_kprep(u3, tabs, k_blk=k_blk, misc_blk=misc_blk)
    y = jnp.zeros((b, s, N_HEADS * HEAD_DIM), BF16)
    for q_off in range(0, s, rows):
        y = _dsa_group(u3, kprep, tabs, y, q_off=q_off, rows=rows, kl=q_off + rows, n_sel=n_sel,
                       misc_blk=misc_blk, **blks)
    return y


def _dil_kernel(q0_ref, k0_ref, v0_ref, q1_ref, k1_ref, v1_ref, q2_ref, k2_ref, v2_ref,
                cos_ref, sin_ref, y_ref, qr_ref, kr_ref, o_ref, lse_ref, t_ref, p_ref, *, seq):
    qk_scale = HEAD_DIM ** -0.5 * LOG2E
    cos_f, sin_s = cos_ref[...], sin_ref[...]
    groups = ((q0_ref, k0_ref, v0_ref), (q1_ref, k1_ref, v1_ref), (q2_ref, k2_ref, v2_ref))
    for g, (q_ref, k_ref, _) in enumerate(groups):
        qr_ref[g] = _rope128(q_ref[...].astype(F32), cos_f, sin_s) * qk_scale
        kr_ref[g] = _rope128(k_ref[...].astype(F32), cos_f, sin_s)

    for g, (window, dil) in enumerate(DILATED_PAIRS):
        v_ref = groups[g][2]
        sub = seq // dil
        qb = min(window // dil, sub)
        tiles = [(r, i) for r in range(dil) for i in range(sub // qb)]

        def rows(r, blk, dil=dil, qb=qb):
            start = r + dil * qb * blk
            return pl.ds(start, qb) if dil == 1 else pl.ds(start, qb, stride=dil)

        for n, (r, i) in enumerate(tiles):
            q = qr_ref[g, rows(r, i), :].astype(BF16)
            k_cur = kr_ref[g, rows(r, i), :].astype(BF16)
            t_ref[n, :qb, qb:2 * qb] = lax.dot_general(q, k_cur, NT_DIMS,
                                                       preferred_element_type=F32)
            if i > 0:
                k_prev = kr_ref[g, rows(r, i - 1), :].astype(BF16)
                t_ref[n, :qb, :qb] = lax.dot_general(q, k_prev, NT_DIMS,
                                                     preferred_element_type=F32)
        row = lax.broadcasted_iota(I32, (qb, qb), 0)
        col = lax.broadcasted_iota(I32, (qb, qb), 1)
        for n, (r, i) in enumerate(tiles):
            t_cur = jnp.where(col <= row, t_ref[n, :qb, qb:2 * qb], NEG)
            m = jnp.max(t_cur, axis=-1, keepdims=True)
            if i > 0:
                t_prev = jnp.where(col >= row, t_ref[n, :qb, :qb], NEG)
                m = jnp.maximum(m, jnp.max(t_prev, axis=-1, keepdims=True))
            p_cur = jnp.exp2(t_cur - m)
            l = jnp.sum(p_cur, axis=-1, keepdims=True)
            p_ref[n, :qb, qb:2 * qb] = p_cur.astype(BF16)
            if i > 0:
                p_prev = jnp.exp2(t_prev - m)
                l = l + jnp.sum(p_prev, axis=-1, keepdims=True)
                p_ref[n, :qb, :qb] = p_prev.astype(BF16)
            t_ref[n, :qb, :LANES] = jnp.broadcast_to(l, (qb, LANES))
            t_ref[n, :qb, LANES:2 * LANES] = jnp.broadcast_to(m, (qb, LANES))
        for n, (r, i) in enumerate(tiles):
            acc = jnp.dot(p_ref[n, :qb, qb:2 * qb], v_ref[rows(r, i), :].astype(BF16),
                          preferred_element_type=F32)
            if i > 0:
                acc = acc + jnp.dot(p_ref[n, :qb, :qb], v_ref[rows(r, i - 1), :].astype(BF16),
                                    preferred_element_type=F32)
            l = t_ref[n, :qb, :LANES]
            o_ref[g, rows(r, i), :] = acc / l
            lse_ref[g, rows(r, i), :] = t_ref[n, :qb, LANES:2 * LANES] + jnp.log2(l)

    l0, l1, l2 = lse_ref[0], lse_ref[1], lse_ref[2]
    m = jnp.maximum(jnp.maximum(l0, l1), l2)
    e0, e1, e2 = jnp.exp2(l0 - m), jnp.exp2(l1 - m), jnp.exp2(l2 - m)
    y = (e0 * o_ref[0] + e1 * o_ref[1] + e2 * o_ref[2]) / (e0 + e1 + e2)
    y_ref[...] = y.astype(y_ref.dtype)


def _dil_attn(ua3, ub3, tabs128, *, q0_blk, k0_blk, v0_blk):
    b, s, _ = ua3.shape
    for window, dil in DILATED_PAIRS:
        assert s % dil == 0 and (s // dil) % min(window // dil, s // dil) == 0
    cos_t, sin_t = tabs128
    tile = min(DILATED_PAIRS[0][0], s)
    n_tiles = s // min(min(w // d, s // d) for w, d in DILATED_PAIRS)

    def spec(blk0):
        return pl.BlockSpec((None, s, HEAD_DIM), lambda bi, a: (bi, 0, blk0 + a))

    tab = pl.BlockSpec((None, s, LANES), lambda bi, a: (bi, 0, 0))
    return pl.pallas_call(
        functools.partial(_dil_kernel, seq=s),
        grid=(b, DIL_HEADS),
        in_specs=[spec(q0_blk), spec(k0_blk), spec(v0_blk)]
        + [spec(j * DIL_HEADS) for j in range(6)] + [tab, tab],
        out_specs=pl.BlockSpec((None, s, HEAD_DIM), lambda bi, a: (bi, 0, a)),
        out_shape=jax.ShapeDtypeStruct((b, s, DIL_WIDTH), BF16),
        scratch_shapes=[pltpu.VMEM((3, s, HEAD_DIM), F32)] * 4
        + [pltpu.VMEM((n_tiles, tile, 2 * LANES), F32), pltpu.VMEM((n_tiles, tile, 2 * LANES), BF16)],
        compiler_params=_cparams(2),
        name="dilated_attn",
    )(ua3, ua3, ua3, ub3, ub3, ub3, ub3, ub3, ub3, cos_t, sin_t)


def _mla_kernel(qn_ref, qr_ref, kn_ref, kr_ref, v_ref, cq_ref, sq_ref, ck_ref, sk_ref,
                o_ref, qcat_ref, kcat_ref, *scratch, tq):
    i = pl.program_id(2)
    qk_scale = (QK_NOPE + QK_ROPE) ** -0.5 * LOG2E
    cat = QK_NOPE + QK_ROPE

    @pl.when(i == 0)
    def _():
        k_rope = _rope64(kr_ref[...].astype(F32), ck_ref[...],
                         sk_ref[...])[:, :QK_ROPE].astype(BF16)
        for a in range(FLASH_HEADS):
            kcat_ref[a, :, :QK_NOPE] = kn_ref[:, a * HEAD_DIM:(a + 1) * HEAD_DIM]
            kcat_ref[a, :, QK_NOPE:cat] = k_rope
            kcat_ref[a, :, cat:] = jnp.zeros((kcat_ref.shape[1], 2 * LANES - cat), BF16)

    cos_q, sin_q = cq_ref[...], sq_ref[...]
    for a2 in range(FLASH_HEADS // 2):
        q_rope = (_rope64(qr_ref[:, a2 * LANES:(a2 + 1) * LANES].astype(F32), cos_q, sin_q)
                  * qk_scale).astype(BF16)
        for a in (2 * a2, 2 * a2 + 1):
            q_nope = qn_ref[:, a * HEAD_DIM:(a + 1) * HEAD_DIM].astype(F32) * qk_scale
            qcat_ref[a, :, :QK_NOPE] = q_nope.astype(BF16)
            qcat_ref[a, :, QK_NOPE:cat] = q_rope[:, (a % 2) * QK_ROPE:(a % 2 + 1) * QK_ROPE]
            qcat_ref[a, :, cat:] = jnp.zeros((tq, 2 * LANES - cat), BF16)

    def raw_scores(a, j):
        off = pl.multiple_of(j * tq, tq)
        return lax.dot_general(qcat_ref[a], kcat_ref[a, pl.ds(off, tq), :], NT_DIMS,
                               preferred_element_type=F32)

    def to_logits(a, j, r0, c, x):
        return x

    def values(a, j):
        off = pl.multiple_of(j * tq, tq)
        return v_ref[pl.ds(off, tq), a * HEAD_DIM:(a + 1) * HEAD_DIM]

    outs = _causal_flash(i, FLASH_HEADS, raw_scores, values, to_logits, scratch)
    for a in range(FLASH_HEADS):
        o_ref[:, a * HEAD_DIM:(a + 1) * HEAD_DIM] = outs[a].astype(o_ref.dtype)


def _mla_attn(qf3, kvf3, u3, tabs64, *, kr_blk):
    b, s, _ = qf3.shape
    tq = min(FLASH_TQ, s)
    gw = FLASH_HEADS * HEAD_DIM
    rw = FLASH_HEADS * QK_ROPE
    n_groups = N_HEADS // FLASH_HEADS
    cos_t, sin_t = tabs64
    return pl.pallas_call(
        functools.partial(_mla_kernel, tq=tq),
        grid=(b, n_groups, s // tq),
        in_specs=[pl.BlockSpec((None, tq, gw), lambda bi, p, i: (bi, i, p)),
                  pl.BlockSpec((None, tq, rw),
                               lambda bi, p, i: (bi, i, N_HEADS * QK_NOPE // rw + p)),
                  pl.BlockSpec((None, s, gw), lambda bi, p, i: (bi, 0, p)),
                  pl.BlockSpec((None, s, LANES), lambda bi, p, i: (bi, 0, kr_blk)),
                  pl.BlockSpec((None, s, gw), lambda bi, p, i: (bi, 0, n_groups + p)),
                  pl.BlockSpec((None, tq, LANES), lambda bi, p, i: (bi, i, 0)),
                  pl.BlockSpec((None, tq, LANES), lambda bi, p, i: (bi, i, 0)),
                  pl.BlockSpec((None, s, LANES), lambda bi, p, i: (bi, 0, 0)),
                  pl.BlockSpec((None, s, LANES), lambda bi, p, i: (bi, 0, 0))],
        out_specs=pl.BlockSpec((None, tq, gw), lambda bi, p, i: (bi, i, p)),
        out_shape=jax.ShapeDtypeStruct((b, s, N_HEADS * HEAD_DIM), BF16),
        scratch_shapes=[pltpu.VMEM((FLASH_HEADS, tq, 2 * LANES), BF16),
                        pltpu.VMEM((FLASH_HEADS, s, 2 * LANES), BF16)]
        + _flash_scratch(FLASH_HEADS, tq),
        compiler_params=_cparams(3),
        name="mla_attn",
    )(qf3, qf3, kvf3, u3, kvf3, cos_t, sin_t, cos_t, sin_t)


def _pack(parts, multiple=512):
    k = next(p.shape[0] for p, _ in parts if p is not None)
    cols = [jnp.zeros((k, w), BF16) if p is None else p.astype(BF16) for p, w in parts]
    n = sum(w for _, w in parts)
    pad = (-n) % multiple
    if pad:
        cols.append(jnp.zeros((k, pad), BF16))
    return jnp.concatenate(cols, axis=1)


def _cols(w, sizes):
    out, acc = [], 0
    for sz in sizes:
        out.append(w[:, acc:acc + sz])
        acc += sz
    return out


def kernel(x, mem, positions, l0_norm, l0_w_in, l0_forget_bias, l0_mem_norm, l0_w_mem_kv, l0_w_out, l1_norm, l1_w_in, l1_mem_norm, l1_w_mem_kv, l1_w_out, l2_norm, l2_w_in, l2_mem_norm, l2_w_mem_kv, l2_w_out, l3_norm, l3_w_in, l3_q_norm, l3_w_uq, l3_kv_norm, l3_w_ukv, l3_mem_norm, l3_w_mem_kv, l3_w_out, final_norm):
    b, s, d = x.shape
    n_mem = mem.shape[1]
    m = b * s
    mw = N_HEADS * HEAD_DIM
    x2 = x.reshape(m, d)
    mem2 = mem.reshape(b * n_mem, d)

    pos_b = jnp.broadcast_to(positions.astype(F32).reshape(m, 1), (m, LANES))
    tabs128 = tuple(t.reshape(b, s, LANES) for t in _rope_tables(pos_b, HEAD_DIM))
    tabs64 = tuple(t.reshape(b, s, LANES) for t in _rope_tables(pos_b, IDX_DIM))

    def mem_kv(g, w):
        return _norm_proj(mem2, g, w.astype(BF16))

    def finish(y, u, x_in, g_mem, w_mem_kv, w_out, *, wy, zy_blk, zm_blk, qm_blk, final_g=None):
        return _out_proj(y, u, mem_kv(g_mem, w_mem_kv), x_in, w_out.astype(BF16), wy=wy,
                         zy_blk=zy_blk, zm_blk=zm_blk, qm_blk=qm_blk, seq=s, n_mem=n_mem,
                         final_g=final_g)

    q_w, k_w, v_w, f_w, qm_w, z_w = _cols(l0_w_in, (mw, mw, mw, N_HEADS, MEM_WIDTH, mw + MEM_WIDTH))
    w0 = _pack([(z_w[:, :mw], mw), (z_w[:, mw:], MEM_WIDTH), (qm_w, MEM_WIDTH), (q_w, mw),
                (k_w, mw), (v_w, mw)])
    u, f_arr = _norm_proj(x2, l0_norm, w0, w_extra=_pack([(f_w, N_HEADS)], multiple=LANES),
                          extra_dtype=F32)
    u3 = u.reshape(b, s, -1)
    base = (mw + 2 * MEM_WIDTH) // LANES
    c_t = _fox_gate(f_arr.reshape(b, s, LANES), l0_forget_bias, 0)
    y = _fox_attn(u3, c_t, q_blk=base // FLASH_HEADS, k_blk=(base + N_HEADS) // FLASH_HEADS,
                  v_blk=(base + 2 * N_HEADS) // FLASH_HEADS)
    x2 = finish(y.reshape(m, mw), u, x2, l0_mem_norm, l0_w_mem_kv, l0_w_out, wy=mw, zy_blk=0,
                zm_blk=mw // MEM_WIDTH, qm_blk=mw // MEM_WIDTH + 1)

    iw = IDX_HEADS * IDX_DIM
    q_w, k_w, v_w, qi_w, ki_w, wi_w, qm_w, z_w = _cols(
        l1_w_in, (mw, HEAD_DIM, HEAD_DIM, iw, IDX_DIM, IDX_HEADS, MEM_WIDTH, mw + MEM_WIDTH))
    w1 = _pack([(z_w[:, :mw], mw), (q_w, mw), (qi_w, iw), (z_w[:, mw:], MEM_WIDTH),
                (qm_w, MEM_WIDTH), (k_w, HEAD_DIM), (v_w, HEAD_DIM), (ki_w, IDX_DIM),
                (wi_w, IDX_HEADS), (None, LANES - IDX_DIM - IDX_HEADS)])
    u = _norm_proj(x2, l1_norm, w1)
    u3 = u.reshape(b, s, -1)
    kv0 = (2 * mw + iw + 2 * MEM_WIDTH) // LANES
    y = _dsa_attn(u3, tabs128 + tabs64, q_blk=1, qi_blk=2 * mw // iw, k_blk=kv0, v_blk=kv0 + 1,
                  misc_blk=kv0 + 2)
    zoff = (2 * mw + iw) // MEM_WIDTH
    x2 = finish(y.reshape(m, mw), u, x2, l1_mem_norm, l1_w_mem_kv, l1_w_out, wy=mw, zy_blk=0,
                zm_blk=zoff, qm_blk=zoff + 1)

    gw = len(DILATED_PAIRS) * DIL_WIDTH
    q_w, k_w, v_w, qm_w, z_w = _cols(l2_w_in, (gw, gw, gw, MEM_WIDTH, DIL_WIDTH + MEM_WIDTH))

    def grp(w, g):
        return (w[:, g * DIL_WIDTH:(g + 1) * DIL_WIDTH], DIL_WIDTH)

    wa = _pack([grp(q_w, 0), (z_w[:, :DIL_WIDTH], DIL_WIDTH), grp(k_w, 0), grp(v_w, 0),
                (z_w[:, DIL_WIDTH:], MEM_WIDTH), (qm_w, MEM_WIDTH)])
    wb = _pack([grp(q_w, 1), grp(k_w, 1), grp(v_w, 1), grp(q_w, 2), grp(k_w, 2), grp(v_w, 2)])
    u = _norm_proj(x2, l2_norm, wa)
    ub = _norm_proj(x2, l2_norm, wb, out_dtype=F32)
    y = _dil_attn(u.reshape(b, s, -1), ub.reshape(b, s, -1), tabs128, q0_blk=0,
                  k0_blk=2 * DIL_HEADS, v0_blk=3 * DIL_HEADS)
    zoff = 4 * DIL_WIDTH // MEM_WIDTH
    x2 = finish(y.reshape(m, DIL_WIDTH), u, x2, l2_mem_norm, l2_w_mem_kv, l2_w_out, wy=DIL_WIDTH,
                zy_blk=1, zm_blk=zoff, qm_blk=zoff + 1)

    q_lora, kv_lora = l3_w_uq.shape[0], l3_w_ukv.shape[0]
    cq_w, ckv_w, kr_w, qm_w, z_w = _cols(l3_w_in, (q_lora, kv_lora, QK_ROPE, MEM_WIDTH,
                                                  mw + MEM_WIDTH))
    assert q_lora == MEM_WIDTH and kv_lora == MEM_WIDTH
    w3 = _pack([(z_w[:, :mw], mw), (z_w[:, mw:], MEM_WIDTH), (qm_w, MEM_WIDTH), (cq_w, q_lora),
                (ckv_w, kv_lora)])
    u, kr_arr = _norm_proj(x2, l3_norm, w3, w_extra=_pack([(kr_w, QK_ROPE)], multiple=LANES))
    cq_blk = (mw + 2 * MEM_WIDTH) // MEM_WIDTH
    uq = l3_w_uq.reshape(q_lora, N_HEADS, QK_NOPE + QK_ROPE)
    w_uq = _pack([(uq[:, :, :QK_NOPE].reshape(q_lora, -1), N_HEADS * QK_NOPE),
                  (uq[:, :, QK_NOPE:].reshape(q_lora, -1), N_HEADS * QK_ROPE)])
    ukv = l3_w_ukv.reshape(kv_lora, N_HEADS, QK_NOPE + HEAD_DIM)
    w_ukv = _pack([(ukv[:, :, :QK_NOPE].reshape(kv_lora, -1), N_HEADS * QK_NOPE),
                   (ukv[:, :, QK_NOPE:].reshape(kv_lora, -1), N_HEADS * HEAD_DIM)])
    qf = _norm_proj(u, l3_q_norm, w_uq, x_col_block=cq_blk)
    kvf = _norm_proj(u, l3_kv_norm, w_ukv, x_col_block=cq_blk + 1)
    y = _mla_attn(qf.reshape(b, s, -1), kvf.reshape(b, s, -1), kr_arr.reshape(b, s, LANES), tabs64,
                  kr_blk=0)
    x2 = finish(y.reshape(m, mw), u, x2, l3_mem_norm, l3_w_mem_kv, l3_w_out, wy=mw, zy_blk=0,
                zm_blk=mw // MEM_WIDTH, qm_blk=mw // MEM_WIDTH + 1, final_g=final_norm)
    return x2.reshape(b, s, d)
```

```python
import functools

import jax
import jax.numpy as jnp
from jax import lax
from jax.experimental import pallas as pl
from jax.experimental.pallas import tpu as pltpu

F32 = jnp.float32
BF16 = jnp.bfloat16
I32 = jnp.int32

EPS = 1e-6
ROPE_THETA = 10000.0
HEAD_DIM = 128
N_HEADS = 16
MEM_HEADS = 4
MEM_WIDTH = MEM_HEADS * HEAD_DIM
IDX_HEADS = 16
IDX_DIM = 64
TOPK_MAX = 256
DILATED_PAIRS = ((128, 1), (512, 4), (2048, 16))
DIL_HEADS = 6
DIL_WIDTH = DIL_HEADS * HEAD_DIM
QK_NOPE = 128
QK_ROPE = 64
LANES = 128
NEG = -1e30
INT_MIN = -(2 ** 31)
VMEM_LIMIT = 56 * 1024 * 1024

NT_DIMS = (((1,), (1,)), ((), ()))
LOG2E = 1.4426950408889634

ROW_TILE = 512
PROJ_TM = 1024
PROJ_TN = (1024, 512)
GATE_TS = 256


def _cparams(n_axes):
    return pltpu.CompilerParams(
        dimension_semantics=("arbitrary",) * n_axes, vmem_limit_bytes=VMEM_LIMIT)


def _rope128(x, cos_f, sin_s):
    return x * cos_f + pltpu.roll(x, 64, 1) * sin_s


def _rope64(x, cos_f, sin_s):
    lane = lax.broadcasted_iota(I32, x.shape, 1)
    partner = jnp.where((lane & 32) == 0, pltpu.roll(x, 96, 1), pltpu.roll(x, 32, 1))
    return x * cos_f + partner * sin_s


def _rope_table_kernel(pos_ref, inv_ref, sgn_ref, cos_ref, sin_ref):
    ang = pos_ref[...] * inv_ref[...]
    cos_ref[...] = jnp.cos(ang)
    sin_ref[...] = jnp.sin(ang) * sgn_ref[...]


def _rope_tables(pos_b, dh):
    m = pos_b.shape[0]
    half = dh // 2
    inv = jnp.power(ROPE_THETA, -jnp.arange(half, dtype=F32) * 2.0 / dh)
    reps = LANES // half
    inv_l = jnp.tile(inv, reps).reshape(1, LANES)
    sgn = jnp.tile(jnp.concatenate([-jnp.ones((half,), F32), jnp.ones((half,), F32)]),
                   reps // 2).reshape(1, LANES)
    ts = min(ROW_TILE, m)
    return pl.pallas_call(
        _rope_table_kernel,
        grid=(m // ts,),
        in_specs=[pl.BlockSpec((ts, LANES), lambda i: (i, 0)),
                  pl.BlockSpec((1, LANES), lambda i: (0, 0)),
                  pl.BlockSpec((1, LANES), lambda i: (0, 0))],
        out_specs=[pl.BlockSpec((ts, LANES), lambda i: (i, 0))] * 2,
        out_shape=[jax.ShapeDtypeStruct((m, LANES), F32)] * 2,
        compiler_params=_cparams(1),
        name="rope_tables",
    )(pos_b, inv_l, sgn)


def _norm_proj_kernel(x_ref, g_ref, w_ref, *rest, has_extra):
    if has_extra:
        wx_ref, o_ref, ox_ref, h_ref = rest
    else:
        o_ref, h_ref = rest

    first = pl.program_id(1) == 0

    @pl.when(first)
    def _():
        tm = o_ref.shape[0]
        parts = 2 if tm % 32 == 0 else 1
        for part in range(parts):
            rows = pl.ds(part * (tm // parts), tm // parts)
            x = x_ref[rows, :].astype(F32)
            ms = jnp.mean(x * x, axis=-1, keepdims=True)
            h_ref[rows, :] = (x * lax.rsqrt(ms + EPS) * g_ref[...]).astype(BF16)
            o_ref[rows, :] = jnp.dot(h_ref[rows, :], w_ref[...],
                                     preferred_element_type=F32).astype(o_ref.dtype)
            if has_extra:
                ox_ref[rows, :] = jnp.dot(h_ref[rows, :], wx_ref[...],
                                          preferred_element_type=F32).astype(ox_ref.dtype)

    @pl.when(jnp.logical_not(first))
    def _():
        o_ref[...] = jnp.dot(h_ref[...], w_ref[...],
                             preferred_element_type=F32).astype(o_ref.dtype)


def _norm_proj(x, g, w, *, x_col_block=0, out_dtype=BF16, w_extra=None, extra_dtype=BF16):
    m = x.shape[0]
    k, n = w.shape
    tm = min(PROJ_TM, m)
    tn = next(t for t in PROJ_TN if n % t == 0)
    assert m % tm == 0
    has_extra = w_extra is not None
    in_specs = [pl.BlockSpec((tm, k), lambda i, j: (i, x_col_block)),
                pl.BlockSpec((1, k), lambda i, j: (0, 0)),
                pl.BlockSpec((k, tn), lambda i, j: (0, j))]
    out_specs = [pl.BlockSpec((tm, tn), lambda i, j: (i, j))]
    out_shape = [jax.ShapeDtypeStruct((m, n), out_dtype)]
    args = [x, g.reshape(1, k).astype(F32), w]
    if has_extra:
        in_specs.append(pl.BlockSpec((k, LANES), lambda i, j: (0, 0)))
        out_specs.append(pl.BlockSpec((tm, LANES), lambda i, j: (i, 0)))
        out_shape.append(jax.ShapeDtypeStruct((m, LANES), extra_dtype))
        args.append(w_extra)
    outs = pl.pallas_call(
        functools.partial(_norm_proj_kernel, has_extra=has_extra),
        grid=(m // tm, n // tn),
        in_specs=in_specs,
        out_specs=out_specs,
        out_shape=out_shape,
        scratch_shapes=[pltpu.VMEM((tm, k), BF16)],
        compiler_params=_cparams(2),
        name="norm_proj",
    )(*args)
    return outs if has_extra else outs[0]


def _out_proj_kernel(y_ref, zy_ref, zm_ref, qm_ref, mkv_ref, x_ref, w_ref, *rest, wy, final):
    if final:
        gf_ref, o_ref, gated_ref = rest
    else:
        o_ref, gated_ref = rest
    scale = HEAD_DIM ** -0.5
    for h in range(MEM_HEADS):
        lo, hi = h * HEAD_DIM, (h + 1) * HEAD_DIM
        s = lax.dot_general(qm_ref[:, lo:hi], mkv_ref[:, lo:hi], NT_DIMS,
                            preferred_element_type=F32) * scale
        m = jnp.max(s, axis=-1, keepdims=True)
        p = jnp.exp(s - m)
        l = jnp.sum(p, axis=-1, keepdims=True)
        o = jnp.dot(p.astype(BF16), mkv_ref[:, MEM_WIDTH + lo:MEM_WIDTH + hi],
                    preferred_element_type=F32) / l
        zm = zm_ref[:, lo:hi].astype(F32)
        gated_ref[:, wy + lo:wy + hi] = (o * (zm * jax.nn.sigmoid(zm))).astype(BF16)
    tm = o_ref.shape[0]
    parts = 2 if tm % 32 == 0 else 1
    for part in range(parts):
        rows = pl.ds(part * (tm // parts), tm // parts)
        zy = zy_ref[rows, :].astype(F32)
        gated_ref[rows, :wy] = (y_ref[rows, :].astype(F32)
                                * (zy * jax.nn.sigmoid(zy))).astype(BF16)
        out = x_ref[rows, :] + jnp.dot(gated_ref[rows, :], w_ref[...],
                                       preferred_element_type=F32)
        if final:
            ms = jnp.mean(out * out, axis=-1, keepdims=True)
            out = out * lax.rsqrt(ms + EPS) * gf_ref[...]
        o_ref[rows, :] = out


def _out_proj(y, u, mkv, x, w_out, *, wy, zy_blk, zm_blk, qm_blk, seq, n_mem, final_g=None):
    m, d = x.shape
    tm = min(ROW_TILE, seq)
    final = final_g is not None
    in_specs = [
        pl.BlockSpec((tm, wy), lambda i: (i, 0)),
        pl.BlockSpec((tm, wy), lambda i: (i, zy_blk)),
        pl.BlockSpec((tm, MEM_WIDTH), lambda i: (i, zm_blk)),
        pl.BlockSpec((tm, MEM_WIDTH), lambda i: (i, qm_blk)),
        pl.BlockSpec((n_mem, 2 * MEM_WIDTH), lambda i: ((i * tm) // seq, 0)),
        pl.BlockSpec((tm, d), lambda i: (i, 0)),
        pl.BlockSpec((wy + MEM_WIDTH, d), lambda i: (0, 0), pipeline_mode=pl.Buffered(1)),
    ]
    args = [y, u, u, u, mkv, x, w_out]
    if final:
        in_specs.append(pl.BlockSpec((1, d), lambda i: (0, 0)))
        args.append(final_g.reshape(1, d).astype(F32))
    return pl.pallas_call(
        functools.partial(_out_proj_kernel, wy=wy, final=final),
        grid=(m // tm,),
        in_specs=in_specs,
        out_specs=pl.BlockSpec((tm, d), lambda i: (i, 0)),
        out_shape=jax.ShapeDtypeStruct((m, d), F32),
        scratch_shapes=[pltpu.VMEM((tm, wy + MEM_WIDTH), BF16)],
        compiler_params=_cparams(1),
        name="out_proj",
    )(*args)


def _fox_gate_kernel(f_ref, b_ref, c_ref, carry_ref):
    @pl.when(pl.program_id(1) == 0)
    def _():
        carry_ref[...] = jnp.zeros_like(carry_ref)

    f_t = f_ref[...].astype(F32).T[:N_HEADS, :] + b_ref[...]
    log_f = jnp.minimum(f_t, 0.0) - jnp.log(1.0 + jnp.exp(-jnp.abs(f_t)))
    ts = log_f.shape[1]
    r = lax.broadcasted_iota(I32, (ts, ts), 0)
    c = lax.broadcasted_iota(I32, (ts, ts), 1)
    upper = jnp.where(r <= c, 1.0, 0.0).astype(F32)
    cs = jnp.dot(log_f, upper, precision=lax.Precision.HIGHEST,
                 preferred_element_type=F32) + carry_ref[...]
    c_ref[...] = cs * LOG2E
    carry_ref[...] = cs[:, ts - 1:ts]


def _fox_gate(u3, bias, f_blk):
    b, s, _ = u3.shape
    ts = min(GATE_TS, s)
    return pl.pallas_call(
        _fox_gate_kernel,
        grid=(b, s // ts),
        in_specs=[pl.BlockSpec((None, ts, LANES), lambda bi, j: (bi, j, f_blk)),
                  pl.BlockSpec((N_HEADS, 1), lambda bi, j: (0, 0))],
        out_specs=pl.BlockSpec((None, N_HEADS, ts), lambda bi, j: (bi, 0, j)),
        out_shape=jax.ShapeDtypeStruct((b, N_HEADS, s), F32),
        scratch_shapes=[pltpu.VMEM((N_HEADS, 1), F32)],
        compiler_params=_cparams(2),
        name="fox_gate",
    )(u3, bias.reshape(N_HEADS, 1).astype(F32))


STRIP = 32
HEADS_PER_STEP = 4
FOX_HEADS = 8


def _softmax_strips(t_ref, p_ref, m_ref, l_ref, alpha_ref, adjust):
    tq, tk = t_ref.shape
    for r0 in range(0, tq, min(STRIP, tq)):
        rs = pl.ds(r0, min(STRIP, tq))
        blocks = [adjust(r0, c, t_ref[rs, c * LANES:(c + 1) * LANES]) for c in range(tk // LANES)]
        live = [x for x in blocks if x is not None]
        mx = live[0]
        for x in live[1:]:
            mx = jnp.maximum(mx, x)
        m_old = m_ref[rs, :]
        m_new = jnp.maximum(m_old, jnp.max(mx, axis=-1, keepdims=True))
        alpha = jnp.exp2(m_old - m_new)
        ps = [None if x is None else jnp.exp2(x - m_new) for x in blocks]
        live = [x for x in ps if x is not None]
        sm = live[0]
        for x in live[1:]:
            sm = sm + x
        l_ref[rs, :] = alpha * l_ref[rs, :] + jnp.sum(sm, axis=-1, keepdims=True)
        m_ref[rs, :] = m_new
        alpha_ref[rs, :] = alpha
        for c, x in enumerate(ps):
            p_ref[rs, c * LANES:(c + 1) * LANES] = (
                jnp.zeros((min(STRIP, tq), LANES), BF16) if x is None else x.astype(BF16))


def _flash_scratch(n_streams, tq):
    return [pltpu.VMEM((n_streams, tq, tq), F32), pltpu.VMEM((n_streams, tq, tq), BF16),
            pltpu.VMEM((n_streams, tq, LANES), F32), pltpu.VMEM((n_streams, tq, LANES), F32),
            pltpu.VMEM((n_streams, tq, LANES), F32), pltpu.VMEM((n_streams, tq, HEAD_DIM), F32)]


def _flash_init(scratch):
    _, _, m_ref, l_ref, _, acc_ref = scratch
    m_ref[...] = jnp.full(m_ref.shape, NEG, F32)
    l_ref[...] = jnp.zeros(l_ref.shape, F32)
    acc_ref[...] = jnp.zeros(acc_ref.shape, F32)


def _flash_chunk(n_streams, j, raw_scores, values, to_logits, scratch, *, diagonal):
    t_ref, p_ref, m_ref, l_ref, alpha_ref, acc_ref = scratch
    strip = min(STRIP, t_ref.shape[1])
    row = lax.broadcasted_iota(I32, (strip, LANES), 0)
    col = lax.broadcasted_iota(I32, (strip, LANES), 1)
    for a in range(n_streams):
        t_ref[a] = raw_scores(a, j)
    for a in range(n_streams):
        def adjust(r0, c, x, a=a):
            t = to_logits(a, j, r0, c, x)
            if not diagonal or c * LANES + LANES - 1 <= r0:
                return t
            if c * LANES > r0 + strip - 1:
                return None
            return jnp.where(col + c * LANES <= row + r0, t, NEG)

        _softmax_strips(t_ref.at[a], p_ref.at[a], m_ref.at[a], l_ref.at[a], alpha_ref.at[a],
                        adjust)
    for a in range(n_streams):
        acc_ref[a] = alpha_ref[a] * acc_ref[a] + jnp.dot(p_ref[a], values(a, j),
                                                         preferred_element_type=F32)


def _flash_finish(n_streams, scratch):
    _, _, _, l_ref, _, acc_ref = scratch
    return [acc_ref[a] / l_ref[a] for a in range(n_streams)]


def _causal_flash(i, n_streams, raw_scores, values, to_logits, scratch):
    _flash_init(scratch)

    def body(j, _):
        _flash_chunk(n_streams, j, raw_scores, values, to_logits, scratch, diagonal=False)
        return 0

    lax.fori_loop(0, i, body, 0)
    _flash_chunk(n_streams, i, raw_scores, values, to_logits, scratch, diagonal=True)
    return _flash_finish(n_streams, scratch)


def _fox_attn_kernel(q_ref, k_ref, v_ref, c_ref, o_ref, qs_ref, *scratch, tq):
    i = pl.program_id(2)
    qk_scale = HEAD_DIM ** -0.5 * LOG2E
    qs_ref[...] = (q_ref[...].astype(F32) * qk_scale).astype(BF16)

    def raw_scores(a, j):
        off = pl.multiple_of(j * tq, tq)
        lo, hi = a * HEAD_DIM, (a + 1) * HEAD_DIM
        return lax.dot_general(qs_ref[:, lo:hi], k_ref[pl.ds(off, tq), lo:hi], NT_DIMS,
                               preferred_element_type=F32)

    def to_logits(a, j, r0, c, x):
        return x - c_ref[a, j, :, c * LANES:(c + 1) * LANES]

    def values(a, j):
        off = pl.multiple_of(j * tq, tq)
        return v_ref[pl.ds(off, tq), a * HEAD_DIM:(a + 1) * HEAD_DIM]

    outs = _causal_flash(i, FOX_HEADS, raw_scores, values, to_logits, scratch)
    for a in range(FOX_HEADS):
        o_ref[:, a * HEAD_DIM:(a + 1) * HEAD_DIM] = outs[a].astype(o_ref.dtype)


def _fox_attn(u3, c_t, *, q_blk, k_blk, v_blk):
    b, s, _ = u3.shape
    tq = min(ROW_TILE, s)
    nq = s // tq
    pw = FOX_HEADS * HEAD_DIM
    c5 = c_t.reshape(b, N_HEADS, nq, 1, tq)
    return pl.pallas_call(
        functools.partial(_fox_attn_kernel, tq=tq),
        grid=(b, N_HEADS // FOX_HEADS, nq),
        in_specs=[pl.BlockSpec((None, tq, pw), lambda bi, p, i: (bi, i, q_blk + p)),
                  pl.BlockSpec((None, s, pw), lambda bi, p, i: (bi, 0, k_blk + p)),
                  pl.BlockSpec((None, s, pw), lambda bi, p, i: (bi, 0, v_blk + p)),
                  pl.BlockSpec((None, FOX_HEADS, nq, 1, tq), lambda bi, p, i: (bi, p, 0, 0, 0))],
        out_specs=pl.BlockSpec((None, tq, pw), lambda bi, p, i: (bi, i, p)),
        out_shape=jax.ShapeDtypeStruct((b, s, N_HEADS * HEAD_DIM), BF16),
        scratch_shapes=[pltpu.VMEM((tq, pw), BF16)] + _flash_scratch(FOX_HEADS, tq),
        compiler_params=_cparams(3),
        name="fox_attn",
    )(u3, u3, u3, c5)


def _dsa_kprep_kernel(k_ref, mk_ref, c128_ref, s128_ref, c64_ref, s64_ref,
                      kr_ref, klo_ref, khi_ref):
    kr_ref[...] = _rope128(k_ref[...].astype(F32), c128_ref[...], s128_ref[...]).astype(BF16)
    ki = _rope64(mk_ref[...].astype(F32), c64_ref[...], s64_ref[...])
    lane = lax.broadcasted_iota(I32, ki.shape, 1)
    lo = jnp.where(lane < IDX_DIM, ki, 0.0)
    klo_ref[...] = lo.astype(BF16)
    khi_ref[...] = pltpu.roll(lo, IDX_DIM, 1).astype(BF16)


def _dsa_kprep(u3, tabs, *, k_blk, misc_blk):
    b, s, _ = u3.shape
    ts = min(ROW_TILE, s)
    c128, s128, c64, s64 = tabs

    def spec(blk):
        return pl.BlockSpec((None, ts, LANES), lambda bi, i: (bi, i, blk))

    return pl.pallas_call(
        _dsa_kprep_kernel,
        grid=(b, s // ts),
        in_specs=[spec(k_blk), spec(misc_blk), spec(0), spec(0), spec(0), spec(0)],
        out_specs=[spec(0)] * 3,
        out_shape=[jax.ShapeDtypeStruct((b, s, LANES), BF16)] * 3,
        compiler_params=_cparams(2),
        name="dsa_kprep",
    )(u3, u3, c128, s128, c64, s64)


def _dsa_kernel(q_ref, qi_ref, mq_ref, v_ref, kr_ref, klo_ref, khi_ref,
                cq128_ref, sq128_ref, cq64_ref, sq64_ref, y_in_ref,
                o_ref,
                qr_ref, qir_ref, w_ref, sc_ref, t4_ref, yo_ref, *flash,
                tq, kl, q_off, n_sel):
    del y_in_ref
    t0 = q_off + pl.program_id(1) * tq
    qk_scale = HEAD_DIM ** -0.5 * LOG2E
    idx_scale = (IDX_DIM ** -0.5) * (IDX_HEADS ** -0.5)
    ck = min(ROW_TILE, kl)
    n_chunks = kl // ck
    strip = min(STRIP, tq)

    cq128, sq128 = cq128_ref[...], sq128_ref[...]
    for h in range(N_HEADS):
        qh = q_ref[:, h * HEAD_DIM:(h + 1) * HEAD_DIM].astype(F32)
        qr_ref[h] = (_rope128(qh, cq128, sq128) * qk_scale).astype(BF16)
    cq64, sq64 = cq64_ref[...], sq64_ref[...]
    for a in range(IDX_HEADS // 2):
        qir_ref[a] = _rope64(qi_ref[:, a * LANES:(a + 1) * LANES].astype(F32),
                             cq64, sq64).astype(BF16)
    w_ref[...] = mq_ref[...].astype(F32) * idx_scale

    row = lax.broadcasted_iota(I32, (strip, LANES), 0)
    col = lax.broadcasted_iota(I32, (strip, LANES), 1)
    group = 4

    def idx_chunk(j, _):
        off = pl.multiple_of(j * ck, ck)
        k_lo, k_hi = klo_ref[pl.ds(off, ck), :], khi_ref[pl.ds(off, ck), :]
        for g in range(IDX_HEADS // group):
            for a2 in range(group // 2):
                x = qir_ref[g * (group // 2) + a2]
                t4_ref[2 * a2] = lax.dot_general(x, k_lo, NT_DIMS, preferred_element_type=F32)
                t4_ref[2 * a2 + 1] = lax.dot_general(x, k_hi, NT_DIMS,
                                                     preferred_element_type=F32)
            last = g == IDX_HEADS // group - 1
            for r0 in range(0, tq, strip):
                rs = pl.ds(r0, strip)
                w_rows = w_ref[rs, :]
                ws = [jnp.broadcast_to(w_rows[:, IDX_DIM + g * group + u:IDX_DIM + g * group + u + 1],
                                       (strip, LANES)) for u in range(group)]
                for c in range(ck // LANES):
                    cs = pl.ds(c * LANES, LANES)
                    acc = ws[0] * jnp.maximum(t4_ref[0, rs, cs], 0.0)
                    for u in range(1, group):
                        acc = acc + ws[u] * jnp.maximum(t4_ref[u, rs, cs], 0.0)
                    if g > 0:
                        acc = acc + sc_ref[j, rs, cs]
                    if last:
                        causal = col + (off + c * LANES) <= row + (t0 + r0)
                        acc = jnp.where(causal, acc, -jnp.inf)
                    sc_ref[j, rs, cs] = acc
        return 0

    lax.fori_loop(0, n_chunks, idx_chunk, 0)

    shape3 = (n_chunks, tq, ck)
    col3 = lax.broadcasted_iota(I32, shape3, 0) * ck + lax.broadcasted_iota(I32, shape3, 2)
    causal3 = col3 <= lax.broadcasted_iota(I32, shape3, 1) + t0

    def count(x):
        return jnp.sum(jnp.sum(x, axis=0), axis=-1, keepdims=True)

    def key_to_f32(key):
        return pltpu.bitcast(jnp.where(key < 0, key ^ jnp.int32(0x7FFFFFFF), key), F32)

    def thr_body(it, key):
        cand = key + (jnp.int32(1) << (31 - it))
        c = count(jnp.where(sc_ref[...] >= key_to_f32(cand), 1.0, 0.0))
        return jnp.where(c >= n_sel, cand, key)

    thr_key = lax.fori_loop(0, 32, thr_body, jnp.full((tq, 1), INT_MIN, I32))
    take_all = thr_key == INT_MIN
    thr = key_to_f32(thr_key)
    score = sc_ref[...]
    need = n_sel - count(jnp.where(score > thr, 1.0, 0.0))
    surplus = jnp.where(take_all, 0.0, count(jnp.where(score == thr, 1.0, 0.0)) - need)
    has_surplus = jnp.max(surplus) > 0.0

    n_bits = kl.bit_length()

    def tie_body(it, jm):
        cand = jm + (jnp.int32(1) << (n_bits - 1 - it))
        f = count(jnp.where(sc_ref[...] == thr, jnp.where(col3 < cand, 1.0, 0.0), 0.0))
        return jnp.where(jnp.logical_and(f < need, cand <= kl), cand, jm)

    jm = lax.fori_loop(0, jnp.where(has_surplus, n_bits, 0), tie_body,
                       jnp.broadcast_to(jnp.where(has_surplus, 0, kl), (tq, 1)).astype(I32))
    keep = jnp.where(jnp.logical_or(score > thr, take_all), 1.0,
                     jnp.where(score == thr, jnp.where(col3 <= jm, 1.0, 0.0), 0.0))
    sc_ref[...] = jnp.where(causal3, jnp.where(keep > 0.0, 0.0, NEG), NEG)

    def raw_scores(h0):
        def f(a, j):
            return lax.dot_general(qr_ref[h0 + a], kr_ref[j * ck:(j + 1) * ck, :], NT_DIMS,
                                   preferred_element_type=F32)
        return f

    def values(a, j):
        return v_ref[j * ck:(j + 1) * ck, :]

    def to_logits(a, j, r0, c, x):
        return x + sc_ref[j, r0:r0 + strip, c * LANES:(c + 1) * LANES]

    def group_body(pp, _):
        h0 = HEADS_PER_STEP * pp
        _flash_init(flash)
        for j in range(n_chunks):
            _flash_chunk(HEADS_PER_STEP, j, raw_scores(h0), values, to_logits, flash,
                         diagonal=False)
        outs = _flash_finish(HEADS_PER_STEP, flash)
        for a in range(HEADS_PER_STEP):
            yo_ref[h0 + a] = outs[a].astype(yo_ref.dtype)
        return 0

    lax.fori_loop(0, N_HEADS // HEADS_PER_STEP, group_body, 0)
    for h in range(N_HEADS):
        o_ref[:, h * HEAD_DIM:(h + 1) * HEAD_DIM] = yo_ref[h]


def _dsa_group(u3, kprep, tabs, y, *, q_off, rows, kl, n_sel, q_blk, v_blk, qi_blk, misc_blk):
    b, s, _ = u3.shape
    tq = rows
    assert kl % min(ROW_TILE, kl) == 0 and tq == min(ROW_TILE, kl)
    qb0 = q_off // tq
    c128, s128, c64, s64 = tabs
    kr, klo, khi = kprep
    qw = N_HEADS * HEAD_DIM
    iw = IDX_HEADS * IDX_DIM
    ck = min(ROW_TILE, kl)

    def qspec(width, blk):
        return pl.BlockSpec((None, tq, width), lambda bi, i: (bi, qb0 + i, blk))

    def kspec(blk):
        return pl.BlockSpec((None, kl, LANES), lambda bi, i: (bi, 0, blk))

    return pl.pallas_call(
        functools.partial(_dsa_kernel, tq=tq, kl=kl, q_off=q_off, n_sel=n_sel),
        grid=(b, rows // tq),
        in_specs=[qspec(qw, q_blk), qspec(iw, qi_blk), qspec(LANES, misc_blk),
                  kspec(v_blk), kspec(0), kspec(0), kspec(0),
                  qspec(LANES, 0), qspec(LANES, 0), qspec(LANES, 0), qspec(LANES, 0),
                  pl.BlockSpec(memory_space=pl.ANY)],
        out_specs=pl.BlockSpec((None, tq, qw), lambda bi, i: (bi, qb0 + i, 0)),
        out_shape=jax.ShapeDtypeStruct((b, s, qw), BF16),
        input_output_aliases={11: 0},
        scratch_shapes=[pltpu.VMEM((N_HEADS, tq, HEAD_DIM), BF16),
                        pltpu.VMEM((IDX_HEADS // 2, tq, LANES), BF16),
                        pltpu.VMEM((tq, LANES), F32),
                        pltpu.VMEM((kl // ck, tq, ck), F32),
                        pltpu.VMEM((4, tq, ck), F32),
                        pltpu.VMEM((N_HEADS, tq, HEAD_DIM), BF16)]
        + _flash_scratch(HEADS_PER_STEP, tq),
        compiler_params=_cparams(2),
        name="dsa_attn",
    )(u3, u3, u3, u3, kr, klo, khi, c128, s128, c64, s64, y)


def _dsa_attn(u3, tabs, *, k_blk, misc_blk, **blks):
    b, s, _ = u3.shape
    n_sel = min(TOPK_MAX, s // 4)
    rows = min(ROW_TILE, s)
    kprep = _dsa_kprep(u3, tabs, k_blk=k_blk, misc_blk=misc_blk)
    y = jnp.zeros((b, s, N_HEADS * HEAD_DIM), BF16)
    for q_off in range(0, s, rows):
        y = _dsa_group(u3, kprep, tabs, y, q_off=q_off, rows=rows, kl=q_off + rows, n_sel=n_sel,
                       misc_blk=misc_blk, **blks)
    return y


def _dil_kernel(q0_ref, k0_ref, v0_ref, q1_ref, k1_ref, v1_ref, q2_ref, k2_ref, v2_ref,
                cos_ref, sin_ref, y_ref, qr_ref, kr_ref, o_ref, lse_ref, t_ref, p_ref, *, seq):
    qk_scale = HEAD_DIM ** -0.5 * LOG2E
    cos_f, sin_s = cos_ref[...], sin_ref[...]
    groups = ((q0_ref, k0_ref, v0_ref), (q1_ref, k1_ref, v1_ref), (q2_ref, k2_ref, v2_ref))
    for g, (q_ref, k_ref, _) in enumerate(groups):
        qr_ref[g] = _rope128(q_ref[...].astype(F32), cos_f, sin_s) * qk_scale
        kr_ref[g] = _rope128(k_ref[...].astype(F32), cos_f, sin_s)

    for g, (window, dil) in enumerate(DILATED_PAIRS):
        v_ref = groups[g][2]
        sub = seq // dil
        qb = min(window // dil, sub)
        tiles = [(r, i) for r in range(dil) for i in range(sub // qb)]

        def rows(r, blk, dil=dil, qb=qb):
            start = r + dil * qb * blk
            return pl.ds(start, qb) if dil == 1 else pl.ds(start, qb, stride=dil)

        for n, (r, i) in enumerate(tiles):
            q = qr_ref[g, rows(r, i), :].astype(BF16)
            k_cur = kr_ref[g, rows(r, i), :].astype(BF16)
            t_ref[n, :qb, qb:2 * qb] = lax.dot_general(q, k_cur, NT_DIMS,
                                                       preferred_element_type=F32)
            if i > 0:
                k_prev = kr_ref[g, rows(r, i - 1), :].astype(BF16)
                t_ref[n, :qb, :qb] = lax.dot_general(q, k_prev, NT_DIMS,
                                                     preferred_element_type=F32)
        row = lax.broadcasted_iota(I32, (qb, qb), 0)
        col = lax.broadcasted_iota(I32, (qb, qb), 1)
        for n, (r, i) in enumerate(tiles):
            t_cur = jnp.where(col <= row, t_ref[n, :qb, qb:2 * qb], NEG)
            m = jnp.max(t_cur, axis=-1, keepdims=True)
            if i > 0:
                t_prev = jnp.where(col >= row, t_ref[n, :qb, :qb], NEG)
                m = jnp.maximum(m, jnp.max(t_prev, axis=-1, keepdims=True))
            p_cur = jnp.exp2(t_cur - m)
            l = jnp.sum(p_cur, axis=-1, keepdims=True)
            p_ref[n, :qb, qb:2 * qb] = p_cur.astype(BF16)
            if i > 0:
                p_prev = jnp.exp2(t_prev - m)
                l = l + jnp.sum(p_prev, axis=-1, keepdims=True)
                p_ref[n, :qb, :qb] = p_prev.astype(BF16)
            t_ref[n, :qb, :LANES] = jnp.broadcast_to(l, (qb, LANES))
            t_ref[n, :qb, LANES:2 * LANES] = jnp.broadcast_to(m, (qb, LANES))
        for n, (r, i) in enumerate(tiles):
            acc = jnp.dot(p_ref[n, :qb, qb:2 * qb], v_ref[rows(r, i), :].astype(BF16),
                          preferred_element_type=F32)
            if i > 0:
                acc = acc + jnp.dot(p_ref[n, :qb, :qb], v_ref[rows(r, i - 1), :].astype(BF16),
                                    preferred_element_type=F32)
            l = t_ref[n, :qb, :LANES]
            o_ref[g, rows(r, i), :] = acc / l
            lse_ref[g, rows(r, i), :] = t_ref[n, :qb, LANES:2 * LANES] + jnp.log2(l)

    l0, l1, l2 = lse_ref[0], lse_ref[1], lse_ref[2]
    m = jnp.maximum(jnp.maximum(l0, l1), l2)
    e0, e1, e2 = jnp.exp2(l0 - m), jnp.exp2(l1 - m), jnp.exp2(l2 - m)
    y = (e0 * o_ref[0] + e1 * o_ref[1] + e2 * o_ref[2]) / (e0 + e1 + e2)
    y_ref[...] = y.astype(y_ref.dtype)


def _dil_attn(ua3, ub3, tabs128, *, q0_blk, k0_blk, v0_blk):
    b, s, _ = ua3.shape
    for window, dil in DILATED_PAIRS:
        assert s % dil == 0 and (s // dil) % min(window // dil, s // dil) == 0
    cos_t, sin_t = tabs128
    tile = min(DILATED_PAIRS[0][0], s)
    n_tiles = s // min(min(w // d, s // d) for w, d in DILATED_PAIRS)

    def spec(blk0):
        return pl.BlockSpec((None, s, HEAD_DIM), lambda bi, a: (bi, 0, blk0 + a))

    tab = pl.BlockSpec((None, s, LANES), lambda bi, a: (bi, 0, 0))
    return pl.pallas_call(
        functools.partial(_dil_kernel, seq=s),
        grid=(b, DIL_HEADS),
        in_specs=[spec(q0_blk), spec(k0_blk), spec(v0_blk)]
        + [spec(j * DIL_HEADS) for j in range(6)] + [tab, tab],
        out_specs=pl.BlockSpec((None, s, HEAD_DIM), lambda bi, a: (bi, 0, a)),
        out_shape=jax.ShapeDtypeStruct((b, s, DIL_WIDTH), BF16),
        scratch_shapes=[pltpu.VMEM((3, s, HEAD_DIM), F32)] * 4
        + [pltpu.VMEM((n_tiles, tile, 2 * LANES), F32), pltpu.VMEM((n_tiles, tile, 2 * LANES), BF16)],
        compiler_params=_cparams(2),
        name="dilated_attn",
    )(ua3, ua3, ua3, ub3, ub3, ub3, ub3, ub3, ub3, cos_t, sin_t)


def _mla_kernel(qn_ref, qr_ref, kn_ref, kr_ref, v_ref, cq_ref, sq_ref, ck_ref, sk_ref,
                o_ref, qcat_ref, kcat_ref, *scratch, tq):
    i = pl.program_id(2)
    qk_scale = (QK_NOPE + QK_ROPE) ** -0.5 * LOG2E
    cat = QK_NOPE + QK_ROPE

    @pl.when(i == 0)
    def _():
        k_rope = _rope64(kr_ref[...].astype(F32), ck_ref[...],
                         sk_ref[...])[:, :QK_ROPE].astype(BF16)
        for a in range(HEADS_PER_STEP):
            kcat_ref[a, :, :QK_NOPE] = kn_ref[:, a * HEAD_DIM:(a + 1) * HEAD_DIM]
            kcat_ref[a, :, QK_NOPE:cat] = k_rope
            kcat_ref[a, :, cat:] = jnp.zeros((kcat_ref.shape[1], 2 * LANES - cat), BF16)

    cos_q, sin_q = cq_ref[...], sq_ref[...]
    for a2 in range(HEADS_PER_STEP // 2):
        q_rope = (_rope64(qr_ref[:, a2 * LANES:(a2 + 1) * LANES].astype(F32), cos_q, sin_q)
                  * qk_scale).astype(BF16)
        for a in (2 * a2, 2 * a2 + 1):
            q_nope = qn_ref[:, a * HEAD_DIM:(a + 1) * HEAD_DIM].astype(F32) * qk_scale
            qcat_ref[a, :, :QK_NOPE] = q_nope.astype(BF16)
            qcat_ref[a, :, QK_NOPE:cat] = q_rope[:, (a % 2) * QK_ROPE:(a % 2 + 1) * QK_ROPE]
            qcat_ref[a, :, cat:] = jnp.zeros((tq, 2 * LANES - cat), BF16)

    def raw_scores(a, j):
        off = pl.multiple_of(j * tq, tq)
        return lax.dot_general(qcat_ref[a], kcat_ref[a, pl.ds(off, tq), :], NT_DIMS,
                               preferred_element_type=F32)

    def to_logits(a, j, r0, c, x):
        return x

    def values(a, j):
        off = pl.multiple_of(j * tq, tq)
        return v_ref[pl.ds(off, tq), a * HEAD_DIM:(a + 1) * HEAD_DIM]

    outs = _causal_flash(i, HEADS_PER_STEP, raw_scores, values, to_logits, scratch)
    for a in range(HEADS_PER_STEP):
        o_ref[:, a * HEAD_DIM:(a + 1) * HEAD_DIM] = outs[a].astype(o_ref.dtype)


def _mla_attn(qf3, kvf3, u3, tabs64, *, kr_blk):
    b, s, _ = qf3.shape
    tq = min(ROW_TILE, s)
    gw = HEADS_PER_STEP * HEAD_DIM
    rw = HEADS_PER_STEP * QK_ROPE
    n_groups = N_HEADS // HEADS_PER_STEP
    cos_t, sin_t = tabs64
    return pl.pallas_call(
        functools.partial(_mla_kernel, tq=tq),
        grid=(b, n_groups, s // tq),
        in_specs=[pl.BlockSpec((None, tq, gw), lambda bi, p, i: (bi, i, p)),
                  pl.BlockSpec((None, tq, rw),
                               lambda bi, p, i: (bi, i, N_HEADS * QK_NOPE // rw + p)),
                  pl.BlockSpec((None, s, gw), lambda bi, p, i: (bi, 0, p)),
                  pl.BlockSpec((None, s, LANES), lambda bi, p, i: (bi, 0, kr_blk)),
                  pl.BlockSpec((None, s, gw), lambda bi, p, i: (bi, 0, n_groups + p)),
                  pl.BlockSpec((None, tq, LANES), lambda bi, p, i: (bi, i, 0)),
                  pl.BlockSpec((None, tq, LANES), lambda bi, p, i: (bi, i, 0)),
                  pl.BlockSpec((None, s, LANES), lambda bi, p, i: (bi, 0, 0)),
                  pl.BlockSpec((None, s, LANES), lambda bi, p, i: (bi, 0, 0))],
        out_specs=pl.BlockSpec((None, tq, gw), lambda bi, p, i: (bi, i, p)),
        out_shape=jax.ShapeDtypeStruct((b, s, N_HEADS * HEAD_DIM), BF16),
        scratch_shapes=[pltpu.VMEM((HEADS_PER_STEP, tq, 2 * LANES), BF16),
                        pltpu.VMEM((HEADS_PER_STEP, s, 2 * LANES), BF16)]
        + _flash_scratch(HEADS_PER_STEP, tq),
        compiler_params=_cparams(3),
        name="mla_attn",
    )(qf3, qf3, kvf3, u3, kvf3, cos_t, sin_t, cos_t, sin_t)


def _pack(parts, multiple=512):
    k = next(p.shape[0] for p, _ in parts if p is not None)
    cols = [jnp.zeros((k, w), BF16) if p is None else p.astype(BF16) for p, w in parts]
    n = sum(w for _, w in parts)
    pad = (-n) % multiple
    if pad:
        cols.append(jnp.zeros((k, pad), BF16))
    return jnp.concatenate(cols, axis=1)


def _cols(w, sizes):
    out, acc = [], 0
    for sz in sizes:
        out.append(w[:, acc:acc + sz])
        acc += sz
    return out


def kernel(x, mem, positions, l0_norm, l0_w_in, l0_forget_bias, l0_mem_norm, l0_w_mem_kv, l0_w_out, l1_norm, l1_w_in, l1_mem_norm, l1_w_mem_kv, l1_w_out, l2_norm, l2_w_in, l2_mem_norm, l2_w_mem_kv, l2_w_out, l3_norm, l3_w_in, l3_q_norm, l3_w_uq, l3_kv_norm, l3_w_ukv, l3_mem_norm, l3_w_mem_kv, l3_w_out, final_norm):
    b, s, d = x.shape
    n_mem = mem.shape[1]
    m = b * s
    mw = N_HEADS * HEAD_DIM
    x2 = x.reshape(m, d)
    mem2 = mem.reshape(b * n_mem, d)

    pos_b = jnp.broadcast_to(positions.astype(F32).reshape(m, 1), (m, LANES))
    tabs128 = tuple(t.reshape(b, s, LANES) for t in _rope_tables(pos_b, HEAD_DIM))
    tabs64 = tuple(t.reshape(b, s, LANES) for t in _rope_tables(pos_b, IDX_DIM))

    def mem_kv(g, w):
        return _norm_proj(mem2, g, w.astype(BF16))

    def finish(y, u, x_in, g_mem, w_mem_kv, w_out, *, wy, zy_blk, zm_blk, qm_blk, final_g=None):
        return _out_proj(y, u, mem_kv(g_mem, w_mem_kv), x_in, w_out.astype(BF16), wy=wy,
                         zy_blk=zy_blk, zm_blk=zm_blk, qm_blk=qm_blk, seq=s, n_mem=n_mem,
                         final_g=final_g)

    q_w, k_w, v_w, f_w, qm_w, z_w = _cols(l0_w_in, (mw, mw, mw, N_HEADS, MEM_WIDTH, mw + MEM_WIDTH))
    w0 = _pack([(z_w[:, :mw], mw), (z_w[:, mw:], MEM_WIDTH), (qm_w, MEM_WIDTH), (q_w, mw),
                (k_w, mw), (v_w, mw)])
    u, f_arr = _norm_proj(x2, l0_norm, w0, w_extra=_pack([(f_w, N_HEADS)], multiple=LANES),
                          extra_dtype=F32)
    u3 = u.reshape(b, s, -1)
    base = (mw + 2 * MEM_WIDTH) // LANES
    c_t = _fox_gate(f_arr.reshape(b, s, LANES), l0_forget_bias, 0)
    y = _fox_attn(u3, c_t, q_blk=base // FOX_HEADS, k_blk=(base + N_HEADS) // FOX_HEADS,
                  v_blk=(base + 2 * N_HEADS) // FOX_HEADS)
    x2 = finish(y.reshape(m, mw), u, x2, l0_mem_norm, l0_w_mem_kv, l0_w_out, wy=mw, zy_blk=0,
                zm_blk=mw // MEM_WIDTH, qm_blk=mw // MEM_WIDTH + 1)

    iw = IDX_HEADS * IDX_DIM
    q_w, k_w, v_w, qi_w, ki_w, wi_w, qm_w, z_w = _cols(
        l1_w_in, (mw, HEAD_DIM, HEAD_DIM, iw, IDX_DIM, IDX_HEADS, MEM_WIDTH, mw + MEM_WIDTH))
    w1 = _pack([(z_w[:, :mw], mw), (q_w, mw), (qi_w, iw), (z_w[:, mw:], MEM_WIDTH),
                (qm_w, MEM_WIDTH), (k_w, HEAD_DIM), (v_w, HEAD_DIM), (ki_w, IDX_DIM),
                (wi_w, IDX_HEADS), (None, LANES - IDX_DIM - IDX_HEADS)])
    u = _norm_proj(x2, l1_norm, w1)
    u3 = u.reshape(b, s, -1)
    kv0 = (2 * mw + iw + 2 * MEM_WIDTH) // LANES
    y = _dsa_attn(u3, tabs128 + tabs64, q_blk=1, qi_blk=2 * mw // iw, k_blk=kv0, v_blk=kv0 + 1,
                  misc_blk=kv0 + 2)
    zoff = (2 * mw + iw) // MEM_WIDTH
    x2 = finish(y.reshape(m, mw), u, x2, l1_mem_norm, l1_w_mem_kv, l1_w_out, wy=mw, zy_blk=0,
                zm_blk=zoff, qm_blk=zoff + 1)

    gw = len(DILATED_PAIRS) * DIL_WIDTH
    q_w, k_w, v_w, qm_w, z_w = _cols(l2_w_in, (gw, gw, gw, MEM_WIDTH, DIL_WIDTH + MEM_WIDTH))

    def grp(w, g):
        return (w[:, g * DIL_WIDTH:(g + 1) * DIL_WIDTH], DIL_WIDTH)

    wa = _pack([grp(q_w, 0), (z_w[:, :DIL_WIDTH], DIL_WIDTH), grp(k_w, 0), grp(v_w, 0),
                (z_w[:, DIL_WIDTH:], MEM_WIDTH), (qm_w, MEM_WIDTH)])
    wb = _pack([grp(q_w, 1), grp(k_w, 1), grp(v_w, 1), grp(q_w, 2), grp(k_w, 2), grp(v_w, 2)])
    u = _norm_proj(x2, l2_norm, wa)
    ub = _norm_proj(x2, l2_norm, wb, out_dtype=F32)
    y = _dil_attn(u.reshape(b, s, -1), ub.reshape(b, s, -1), tabs128, q0_blk=0,
                  k0_blk=2 * DIL_HEADS, v0_blk=3 * DIL_HEADS)
    zoff = 4 * DIL_WIDTH // MEM_WIDTH
    x2 = finish(y.reshape(m, DIL_WIDTH), u, x2, l2_mem_norm, l2_w_mem_kv, l2_w_out, wy=DIL_WIDTH,
                zy_blk=1, zm_blk=zoff, qm_blk=zoff + 1)

    q_lora, kv_lora = l3_w_uq.shape[0], l3_w_ukv.shape[0]
    cq_w, ckv_w, kr_w, qm_w, z_w = _cols(l3_w_in, (q_lora, kv_lora, QK_ROPE, MEM_WIDTH,
                                                  mw + MEM_WIDTH))
    assert q_lora == MEM_WIDTH and kv_lora == MEM_WIDTH
    w3 = _pack([(z_w[:, :mw], mw), (z_w[:, mw:], MEM_WIDTH), (qm_w, MEM_WIDTH), (cq_w, q_lora),
                (ckv_w, kv_lora)])
    u, kr_arr = _norm_proj(x2, l3_norm, w3, w_extra=_pack([(kr_w, QK_ROPE)], multiple=LANES))
    cq_blk = (mw + 2 * MEM_WIDTH) // MEM_WIDTH
    uq = l3_w_uq.reshape(q_lora, N_HEADS, QK_NOPE + QK_ROPE)
    w_uq = _pack([(uq[:, :, :QK_NOPE].reshape(q_lora, -1), N_HEADS * QK_NOPE),
                  (uq[:, :, QK_NOPE:].reshape(q_lora, -1), N_HEADS * QK_ROPE)])
    ukv = l3_w_ukv.reshape(kv_lora, N_HEADS, QK_NOPE + HEAD_DIM)
    w_ukv = _pack([(ukv[:, :, :QK_NOPE].reshape(kv_lora, -1), N_HEADS * QK_NOPE),
                   (ukv[:, :, QK_NOPE:].reshape(kv_lora, -1), N_HEADS * HEAD_DIM)])
    qf = _norm_proj(u, l3_q_norm, w_uq, x_col_block=cq_blk)
    kvf = _norm_proj(u, l3_kv_norm, w_ukv, x_col_block=cq_blk + 1)
    y = _mla_attn(qf.reshape(b, s, -1), kvf.reshape(b, s, -1), kr_arr.reshape(b, s, LANES), tabs64,
                  kr_blk=0)
    x2 = finish(y.reshape(m, mw), u, x2, l3_mem_norm, l3_w_mem_kv, l3_w_out, wy=mw, zy_blk=0,
                zm_blk=mw // MEM_WIDTH, qm_blk=mw // MEM_WIDTH + 1, final_g=final_norm)
    return x2.reshape(b, s, d)
```

```python
import functools

import jax
import jax.numpy as jnp
from jax import lax
from jax.experimental import pallas as pl
from jax.experimental.pallas import tpu as pltpu

F32 = jnp.float32
BF16 = jnp.bfloat16
I32 = jnp.int32

EPS = 1e-6
ROPE_THETA = 10000.0
HEAD_DIM = 128
N_HEADS = 16
MEM_HEADS = 4
MEM_WIDTH = MEM_HEADS * HEAD_DIM
IDX_HEADS = 16
IDX_DIM = 64
TOPK_MAX = 256
DILATED_PAIRS = ((128, 1), (512, 4), (2048, 16))
DIL_HEADS = 6
DIL_WIDTH = DIL_HEADS * HEAD_DIM
QK_NOPE = 128
QK_ROPE = 64
LANES = 128
NEG = -1e30
INT_MIN = -(2 ** 31)
VMEM_LIMIT = 56 * 1024 * 1024

NT_DIMS = (((1,), (1,)), ((), ()))
LOG2E = 1.4426950408889634


def _cparams(n_axes):
    return pltpu.CompilerParams(
        dimension_semantics=("arbitrary",) * n_axes, vmem_limit_bytes=VMEM_LIMIT)


def _rope128(x, cos_f, sin_s):
    return x * cos_f + pltpu.roll(x, 64, 1) * sin_s


def _rope64(x, cos_f, sin_s):
    lane = lax.broadcasted_iota(I32, x.shape, 1)
    partner = jnp.where((lane & 32) == 0, pltpu.roll(x, 96, 1), pltpu.roll(x, 32, 1))
    return x * cos_f + partner * sin_s


def _rope_table_kernel(pos_ref, inv_ref, sgn_ref, cos_ref, sin_ref):
    ang = pos_ref[...] * inv_ref[...]
    cos_ref[...] = jnp.cos(ang)
    sin_ref[...] = jnp.sin(ang) * sgn_ref[...]


def _rope_tables(pos_b, dh):
    m = pos_b.shape[0]
    half = dh // 2
    inv = jnp.power(ROPE_THETA, -jnp.arange(half, dtype=F32) * 2.0 / dh)
    reps = LANES // half
    inv_l = jnp.tile(inv, reps).reshape(1, LANES)
    sgn = jnp.tile(jnp.concatenate([-jnp.ones((half,), F32), jnp.ones((half,), F32)]),
                   reps // 2).reshape(1, LANES)
    ts = min(512, m)
    return pl.pallas_call(
        _rope_table_kernel,
        grid=(m // ts,),
        in_specs=[pl.BlockSpec((ts, LANES), lambda i: (i, 0)),
                  pl.BlockSpec((1, LANES), lambda i: (0, 0)),
                  pl.BlockSpec((1, LANES), lambda i: (0, 0))],
        out_specs=[pl.BlockSpec((ts, LANES), lambda i: (i, 0))] * 2,
        out_shape=[jax.ShapeDtypeStruct((m, LANES), F32)] * 2,
        compiler_params=_cparams(1),
        name="rope_tables",
    )(pos_b, inv_l, sgn)


def _norm_proj_kernel(x_ref, g_ref, w_ref, *rest, has_extra):
    if has_extra:
        wx_ref, o_ref, ox_ref, h_ref = rest
    else:
        o_ref, h_ref = rest

    first = pl.program_id(1) == 0

    @pl.when(first)
    def _():
        tm = o_ref.shape[0]
        parts = 2 if tm % 32 == 0 else 1
        for part in range(parts):
            rows = pl.ds(part * (tm // parts), tm // parts)
            x = x_ref[rows, :].astype(F32)
            ms = jnp.mean(x * x, axis=-1, keepdims=True)
            h_ref[rows, :] = (x * lax.rsqrt(ms + EPS) * g_ref[...]).astype(BF16)
            o_ref[rows, :] = jnp.dot(h_ref[rows, :], w_ref[...],
                                     preferred_element_type=F32).astype(o_ref.dtype)
            if has_extra:
                ox_ref[rows, :] = jnp.dot(h_ref[rows, :], wx_ref[...],
                                          preferred_element_type=F32).astype(ox_ref.dtype)

    @pl.when(jnp.logical_not(first))
    def _():
        o_ref[...] = jnp.dot(h_ref[...], w_ref[...],
                             preferred_element_type=F32).astype(o_ref.dtype)


def _norm_proj(x, g, w, *, x_col_block=0, out_dtype=BF16, w_extra=None, extra_dtype=BF16):
    m = x.shape[0]
    k, n = w.shape
    tm = min(2048 if k <= 512 else 1024, m)
    tn = 1024 if n % 1024 == 0 else 512
    assert m % tm == 0 and n % tn == 0
    has_extra = w_extra is not None
    in_specs = [pl.BlockSpec((tm, k), lambda i, j: (i, x_col_block)),
                pl.BlockSpec((1, k), lambda i, j: (0, 0)),
                pl.BlockSpec((k, tn), lambda i, j: (0, j))]
    out_specs = [pl.BlockSpec((tm, tn), lambda i, j: (i, j))]
    out_shape = [jax.ShapeDtypeStruct((m, n), out_dtype)]
    args = [x, g.reshape(1, k).astype(F32), w]
    if has_extra:
        in_specs.append(pl.BlockSpec((k, LANES), lambda i, j: (0, 0)))
        out_specs.append(pl.BlockSpec((tm, LANES), lambda i, j: (i, 0)))
        out_shape.append(jax.ShapeDtypeStruct((m, LANES), extra_dtype))
        args.append(w_extra)
    outs = pl.pallas_call(
        functools.partial(_norm_proj_kernel, has_extra=has_extra),
        grid=(m // tm, n // tn),
        in_specs=in_specs,
        out_specs=out_specs,
        out_shape=out_shape,
        scratch_shapes=[pltpu.VMEM((tm, k), BF16)],
        compiler_params=_cparams(2),
        name="norm_proj",
    )(*args)
    return outs if has_extra else outs[0]


def _out_proj_kernel(y_ref, zy_ref, zm_ref, qm_ref, mkv_ref, x_ref, w_ref, *rest, wy, final):
    if final:
        gf_ref, o_ref, gated_ref = rest
    else:
        o_ref, gated_ref = rest
    scale = HEAD_DIM ** -0.5
    for h in range(MEM_HEADS):
        lo, hi = h * HEAD_DIM, (h + 1) * HEAD_DIM
        s = lax.dot_general(qm_ref[:, lo:hi], mkv_ref[:, lo:hi], NT_DIMS,
                            preferred_element_type=F32) * scale
        m = jnp.max(s, axis=-1, keepdims=True)
        p = jnp.exp(s - m)
        l = jnp.sum(p, axis=-1, keepdims=True)
        o = jnp.dot(p.astype(BF16), mkv_ref[:, MEM_WIDTH + lo:MEM_WIDTH + hi],
                    preferred_element_type=F32) / l
        zm = zm_ref[:, lo:hi].astype(F32)
        gated_ref[:, wy + lo:wy + hi] = (o * (zm * jax.nn.sigmoid(zm))).astype(BF16)
    tm = o_ref.shape[0]
    parts = 2 if tm % 32 == 0 else 1
    for part in range(parts):
        rows = pl.ds(part * (tm // parts), tm // parts)
        zy = zy_ref[rows, :].astype(F32)
        gated_ref[rows, :wy] = (y_ref[rows, :].astype(F32)
                                * (zy * jax.nn.sigmoid(zy))).astype(BF16)
        out = x_ref[rows, :] + jnp.dot(gated_ref[rows, :], w_ref[...],
                                       preferred_element_type=F32)
        if final:
            ms = jnp.mean(out * out, axis=-1, keepdims=True)
            out = out * lax.rsqrt(ms + EPS) * gf_ref[...]
        o_ref[rows, :] = out


def _out_proj(y, u, mkv, x, w_out, *, wy, zy_blk, zm_blk, qm_blk, seq, n_mem, final_g=None):
    m, d = x.shape
    tm = min(512, seq)
    final = final_g is not None
    in_specs = [
        pl.BlockSpec((tm, wy), lambda i: (i, 0)),
        pl.BlockSpec((tm, wy), lambda i: (i, zy_blk)),
        pl.BlockSpec((tm, MEM_WIDTH), lambda i: (i, zm_blk)),
        pl.BlockSpec((tm, MEM_WIDTH), lambda i: (i, qm_blk)),
        pl.BlockSpec((n_mem, 2 * MEM_WIDTH), lambda i: ((i * tm) // seq, 0)),
        pl.BlockSpec((tm, d), lambda i: (i, 0)),
        pl.BlockSpec((wy + MEM_WIDTH, d), lambda i: (0, 0), pipeline_mode=pl.Buffered(1)),
    ]
    args = [y, u, u, u, mkv, x, w_out]
    if final:
        in_specs.append(pl.BlockSpec((1, d), lambda i: (0, 0)))
        args.append(final_g.reshape(1, d).astype(F32))
    return pl.pallas_call(
        functools.partial(_out_proj_kernel, wy=wy, final=final),
        grid=(m // tm,),
        in_specs=in_specs,
        out_specs=pl.BlockSpec((tm, d), lambda i: (i, 0)),
        out_shape=jax.ShapeDtypeStruct((m, d), F32),
        scratch_shapes=[pltpu.VMEM((tm, wy + MEM_WIDTH), BF16)],
        compiler_params=_cparams(1),
        name="out_proj",
    )(*args)


def _fox_gate_kernel(f_ref, b_ref, c_ref, carry_ref):
    @pl.when(pl.program_id(1) == 0)
    def _():
        carry_ref[...] = jnp.zeros_like(carry_ref)

    f_t = f_ref[...].astype(F32).T[:N_HEADS, :] + b_ref[...]
    log_f = jnp.minimum(f_t, 0.0) - jnp.log(1.0 + jnp.exp(-jnp.abs(f_t)))
    ts = log_f.shape[1]
    r = lax.broadcasted_iota(I32, (ts, ts), 0)
    c = lax.broadcasted_iota(I32, (ts, ts), 1)
    upper = jnp.where(r <= c, 1.0, 0.0).astype(F32)
    cs = jnp.dot(log_f, upper, precision=lax.Precision.HIGHEST,
                 preferred_element_type=F32) + carry_ref[...]
    c_ref[...] = cs * LOG2E
    carry_ref[...] = cs[:, ts - 1:ts]


def _fox_gate(u3, bias, f_blk):
    b, s, _ = u3.shape
    ts = min(256, s)
    return pl.pallas_call(
        _fox_gate_kernel,
        grid=(b, s // ts),
        in_specs=[pl.BlockSpec((None, ts, LANES), lambda bi, j: (bi, j, f_blk)),
                  pl.BlockSpec((N_HEADS, 1), lambda bi, j: (0, 0))],
        out_specs=pl.BlockSpec((None, N_HEADS, ts), lambda bi, j: (bi, 0, j)),
        out_shape=jax.ShapeDtypeStruct((b, N_HEADS, s), F32),
        scratch_shapes=[pltpu.VMEM((N_HEADS, 1), F32)],
        compiler_params=_cparams(2),
        name="fox_gate",
    )(u3, bias.reshape(N_HEADS, 1).astype(F32))


STRIP = 32
HEADS_PER_STEP = 4


def _softmax_strips(t_ref, p_ref, m_ref, l_ref, alpha_ref, adjust):
    tq, tk = t_ref.shape
    for r0 in range(0, tq, min(STRIP, tq)):
        rs = pl.ds(r0, min(STRIP, tq))
        blocks = [adjust(r0, c, t_ref[rs, c * LANES:(c + 1) * LANES]) for c in range(tk // LANES)]
        live = [x for x in blocks if x is not None]
        mx = live[0]
        for x in live[1:]:
            mx = jnp.maximum(mx, x)
        m_old = m_ref[rs, :]
        m_new = jnp.maximum(m_old, jnp.max(mx, axis=-1, keepdims=True))
        alpha = jnp.exp2(m_old - m_new)
        ps = [None if x is None else jnp.exp2(x - m_new) for x in blocks]
        live = [x for x in ps if x is not None]
        sm = live[0]
        for x in live[1:]:
            sm = sm + x
        l_ref[rs, :] = alpha * l_ref[rs, :] + jnp.sum(sm, axis=-1, keepdims=True)
        m_ref[rs, :] = m_new
        alpha_ref[rs, :] = alpha
        for c, x in enumerate(ps):
            p_ref[rs, c * LANES:(c + 1) * LANES] = (
                jnp.zeros((min(STRIP, tq), LANES), BF16) if x is None else x.astype(BF16))


def _flash_scratch(n_streams, tq):
    return [pltpu.VMEM((n_streams, tq, tq), F32), pltpu.VMEM((n_streams, tq, tq), BF16),
            pltpu.VMEM((n_streams, tq, LANES), F32), pltpu.VMEM((n_streams, tq, LANES), F32),
            pltpu.VMEM((n_streams, tq, LANES), F32), pltpu.VMEM((n_streams, tq, HEAD_DIM), F32)]


def _flash_init(scratch):
    _, _, m_ref, l_ref, _, acc_ref = scratch
    m_ref[...] = jnp.full(m_ref.shape, NEG, F32)
    l_ref[...] = jnp.zeros(l_ref.shape, F32)
    acc_ref[...] = jnp.zeros(acc_ref.shape, F32)


def _flash_chunk(n_streams, j, raw_scores, values, to_logits, scratch, *, diagonal):
    t_ref, p_ref, m_ref, l_ref, alpha_ref, acc_ref = scratch
    strip = min(STRIP, t_ref.shape[1])
    row = lax.broadcasted_iota(I32, (strip, LANES), 0)
    col = lax.broadcasted_iota(I32, (strip, LANES), 1)
    for a in range(n_streams):
        t_ref[a] = raw_scores(a, j)
    for a in range(n_streams):
        def adjust(r0, c, x, a=a):
            t = to_logits(a, j, r0, c, x)
            if not diagonal or c * LANES + LANES - 1 <= r0:
                return t
            if c * LANES > r0 + strip - 1:
                return None
            return jnp.where(col + c * LANES <= row + r0, t, NEG)

        _softmax_strips(t_ref.at[a], p_ref.at[a], m_ref.at[a], l_ref.at[a], alpha_ref.at[a],
                        adjust)
    for a in range(n_streams):
        acc_ref[a] = alpha_ref[a] * acc_ref[a] + jnp.dot(p_ref[a], values(a, j),
                                                         preferred_element_type=F32)


def _flash_finish(n_streams, scratch):
    _, _, _, l_ref, _, acc_ref = scratch
    return [acc_ref[a] / l_ref[a] for a in range(n_streams)]


def _causal_flash(i, n_streams, raw_scores, values, to_logits, scratch):
    _flash_init(scratch)

    def body(j, _):
        _flash_chunk(n_streams, j, raw_scores, values, to_logits, scratch, diagonal=False)
        return 0

    lax.fori_loop(0, i, body, 0)
    _flash_chunk(n_streams, i, raw_scores, values, to_logits, scratch, diagonal=True)
    return _flash_finish(n_streams, scratch)


def _fox_attn_kernel(q_ref, k_ref, v_ref, c_ref, o_ref, qs_ref, *scratch, tq):
    i = pl.program_id(2)
    qk_scale = HEAD_DIM ** -0.5 * LOG2E
    qs_ref[...] = (q_ref[...].astype(F32) * qk_scale).astype(BF16)

    def raw_scores(a, j):
        off = pl.multiple_of(j * tq, tq)
        lo, hi = a * HEAD_DIM, (a + 1) * HEAD_DIM
        return lax.dot_general(qs_ref[:, lo:hi], k_ref[pl.ds(off, tq), lo:hi], NT_DIMS,
                               preferred_element_type=F32)

    def to_logits(a, j, r0, c, x):
        return x - c_ref[a, j, :, c * LANES:(c + 1) * LANES]

    def values(a, j):
        off = pl.multiple_of(j * tq, tq)
        return v_ref[pl.ds(off, tq), a * HEAD_DIM:(a + 1) * HEAD_DIM]

    outs = _causal_flash(i, HEADS_PER_STEP, raw_scores, values, to_logits, scratch)
    for a in range(HEADS_PER_STEP):
        o_ref[:, a * HEAD_DIM:(a + 1) * HEAD_DIM] = outs[a].astype(o_ref.dtype)


def _fox_attn(u3, c_t, *, q_blk, k_blk, v_blk):
    b, s, _ = u3.shape
    tq = min(512, s)
    nq = s // tq
    pw = HEADS_PER_STEP * HEAD_DIM
    c5 = c_t.reshape(b, N_HEADS, nq, 1, tq)
    return pl.pallas_call(
        functools.partial(_fox_attn_kernel, tq=tq),
        grid=(b, N_HEADS // HEADS_PER_STEP, nq),
        in_specs=[pl.BlockSpec((None, tq, pw), lambda bi, p, i: (bi, i, q_blk + p)),
                  pl.BlockSpec((None, s, pw), lambda bi, p, i: (bi, 0, k_blk + p)),
                  pl.BlockSpec((None, s, pw), lambda bi, p, i: (bi, 0, v_blk + p)),
                  pl.BlockSpec((None, HEADS_PER_STEP, nq, 1, tq), lambda bi, p, i: (bi, p, 0, 0, 0))],
        out_specs=pl.BlockSpec((None, tq, pw), lambda bi, p, i: (bi, i, p)),
        out_shape=jax.ShapeDtypeStruct((b, s, N_HEADS * HEAD_DIM), BF16),
        scratch_shapes=[pltpu.VMEM((tq, pw), BF16)] + _flash_scratch(HEADS_PER_STEP, tq),
        compiler_params=_cparams(3),
        name="fox_attn",
    )(u3, u3, u3, c5)


def _dsa_kprep_kernel(k_ref, mk_ref, c128_ref, s128_ref, c64_ref, s64_ref,
                      kr_ref, klo_ref, khi_ref):
    kr_ref[...] = _rope128(k_ref[...].astype(F32), c128_ref[...], s128_ref[...]).astype(BF16)
    ki = _rope64(mk_ref[...].astype(F32), c64_ref[...], s64_ref[...])
    lane = lax.broadcasted_iota(I32, ki.shape, 1)
    lo = jnp.where(lane < IDX_DIM, ki, 0.0)
    klo_ref[...] = lo.astype(BF16)
    khi_ref[...] = pltpu.roll(lo, IDX_DIM, 1).astype(BF16)


def _dsa_kprep(u3, tabs, *, k_blk, misc_blk):
    b, s, _ = u3.shape
    ts = min(512, s)
    c128, s128, c64, s64 = tabs

    def spec(blk):
        return pl.BlockSpec((None, ts, LANES), lambda bi, i: (bi, i, blk))

    return pl.pallas_call(
        _dsa_kprep_kernel,
        grid=(b, s // ts),
        in_specs=[spec(k_blk), spec(misc_blk), spec(0), spec(0), spec(0), spec(0)],
        out_specs=[spec(0)] * 3,
        out_shape=[jax.ShapeDtypeStruct((b, s, LANES), BF16)] * 3,
        compiler_params=_cparams(2),
        name="dsa_kprep",
    )(u3, u3, c128, s128, c64, s64)


def _dsa_kernel(q_ref, qi_ref, mq_ref, v_ref, kr_ref, klo_ref, khi_ref,
                cq128_ref, sq128_ref, cq64_ref, sq64_ref, y_in_ref,
                o_ref,
                qr_ref, qir_ref, w_ref, sc_ref, t4_ref, yo_ref, *flash,
                tq, kl, q_off, n_sel):
    del y_in_ref
    t0 = q_off + pl.program_id(1) * tq
    qk_scale = HEAD_DIM ** -0.5 * LOG2E
    idx_scale = (IDX_DIM ** -0.5) * (IDX_HEADS ** -0.5)
    ck = min(512, kl)
    n_chunks = kl // ck
    strip = min(STRIP, tq)

    cq128, sq128 = cq128_ref[...], sq128_ref[...]
    for h in range(N_HEADS):
        qh = q_ref[:, h * HEAD_DIM:(h + 1) * HEAD_DIM].astype(F32)
        qr_ref[h] = (_rope128(qh, cq128, sq128) * qk_scale).astype(BF16)
    cq64, sq64 = cq64_ref[...], sq64_ref[...]
    for a in range(IDX_HEADS // 2):
        qir_ref[a] = _rope64(qi_ref[:, a * LANES:(a + 1) * LANES].astype(F32),
                             cq64, sq64).astype(BF16)
    w_ref[...] = mq_ref[...].astype(F32) * idx_scale

    row = lax.broadcasted_iota(I32, (strip, LANES), 0)
    col = lax.broadcasted_iota(I32, (strip, LANES), 1)
    group = 4

    def idx_chunk(j, _):
        off = pl.multiple_of(j * ck, ck)
        k_lo, k_hi = klo_ref[pl.ds(off, ck), :], khi_ref[pl.ds(off, ck), :]
        for g in range(IDX_HEADS // group):
            for a2 in range(group // 2):
                x = qir_ref[g * (group // 2) + a2]
                t4_ref[2 * a2] = lax.dot_general(x, k_lo, NT_DIMS, preferred_element_type=F32)
                t4_ref[2 * a2 + 1] = lax.dot_general(x, k_hi, NT_DIMS,
                                                     preferred_element_type=F32)
            last = g == IDX_HEADS // group - 1
            for r0 in range(0, tq, strip):
                rs = pl.ds(r0, strip)
                w_rows = w_ref[rs, :]
                ws = [jnp.broadcast_to(w_rows[:, IDX_DIM + g * group + u:IDX_DIM + g * group + u + 1],
                                       (strip, LANES)) for u in range(group)]
                for c in range(ck // LANES):
                    cs = pl.ds(c * LANES, LANES)
                    acc = ws[0] * jnp.maximum(t4_ref[0, rs, cs], 0.0)
                    for u in range(1, group):
                        acc = acc + ws[u] * jnp.maximum(t4_ref[u, rs, cs], 0.0)
                    if g > 0:
                        acc = acc + sc_ref[j, rs, cs]
                    if last:
                        causal = col + (off + c * LANES) <= row + (t0 + r0)
                        acc = jnp.where(causal, acc, -jnp.inf)
                    sc_ref[j, rs, cs] = acc
        return 0

    lax.fori_loop(0, n_chunks, idx_chunk, 0)

    shape3 = (n_chunks, tq, ck)
    col3 = lax.broadcasted_iota(I32, shape3, 0) * ck + lax.broadcasted_iota(I32, shape3, 2)
    causal3 = col3 <= lax.broadcasted_iota(I32, shape3, 1) + t0

    def count(x):
        return jnp.sum(jnp.sum(x, axis=0), axis=-1, keepdims=True)

    def key_to_f32(key):
        return pltpu.bitcast(jnp.where(key < 0, key ^ jnp.int32(0x7FFFFFFF), key), F32)

    def thr_body(it, key):
        cand = key + (jnp.int32(1) << (31 - it))
        c = count(jnp.where(sc_ref[...] >= key_to_f32(cand), 1.0, 0.0))
        return jnp.where(c >= n_sel, cand, key)

    thr_key = lax.fori_loop(0, 32, thr_body, jnp.full((tq, 1), INT_MIN, I32))
    take_all = thr_key == INT_MIN
    thr = key_to_f32(thr_key)
    score = sc_ref[...]
    need = n_sel - count(jnp.where(score > thr, 1.0, 0.0))
    surplus = jnp.where(take_all, 0.0, count(jnp.where(score == thr, 1.0, 0.0)) - need)
    has_surplus = jnp.max(surplus) > 0.0

    n_bits = kl.bit_length()

    def tie_body(it, jm):
        cand = jm + (jnp.int32(1) << (n_bits - 1 - it))
        f = count(jnp.where(sc_ref[...] == thr, jnp.where(col3 < cand, 1.0, 0.0), 0.0))
        return jnp.where(jnp.logical_and(f < need, cand <= kl), cand, jm)

    jm = lax.fori_loop(0, jnp.where(has_surplus, n_bits, 0), tie_body,
                       jnp.broadcast_to(jnp.where(has_surplus, 0, kl), (tq, 1)).astype(I32))
    keep = jnp.where(jnp.logical_or(score > thr, take_all), 1.0,
                     jnp.where(score == thr, jnp.where(col3 <= jm, 1.0, 0.0), 0.0))
    sc_ref[...] = jnp.where(causal3, jnp.where(keep > 0.0, 0.0, NEG), NEG)

    def raw_scores(h0):
        def f(a, j):
            return lax.dot_general(qr_ref[h0 + a], kr_ref[j * ck:(j + 1) * ck, :], NT_DIMS,
                                   preferred_element_type=F32)
        return f

    def values(a, j):
        return v_ref[j * ck:(j + 1) * ck, :]

    def to_logits(a, j, r0, c, x):
        return x + sc_ref[j, r0:r0 + strip, c * LANES:(c + 1) * LANES]

    def group_body(pp, _):
        h0 = HEADS_PER_STEP * pp
        _flash_init(flash)
        for j in range(n_chunks):
            _flash_chunk(HEADS_PER_STEP, j, raw_scores(h0), values, to_logits, flash,
                         diagonal=False)
        outs = _flash_finish(HEADS_PER_STEP, flash)
        for a in range(HEADS_PER_STEP):
            yo_ref[h0 + a] = outs[a].astype(yo_ref.dtype)
        return 0

    lax.fori_loop(0, N_HEADS // HEADS_PER_STEP, group_body, 0)
    for h in range(N_HEADS):
        o_ref[:, h * HEAD_DIM:(h + 1) * HEAD_DIM] = yo_ref[h]


def _dsa_group(u3, kprep, tabs, y, *, q_off, rows, kl, n_sel, q_blk, v_blk, qi_blk, misc_blk):
    b, s, _ = u3.shape
    tq = rows
    assert kl % min(512, kl) == 0 and tq == min(512, kl)
    qb0 = q_off // tq
    c128, s128, c64, s64 = tabs
    kr, klo, khi = kprep
    qw = N_HEADS * HEAD_DIM
    iw = IDX_HEADS * IDX_DIM
    ck = min(512, kl)

    def qspec(width, blk):
        return pl.BlockSpec((None, tq, width), lambda bi, i: (bi, qb0 + i, blk))

    def kspec(blk):
        return pl.BlockSpec((None, kl, LANES), lambda bi, i: (bi, 0, blk))

    return pl.pallas_call(
        functools.partial(_dsa_kernel, tq=tq, kl=kl, q_off=q_off, n_sel=n_sel),
        grid=(b, rows // tq),
        in_specs=[qspec(qw, q_blk), qspec(iw, qi_blk), qspec(LANES, misc_blk),
                  kspec(v_blk), kspec(0), kspec(0), kspec(0),
                  qspec(LANES, 0), qspec(LANES, 0), qspec(LANES, 0), qspec(LANES, 0),
                  pl.BlockSpec(memory_space=pl.ANY)],
        out_specs=pl.BlockSpec((None, tq, qw), lambda bi, i: (bi, qb0 + i, 0)),
        out_shape=jax.ShapeDtypeStruct((b, s, qw), BF16),
        input_output_aliases={11: 0},
        scratch_shapes=[pltpu.VMEM((N_HEADS, tq, HEAD_DIM), BF16),
                        pltpu.VMEM((IDX_HEADS // 2, tq, LANES), BF16),
                        pltpu.VMEM((tq, LANES), F32),
                        pltpu.VMEM((kl // ck, tq, ck), F32),
                        pltpu.VMEM((4, tq, ck), F32),
                        pltpu.VMEM((N_HEADS, tq, HEAD_DIM), BF16)]
        + _flash_scratch(HEADS_PER_STEP, tq),
        compiler_params=_cparams(2),
        name="dsa_attn",
    )(u3, u3, u3, u3, kr, klo, khi, c128, s128, c64, s64, y)


def _dsa_attn(u3, tabs, *, k_blk, misc_blk, **blks):
    b, s, _ = u3.shape
    n_sel = min(TOPK_MAX, s // 4)
    rows = min(512, s)
    kprep = _dsa_kprep(u3, tabs, k_blk=k_blk, misc_blk=misc_blk)
    y = jnp.zeros((b, s, N_HEADS * HEAD_DIM), BF16)
    for q_off in range(0, s, rows):
        y = _dsa_group(u3, kprep, tabs, y, q_off=q_off, rows=rows, kl=q_off + rows, n_sel=n_sel,
                       misc_blk=misc_blk, **blks)
    return y


def _dil_kernel(q0_ref, k0_ref, v0_ref, q1_ref, k1_ref, v1_ref, q2_ref, k2_ref, v2_ref,
                cos_ref, sin_ref, y_ref, qr_ref, kr_ref, o_ref, lse_ref, t_ref, p_ref, *, seq):
    qk_scale = HEAD_DIM ** -0.5 * LOG2E
    cos_f, sin_s = cos_ref[...], sin_ref[...]
    groups = ((q0_ref, k0_ref, v0_ref), (q1_ref, k1_ref, v1_ref), (q2_ref, k2_ref, v2_ref))
    for g, (q_ref, k_ref, _) in enumerate(groups):
        qr_ref[g] = _rope128(q_ref[...].astype(F32), cos_f, sin_s) * qk_scale
        kr_ref[g] = _rope128(k_ref[...].astype(F32), cos_f, sin_s)

    for g, (window, dil) in enumerate(DILATED_PAIRS):
        v_ref = groups[g][2]
        sub = seq // dil
        qb = min(window // dil, sub)
        tiles = [(r, i) for r in range(dil) for i in range(sub // qb)]

        def rows(r, blk, dil=dil, qb=qb):
            start = r + dil * qb * blk
            return pl.ds(start, qb) if dil == 1 else pl.ds(start, qb, stride=dil)

        for n, (r, i) in enumerate(tiles):
            q = qr_ref[g, rows(r, i), :].astype(BF16)
            k_cur = kr_ref[g, rows(r, i), :].astype(BF16)
            t_ref[n, :qb, qb:2 * qb] = lax.dot_general(q, k_cur, NT_DIMS,
                                                       preferred_element_type=F32)
            if i > 0:
                k_prev = kr_ref[g, rows(r, i - 1), :].astype(BF16)
                t_ref[n, :qb, :qb] = lax.dot_general(q, k_prev, NT_DIMS,
                                                     preferred_element_type=F32)
        row = lax.broadcasted_iota(I32, (qb, qb), 0)
        col = lax.broadcasted_iota(I32, (qb, qb), 1)
        for n, (r, i) in enumerate(tiles):
            t_cur = jnp.where(col <= row, t_ref[n, :qb, qb:2 * qb], NEG)
            m = jnp.max(t_cur, axis=-1, keepdims=True)
            if i > 0:
                t_prev = jnp.where(col >= row, t_ref[n, :qb, :qb], NEG)
                m = jnp.maximum(m, jnp.max(t_prev, axis=-1, keepdims=True))
            p_cur = jnp.exp2(t_cur - m)
            l = jnp.sum(p_cur, axis=-1, keepdims=True)
            p_ref[n, :qb, qb:2 * qb] = p_cur.astype(BF16)
            if i > 0:
                p_prev = jnp.exp2(t_prev - m)
                l = l + jnp.sum(p_prev, axis=-1, keepdims=True)
                p_ref[n, :qb, :qb] = p_prev.astype(BF16)
            t_ref[n, :qb, :LANES] = jnp.broadcast_to(l, (qb, LANES))
            t_ref[n, :qb, LANES:2 * LANES] = jnp.broadcast_to(m, (qb, LANES))
        for n, (r, i) in enumerate(tiles):
            acc = jnp.dot(p_ref[n, :qb, qb:2 * qb], v_ref[rows(r, i), :].astype(BF16),
                          preferred_element_type=F32)
            if i > 0:
                acc = acc + jnp.dot(p_ref[n, :qb, :qb], v_ref[rows(r, i - 1), :].astype(BF16),
                                    preferred_element_type=F32)
            l = t_ref[n, :qb, :LANES]
            o_ref[g, rows(r, i), :] = acc / l
            lse_ref[g, rows(r, i), :] = t_ref[n, :qb, LANES:2 * LANES] + jnp.log2(l)

    l0, l1, l2 = lse_ref[0], lse_ref[1], lse_ref[2]
    m = jnp.maximum(jnp.maximum(l0, l1), l2)
    e0, e1, e2 = jnp.exp2(l0 - m), jnp.exp2(l1 - m), jnp.exp2(l2 - m)
    y = (e0 * o_ref[0] + e1 * o_ref[1] + e2 * o_ref[2]) / (e0 + e1 + e2)
    y_ref[...] = y.astype(y_ref.dtype)


def _dil_attn(ua3, ub3, tabs128, *, q0_blk, k0_blk, v0_blk):
    b, s, _ = ua3.shape
    for window, dil in DILATED_PAIRS:
        assert s % dil == 0 and (s // dil) % min(window // dil, s // dil) == 0
    cos_t, sin_t = tabs128
    tile = min(DILATED_PAIRS[0][0], s)
    n_tiles = s // min(min(w // d, s // d) for w, d in DILATED_PAIRS)

    def spec(blk0):
        return pl.BlockSpec((None, s, HEAD_DIM), lambda bi, a: (bi, 0, blk0 + a))

    tab = pl.BlockSpec((None, s, LANES), lambda bi, a: (bi, 0, 0))
    return pl.pallas_call(
        functools.partial(_dil_kernel, seq=s),
        grid=(b, DIL_HEADS),
        in_specs=[spec(q0_blk), spec(k0_blk), spec(v0_blk)]
        + [spec(j * DIL_HEADS) for j in range(6)] + [tab, tab],
        out_specs=pl.BlockSpec((None, s, HEAD_DIM), lambda bi, a: (bi, 0, a)),
        out_shape=jax.ShapeDtypeStruct((b, s, DIL_WIDTH), BF16),
        scratch_shapes=[pltpu.VMEM((3, s, HEAD_DIM), F32)] * 4
        + [pltpu.VMEM((n_tiles, tile, 2 * LANES), F32), pltpu.VMEM((n_tiles, tile, 2 * LANES), BF16)],
        compiler_params=_cparams(2),
        name="dilated_attn",
    )(ua3, ua3, ua3, ub3, ub3, ub3, ub3, ub3, ub3, cos_t, sin_t)


def _mla_kernel(qn_ref, qr_ref, kn_ref, kr_ref, v_ref, cq_ref, sq_ref, ck_ref, sk_ref,
                o_ref, qcat_ref, kcat_ref, *scratch, tq):
    i = pl.program_id(2)
    qk_scale = (QK_NOPE + QK_ROPE) ** -0.5 * LOG2E
    cat = QK_NOPE + QK_ROPE

    @pl.when(i == 0)
    def _():
        k_rope = _rope64(kr_ref[...].astype(F32), ck_ref[...],
                         sk_ref[...])[:, :QK_ROPE].astype(BF16)
        for a in range(HEADS_PER_STEP):
            kcat_ref[a, :, :QK_NOPE] = kn_ref[:, a * HEAD_DIM:(a + 1) * HEAD_DIM]
            kcat_ref[a, :, QK_NOPE:cat] = k_rope
            kcat_ref[a, :, cat:] = jnp.zeros((kcat_ref.shape[1], 2 * LANES - cat), BF16)

    cos_q, sin_q = cq_ref[...], sq_ref[...]
    for a2 in range(HEADS_PER_STEP // 2):
        q_rope = (_rope64(qr_ref[:, a2 * LANES:(a2 + 1) * LANES].astype(F32), cos_q, sin_q)
                  * qk_scale).astype(BF16)
        for a in (2 * a2, 2 * a2 + 1):
            q_nope = qn_ref[:, a * HEAD_DIM:(a + 1) * HEAD_DIM].astype(F32) * qk_scale
            qcat_ref[a, :, :QK_NOPE] = q_nope.astype(BF16)
            qcat_ref[a, :, QK_NOPE:cat] = q_rope[:, (a % 2) * QK_ROPE:(a % 2 + 1) * QK_ROPE]
            qcat_ref[a, :, cat:] = jnp.zeros((tq, 2 * LANES - cat), BF16)

    def raw_scores(a, j):
        off = pl.multiple_of(j * tq, tq)
        return lax.dot_general(qcat_ref[a], kcat_ref[a, pl.ds(off, tq), :], NT_DIMS,
                               preferred_element_type=F32)

    def to_logits(a, j, r0, c, x):
        return x

    def values(a, j):
        off = pl.multiple_of(j * tq, tq)
        return v_ref[pl.ds(off, tq), a * HEAD_DIM:(a + 1) * HEAD_DIM]

    outs = _causal_flash(i, HEADS_PER_STEP, raw_scores, values, to_logits, scratch)
    for a in range(HEADS_PER_STEP):
        o_ref[:, a * HEAD_DIM:(a + 1) * HEAD_DIM] = outs[a].astype(o_ref.dtype)


def _mla_attn(qf3, kvf3, u3, tabs64, *, kr_blk):
    b, s, _ = qf3.shape
    tq = min(512, s)
    gw = HEADS_PER_STEP * HEAD_DIM
    rw = HEADS_PER_STEP * QK_ROPE
    n_groups = N_HEADS // HEADS_PER_STEP
    cos_t, sin_t = tabs64
    return pl.pallas_call(
        functools.partial(_mla_kernel, tq=tq),
        grid=(b, n_groups, s // tq),
        in_specs=[pl.BlockSpec((None, tq, gw), lambda bi, p, i: (bi, i, p)),
                  pl.BlockSpec((None, tq, rw),
                               lambda bi, p, i: (bi, i, N_HEADS * QK_NOPE // rw + p)),
                  pl.BlockSpec((None, s, gw), lambda bi, p, i: (bi, 0, p)),
                  pl.BlockSpec((None, s, LANES), lambda bi, p, i: (bi, 0, kr_blk)),
                  pl.BlockSpec((None, s, gw), lambda bi, p, i: (bi, 0, n_groups + p)),
                  pl.BlockSpec((None, tq, LANES), lambda bi, p, i: (bi, i, 0)),
                  pl.BlockSpec((None, tq, LANES), lambda bi, p, i: (bi, i, 0)),
                  pl.BlockSpec((None, s, LANES), lambda bi, p, i: (bi, 0, 0)),
                  pl.BlockSpec((None, s, LANES), lambda bi, p, i: (bi, 0, 0))],
        out_specs=pl.BlockSpec((None, tq, gw), lambda bi, p, i: (bi, i, p)),
        out_shape=jax.ShapeDtypeStruct((b, s, N_HEADS * HEAD_DIM), BF16),
        scratch_shapes=[pltpu.VMEM((HEADS_PER_STEP, tq, 2 * LANES), BF16),
                        pltpu.VMEM((HEADS_PER_STEP, s, 2 * LANES), BF16)]
        + _flash_scratch(HEADS_PER_STEP, tq),
        compiler_params=_cparams(3),
        name="mla_attn",
    )(qf3, qf3, kvf3, u3, kvf3, cos_t, sin_t, cos_t, sin_t)


def _pack(parts, multiple=512):
    k = next(p.shape[0] for p, _ in parts if p is not None)
    cols = [jnp.zeros((k, w), BF16) if p is None else p.astype(BF16) for p, w in parts]
    n = sum(w for _, w in parts)
    pad = (-n) % multiple
    if pad:
        cols.append(jnp.zeros((k, pad), BF16))
    return jnp.concatenate(cols, axis=1)


def _cols(w, sizes):
    out, acc = [], 0
    for sz in sizes:
        out.append(w[:, acc:acc + sz])
        acc += sz
    return out


def kernel(x, mem, positions, l0_norm, l0_w_in, l0_forget_bias, l0_mem_norm, l0_w_mem_kv, l0_w_out, l1_norm, l1_w_in, l1_mem_norm, l1_w_mem_kv, l1_w_out, l2_norm, l2_w_in, l2_mem_norm, l2_w_mem_kv, l2_w_out, l3_norm, l3_w_in, l3_q_norm, l3_w_uq, l3_kv_norm, l3_w_ukv, l3_mem_norm, l3_w_mem_kv, l3_w_out, final_norm):
    b, s, d = x.shape
    n_mem = mem.shape[1]
    m = b * s
    mw = N_HEADS * HEAD_DIM
    x2 = x.reshape(m, d)
    mem2 = mem.reshape(b * n_mem, d)

    pos_b = jnp.broadcast_to(positions.astype(F32).reshape(m, 1), (m, LANES))
    tabs128 = tuple(t.reshape(b, s, LANES) for t in _rope_tables(pos_b, HEAD_DIM))
    tabs64 = tuple(t.reshape(b, s, LANES) for t in _rope_tables(pos_b, IDX_DIM))

    def mem_kv(g, w):
        return _norm_proj(mem2, g, w.astype(BF16))

    def finish(y, u, x_in, g_mem, w_mem_kv, w_out, *, wy, zy_blk, zm_blk, qm_blk, final_g=None):
        return _out_proj(y, u, mem_kv(g_mem, w_mem_kv), x_in, w_out.astype(BF16), wy=wy,
                         zy_blk=zy_blk, zm_blk=zm_blk, qm_blk=qm_blk, seq=s, n_mem=n_mem,
                         final_g=final_g)

    q_w, k_w, v_w, f_w, qm_w, z_w = _cols(l0_w_in, (mw, mw, mw, N_HEADS, MEM_WIDTH, mw + MEM_WIDTH))
    w0 = _pack([(z_w[:, :mw], mw), (z_w[:, mw:], MEM_WIDTH), (qm_w, MEM_WIDTH), (q_w, mw),
                (k_w, mw), (v_w, mw)])
    u, f_arr = _norm_proj(x2, l0_norm, w0, w_extra=_pack([(f_w, N_HEADS)], multiple=LANES),
                          extra_dtype=F32)
    u3 = u.reshape(b, s, -1)
    base = (mw + 2 * MEM_WIDTH) // LANES
    c_t = _fox_gate(f_arr.reshape(b, s, LANES), l0_forget_bias, 0)
    y = _fox_attn(u3, c_t, q_blk=base // HEADS_PER_STEP, k_blk=(base + N_HEADS) // HEADS_PER_STEP,
                  v_blk=(base + 2 * N_HEADS) // HEADS_PER_STEP)
    x2 = finish(y.reshape(m, mw), u, x2, l0_mem_norm, l0_w_mem_kv, l0_w_out, wy=mw, zy_blk=0,
                zm_blk=mw // MEM_WIDTH, qm_blk=mw // MEM_WIDTH + 1)

    iw = IDX_HEADS * IDX_DIM
    q_w, k_w, v_w, qi_w, ki_w, wi_w, qm_w, z_w = _cols(
        l1_w_in, (mw, HEAD_DIM, HEAD_DIM, iw, IDX_DIM, IDX_HEADS, MEM_WIDTH, mw + MEM_WIDTH))
    w1 = _pack([(z_w[:, :mw], mw), (q_w, mw), (qi_w, iw), (z_w[:, mw:], MEM_WIDTH),
                (qm_w, MEM_WIDTH), (k_w, HEAD_DIM), (v_w, HEAD_DIM), (ki_w, IDX_DIM),
                (wi_w, IDX_HEADS), (None, LANES - IDX_DIM - IDX_HEADS)])
    u = _norm_proj(x2, l1_norm, w1)
    u3 = u.reshape(b, s, -1)
    kv0 = (2 * mw + iw + 2 * MEM_WIDTH) // LANES
    y = _dsa_attn(u3, tabs128 + tabs64, q_blk=1, qi_blk=2 * mw // iw, k_blk=kv0, v_blk=kv0 + 1,
                  misc_blk=kv0 + 2)
    zoff = (2 * mw + iw) // MEM_WIDTH
    x2 = finish(y.reshape(m, mw), u, x2, l1_mem_norm, l1_w_mem_kv, l1_w_out, wy=mw, zy_blk=0,
                zm_blk=zoff, qm_blk=zoff + 1)

    gw = len(DILATED_PAIRS) * DIL_WIDTH
    q_w, k_w, v_w, qm_w, z_w = _cols(l2_w_in, (gw, gw, gw, MEM_WIDTH, DIL_WIDTH + MEM_WIDTH))

    def grp(w, g):
        return (w[:, g * DIL_WIDTH:(g + 1) * DIL_WIDTH], DIL_WIDTH)

    wa = _pack([grp(q_w, 0), (z_w[:, :DIL_WIDTH], DIL_WIDTH), grp(k_w, 0), grp(v_w, 0),
                (z_w[:, DIL_WIDTH:], MEM_WIDTH), (qm_w, MEM_WIDTH)])
    wb = _pack([grp(q_w, 1), grp(k_w, 1), grp(v_w, 1), grp(q_w, 2), grp(k_w, 2), grp(v_w, 2)])
    u = _norm_proj(x2, l2_norm, wa)
    ub = _norm_proj(x2, l2_norm, wb, out_dtype=F32)
    y = _dil_attn(u.reshape(b, s, -1), ub.reshape(b, s, -1), tabs128, q0_blk=0,
                  k0_blk=2 * DIL_HEADS, v0_blk=3 * DIL_HEADS)
    zoff = 4 * DIL_WIDTH // MEM_WIDTH
    x2 = finish(y.reshape(m, DIL_WIDTH), u, x2, l2_mem_norm, l2_w_mem_kv, l2_w_out, wy=DIL_WIDTH,
                zy_blk=1, zm_blk=zoff, qm_blk=zoff + 1)

    q_lora, kv_lora = l3_w_uq.shape[0], l3_w_ukv.shape[0]
    cq_w, ckv_w, kr_w, qm_w, z_w = _cols(l3_w_in, (q_lora, kv_lora, QK_ROPE, MEM_WIDTH,
                                                  mw + MEM_WIDTH))
    assert q_lora == MEM_WIDTH and kv_lora == MEM_WIDTH
    w3 = _pack([(z_w[:, :mw], mw), (z_w[:, mw:], MEM_WIDTH), (qm_w, MEM_WIDTH), (cq_w, q_lora),
                (ckv_w, kv_lora)])
    u, kr_arr = _norm_proj(x2, l3_norm, w3, w_extra=_pack([(kr_w, QK_ROPE)], multiple=LANES))
    cq_blk = (mw + 2 * MEM_WIDTH) // MEM_WIDTH
    uq = l3_w_uq.reshape(q_lora, N_HEADS, QK_NOPE + QK_ROPE)
    w_uq = _pack([(uq[:, :, :QK_NOPE].reshape(q_lora, -1), N_HEADS * QK_NOPE),
                  (uq[:, :, QK_NOPE:].reshape(q_lora, -1), N_HEADS * QK_ROPE)])
    ukv = l3_w_ukv.reshape(kv_lora, N_HEADS, QK_NOPE + HEAD_DIM)
    w_ukv = _pack([(ukv[:, :, :QK_NOPE].reshape(kv_lora, -1), N_HEADS * QK_NOPE),
                   (ukv[:, :, QK_NOPE:].reshape(kv_lora, -1), N_HEADS * HEAD_DIM)])
    qf = _norm_proj(u, l3_q_norm, w_uq, x_col_block=cq_blk)
    kvf = _norm_proj(u, l3_kv_norm, w_ukv, x_col_block=cq_blk + 1)
    y = _mla_attn(qf.reshape(b, s, -1), kvf.reshape(b, s, -1), kr_arr.reshape(b, s, LANES), tabs64,
                  kr_blk=0)
    x2 = finish(y.reshape(m, mw), u, x2, l3_mem_norm, l3_w_mem_kv, l3_w_out, wy=mw, zy_blk=0,
                zm_blk=mw // MEM_WIDTH, qm_blk=mw // MEM_WIDTH + 1, final_g=final_norm)
    return x2.reshape(b, s, d)
```
